```python
import jax, jax.numpy as jnp
from jax import lax
import numpy as np

D_MODEL = 1024
BATCH = 2
SEQ = 8192
DEPTH = 1

PLE_DIM = 256
HEAD_DIM = 64
RWKV_HEADS = 8
RWKV_DIM = RWKV_HEADS * HEAD_DIM
W_LORA = 64
A_LORA = 64
G_LORA = 128
ATTN_GROUPS = ((128, 1), (512, 4), (2048, 16))
HEADS_PER_GROUP = 8
ATTN_HEADS = HEADS_PER_GROUP * len(ATTN_GROUPS)
ATTN_DIM = ATTN_HEADS * HEAD_DIM
ATTN_OUT_DIM = HEADS_PER_GROUP * HEAD_DIM
Q_BLOCK = 128
N_BRANCHES = 2
RWKV_COLS = 3 * RWKV_DIM + W_LORA + A_LORA + G_LORA
ATTN_COLS = 3 * ATTN_DIM
GATE_COLS = N_BRANCHES * D_MODEL
IN_COLS = RWKV_COLS + ATTN_COLS + GATE_COLS
N_EXPERT_GROUPS = 4
EXPERTS_PER_GROUP = 8
TOP_K_EXPERT = 2
D_FF_EXPERT = 256
NORM_EPS = 1e-6
RWKV_GN_EPS = 64e-5

kernel_name = "hybrid_rwkv7_dilated_alibi_hmoe_block"


def rmsnorm(x, gain):
    x32 = x.astype(jnp.float32)
    return x32 * lax.rsqrt(jnp.mean(x32 * x32, axis=-1, keepdims=True) + NORM_EPS) * gain


def token_shift(t, mu):
    prev = jnp.pad(t[:, :-1], ((0, 0), (1, 0), (0, 0)))
    return t + (prev - t) * mu


def wkv7_scan(r, decay, k, v, kk, a):
    b, s, h, n = r.shape

    def step(state, inp):
        r_t, w_t, k_t, v_t, kk_t, a_t = inp
        sa = jnp.einsum('bhij,bhj->bhi', state, -kk_t)
        state = (state * w_t[:, :, None, :]
                 + sa[..., :, None] * (kk_t * a_t)[:, :, None, :]
                 + v_t[..., :, None] * k_t[:, :, None, :])
        y = jnp.einsum('bhij,bhj->bhi', state, r_t)
        return state, y

    xs = tuple(jnp.moveaxis(t, 1, 0) for t in (r, decay, k, v, kk, a))
    state0 = jnp.zeros((b, h, n, n), jnp.float32)
    _, ys = lax.scan(step, state0, xs)
    return jnp.moveaxis(ys, 0, 1)


def rwkv7_time_mix(z, w0, w2, a0, a2, g2, k_k, k_a, r_k, ln_w, ln_b):
    z = z.astype(jnp.float32)
    b, s, _ = z.shape
    o1, o2, o3 = RWKV_DIM, 2 * RWKV_DIM, 3 * RWKV_DIM
    o4, o5 = o3 + W_LORA, o3 + W_LORA + A_LORA
    r, k, v = z[..., :o1], z[..., o1:o2], z[..., o2:o3]
    w_d, a_d, g_d = z[..., o3:o4], z[..., o4:o5], z[..., o5:]
    w_log = -jax.nn.softplus(-(w0 + jnp.tanh(w_d) @ w2)) - 0.5
    decay = jnp.exp(-jnp.exp(w_log))
    a = jax.nn.sigmoid(a0 + a_d @ a2)
    g = jax.nn.sigmoid(g_d) @ g2
    heads = lambda t: t.reshape(b, s, RWKV_HEADS, HEAD_DIM)
    kk = heads(k * k_k)
    kk = kk * lax.rsqrt(jnp.sum(kk * kk, axis=-1, keepdims=True) + 1e-12)
    k = k * (1.0 + (a - 1.0) * k_a)
    rh, kh, vh = heads(r), heads(k), heads(v)
    y = wkv7_scan(rh, heads(decay), kh, vh, kk, heads(a))
    mean = jnp.mean(y, axis=-1, keepdims=True)
    var = jnp.mean(jnp.square(y - mean), axis=-1, keepdims=True)
    y = ((y - mean) * lax.rsqrt(var + RWKV_GN_EPS)).reshape(b, s, RWKV_DIM) * ln_w + ln_b
    y = y + (jnp.sum(rh * kh * r_k, axis=-1, keepdims=True) * vh).reshape(b, s, RWKV_DIM)
    return y * g


def alibi_slopes(n_heads):
    return jnp.asarray(2.0 ** (-8.0 * np.arange(1, n_heads + 1) / n_heads), dtype=jnp.float32)


def dilated_window_attention(q, k, v, slopes, window, dilation):
    b, s, h, hd = q.shape
    d = dilation
    n_back = window // dilation
    span = d * Q_BLOCK
    s_pad = -(-s // span) * span
    L = s_pad // d
    nb = L // Q_BLOCK

    def to_blocks(t):
        t = jnp.pad(t.astype(jnp.float32), ((0, 0), (0, s_pad - s), (0, 0), (0, 0)))
        t = t.reshape(b, L, d, h, hd).transpose(0, 2, 1, 3, 4)
        return t.reshape(b * d, nb, Q_BLOCK, h, hd)

    def with_prev(t):
        prev = jnp.pad(t[:, :-1], ((0, 0), (1, 0), (0, 0), (0, 0), (0, 0)))
        return jnp.concatenate([prev, t], axis=2)

    qb = to_blocks(q)
    kc, vc = with_prev(to_blocks(k)), with_prev(to_blocks(v))
    scores = jnp.einsum('znqhd,znkhd->znhqk', qb, kc) * (HEAD_DIM ** -0.5)
    delta = (jnp.arange(Q_BLOCK)[:, None] + Q_BLOCK) - jnp.arange(2 * Q_BLOCK)[None, :]
    key_idx = jnp.arange(nb)[:, None, None] * Q_BLOCK + jnp.arange(2 * Q_BLOCK)[None, None, :] - Q_BLOCK
    valid = (delta >= 0)[None] & (delta <= n_back)[None] & (key_idx >= 0)
    alibi = -slopes[:, None, None] * (d * delta).astype(jnp.float32)[None]
    scores = jnp.where(valid[None, :, None], scores + alibi[None, None], -jnp.inf)
    lse = jax.nn.logsumexp(scores, axis=-1)
    probs = jnp.exp(scores - lse[..., None])
    out = jnp.einsum('znhqk,znkhd->znqhd', probs, vc)
    out = out.reshape(b, d, L, h, hd).transpose(0, 2, 1, 3, 4).reshape(b, s_pad, h, hd)[:, :s]
    lse = lse.transpose(0, 1, 3, 2).reshape(b, d, L, h).transpose(0, 2, 1, 3).reshape(b, s_pad, h)[:, :s]
    return out, lse


def dilated_mixture_attention(qkv, slopes):
    b, s, _ = qkv.shape
    qkv = qkv.reshape(b, s, 3, ATTN_HEADS, HEAD_DIM)
    q, k, v = qkv[:, :, 0], qkv[:, :, 1], qkv[:, :, 2]
    outs, lses = [], []
    for g, (window, dilation) in enumerate(ATTN_GROUPS):
        sl = slice(g * HEADS_PER_GROUP, (g + 1) * HEADS_PER_GROUP)
        o, l = dilated_window_attention(q[:, :, sl], k[:, :, sl], v[:, :, sl], slopes[sl], window, dilation)
        outs.append(o)
        lses.append(l)
    outs = jnp.stack(outs, axis=0)
    alpha = jax.nn.softmax(jnp.stack(lses, axis=0), axis=0)
    return jnp.sum(alpha[..., None] * outs, axis=0).reshape(b, s, ATTN_OUT_DIM)


def hierarchical_moe(h, w_router_group, w_router_expert, w_exp_gate, w_exp_up, w_exp_down):
    b, s, dm = h.shape
    ht = h.reshape(b * s, dm)
    grp_probs = jax.nn.softmax((ht @ w_router_group).astype(jnp.float32), axis=-1)
    grp_w, grp_idx = lax.top_k(grp_probs, 1)
    exp_logits = jnp.einsum('td,gde->tge', ht, w_router_expert).astype(jnp.float32)
    sel = jnp.take_along_axis(exp_logits, grp_idx[:, :, None], axis=1)[:, 0]
    top_v, top_i = lax.top_k(sel, TOP_K_EXPERT)
    top_w = jax.nn.softmax(top_v, axis=-1) * grp_w
    within = jnp.sum(jax.nn.one_hot(top_i, EXPERTS_PER_GROUP, dtype=jnp.float32) * top_w[..., None], axis=1)
    combine = jax.nn.one_hot(grp_idx[:, 0], N_EXPERT_GROUPS, dtype=jnp.float32)[:, :, None] * within[:, None, :]
    y = jnp.zeros((b * s, dm), jnp.float32)
    for g in range(N_EXPERT_GROUPS):
        hg = jnp.einsum('td,edf->tef', ht, w_exp_gate[g])
        hu = jnp.einsum('td,edf->tef', ht, w_exp_up[g])
        act = jax.nn.silu(hg) * hu * combine[:, g, :, None]
        y = y + jnp.einsum('tef,efd->td', act, w_exp_down[g])
    return y.reshape(b, s, dm)


def setup_inputs(seed: int = 0) -> dict:
    key = jax.random.key(seed)
    ks = jax.random.split(key, 28)
    f32 = jnp.float32
    nrm = lambda k, shape, scale: jax.random.normal(k, shape, f32) * scale
    gain = lambda k, shape: 1.0 + 0.05 * jax.random.normal(k, shape, f32)
    E, G, F = EXPERTS_PER_GROUP, N_EXPERT_GROUPS, D_FF_EXPERT
    return {
        "x": nrm(ks[0], (BATCH, SEQ, D_MODEL), 1.0),
        "p": nrm(ks[1], (DEPTH, BATCH, SEQ, PLE_DIM), 1.0),
        "norm_mix": gain(ks[2], (DEPTH, D_MODEL)),
        "w_in": nrm(ks[3], (DEPTH, D_MODEL, IN_COLS), D_MODEL ** -0.5),
        "rwkv_mu": jax.random.uniform(ks[4], (DEPTH, RWKV_COLS), f32),
        "rwkv_w0": jax.random.uniform(ks[5], (DEPTH, RWKV_DIM), f32, -6.0, 1.0),
        "rwkv_w2": nrm(ks[6], (DEPTH, W_LORA, RWKV_DIM), 0.5 * W_LORA ** -0.5),
        "rwkv_a0": nrm(ks[7], (DEPTH, RWKV_DIM), 0.5),
        "rwkv_a2": nrm(ks[8], (DEPTH, A_LORA, RWKV_DIM), 0.5 * A_LORA ** -0.5),
        "rwkv_g2": nrm(ks[9], (DEPTH, G_LORA, RWKV_DIM), G_LORA ** -0.5),
        "rwkv_k_k": 0.85 + nrm(ks[10], (DEPTH, RWKV_DIM), 0.05),
        "rwkv_k_a": 1.0 + nrm(ks[11], (DEPTH, RWKV_DIM), 0.05),
        "rwkv_r_k": nrm(ks[12], (DEPTH, RWKV_HEADS, HEAD_DIM), 0.1),
        "rwkv_ln_w": gain(ks[13], (DEPTH, RWKV_DIM)),
        "rwkv_ln_b": nrm(ks[14], (DEPTH, RWKV_DIM), 0.02),
        "w_proj_a": nrm(ks[15], (DEPTH, RWKV_DIM, D_MODEL), RWKV_DIM ** -0.5),
        "w_proj_b": nrm(ks[16], (DEPTH, ATTN_OUT_DIM, D_MODEL), ATTN_OUT_DIM ** -0.5),
        "w_out": nrm(ks[17], (DEPTH, D_MODEL, D_MODEL), D_MODEL ** -0.5),
        "norm_moe": gain(ks[18], (DEPTH, D_MODEL)),
        "w_router_group": nrm(ks[19], (DEPTH, D_MODEL, G), D_MODEL ** -0.5),
        "w_router_expert": nrm(ks[20], (DEPTH, G, D_MODEL, E), D_MODEL ** -0.5),
        "w_exp_gate": nrm(ks[21], (DEPTH, G, E, D_MODEL, F), D_MODEL ** -0.5),
        "w_exp_up": nrm(ks[22], (DEPTH, G, E, D_MODEL, F), D_MODEL ** -0.5),
        "w_exp_down": nrm(ks[23], (DEPTH, G, E, F, D_MODEL), F ** -0.5),
        "norm_ple": gain(ks[24], (DEPTH, D_MODEL)),
        "w_ple_gate": nrm(ks[25], (DEPTH, D_MODEL, D_MODEL), D_MODEL ** -0.5),
        "w_ple_proj": nrm(ks[26], (DEPTH, PLE_DIM, D_MODEL), PLE_DIM ** -0.5),
        "norm_final": gain(ks[27], (D_MODEL,)),
    }


def reference(x, p, norm_mix, w_in, rwkv_mu, rwkv_w0, rwkv_w2, rwkv_a0, rwkv_a2, rwkv_g2,
              rwkv_k_k, rwkv_k_a, rwkv_r_k, rwkv_ln_w, rwkv_ln_b, w_proj_a, w_proj_b, w_out,
              norm_moe, w_router_group, w_router_expert, w_exp_gate, w_exp_up, w_exp_down,
              norm_ple, w_ple_gate, w_ple_proj, norm_final):
    slopes = alibi_slopes(ATTN_HEADS)
    b, s, _ = x.shape
    x = x.astype(jnp.float32)
    for i in range(DEPTH):
        h = rmsnorm(x, norm_mix[i])
        proj = h @ w_in[i]
        z_rwkv = token_shift(proj[..., :RWKV_COLS], rwkv_mu[i])
        qkv = proj[..., RWKV_COLS:RWKV_COLS + ATTN_COLS]
        gates = jax.nn.sigmoid(proj[..., RWKV_COLS + ATTN_COLS:].astype(jnp.float32))
        gates = gates.reshape(b, s, N_BRANCHES, D_MODEL)
        y_a = rwkv7_time_mix(z_rwkv, rwkv_w0[i], rwkv_w2[i], rwkv_a0[i], rwkv_a2[i], rwkv_g2[i],
                             rwkv_k_k[i], rwkv_k_a[i], rwkv_r_k[i], rwkv_ln_w[i], rwkv_ln_b[i])
        y_b = dilated_mixture_attention(qkv, slopes)
        merged = gates[:, :, 0] * (y_a @ w_proj_a[i]) + gates[:, :, 1] * (y_b @ w_proj_b[i])
        x = x + merged @ w_out[i]
        x = x + hierarchical_moe(rmsnorm(x, norm_moe[i]), w_router_group[i], w_router_expert[i],
                                 w_exp_gate[i], w_exp_up[i], w_exp_down[i])
        ple_gate = jax.nn.sigmoid(rmsnorm(x, norm_ple[i]) @ w_ple_gate[i])
        x = x + ple_gate * (p[i] @ w_ple_proj[i])
    return rmsnorm(x, norm_final)
```

```python
import functools

import numpy as np
import jax
import jax.numpy as jnp
from jax import lax
from jax.experimental import pallas as pl
from jax.experimental.pallas import tpu as pltpu

F32 = jnp.float32
BF16 = jnp.bfloat16
HIGHEST = lax.Precision.HIGHEST

D_MODEL = 1024
PLE_DIM = 256
HEAD_DIM = 64
RWKV_HEADS = 8
RWKV_DIM = RWKV_HEADS * HEAD_DIM
W_LORA, A_LORA, G_LORA = 64, 64, 128
LORA_COLS = W_LORA + A_LORA + G_LORA
ATTN_GROUPS = ((128, 1), (512, 4), (2048, 16))
HEADS_PER_GROUP = 8
ATTN_HEADS = HEADS_PER_GROUP * len(ATTN_GROUPS)
ATTN_DIM = ATTN_HEADS * HEAD_DIM
ATTN_OUT_DIM = HEADS_PER_GROUP * HEAD_DIM
Q_BLOCK = 128
RWKV_COLS = 3 * RWKV_DIM + LORA_COLS
ATTN_COLS = 3 * ATTN_DIM
N_EXPERT_GROUPS = 4
EXPERTS_PER_GROUP = 8
N_EXPERTS = N_EXPERT_GROUPS * EXPERTS_PER_GROUP
D_FF_EXPERT = 256
NORM_EPS = 1e-6
RWKV_GN_EPS = 64e-5
LANES = 128
MASKED = -1e30
WKV_CHUNK = 64
VMEM_LIMIT = 48 * 1024 * 1024


def _mm(a, b):
    return jnp.dot(a.astype(BF16), b.astype(BF16), preferred_element_type=F32)


def _mm_nt(a, b):
    return lax.dot_general(a.astype(BF16), b.astype(BF16), (((1,), (1,)), ((), ())),
                           preferred_element_type=F32)


def _mm_tn(a, b):
    return lax.dot_general(a.astype(BF16), b.astype(BF16), (((0,), (0,)), ((), ())),
                           preferred_element_type=F32)


def _sigmoid(x):
    return 1.0 / (1.0 + jnp.exp(-x))


def _rms(x, gain):
    return x * lax.rsqrt(jnp.mean(x * x, axis=-1, keepdims=True) + NORM_EPS) * gain


def _params(*sem):
    return pltpu.CompilerParams(dimension_semantics=sem, vmem_limit_bytes=VMEM_LIMIT)


def _norm_matmul_kernel(x_ref, g_ref, w_ref, o_ref, h_ref, *, sigmoid):
    @pl.when(pl.program_id(1) == 0)
    def _():
        h_ref[...] = _rms(x_ref[...], g_ref[...]).astype(BF16)

    acc = jnp.dot(h_ref[...], w_ref[...], preferred_element_type=F32)
    if sigmoid:
        acc = _sigmoid(acc)
    o_ref[...] = acc.astype(o_ref.dtype)


def _norm_matmul(x, gain, w, out_dtype, *, tm, tn, sigmoid=False):
    t, d = x.shape
    n = w.shape[1]
    return pl.pallas_call(
        functools.partial(_norm_matmul_kernel, sigmoid=sigmoid),
        grid=(t // tm, n // tn),
        in_specs=[pl.BlockSpec((tm, d), lambda i, j: (i, 0)),
                  pl.BlockSpec((1, d), lambda i, j: (0, 0)),
                  pl.BlockSpec((d, tn), lambda i, j: (0, j))],
        out_specs=pl.BlockSpec((tm, tn), lambda i, j: (i, j)),
        out_shape=jax.ShapeDtypeStruct((t, n), out_dtype),
        scratch_shapes=[pltpu.VMEM((tm, d), BF16)],
        compiler_params=_params("parallel", "arbitrary"),
        name="norm_matmul",
    )(x, gain, w)


def _seg_sum(x, seg_ones):
    hi = x.astype(BF16)
    lo = (x - hi.astype(F32)).astype(BF16)
    return (jnp.dot(hi, seg_ones, preferred_element_type=F32)
            + jnp.dot(lo, seg_ones, preferred_element_type=F32))


def _rwkv_prep_kernel(z_ref, zp_ref, mu_ref, w0_ref, w2_ref, a0_ref, a2_ref, g2_ref, kk_ref, ka_ref,
                      seg_ref, r_o, lw_o, k_o, v_o, kk_o, b_o, g_o, *, tiles_per_seq):
    i = pl.program_id(0)
    z = z_ref[...]
    last = zp_ref[...][7:8, :]
    last = jnp.where(i % tiles_per_seq == 0, 0.0, last)
    row = lax.broadcasted_iota(jnp.int32, z.shape, 0)
    prev = jnp.where(row == 0, last, pltpu.roll(z, 1, axis=0))
    zs = z + (prev - z) * mu_ref[...]
    d = RWKV_DIM
    r, k, v = zs[:, :d], zs[:, d:2 * d], zs[:, 2 * d:3 * d]
    lora = zs[:, 3 * d:]
    lane = lax.broadcasted_iota(jnp.int32, lora.shape, 1)
    lora = jnp.where(lane < W_LORA, jnp.tanh(lora),
                     jnp.where(lane < W_LORA + A_LORA, lora, _sigmoid(lora)))
    w_lin = w0_ref[...] + jnp.dot(lora, w2_ref[...], precision=HIGHEST, preferred_element_type=F32)
    a_lin = a0_ref[...] + jnp.dot(lora, a2_ref[...], precision=HIGHEST, preferred_element_type=F32)
    g = jnp.dot(lora, g2_ref[...], precision=HIGHEST, preferred_element_type=F32)
    u = -w_lin
    softplus = jnp.maximum(u, 0.0) + jnp.log(1.0 + jnp.exp(-jnp.abs(u)))
    lw = -jnp.exp(-softplus - 0.5)
    a = _sigmoid(a_lin)
    kk = k * kk_ref[...]
    kk = kk * lax.rsqrt(_seg_sum(kk * kk, seg_ref[...]) + 1e-12)
    k = k * (1.0 + (a - 1.0) * ka_ref[...])
    r_o[...] = r
    lw_o[...] = lw
    k_o[...] = k
    v_o[...] = v
    kk_o[...] = kk
    b_o[...] = kk * a
    g_o[...] = g


def _rwkv_prep(z, seq, mu, w0, w2p, a0, a2p, g2p, k_k, k_a, seg_ones, *, tm):
    t = z.shape[0]
    d = RWKV_DIM
    row = lambda n: pl.BlockSpec((1, n), lambda i: (0, 0))
    full = lambda a: pl.BlockSpec(a.shape, lambda i: (0, 0))
    out = jax.ShapeDtypeStruct((t, d), F32)
    return pl.pallas_call(
        functools.partial(_rwkv_prep_kernel, tiles_per_seq=seq // tm),
        grid=(t // tm,),
        in_specs=[pl.BlockSpec((tm, RWKV_COLS), lambda i: (i, 0)),
                  pl.BlockSpec((8, RWKV_COLS), lambda i: (jnp.maximum(i * (tm // 8) - 1, 0), 0)),
                  row(RWKV_COLS), row(d), full(w2p), row(d), full(a2p), full(g2p), row(d), row(d),
                  full(seg_ones)],
        out_specs=[pl.BlockSpec((tm, d), lambda i: (i, 0))] * 7,
        out_shape=[out] * 7,
        compiler_params=_params("parallel"),
        name="rwkv_prep",
    )(z, z, mu, w0, w2p, a0, a2p, g2p, k_k, k_a, seg_ones)


def _wkv_chunk(state, r, lw, k, v, kk, b, tri, strict, incl, eye):
    c = r.shape[0]
    cum = jnp.dot(tri, lw, precision=HIGHEST, preferred_element_type=F32)
    g_incl = jnp.exp(cum)
    g_excl = jnp.exp(cum - lw)
    g_inv = jnp.exp(-cum)
    total = cum[c - 1:c, :]
    g_tail = jnp.exp(total - cum)
    kq, rq = kk * g_excl, r * g_incl
    bi, ki = b * g_inv, k * g_inv
    a_b = jnp.where(strict, _mm_nt(kq, bi), 0.0)
    a_k = jnp.where(strict, _mm_nt(kq, ki), 0.0)
    a_rb = jnp.where(incl, _mm_nt(rq, bi), 0.0)
    a_rk = jnp.where(incl, _mm_nt(rq, ki), 0.0)
    inv = eye - a_b
    power = a_b
    steps = int(np.log2(c)) - 1
    for _ in range(steps):
        power = _mm(power, power)
        inv = inv + _mm(inv, power)
    w = _mm(inv, kq)
    u0 = -_mm(inv, _mm(a_k, v))
    u = u0 - _mm_nt(w, state)
    y = _mm_nt(rq, state) + _mm(a_rb, u) + _mm(a_rk, v)
    new_state = state * jnp.exp(total) + _mm_tn(u, b * g_tail) + _mm_tn(v, k * g_tail)
    return new_state, y


def _wkv_kernel(r_ref, lw_ref, k_ref, v_ref, kk_ref, b_ref, g_ref, lnw_ref, lnb_ref, rk_ref,
                o_ref, s_ref, *, chunk):
    @pl.when(pl.program_id(2) == 0)
    def _():
        s_ref[...] = jnp.zeros_like(s_ref)

    tb = r_ref.shape[1]
    ii = lax.broadcasted_iota(jnp.int32, (chunk, chunk), 0)
    jj = lax.broadcasted_iota(jnp.int32, (chunk, chunk), 1)
    incl, strict = ii >= jj, ii > jj
    tri = incl.astype(F32)
    eye = (ii == jj).astype(F32)
    for h in range(LANES // HEAD_DIM):
        hs = slice(h * HEAD_DIM, (h + 1) * HEAD_DIM)
        state = s_ref[h]
        for c in range(tb // chunk):
            rs = slice(c * chunk, (c + 1) * chunk)
            r, k, v = r_ref[0, rs, hs], k_ref[0, rs, hs], v_ref[0, rs, hs]
            state, y = _wkv_chunk(state, r, lw_ref[0, rs, hs], k, v, kk_ref[0, rs, hs],
                                  b_ref[0, rs, hs], tri, strict, incl, eye)
            mean = jnp.mean(y, axis=-1, keepdims=True)
            yc = y - mean
            var = jnp.mean(yc * yc, axis=-1, keepdims=True)
            y = yc * lax.rsqrt(var + RWKV_GN_EPS) * lnw_ref[:, hs] + lnb_ref[:, hs]
            y = y + jnp.sum(r * k * rk_ref[:, hs], axis=-1, keepdims=True) * v
            o_ref[0, rs, hs] = y * g_ref[0, rs, hs]
        s_ref[h] = state


def _wkv(r, lw, k, v, kk, b, g, ln_w, ln_b, r_k, *, tb):
    bsz, seq, d = r.shape
    tok = pl.BlockSpec((1, tb, LANES), lambda bi, hp, t: (bi, t, hp))
    par = pl.BlockSpec((1, LANES), lambda bi, hp, t: (0, hp))
    return pl.pallas_call(
        functools.partial(_wkv_kernel, chunk=WKV_CHUNK),
        grid=(bsz, d // LANES, seq // tb),
        in_specs=[tok] * 7 + [par] * 3,
        out_specs=tok,
        out_shape=jax.ShapeDtypeStruct((bsz, seq, d), F32),
        scratch_shapes=[pltpu.VMEM((LANES // HEAD_DIM, HEAD_DIM, HEAD_DIM), F32)],
        compiler_params=_params("parallel", "parallel", "arbitrary"),
        name="wkv_scan",
    )(r, lw, k, v, kk, b, g, ln_w, ln_b, r_k)


def _attn_kernel(q_ref, kp_ref, kc_ref, vp_ref, vc_ref, bias_ref, o_ref, l_ref):
    first = pl.program_id(2) == 0
    q = q_ref[0] * (HEAD_DIM ** -0.5)
    for h in range(HEADS_PER_GROUP):
        hs = slice(h * HEAD_DIM, (h + 1) * HEAD_DIM)
        qh = q[:, hs]
        bias = bias_ref[h]
        sp = _mm_nt(qh, kp_ref[0, :, hs]) + jnp.where(first, MASKED, bias[:, :Q_BLOCK])
        sc = _mm_nt(qh, kc_ref[0, :, hs]) + bias[:, Q_BLOCK:]
        m = jnp.maximum(jnp.max(sp, axis=-1, keepdims=True), jnp.max(sc, axis=-1, keepdims=True))
        pp, pc = jnp.exp(sp - m), jnp.exp(sc - m)
        den = jnp.sum(pp, axis=-1, keepdims=True) + jnp.sum(pc, axis=-1, keepdims=True)
        acc = _mm(pp, vp_ref[0, :, hs]) + _mm(pc, vc_ref[0, :, hs])
        o_ref[0, :, hs] = acc * (1.0 / den)
        l_ref[0, :, hs] = jnp.broadcast_to(m + jnp.log(den), (Q_BLOCK, HEAD_DIM))


def _attn_bias(group):
    window, dilation = ATTN_GROUPS[group]
    n_back = window // dilation
    heads = np.arange(group * HEADS_PER_GROUP, (group + 1) * HEADS_PER_GROUP)
    slopes = (2.0 ** (-8.0 * (heads + 1) / ATTN_HEADS)).astype(np.float32)
    delta = (np.arange(Q_BLOCK)[:, None] + Q_BLOCK) - np.arange(2 * Q_BLOCK)[None, :]
    valid = (delta >= 0) & (delta <= n_back)
    alibi = -slopes[:, None, None] * (dilation * delta).astype(np.float32)[None]
    return jnp.asarray(np.where(valid[None], alibi, np.float32(MASKED)).astype(np.float32))


def _attn_group(qkv, group):
    bsz, seq, cols = qkv.shape
    dil = ATTN_GROUPS[group][1]
    comp = seq // dil
    assert comp % Q_BLOCK == 0
    view = qkv.reshape(bsz, comp, dil * cols)
    per_row = cols // ATTN_OUT_DIM
    groups = len(ATTN_GROUPS)

    def col(which):
        return lambda bi, r, n: (bi, n, r * per_row + which * groups + group)

    def col_prev(which):
        return lambda bi, r, n: (bi, jnp.maximum(n - 1, 0), r * per_row + which * groups + group)

    blk = (1, Q_BLOCK, ATTN_OUT_DIM)
    bias = _attn_bias(group)
    out_spec = pl.BlockSpec(blk, lambda bi, r, n: (bi, n, r))
    out = jax.ShapeDtypeStruct((bsz, comp, dil * ATTN_OUT_DIM), F32)
    o, lse = pl.pallas_call(
        _attn_kernel,
        grid=(bsz, dil, comp // Q_BLOCK),
        in_specs=[pl.BlockSpec(blk, col(0)),
                  pl.BlockSpec(blk, col_prev(1)), pl.BlockSpec(blk, col(1)),
                  pl.BlockSpec(blk, col_prev(2)), pl.BlockSpec(blk, col(2)),
                  pl.BlockSpec(bias.shape, lambda bi, r, n: (0, 0, 0))],
        out_specs=[out_spec, out_spec],
        out_shape=[out, out],
        compiler_params=_params("parallel", "parallel", "parallel"),
        name=f"dilated_attn_{group}",
    )(view, view, view, view, view, bias)
    return o.reshape(bsz * seq, ATTN_OUT_DIM), lse.reshape(bsz * seq, ATTN_OUT_DIM)


def _route(logits):
    lane = lax.broadcasted_iota(jnp.int32, logits.shape, 1)
    neg = -jnp.inf
    first_max = lambda vals, mx: jnp.min(jnp.where(vals == mx, lane, LANES), axis=-1, keepdims=True)
    is_grp = (lane >= N_EXPERTS) & (lane < N_EXPERTS + N_EXPERT_GROUPS)
    gl = jnp.where(is_grp, logits, neg)
    gmax = jnp.max(gl, axis=-1, keepdims=True)
    grp_w = 1.0 / jnp.sum(jnp.exp(gl - gmax), axis=-1, keepdims=True)
    grp = first_max(gl, gmax) - N_EXPERTS
    in_grp = (lane >= grp * EXPERTS_PER_GROUP) & (lane < (grp + 1) * EXPERTS_PER_GROUP)
    el = jnp.where(in_grp, logits, neg)
    v1 = jnp.max(el, axis=-1, keepdims=True)
    i1 = first_max(el, v1)
    el2 = jnp.where(lane == i1, neg, el)
    v2 = jnp.max(el2, axis=-1, keepdims=True)
    i2 = first_max(el2, v2)
    e2 = jnp.exp(v2 - v1)
    w1 = grp_w / (1.0 + e2)
    return jnp.where(lane == i1, w1, 0.0) + jnp.where(lane == i2, w1 * e2, 0.0)


def _merge_kernel(x_ref, ya_ref, o0_ref, o1_ref, o2_ref, l0_ref, l1_ref, l2_ref, gt_ref,
                  wa_ref, wb_ref, wo_ref, gain_ref, wr_ref, x1_ref, h_ref, comb_ref):
    l0, l1, l2 = l0_ref[...], l1_ref[...], l2_ref[...]
    m = jnp.maximum(jnp.maximum(l0, l1), l2)
    e0, e1, e2 = jnp.exp(l0 - m), jnp.exp(l1 - m), jnp.exp(l2 - m)
    yb = (e0 * o0_ref[...] + e1 * o1_ref[...] + e2 * o2_ref[...]) / (e0 + e1 + e2)
    gates = gt_ref[...]
    merged = (gates[:, :D_MODEL] * _mm(ya_ref[...], wa_ref[...])
              + gates[:, D_MODEL:] * _mm(yb, wb_ref[...]))
    x1 = x_ref[...] + _mm(merged, wo_ref[...])
    x1_ref[...] = x1
    h = _rms(x1, gain_ref[...])
    h_ref[...] = h.astype(BF16)
    logits = jnp.dot(h, wr_ref[...], precision=HIGHEST, preferred_element_type=F32)
    comb_ref[...] = _route(logits)


def _merge(x, ya, outs, lses, gates, wa, wb, wo, gain, wr, *, tm):
    t = x.shape[0]
    tok = lambda n: pl.BlockSpec((tm, n), lambda i: (i, 0))
    full = lambda a: pl.BlockSpec(a.shape, lambda i: (0, 0))
    return pl.pallas_call(
        _merge_kernel,
        grid=(t // tm,),
        in_specs=[tok(D_MODEL), tok(RWKV_DIM)] + [tok(ATTN_OUT_DIM)] * 6 + [tok(2 * D_MODEL),
                  full(wa), full(wb), full(wo), full(gain), full(wr)],
        out_specs=[tok(D_MODEL), tok(D_MODEL), tok(LANES)],
        out_shape=[jax.ShapeDtypeStruct((t, D_MODEL), F32), jax.ShapeDtypeStruct((t, D_MODEL), BF16),
                   jax.ShapeDtypeStruct((t, LANES), F32)],
        compiler_params=_params("parallel"),
        name="merge_route",
    )(x, ya, *outs, *lses, gates, wa, wb, wo, gain, wr)


def _moe_kernel(h_ref, comb_ref, wg_ref, wu_ref, wd_ref, o_ref):
    e = pl.program_id(1)

    @pl.when(e == 0)
    def _():
        o_ref[...] = jnp.zeros_like(o_ref)

    h = h_ref[...]
    comb = comb_ref[...]
    lane = lax.broadcasted_iota(jnp.int32, comb.shape, 1)
    weight = jnp.sum(jnp.where(lane == e, comb, 0.0), axis=-1, keepdims=True)
    gate = jnp.dot(h, wg_ref[0], preferred_element_type=F32)
    up = jnp.dot(h, wu_ref[0], preferred_element_type=F32)
    act = gate * _sigmoid(gate) * up * weight
    o_ref[...] += _mm(act, wd_ref[0])


def _moe(h, comb, wg, wu, wd, *, tm):
    t = h.shape[0]
    return pl.pallas_call(
        _moe_kernel,
        grid=(t // tm, N_EXPERTS),
        in_specs=[pl.BlockSpec((tm, D_MODEL), lambda i, e: (i, 0)),
                  pl.BlockSpec((tm, LANES), lambda i, e: (i, 0)),
                  pl.BlockSpec((1, D_MODEL, D_FF_EXPERT), lambda i, e: (e, 0, 0)),
                  pl.BlockSpec((1, D_MODEL, D_FF_EXPERT), lambda i, e: (e, 0, 0)),
                  pl.BlockSpec((1, D_FF_EXPERT, D_MODEL), lambda i, e: (e, 0, 0))],
        out_specs=pl.BlockSpec((tm, D_MODEL), lambda i, e: (i, 0)),
        out_shape=jax.ShapeDtypeStruct((t, D_MODEL), F32),
        compiler_params=_params("parallel", "arbitrary"),
        name="moe_dense",
    )(h, comb, wg, wu, wd)


def _ple_kernel(x_ref, y_ref, p_ref, gp_ref, wg_ref, wp_ref, gf_ref, o_ref):
    x = x_ref[...] + y_ref[...]
    gate = _sigmoid(_mm(_rms(x, gp_ref[...]), wg_ref[...]))
    x = x + gate * _mm(p_ref[...], wp_ref[...])
    o_ref[...] = _rms(x, gf_ref[...])


def _ple(x, y, p, gain_ple, w_gate, w_proj, gain_final, *, tm):
    t = x.shape[0]
    tok = lambda n: pl.BlockSpec((tm, n), lambda i: (i, 0))
    full = lambda a: pl.BlockSpec(a.shape, lambda i: (0, 0))
    return pl.pallas_call(
        _ple_kernel,
        grid=(t // tm,),
        in_specs=[tok(D_MODEL), tok(D_MODEL), tok(PLE_DIM), full(gain_ple), full(w_gate), full(w_proj),
                  full(gain_final)],
        out_specs=tok(D_MODEL),
        out_shape=jax.ShapeDtypeStruct((t, D_MODEL), F32),
        compiler_params=_params("parallel"),
        name="ple_final",
    )(x, y, p, gain_ple, w_gate, w_proj, gain_final)


def _pad_rows(w, start, total):
    return jnp.zeros((total, w.shape[1]), w.dtype).at[start:start + w.shape[0]].set(w)


def _layer(x, p, seq, norm_mix, w_in, rwkv_mu, rwkv_w0, rwkv_w2, rwkv_a0, rwkv_a2, rwkv_g2, rwkv_k_k,
           rwkv_k_a, rwkv_r_k, rwkv_ln_w, rwkv_ln_b, w_proj_a, w_proj_b, w_out, norm_moe, w_router_group,
           w_router_expert, w_exp_gate, w_exp_up, w_exp_down, norm_ple, w_ple_gate, w_ple_proj, norm_final):
    t = x.shape[0]
    bsz = t // seq
    row = lambda a: a.reshape(1, -1)
    w_in16 = w_in.astype(BF16)
    c0, c1 = RWKV_COLS, RWKV_COLS + ATTN_COLS
    z = _norm_matmul(x, row(norm_mix), w_in16[:, :c0], F32, tm=512, tn=RWKV_COLS // 2)
    qkv = _norm_matmul(x, row(norm_mix), w_in16[:, c0:c1], BF16, tm=512, tn=ATTN_COLS // 3)
    gates = _norm_matmul(x, row(norm_mix), w_in16[:, c1:], F32, tm=512, tn=D_MODEL, sigmoid=True)

    seg = np.arange(RWKV_DIM) // HEAD_DIM
    seg_ones = jnp.asarray(seg[:, None] == seg[None, :], dtype=BF16)
    r, lw, k, v, kk, b, g = _rwkv_prep(
        z, seq, row(rwkv_mu), row(rwkv_w0), _pad_rows(rwkv_w2, 0, LORA_COLS), row(rwkv_a0),
        _pad_rows(rwkv_a2, W_LORA, LORA_COLS), _pad_rows(rwkv_g2, W_LORA + A_LORA, LORA_COLS),
        row(rwkv_k_k), row(rwkv_k_a), seg_ones, tm=512)
    tok3 = lambda a: a.reshape(bsz, seq, RWKV_DIM)
    ya = _wkv(tok3(r), tok3(lw), tok3(k), tok3(v), tok3(kk), tok3(b), tok3(g), row(rwkv_ln_w),
              row(rwkv_ln_b), row(rwkv_r_k), tb=256).reshape(t, RWKV_DIM)

    qkv3 = qkv.reshape(bsz, seq, ATTN_COLS)
    attn = [_attn_group(qkv3, grp) for grp in range(len(ATTN_GROUPS))]

    w_route = jnp.concatenate(
        [jnp.moveaxis(w_router_expert, 0, 1).reshape(D_MODEL, N_EXPERTS), w_router_group,
         jnp.zeros((D_MODEL, LANES - N_EXPERTS - N_EXPERT_GROUPS), F32)], axis=1)
    x1, h_moe, comb = _merge(x, ya, [o for o, _ in attn], [l for _, l in attn], gates,
                             w_proj_a.astype(BF16), w_proj_b.astype(BF16), w_out.astype(BF16),
                             row(norm_moe), w_route, tm=512)
    y_moe = _moe(h_moe, comb,
                 w_exp_gate.reshape(N_EXPERTS, D_MODEL, D_FF_EXPERT).astype(BF16),
                 w_exp_up.reshape(N_EXPERTS, D_MODEL, D_FF_EXPERT).astype(BF16),
                 w_exp_down.reshape(N_EXPERTS, D_FF_EXPERT, D_MODEL).astype(BF16), tm=1024)
    return _ple(x1, y_moe, p, row(norm_ple), w_ple_gate.astype(BF16), w_ple_proj.astype(BF16),
                row(norm_final), tm=512)


def kernel(x, p, norm_mix, w_in, rwkv_mu, rwkv_w0, rwkv_w2, rwkv_a0, rwkv_a2, rwkv_g2, rwkv_k_k, rwkv_k_a, rwkv_r_k, rwkv_ln_w, rwkv_ln_b, w_proj_a, w_proj_b, w_out, norm_moe, w_router_group, w_router_expert, w_exp_gate, w_exp_up, w_exp_down, norm_ple, w_ple_gate, w_ple_proj, norm_final):
    bsz, seq, d = x.shape
    depth = w_in.shape[0]
    assert depth == 1, "the final norm is fused into the (single) layer"
    out = _layer(x.reshape(bsz * seq, d).astype(F32), p[0].reshape(bsz * seq, PLE_DIM), seq,
                 norm_mix[0], w_in[0], rwkv_mu[0], rwkv_w0[0], rwkv_w2[0], rwkv_a0[0], rwkv_a2[0],
                 rwkv_g2[0], rwkv_k_k[0], rwkv_k_a[0], rwkv_r_k[0], rwkv_ln_w[0], rwkv_ln_b[0],
                 w_proj_a[0], w_proj_b[0], w_out[0], norm_moe[0], w_router_group[0], w_router_expert[0],
                 w_exp_gate[0], w_exp_up[0], w_exp_down[0], norm_ple[0], w_ple_gate[0], w_ple_proj[0],
                 norm_final)
    return out.reshape(bsz, seq, d)
```

```python
import functools

import numpy as np
import jax
import jax.numpy as jnp
from jax import lax
from jax.experimental import pallas as pl
from jax.experimental.pallas import tpu as pltpu

F32 = jnp.float32
BF16 = jnp.bfloat16
HIGHEST = lax.Precision.HIGHEST

D_MODEL = 1024
PLE_DIM = 256
HEAD_DIM = 64
RWKV_HEADS = 8
RWKV_DIM = RWKV_HEADS * HEAD_DIM
W_LORA, A_LORA, G_LORA = 64, 64, 128
LORA_COLS = W_LORA + A_LORA + G_LORA
ATTN_GROUPS = ((128, 1), (512, 4), (2048, 16))
HEADS_PER_GROUP = 8
ATTN_HEADS = HEADS_PER_GROUP * len(ATTN_GROUPS)
ATTN_DIM = ATTN_HEADS * HEAD_DIM
ATTN_OUT_DIM = HEADS_PER_GROUP * HEAD_DIM
Q_BLOCK = 128
RWKV_COLS = 3 * RWKV_DIM + LORA_COLS
ATTN_COLS = 3 * ATTN_DIM
N_EXPERT_GROUPS = 4
EXPERTS_PER_GROUP = 8
N_EXPERTS = N_EXPERT_GROUPS * EXPERTS_PER_GROUP
D_FF_EXPERT = 256
NORM_EPS = 1e-6
RWKV_GN_EPS = 64e-5
LANES = 128
MASKED = -1e30
WKV_CHUNK = 64
VMEM_LIMIT = 48 * 1024 * 1024


def _mm(a, b):
    return jnp.dot(a.astype(BF16), b.astype(BF16), preferred_element_type=F32)


def _mm_nt(a, b):
    return lax.dot_general(a.astype(BF16), b.astype(BF16), (((1,), (1,)), ((), ())),
                           preferred_element_type=F32)


def _mm_tn(a, b):
    return lax.dot_general(a.astype(BF16), b.astype(BF16), (((0,), (0,)), ((), ())),
                           preferred_element_type=F32)


def _sigmoid(x):
    return 1.0 / (1.0 + jnp.exp(-x))


def _rms(x, gain):
    return x * lax.rsqrt(jnp.mean(x * x, axis=-1, keepdims=True) + NORM_EPS) * gain


def _params(*sem):
    return pltpu.CompilerParams(dimension_semantics=sem, vmem_limit_bytes=VMEM_LIMIT)


def _norm_matmul_kernel(x_ref, g_ref, w_ref, o_ref, h_ref, *, sigmoid):
    @pl.when(pl.program_id(1) == 0)
    def _():
        h_ref[...] = _rms(x_ref[...], g_ref[...]).astype(BF16)

    acc = jnp.dot(h_ref[...], w_ref[...], preferred_element_type=F32)
    if sigmoid:
        acc = _sigmoid(acc)
    o_ref[...] = acc.astype(o_ref.dtype)


def _norm_matmul(x, gain, w, out_dtype, *, tm, tn, sigmoid=False):
    t, d = x.shape
    n = w.shape[1]
    return pl.pallas_call(
        functools.partial(_norm_matmul_kernel, sigmoid=sigmoid),
        grid=(t // tm, n // tn),
        in_specs=[pl.BlockSpec((tm, d), lambda i, j: (i, 0)),
                  pl.BlockSpec((1, d), lambda i, j: (0, 0)),
                  pl.BlockSpec((d, tn), lambda i, j: (0, j))],
        out_specs=pl.BlockSpec((tm, tn), lambda i, j: (i, j)),
        out_shape=jax.ShapeDtypeStruct((t, n), out_dtype),
        scratch_shapes=[pltpu.VMEM((tm, d), BF16)],
        compiler_params=_params("parallel", "arbitrary"),
        name="norm_matmul",
    )(x, gain, w)


def _seg_sum(x, seg_ones):
    hi = x.astype(BF16)
    lo = (x - hi.astype(F32)).astype(BF16)
    return (jnp.dot(hi, seg_ones, preferred_element_type=F32)
            + jnp.dot(lo, seg_ones, preferred_element_type=F32))


def _rwkv_prep_kernel(z_ref, zp_ref, mu_ref, w0_ref, w2_ref, a0_ref, a2_ref, g2_ref, kk_ref, ka_ref,
                      seg_ref, r_o, lw_o, k_o, v_o, kk_o, b_o, g_o, *, tiles_per_seq):
    i = pl.program_id(0)
    z = z_ref[...]
    last = zp_ref[...][7:8, :]
    last = jnp.where(i % tiles_per_seq == 0, 0.0, last)
    row = lax.broadcasted_iota(jnp.int32, z.shape, 0)
    prev = jnp.where(row == 0, last, pltpu.roll(z, 1, axis=0))
    zs = z + (prev - z) * mu_ref[...]
    d = RWKV_DIM
    r, k, v = zs[:, :d], zs[:, d:2 * d], zs[:, 2 * d:3 * d]
    lora = zs[:, 3 * d:]
    lane = lax.broadcasted_iota(jnp.int32, lora.shape, 1)
    lora = jnp.where(lane < W_LORA, jnp.tanh(lora),
                     jnp.where(lane < W_LORA + A_LORA, lora, _sigmoid(lora)))
    w_lin = w0_ref[...] + jnp.dot(lora, w2_ref[...], precision=HIGHEST, preferred_element_type=F32)
    a_lin = a0_ref[...] + jnp.dot(lora, a2_ref[...], precision=HIGHEST, preferred_element_type=F32)
    g = jnp.dot(lora, g2_ref[...], precision=HIGHEST, preferred_element_type=F32)
    u = -w_lin
    softplus = jnp.maximum(u, 0.0) + jnp.log(1.0 + jnp.exp(-jnp.abs(u)))
    lw = -jnp.exp(-softplus - 0.5)
    a = _sigmoid(a_lin)
    kk = k * kk_ref[...]
    kk = kk * lax.rsqrt(_seg_sum(kk * kk, seg_ref[...]) + 1e-12)
    k = k * (1.0 + (a - 1.0) * ka_ref[...])
    r_o[...] = r
    lw_o[...] = lw
    k_o[...] = k
    v_o[...] = v
    kk_o[...] = kk
    b_o[...] = kk * a
    g_o[...] = g


def _rwkv_prep(z, seq, mu, w0, w2p, a0, a2p, g2p, k_k, k_a, seg_ones, *, tm):
    t = z.shape[0]
    d = RWKV_DIM
    row = lambda n: pl.BlockSpec((1, n), lambda i: (0, 0))
    full = lambda a: pl.BlockSpec(a.shape, lambda i: (0, 0))
    out = jax.ShapeDtypeStruct((t, d), F32)
    return pl.pallas_call(
        functools.partial(_rwkv_prep_kernel, tiles_per_seq=seq // tm),
        grid=(t // tm,),
        in_specs=[pl.BlockSpec((tm, RWKV_COLS), lambda i: (i, 0)),
                  pl.BlockSpec((8, RWKV_COLS), lambda i: (jnp.maximum(i * (tm // 8) - 1, 0), 0)),
                  row(RWKV_COLS), row(d), full(w2p), row(d), full(a2p), full(g2p), row(d), row(d),
                  full(seg_ones)],
        out_specs=[pl.BlockSpec((tm, d), lambda i: (i, 0))] * 7,
        out_shape=[out] * 7,
        compiler_params=_params("parallel"),
        name="rwkv_prep",
    )(z, z, mu, w0, w2p, a0, a2p, g2p, k_k, k_a, seg_ones)


def _bmm(a, b):
    return lax.dot_general(a.astype(BF16), b.astype(BF16), (((2,), (1,)), ((0,), (0,))),
                           preferred_element_type=F32)


def _bmm_nt(a, b):
    return lax.dot_general(a.astype(BF16), b.astype(BF16), (((2,), (2,)), ((0,), (0,))),
                           preferred_element_type=F32)


def _bmm_tn(a, b):
    return lax.dot_general(a.astype(BF16), b.astype(BF16), (((1,), (1,)), ((0,), (0,))),
                           preferred_element_type=F32)


def _wkv_kernel(r_ref, lw_ref, k_ref, v_ref, kk_ref, b_ref, g_ref, lnw_ref, lnb_ref, rk_ref, seg_ref,
                o_ref, s_ref, *, chunk):
    @pl.when(pl.program_id(1) == 0)
    def _():
        s_ref[...] = jnp.zeros_like(s_ref)

    tb, d = r_ref.shape[1:]
    nc, nh = tb // chunk, d // HEAD_DIM
    ii = lax.broadcasted_iota(jnp.int32, (chunk, chunk), 0)
    jj = lax.broadcasted_iota(jnp.int32, (chunk, chunk), 1)
    incl, strict = ii >= jj, ii > jj
    tri = incl.astype(F32)
    eye = (ii == jj).astype(F32)
    rows = lambda c: slice(c * chunk, (c + 1) * chunk)
    lanes = lambda h: slice(h * HEAD_DIM, (h + 1) * HEAD_DIM)

    def split(x):
        return jnp.stack([x[rows(c), lanes(h)] for c in range(nc) for h in range(nh)])

    r, lw, k, v, kk, b = r_ref[0], lw_ref[0], k_ref[0], v_ref[0], kk_ref[0], b_ref[0]
    cums = [jnp.dot(tri, lw[rows(c)], precision=HIGHEST, preferred_element_type=F32) for c in range(nc)]
    cum = jnp.concatenate(cums, axis=0)
    total = jnp.concatenate([jnp.broadcast_to(cs[chunk - 1:], (chunk, d)) for cs in cums], axis=0)
    g_inv = jnp.exp(-cum)
    g_tail = jnp.exp(total - cum)
    kq, rq = split(kk * jnp.exp(cum - lw)), split(r * jnp.exp(cum))
    bi, ki = split(b * g_inv), split(k * g_inv)
    bh, kh = split(b * g_tail), split(k * g_tail)
    vs = split(v)
    decay = jnp.stack([jnp.exp(cs[chunk - 1:, lanes(h)]) for cs in cums for h in range(nh)])

    a_b = jnp.where(strict, _bmm_nt(kq, bi), 0.0)
    a_k = jnp.where(strict, _bmm_nt(kq, ki), 0.0)
    a_rb = jnp.where(incl, _bmm_nt(rq, bi), 0.0)
    a_rk = jnp.where(incl, _bmm_nt(rq, ki), 0.0)
    inv = eye - a_b
    power = a_b
    for _ in range(int(np.log2(chunk)) - 1):
        power = _bmm(power, power)
        inv = inv + _bmm(inv, power)
    w = _bmm(inv, kq)
    u0 = -_bmm(inv, _bmm(a_k, vs))
    qh = rq - _bmm(a_rb, w)
    y0 = _bmm(a_rb, u0) + _bmm(a_rk, vs)
    mp = _bmm_tn(w, bh)
    s_add = _bmm_tn(u0, bh) + _bmm_tn(vs, kh)

    state = s_ref[...]
    for c in range(nc):
        sl = slice(c * nh, (c + 1) * nh)
        y = _bmm_nt(qh[sl], state) + y0[sl]
        for h in range(nh):
            o_ref[0, rows(c), lanes(h)] = y[h]
        state = state * decay[sl] - _bmm(state, mp[sl]) + s_add[sl]
    s_ref[...] = state

    seg = seg_ref[...]
    y = o_ref[0]
    yc = y - _seg_sum(y, seg) * (1.0 / HEAD_DIM)
    var = _seg_sum(yc * yc, seg) * (1.0 / HEAD_DIM)
    y = yc * lax.rsqrt(var + RWKV_GN_EPS) * lnw_ref[...] + lnb_ref[...]
    y = y + _seg_sum(r * k * rk_ref[...], seg) * v
    o_ref[0] = y * g_ref[0]


def _wkv(r, lw, k, v, kk, b, g, ln_w, ln_b, r_k, seg_ones, *, tb):
    bsz, seq, d = r.shape
    tok = pl.BlockSpec((1, tb, d), lambda bi, t: (bi, t, 0))
    par = pl.BlockSpec((1, d), lambda bi, t: (0, 0))
    return pl.pallas_call(
        functools.partial(_wkv_kernel, chunk=WKV_CHUNK),
        grid=(bsz, seq // tb),
        in_specs=[tok] * 7 + [par] * 3 + [pl.BlockSpec(seg_ones.shape, lambda bi, t: (0, 0))],
        out_specs=tok,
        out_shape=jax.ShapeDtypeStruct((bsz, seq, d), F32),
        scratch_shapes=[pltpu.VMEM((d // HEAD_DIM, HEAD_DIM, HEAD_DIM), F32)],
        compiler_params=_params("parallel", "arbitrary"),
        name="wkv_scan",
    )(r, lw, k, v, kk, b, g, ln_w, ln_b, r_k, seg_ones)


def _attn_kernel(q_ref, kp_ref, kc_ref, vp_ref, vc_ref, bias_ref, o_ref, l_ref):
    first = pl.program_id(2) == 0
    q = q_ref[0] * (HEAD_DIM ** -0.5)
    for h in range(HEADS_PER_GROUP):
        hs = slice(h * HEAD_DIM, (h + 1) * HEAD_DIM)
        qh = q[:, hs]
        bias = bias_ref[h]
        sp = _mm_nt(qh, kp_ref[0, :, hs]) + jnp.where(first, MASKED, bias[:, :Q_BLOCK])
        sc = _mm_nt(qh, kc_ref[0, :, hs]) + bias[:, Q_BLOCK:]
        m = jnp.maximum(jnp.max(sp, axis=-1, keepdims=True), jnp.max(sc, axis=-1, keepdims=True))
        pp, pc = jnp.exp(sp - m), jnp.exp(sc - m)
        den = jnp.sum(pp, axis=-1, keepdims=True) + jnp.sum(pc, axis=-1, keepdims=True)
        acc = _mm(pp, vp_ref[0, :, hs]) + _mm(pc, vc_ref[0, :, hs])
        o_ref[0, :, hs] = acc * (1.0 / den)
        l_ref[0, :, hs] = jnp.broadcast_to(m + jnp.log(den), (Q_BLOCK, HEAD_DIM))


def _attn_bias(group):
    window, dilation = ATTN_GROUPS[group]
    n_back = window // dilation
    heads = np.arange(group * HEADS_PER_GROUP, (group + 1) * HEADS_PER_GROUP)
    slopes = (2.0 ** (-8.0 * (heads + 1) / ATTN_HEADS)).astype(np.float32)
    delta = (np.arange(Q_BLOCK)[:, None] + Q_BLOCK) - np.arange(2 * Q_BLOCK)[None, :]
    valid = (delta >= 0) & (delta <= n_back)
    alibi = -slopes[:, None, None] * (dilation * delta).astype(np.float32)[None]
    return jnp.asarray(np.where(valid[None], alibi, np.float32(MASKED)).astype(np.float32))


def _attn_group(qkv, group):
    bsz, seq, cols = qkv.shape
    dil = ATTN_GROUPS[group][1]
    comp = seq // dil
    assert comp % Q_BLOCK == 0
    view = qkv.reshape(bsz, comp, dil * cols)
    per_row = cols // ATTN_OUT_DIM
    groups = len(ATTN_GROUPS)

    def col(which):
        return lambda bi, r, n: (bi, n, r * per_row + which * groups + group)

    def col_prev(which):
        return lambda bi, r, n: (bi, jnp.maximum(n - 1, 0), r * per_row + which * groups + group)

    blk = (1, Q_BLOCK, ATTN_OUT_DIM)
    bias = _attn_bias(group)
    out_spec = pl.BlockSpec(blk, lambda bi, r, n: (bi, n, r))
    out = jax.ShapeDtypeStruct((bsz, comp, dil * ATTN_OUT_DIM), F32)
    o, lse = pl.pallas_call(
        _attn_kernel,
        grid=(bsz, dil, comp // Q_BLOCK),
        in_specs=[pl.BlockSpec(blk, col(0)),
                  pl.BlockSpec(blk, col_prev(1)), pl.BlockSpec(blk, col(1)),
                  pl.BlockSpec(blk, col_prev(2)), pl.BlockSpec(blk, col(2)),
                  pl.BlockSpec(bias.shape, lambda bi, r, n: (0, 0, 0))],
        out_specs=[out_spec, out_spec],
        out_shape=[out, out],
        compiler_params=_params("parallel", "parallel", "parallel"),
        name=f"dilated_attn_{group}",
    )(view, view, view, view, view, bias)
    return o.reshape(bsz * seq, ATTN_OUT_DIM), lse.reshape(bsz * seq, ATTN_OUT_DIM)


def _route(logits):
    lane = lax.broadcasted_iota(jnp.int32, logits.shape, 1)
    neg = -jnp.inf
    first_max = lambda vals, mx: jnp.min(jnp.where(vals == mx, lane, LANES), axis=-1, keepdims=True)
    is_grp = (lane >= N_EXPERTS) & (lane < N_EXPERTS + N_EXPERT_GROUPS)
    gl = jnp.where(is_grp, logits, neg)
    gmax = jnp.max(gl, axis=-1, keepdims=True)
    grp_w = 1.0 / jnp.sum(jnp.exp(gl - gmax), axis=-1, keepdims=True)
    grp = first_max(gl, gmax) - N_EXPERTS
    in_grp = (lane >= grp * EXPERTS_PER_GROUP) & (lane < (grp + 1) * EXPERTS_PER_GROUP)
    el = jnp.where(in_grp, logits, neg)
    v1 = jnp.max(el, axis=-1, keepdims=True)
    i1 = first_max(el, v1)
    el2 = jnp.where(lane == i1, neg, el)
    v2 = jnp.max(el2, axis=-1, keepdims=True)
    i2 = first_max(el2, v2)
    e2 = jnp.exp(v2 - v1)
    w1 = grp_w / (1.0 + e2)
    return jnp.where(lane == i1, w1, 0.0) + jnp.where(lane == i2, w1 * e2, 0.0)


def _merge_kernel(x_ref, ya_ref, o0_ref, o1_ref, o2_ref, l0_ref, l1_ref, l2_ref, gt_ref,
                  wa_ref, wb_ref, wo_ref, gain_ref, wr_ref, x1_ref, h_ref, comb_ref):
    l0, l1, l2 = l0_ref[...], l1_ref[...], l2_ref[...]
    m = jnp.maximum(jnp.maximum(l0, l1), l2)
    e0, e1, e2 = jnp.exp(l0 - m), jnp.exp(l1 - m), jnp.exp(l2 - m)
    yb = (e0 * o0_ref[...] + e1 * o1_ref[...] + e2 * o2_ref[...]) / (e0 + e1 + e2)
    gates = gt_ref[...]
    merged = (gates[:, :D_MODEL] * _mm(ya_ref[...], wa_ref[...])
              + gates[:, D_MODEL:] * _mm(yb, wb_ref[...]))
    x1 = x_ref[...] + _mm(merged, wo_ref[...])
    x1_ref[...] = x1
    h = _rms(x1, gain_ref[...])
    h_ref[...] = h.astype(BF16)
    logits = jnp.dot(h, wr_ref[...], precision=HIGHEST, preferred_element_type=F32)
    comb_ref[...] = _route(logits)


def _merge(x, ya, outs, lses, gates, wa, wb, wo, gain, wr, *, tm):
    t = x.shape[0]
    tok = lambda n: pl.BlockSpec((tm, n), lambda i: (i, 0))
    full = lambda a: pl.BlockSpec(a.shape, lambda i: (0, 0))
    return pl.pallas_call(
        _merge_kernel,
        grid=(t // tm,),
        in_specs=[tok(D_MODEL), tok(RWKV_DIM)] + [tok(ATTN_OUT_DIM)] * 6 + [tok(2 * D_MODEL),
                  full(wa), full(wb), full(wo), full(gain), full(wr)],
        out_specs=[tok(D_MODEL), tok(D_MODEL), tok(LANES)],
        out_shape=[jax.ShapeDtypeStruct((t, D_MODEL), F32), jax.ShapeDtypeStruct((t, D_MODEL), BF16),
                   jax.ShapeDtypeStruct((t, LANES), F32)],
        compiler_params=_params("parallel"),
        name="merge_route",
    )(x, ya, *outs, *lses, gates, wa, wb, wo, gain, wr)


def _moe_kernel(h_ref, comb_ref, wg_ref, wu_ref, wd_ref, o_ref):
    e = pl.program_id(1)

    @pl.when(e == 0)
    def _():
        o_ref[...] = jnp.zeros_like(o_ref)

    h = h_ref[...]
    comb = comb_ref[...]
    lane = lax.broadcasted_iota(jnp.int32, comb.shape, 1)
    weight = jnp.sum(jnp.where(lane == e, comb, 0.0), axis=-1, keepdims=True)
    gate = jnp.dot(h, wg_ref[0], preferred_element_type=F32)
    up = jnp.dot(h, wu_ref[0], preferred_element_type=F32)
    act = gate * _sigmoid(gate) * up * weight
    o_ref[...] += _mm(act, wd_ref[0])


def _moe(h, comb, wg, wu, wd, *, tm):
    t = h.shape[0]
    return pl.pallas_call(
        _moe_kernel,
        grid=(t // tm, N_EXPERTS),
        in_specs=[pl.BlockSpec((tm, D_MODEL), lambda i, e: (i, 0)),
                  pl.BlockSpec((tm, LANES), lambda i, e: (i, 0)),
                  pl.BlockSpec((1, D_MODEL, D_FF_EXPERT), lambda i, e: (e, 0, 0)),
                  pl.BlockSpec((1, D_MODEL, D_FF_EXPERT), lambda i, e: (e, 0, 0)),
                  pl.BlockSpec((1, D_FF_EXPERT, D_MODEL), lambda i, e: (e, 0, 0))],
        out_specs=pl.BlockSpec((tm, D_MODEL), lambda i, e: (i, 0)),
        out_shape=jax.ShapeDtypeStruct((t, D_MODEL), F32),
        compiler_params=_params("parallel", "arbitrary"),
        name="moe_dense",
    )(h, comb, wg, wu, wd)


def _ple_kernel(x_ref, y_ref, p_ref, gp_ref, wg_ref, wp_ref, gf_ref, o_ref):
    x = x_ref[...] + y_ref[...]
    gate = _sigmoid(_mm(_rms(x, gp_ref[...]), wg_ref[...]))
    x = x + gate * _mm(p_ref[...], wp_ref[...])
    o_ref[...] = _rms(x, gf_ref[...])


def _ple(x, y, p, gain_ple, w_gate, w_proj, gain_final, *, tm):
    t = x.shape[0]
    tok = lambda n: pl.BlockSpec((tm, n), lambda i: (i, 0))
    full = lambda a: pl.BlockSpec(a.shape, lambda i: (0, 0))
    return pl.pallas_call(
        _ple_kernel,
        grid=(t // tm,),
        in_specs=[tok(D_MODEL), tok(D_MODEL), tok(PLE_DIM), full(gain_ple), full(w_gate), full(w_proj),
                  full(gain_final)],
        out_specs=tok(D_MODEL),
        out_shape=jax.ShapeDtypeStruct((t, D_MODEL), F32),
        compiler_params=_params("parallel"),
        name="ple_final",
    )(x, y, p, gain_ple, w_gate, w_proj, gain_final)


def _pad_rows(w, start, total):
    return jnp.zeros((total, w.shape[1]), w.dtype).at[start:start + w.shape[0]].set(w)


def _layer(x, p, seq, norm_mix, w_in, rwkv_mu, rwkv_w0, rwkv_w2, rwkv_a0, rwkv_a2, rwkv_g2, rwkv_k_k,
           rwkv_k_a, rwkv_r_k, rwkv_ln_w, rwkv_ln_b, w_proj_a, w_proj_b, w_out, norm_moe, w_router_group,
           w_router_expert, w_exp_gate, w_exp_up, w_exp_down, norm_ple, w_ple_gate, w_ple_proj, norm_final):
    t = x.shape[0]
    bsz = t // seq
    row = lambda a: a.reshape(1, -1)
    w_in16 = w_in.astype(BF16)
    c0, c1 = RWKV_COLS, RWKV_COLS + ATTN_COLS
    z = _norm_matmul(x, row(norm_mix), w_in16[:, :c0], F32, tm=512, tn=RWKV_COLS // 2)
    qkv = _norm_matmul(x, row(norm_mix), w_in16[:, c0:c1], BF16, tm=512, tn=ATTN_COLS // 3)
    gates = _norm_matmul(x, row(norm_mix), w_in16[:, c1:], F32, tm=512, tn=D_MODEL, sigmoid=True)

    seg = np.arange(RWKV_DIM) // HEAD_DIM
    seg_ones = jnp.asarray(seg[:, None] == seg[None, :], dtype=BF16)
    r, lw, k, v, kk, b, g = _rwkv_prep(
        z, seq, row(rwkv_mu), row(rwkv_w0), _pad_rows(rwkv_w2, 0, LORA_COLS), row(rwkv_a0),
        _pad_rows(rwkv_a2, W_LORA, LORA_COLS), _pad_rows(rwkv_g2, W_LORA + A_LORA, LORA_COLS),
        row(rwkv_k_k), row(rwkv_k_a), seg_ones, tm=512)
    tok3 = lambda a: a.reshape(bsz, seq, RWKV_DIM)
    ya = _wkv(tok3(r), tok3(lw), tok3(k), tok3(v), tok3(kk), tok3(b), tok3(g), row(rwkv_ln_w),
              row(rwkv_ln_b), row(rwkv_r_k), seg_ones, tb=256).reshape(t, RWKV_DIM)

    qkv3 = qkv.reshape(bsz, seq, ATTN_COLS)
    attn = [_attn_group(qkv3, grp) for grp in range(len(ATTN_GROUPS))]

    w_route = jnp.concatenate(
        [jnp.moveaxis(w_router_expert, 0, 1).reshape(D_MODEL, N_EXPERTS), w_router_group,
         jnp.zeros((D_MODEL, LANES - N_EXPERTS - N_EXPERT_GROUPS), F32)], axis=1)
    x1, h_moe, comb = _merge(x, ya, [o for o, _ in attn], [l for _, l in attn], gates,
                             w_proj_a.astype(BF16), w_proj_b.astype(BF16), w_out.astype(BF16),
                             row(norm_moe), w_route, tm=512)
    y_moe = _moe(h_moe, comb,
                 w_exp_gate.reshape(N_EXPERTS, D_MODEL, D_FF_EXPERT).astype(BF16),
                 w_exp_up.reshape(N_EXPERTS, D_MODEL, D_FF_EXPERT).astype(BF16),
                 w_exp_down.reshape(N_EXPERTS, D_FF_EXPERT, D_MODEL).astype(BF16), tm=1024)
    return _ple(x1, y_moe, p, row(norm_ple), w_ple_gate.astype(BF16), w_ple_proj.astype(BF16),
                row(norm_final), tm=512)


def kernel(x, p, norm_mix, w_in, rwkv_mu, rwkv_w0, rwkv_w2, rwkv_a0, rwkv_a2, rwkv_g2, rwkv_k_k, rwkv_k_a, rwkv_r_k, rwkv_ln_w, rwkv_ln_b, w_proj_a, w_proj_b, w_out, norm_moe, w_router_group, w_router_expert, w_exp_gate, w_exp_up, w_exp_down, norm_ple, w_ple_gate, w_ple_proj, norm_final):
    bsz, seq, d = x.shape
    depth = w_in.shape[0]
    assert depth == 1, "the final norm is fused into the (single) layer"
    out = _layer(x.reshape(bsz * seq, d).astype(F32), p[0].reshape(bsz * seq, PLE_DIM), seq,
                 norm_mix[0], w_in[0], rwkv_mu[0], rwkv_w0[0], rwkv_w2[0], rwkv_a0[0], rwkv_a2[0],
                 rwkv_g2[0], rwkv_k_k[0], rwkv_k_a[0], rwkv_r_k[0], rwkv_ln_w[0], rwkv_ln_b[0],
                 w_proj_a[0], w_proj_b[0], w_out[0], norm_moe[0], w_router_group[0], w_router_expert[0],
                 w_exp_gate[0], w_exp_up[0], w_exp_down[0], norm_ple[0], w_ple_gate[0], w_ple_proj[0],
                 norm_final)
    return out.reshape(bsz, seq, d)
```

```python
import functools

import numpy as np
import jax
import jax.numpy as jnp
from jax import lax
from jax.experimental import pallas as pl
from jax.experimental.pallas import tpu as pltpu

F32 = jnp.float32
BF16 = jnp.bfloat16
HIGHEST = lax.Precision.HIGHEST

D_MODEL = 1024
PLE_DIM = 256
HEAD_DIM = 64
RWKV_HEADS = 8
RWKV_DIM = RWKV_HEADS * HEAD_DIM
W_LORA, A_LORA, G_LORA = 64, 64, 128
LORA_COLS = W_LORA + A_LORA + G_LORA
ATTN_GROUPS = ((128, 1), (512, 4), (2048, 16))
HEADS_PER_GROUP = 8
ATTN_HEADS = HEADS_PER_GROUP * len(ATTN_GROUPS)
ATTN_DIM = ATTN_HEADS * HEAD_DIM
ATTN_OUT_DIM = HEADS_PER_GROUP * HEAD_DIM
Q_BLOCK = 128
ATTN_TQ = 256
RWKV_COLS = 3 * RWKV_DIM + LORA_COLS
ATTN_COLS = 3 * ATTN_DIM
N_EXPERT_GROUPS = 4
EXPERTS_PER_GROUP = 8
N_EXPERTS = N_EXPERT_GROUPS * EXPERTS_PER_GROUP
D_FF_EXPERT = 256
NORM_EPS = 1e-6
RWKV_GN_EPS = 64e-5
LANES = 128
MASKED = -1e30
WKV_CHUNK = 64
VMEM_LIMIT = 48 * 1024 * 1024


def _mm(a, b):
    return jnp.dot(a.astype(BF16), b.astype(BF16), preferred_element_type=F32)


def _mm_nt(a, b):
    return lax.dot_general(a.astype(BF16), b.astype(BF16), (((1,), (1,)), ((), ())),
                           preferred_element_type=F32)


def _mm_tn(a, b):
    return lax.dot_general(a.astype(BF16), b.astype(BF16), (((0,), (0,)), ((), ())),
                           preferred_element_type=F32)


def _sigmoid(x):
    return 1.0 / (1.0 + jnp.exp(-x))


def _rms(x, gain):
    return x * lax.rsqrt(jnp.mean(x * x, axis=-1, keepdims=True) + NORM_EPS) * gain


def _params(*sem):
    return pltpu.CompilerParams(dimension_semantics=sem, vmem_limit_bytes=VMEM_LIMIT)


def _norm_matmul_kernel(x_ref, g_ref, w_ref, o_ref, h_ref, *, sigmoid):
    @pl.when(pl.program_id(1) == 0)
    def _():
        h_ref[...] = _rms(x_ref[...], g_ref[...]).astype(BF16)

    acc = jnp.dot(h_ref[...], w_ref[...], preferred_element_type=F32)
    if sigmoid:
        acc = _sigmoid(acc)
    o_ref[...] = acc.astype(o_ref.dtype)


def _norm_matmul(x, gain, w, out_dtype, *, tm, tn, sigmoid=False):
    t, d = x.shape
    n = w.shape[1]
    return pl.pallas_call(
        functools.partial(_norm_matmul_kernel, sigmoid=sigmoid),
        grid=(t // tm, n // tn),
        in_specs=[pl.BlockSpec((tm, d), lambda i, j: (i, 0)),
                  pl.BlockSpec((1, d), lambda i, j: (0, 0)),
                  pl.BlockSpec((d, tn), lambda i, j: (0, j))],
        out_specs=pl.BlockSpec((tm, tn), lambda i, j: (i, j)),
        out_shape=jax.ShapeDtypeStruct((t, n), out_dtype),
        scratch_shapes=[pltpu.VMEM((tm, d), BF16)],
        compiler_params=_params("parallel", "arbitrary"),
        name="norm_matmul",
    )(x, gain, w)


def _seg_sum(x, seg_ones):
    hi = x.astype(BF16)
    lo = (x - hi.astype(F32)).astype(BF16)
    return (jnp.dot(hi, seg_ones, preferred_element_type=F32)
            + jnp.dot(lo, seg_ones, preferred_element_type=F32))


def _rwkv_prep_kernel(z_ref, zp_ref, mu_ref, w0_ref, w2_ref, a0_ref, a2_ref, g2_ref, kk_ref, ka_ref,
                      seg_ref, r_o, lw_o, k_o, v_o, kk_o, b_o, g_o, *, tiles_per_seq):
    i = pl.program_id(0)
    z = z_ref[...]
    last = zp_ref[...][7:8, :]
    last = jnp.where(i % tiles_per_seq == 0, 0.0, last)
    row = lax.broadcasted_iota(jnp.int32, z.shape, 0)
    prev = jnp.where(row == 0, last, pltpu.roll(z, 1, axis=0))
    zs = z + (prev - z) * mu_ref[...]
    d = RWKV_DIM
    r, k, v = zs[:, :d], zs[:, d:2 * d], zs[:, 2 * d:3 * d]
    lora = zs[:, 3 * d:]
    lane = lax.broadcasted_iota(jnp.int32, lora.shape, 1)
    lora = jnp.where(lane < W_LORA, jnp.tanh(lora),
                     jnp.where(lane < W_LORA + A_LORA, lora, _sigmoid(lora)))
    w_lin = w0_ref[...] + jnp.dot(lora, w2_ref[...], precision=HIGHEST, preferred_element_type=F32)
    a_lin = a0_ref[...] + jnp.dot(lora, a2_ref[...], precision=HIGHEST, preferred_element_type=F32)
    g = jnp.dot(lora, g2_ref[...], precision=HIGHEST, preferred_element_type=F32)
    u = -w_lin
    softplus = jnp.maximum(u, 0.0) + jnp.log(1.0 + jnp.exp(-jnp.abs(u)))
    lw = -jnp.exp(-softplus - 0.5)
    a = _sigmoid(a_lin)
    kk = k * kk_ref[...]
    kk = kk * lax.rsqrt(_seg_sum(kk * kk, seg_ref[...]) + 1e-12)
    k = k * (1.0 + (a - 1.0) * ka_ref[...])
    r_o[...] = r
    lw_o[...] = lw
    k_o[...] = k
    v_o[...] = v
    kk_o[...] = kk
    b_o[...] = kk * a
    g_o[...] = g


def _rwkv_prep(z, seq, mu, w0, w2p, a0, a2p, g2p, k_k, k_a, seg_ones, *, tm):
    t = z.shape[0]
    d = RWKV_DIM
    row = lambda n: pl.BlockSpec((1, n), lambda i: (0, 0))
    full = lambda a: pl.BlockSpec(a.shape, lambda i: (0, 0))
    out = jax.ShapeDtypeStruct((t, d), F32)
    return pl.pallas_call(
        functools.partial(_rwkv_prep_kernel, tiles_per_seq=seq // tm),
        grid=(t // tm,),
        in_specs=[pl.BlockSpec((tm, RWKV_COLS), lambda i: (i, 0)),
                  pl.BlockSpec((8, RWKV_COLS), lambda i: (jnp.maximum(i * (tm // 8) - 1, 0), 0)),
                  row(RWKV_COLS), row(d), full(w2p), row(d), full(a2p), full(g2p), row(d), row(d),
                  full(seg_ones)],
        out_specs=[pl.BlockSpec((tm, d), lambda i: (i, 0))] * 7,
        out_shape=[out] * 7,
        compiler_params=_params("parallel"),
        name="rwkv_prep",
    )(z, z, mu, w0, w2p, a0, a2p, g2p, k_k, k_a, seg_ones)


def _bmm(a, b):
    return lax.dot_general(a.astype(BF16), b.astype(BF16), (((2,), (1,)), ((0,), (0,))),
                           preferred_element_type=F32)


def _bmm_nt(a, b):
    return lax.dot_general(a.astype(BF16), b.astype(BF16), (((2,), (2,)), ((0,), (0,))),
                           preferred_element_type=F32)


def _bmm_tn(a, b):
    return lax.dot_general(a.astype(BF16), b.astype(BF16), (((1,), (1,)), ((0,), (0,))),
                           preferred_element_type=F32)


def _wkv_kernel(r_ref, lw_ref, k_ref, v_ref, kk_ref, b_ref, g_ref, lnw_ref, lnb_ref, rk_ref, seg_ref,
                o_ref, s_ref, *, chunk):
    @pl.when(pl.program_id(1) == 0)
    def _():
        s_ref[...] = jnp.zeros_like(s_ref)

    tb, d = r_ref.shape[1:]
    nc, nh = tb // chunk, d // HEAD_DIM
    ii = lax.broadcasted_iota(jnp.int32, (chunk, chunk), 0)
    jj = lax.broadcasted_iota(jnp.int32, (chunk, chunk), 1)
    incl, strict = ii >= jj, ii > jj
    tri = incl.astype(F32)
    eye = (ii == jj).astype(F32)
    rows = lambda c: slice(c * chunk, (c + 1) * chunk)
    lanes = lambda h: slice(h * HEAD_DIM, (h + 1) * HEAD_DIM)

    def split(x):
        return jnp.stack([x[rows(c), lanes(h)] for c in range(nc) for h in range(nh)])

    r, lw, k, v, kk, b = r_ref[0], lw_ref[0], k_ref[0], v_ref[0], kk_ref[0], b_ref[0]
    cums = [jnp.dot(tri, lw[rows(c)], precision=HIGHEST, preferred_element_type=F32) for c in range(nc)]
    cum = jnp.concatenate(cums, axis=0)
    total = jnp.concatenate([jnp.broadcast_to(cs[chunk - 1:], (chunk, d)) for cs in cums], axis=0)
    g_inv = jnp.exp(-cum)
    g_tail = jnp.exp(total - cum)
    kq, rq = split(kk * jnp.exp(cum - lw)), split(r * jnp.exp(cum))
    bi, ki = split(b * g_inv), split(k * g_inv)
    bh, kh = split(b * g_tail), split(k * g_tail)
    vs = split(v)
    decay = jnp.stack([jnp.exp(cs[chunk - 1:, lanes(h)]) for cs in cums for h in range(nh)])

    a_b = jnp.where(strict, _bmm_nt(kq, bi), 0.0)
    a_k = jnp.where(strict, _bmm_nt(kq, ki), 0.0)
    a_rb = jnp.where(incl, _bmm_nt(rq, bi), 0.0)
    a_rk = jnp.where(incl, _bmm_nt(rq, ki), 0.0)
    inv = eye - a_b
    power = a_b
    for _ in range(int(np.log2(chunk)) - 1):
        power = _bmm(power, power)
        inv = inv + _bmm(inv, power)
    w = _bmm(inv, kq)
    u0 = -_bmm(inv, _bmm(a_k, vs))
    qh = rq - _bmm(a_rb, w)
    y0 = _bmm(a_rb, u0) + _bmm(a_rk, vs)
    mp = _bmm_tn(w, bh)
    s_add = _bmm_tn(u0, bh) + _bmm_tn(vs, kh)

    state = s_ref[...]
    for c in range(nc):
        sl = slice(c * nh, (c + 1) * nh)
        y = _bmm_nt(qh[sl], state) + y0[sl]
        for h in range(nh):
            o_ref[0, rows(c), lanes(h)] = y[h]
        state = state * decay[sl] - _bmm(state, mp[sl]) + s_add[sl]
    s_ref[...] = state

    seg = seg_ref[...]
    y = o_ref[0]
    yc = y - _seg_sum(y, seg) * (1.0 / HEAD_DIM)
    var = _seg_sum(yc * yc, seg) * (1.0 / HEAD_DIM)
    y = yc * lax.rsqrt(var + RWKV_GN_EPS) * lnw_ref[...] + lnb_ref[...]
    y = y + _seg_sum(r * k * rk_ref[...], seg) * v
    o_ref[0] = y * g_ref[0]


def _wkv(r, lw, k, v, kk, b, g, ln_w, ln_b, r_k, seg_ones, *, tb):
    bsz, seq, d = r.shape
    tok = pl.BlockSpec((1, tb, d), lambda bi, t: (bi, t, 0))
    par = pl.BlockSpec((1, d), lambda bi, t: (0, 0))
    return pl.pallas_call(
        functools.partial(_wkv_kernel, chunk=WKV_CHUNK),
        grid=(bsz, seq // tb),
        in_specs=[tok] * 7 + [par] * 3 + [pl.BlockSpec(seg_ones.shape, lambda bi, t: (0, 0))],
        out_specs=tok,
        out_shape=jax.ShapeDtypeStruct((bsz, seq, d), F32),
        scratch_shapes=[pltpu.VMEM((d // HEAD_DIM, HEAD_DIM, HEAD_DIM), F32)],
        compiler_params=_params("parallel", "arbitrary"),
        name="wkv_scan",
    )(r, lw, k, v, kk, b, g, ln_w, ln_b, r_k, seg_ones)


def _attn_kernel(q_ref, kp_ref, kc_ref, vp_ref, vc_ref, bias_ref, o_ref, l_ref):
    first = pl.program_id(2) == 0
    q = q_ref[0] * (HEAD_DIM ** -0.5)
    kc, vc = kc_ref[0], vc_ref[0]
    subs = q.shape[0] // Q_BLOCK
    rows = lambda j: slice(j * Q_BLOCK, (j + 1) * Q_BLOCK)
    lanes = lambda h: slice(h * HEAD_DIM, (h + 1) * HEAD_DIM)
    k_prev = [kp_ref[0]] + [kc[rows(j)] for j in range(subs - 1)]
    v_prev = [vp_ref[0]] + [vc[rows(j)] for j in range(subs - 1)]
    work = [(j, h) for j in range(subs) for h in range(HEADS_PER_GROUP)]
    scores = []
    for j, h in work:
        bias = bias_ref[h]
        bias_prev = bias[:, :Q_BLOCK]
        if j == 0:
            bias_prev = jnp.where(first, MASKED, bias_prev)
        qh = q[rows(j), lanes(h)]
        scores.append((_mm_nt(qh, k_prev[j][:, lanes(h)]) + bias_prev,
                       _mm_nt(qh, kc[rows(j), lanes(h)]) + bias[:, Q_BLOCK:]))
    tops = [jnp.max(jnp.maximum(sp, sc), axis=-1, keepdims=True) for sp, sc in scores]
    probs = [(jnp.exp(sp - m), jnp.exp(sc - m)) for (sp, sc), m in zip(scores, tops)]
    dens = [jnp.sum(pp + pc, axis=-1, keepdims=True) for pp, pc in probs]
    accs = [_mm(pp, v_prev[j][:, lanes(h)]) + _mm(pc, vc[rows(j), lanes(h)])
            for (pp, pc), (j, h) in zip(probs, work)]
    lane = lax.broadcasted_iota(jnp.int32, (Q_BLOCK, LANES), 1)
    for j in range(subs):
        lse = jnp.zeros((Q_BLOCK, LANES), F32)
        for h in range(HEADS_PER_GROUP):
            i = j * HEADS_PER_GROUP + h
            o_ref[0, rows(j), lanes(h)] = accs[i] * (1.0 / dens[i])
            lse = jnp.where(lane == h, tops[i] + jnp.log(dens[i]), lse)
        l_ref[0, rows(j), :] = lse


def _attn_bias(group):
    window, dilation = ATTN_GROUPS[group]
    n_back = window // dilation
    heads = np.arange(group * HEADS_PER_GROUP, (group + 1) * HEADS_PER_GROUP)
    slopes = (2.0 ** (-8.0 * (heads + 1) / ATTN_HEADS)).astype(np.float32)
    delta = (np.arange(Q_BLOCK)[:, None] + Q_BLOCK) - np.arange(2 * Q_BLOCK)[None, :]
    valid = (delta >= 0) & (delta <= n_back)
    alibi = -slopes[:, None, None] * (dilation * delta).astype(np.float32)[None]
    return jnp.asarray(np.where(valid[None], alibi, np.float32(MASKED)).astype(np.float32))


def _attn_group(qkv, group):
    bsz, seq, cols = qkv.shape
    dil = ATTN_GROUPS[group][1]
    comp = seq // dil
    assert comp % ATTN_TQ == 0
    view = qkv.reshape(bsz, comp, dil * cols)
    per_row = cols // ATTN_OUT_DIM
    groups = len(ATTN_GROUPS)
    sub = ATTN_TQ // Q_BLOCK

    def col(which):
        return lambda bi, r, n: (bi, n, r * per_row + which * groups + group)

    def col_prev(which):
        return lambda bi, r, n: (bi, jnp.maximum(n * sub - 1, 0), r * per_row + which * groups + group)

    cur, prev = (1, ATTN_TQ, ATTN_OUT_DIM), (1, Q_BLOCK, ATTN_OUT_DIM)
    bias = _attn_bias(group)
    o, lse = pl.pallas_call(
        _attn_kernel,
        grid=(bsz, dil, comp // ATTN_TQ),
        in_specs=[pl.BlockSpec(cur, col(0)),
                  pl.BlockSpec(prev, col_prev(1)), pl.BlockSpec(cur, col(1)),
                  pl.BlockSpec(prev, col_prev(2)), pl.BlockSpec(cur, col(2)),
                  pl.BlockSpec(bias.shape, lambda bi, r, n: (0, 0, 0))],
        out_specs=[pl.BlockSpec(cur, lambda bi, r, n: (bi, n, r)),
                   pl.BlockSpec((1, ATTN_TQ, LANES), lambda bi, r, n: (bi, n, r))],
        out_shape=[jax.ShapeDtypeStruct((bsz, comp, dil * ATTN_OUT_DIM), F32),
                   jax.ShapeDtypeStruct((bsz, comp, dil * LANES), F32)],
        compiler_params=_params("parallel", "parallel", "parallel"),
        name=f"dilated_attn_{group}",
    )(view, view, view, view, view, bias)
    return o.reshape(bsz * seq, ATTN_OUT_DIM), lse.reshape(bsz * seq, LANES)


def _route(logits):
    lane = lax.broadcasted_iota(jnp.int32, logits.shape, 1)
    neg = -jnp.inf
    first_max = lambda vals, mx: jnp.min(jnp.where(vals == mx, lane, LANES), axis=-1, keepdims=True)
    is_grp = (lane >= N_EXPERTS) & (lane < N_EXPERTS + N_EXPERT_GROUPS)
    gl = jnp.where(is_grp, logits, neg)
    gmax = jnp.max(gl, axis=-1, keepdims=True)
    grp_w = 1.0 / jnp.sum(jnp.exp(gl - gmax), axis=-1, keepdims=True)
    grp = first_max(gl, gmax) - N_EXPERTS
    in_grp = (lane >= grp * EXPERTS_PER_GROUP) & (lane < (grp + 1) * EXPERTS_PER_GROUP)
    el = jnp.where(in_grp, logits, neg)
    v1 = jnp.max(el, axis=-1, keepdims=True)
    i1 = first_max(el, v1)
    el2 = jnp.where(lane == i1, neg, el)
    v2 = jnp.max(el2, axis=-1, keepdims=True)
    i2 = first_max(el2, v2)
    e2 = jnp.exp(v2 - v1)
    w1 = grp_w / (1.0 + e2)
    return jnp.where(lane == i1, w1, 0.0) + jnp.where(lane == i2, w1 * e2, 0.0)


def _merge_kernel(x_ref, ya_ref, o0_ref, o1_ref, o2_ref, l0_ref, l1_ref, l2_ref, gt_ref,
                  wa_ref, wb_ref, wo_ref, gain_ref, wr_ref, ex_ref, x1_ref, h_ref, comb_ref):
    l0, l1, l2 = l0_ref[...], l1_ref[...], l2_ref[...]
    m = jnp.maximum(jnp.maximum(l0, l1), l2)
    e0, e1, e2 = jnp.exp(l0 - m), jnp.exp(l1 - m), jnp.exp(l2 - m)
    inv = 1.0 / (e0 + e1 + e2)
    spread = ex_ref[...]
    yb = (_seg_sum(e0 * inv, spread) * o0_ref[...] + _seg_sum(e1 * inv, spread) * o1_ref[...]
          + _seg_sum(e2 * inv, spread) * o2_ref[...])
    gates = gt_ref[...]
    merged = (gates[:, :D_MODEL] * _mm(ya_ref[...], wa_ref[...])
              + gates[:, D_MODEL:] * _mm(yb, wb_ref[...]))
    x1 = x_ref[...] + _mm(merged, wo_ref[...])
    x1_ref[...] = x1
    h = _rms(x1, gain_ref[...])
    h_ref[...] = h.astype(BF16)
    logits = jnp.dot(h, wr_ref[...], precision=HIGHEST, preferred_element_type=F32)
    comb_ref[...] = _route(logits)


def _merge(x, ya, outs, lses, gates, wa, wb, wo, gain, wr, spread, *, tm):
    t = x.shape[0]
    tok = lambda n: pl.BlockSpec((tm, n), lambda i: (i, 0))
    full = lambda a: pl.BlockSpec(a.shape, lambda i: (0, 0))
    return pl.pallas_call(
        _merge_kernel,
        grid=(t // tm,),
        in_specs=[tok(D_MODEL), tok(RWKV_DIM)] + [tok(ATTN_OUT_DIM)] * 3 + [tok(LANES)] * 3 + [
                  tok(2 * D_MODEL), full(wa), full(wb), full(wo), full(gain), full(wr), full(spread)],
        out_specs=[tok(D_MODEL), tok(D_MODEL), tok(LANES)],
        out_shape=[jax.ShapeDtypeStruct((t, D_MODEL), F32), jax.ShapeDtypeStruct((t, D_MODEL), BF16),
                   jax.ShapeDtypeStruct((t, LANES), F32)],
        compiler_params=_params("parallel"),
        name="merge_route",
    )(x, ya, *outs, *lses, gates, wa, wb, wo, gain, wr, spread)


def _moe_kernel(h_ref, comb_ref, wg_ref, wu_ref, wd_ref, o_ref):
    e = pl.program_id(1)

    @pl.when(e == 0)
    def _():
        o_ref[...] = jnp.zeros_like(o_ref)

    h = h_ref[...]
    comb = comb_ref[...]
    lane = lax.broadcasted_iota(jnp.int32, comb.shape, 1)
    weight = jnp.sum(jnp.where(lane == e, comb, 0.0), axis=-1, keepdims=True)
    gate = _mm(h, wg_ref[0])
    up = _mm(h, wu_ref[0])
    act = gate * _sigmoid(gate) * up * weight
    o_ref[...] += _mm(act, wd_ref[0])


def _moe(h, comb, wg, wu, wd, *, tm):
    t = h.shape[0]
    return pl.pallas_call(
        _moe_kernel,
        grid=(t // tm, N_EXPERTS),
        in_specs=[pl.BlockSpec((tm, D_MODEL), lambda i, e: (i, 0)),
                  pl.BlockSpec((tm, LANES), lambda i, e: (i, 0)),
                  pl.BlockSpec((1, D_MODEL, D_FF_EXPERT), lambda i, e: (e, 0, 0)),
                  pl.BlockSpec((1, D_MODEL, D_FF_EXPERT), lambda i, e: (e, 0, 0)),
                  pl.BlockSpec((1, D_FF_EXPERT, D_MODEL), lambda i, e: (e, 0, 0))],
        out_specs=pl.BlockSpec((tm, D_MODEL), lambda i, e: (i, 0)),
        out_shape=jax.ShapeDtypeStruct((t, D_MODEL), F32),
        compiler_params=_params("parallel", "arbitrary"),
        name="moe_dense",
    )(h, comb, wg, wu, wd)


def _ple_kernel(x_ref, y_ref, p_ref, gp_ref, wg_ref, wp_ref, gf_ref, o_ref):
    x = x_ref[...] + y_ref[...]
    gate = _sigmoid(_mm(_rms(x, gp_ref[...]), wg_ref[...]))
    x = x + gate * _mm(p_ref[...], wp_ref[...])
    o_ref[...] = _rms(x, gf_ref[...])


def _ple(x, y, p, gain_ple, w_gate, w_proj, gain_final, *, tm):
    t = x.shape[0]
    tok = lambda n: pl.BlockSpec((tm, n), lambda i: (i, 0))
    full = lambda a: pl.BlockSpec(a.shape, lambda i: (0, 0))
    return pl.pallas_call(
        _ple_kernel,
        grid=(t // tm,),
        in_specs=[tok(D_MODEL), tok(D_MODEL), tok(PLE_DIM), full(gain_ple), full(w_gate), full(w_proj),
                  full(gain_final)],
        out_specs=tok(D_MODEL),
        out_shape=jax.ShapeDtypeStruct((t, D_MODEL), F32),
        compiler_params=_params("parallel"),
        name="ple_final",
    )(x, y, p, gain_ple, w_gate, w_proj, gain_final)


def _pad_rows(w, start, total):
    return jnp.zeros((total, w.shape[1]), w.dtype).at[start:start + w.shape[0]].set(w)


def _layer(x, p, seq, norm_mix, w_in, rwkv_mu, rwkv_w0, rwkv_w2, rwkv_a0, rwkv_a2, rwkv_g2, rwkv_k_k,
           rwkv_k_a, rwkv_r_k, rwkv_ln_w, rwkv_ln_b, w_proj_a, w_proj_b, w_out, norm_moe, w_router_group,
           w_router_expert, w_exp_gate, w_exp_up, w_exp_down, norm_ple, w_ple_gate, w_ple_proj, norm_final):
    t = x.shape[0]
    bsz = t // seq
    row = lambda a: a.reshape(1, -1)
    w_in16 = w_in.astype(BF16)
    c0, c1 = RWKV_COLS, RWKV_COLS + ATTN_COLS
    z = _norm_matmul(x, row(norm_mix), w_in16[:, :c0], F32, tm=512, tn=RWKV_COLS // 2)
    qkv = _norm_matmul(x, row(norm_mix), w_in16[:, c0:c1], BF16, tm=512, tn=ATTN_COLS // 3)
    gates = _norm_matmul(x, row(norm_mix), w_in16[:, c1:], F32, tm=512, tn=D_MODEL, sigmoid=True)

    seg = np.arange(RWKV_DIM) // HEAD_DIM
    seg_ones = jnp.asarray(seg[:, None] == seg[None, :], dtype=BF16)
    r, lw, k, v, kk, b, g = _rwkv_prep(
        z, seq, row(rwkv_mu), row(rwkv_w0), _pad_rows(rwkv_w2, 0, LORA_COLS), row(rwkv_a0),
        _pad_rows(rwkv_a2, W_LORA, LORA_COLS), _pad_rows(rwkv_g2, W_LORA + A_LORA, LORA_COLS),
        row(rwkv_k_k), row(rwkv_k_a), seg_ones, tm=512)
    tok3 = lambda a: a.reshape(bsz, seq, RWKV_DIM)
    ya = _wkv(tok3(r), tok3(lw), tok3(k), tok3(v), tok3(kk), tok3(b), tok3(g), row(rwkv_ln_w),
              row(rwkv_ln_b), row(rwkv_r_k), seg_ones, tb=256).reshape(t, RWKV_DIM)

    qkv3 = qkv.reshape(bsz, seq, ATTN_COLS)
    attn = [_attn_group(qkv3, grp) for grp in range(len(ATTN_GROUPS))]

    head_of_lane = np.arange(ATTN_OUT_DIM) // HEAD_DIM
    spread = jnp.asarray(np.arange(LANES)[:, None] == head_of_lane[None, :], dtype=BF16)
    w_route = jnp.concatenate(
        [jnp.moveaxis(w_router_expert, 0, 1).reshape(D_MODEL, N_EXPERTS), w_router_group,
         jnp.zeros((D_MODEL, LANES - N_EXPERTS - N_EXPERT_GROUPS), F32)], axis=1)
    x1, h_moe, comb = _merge(x, ya, [o for o, _ in attn], [l for _, l in attn], gates,
                             w_proj_a.astype(BF16), w_proj_b.astype(BF16), w_out.astype(BF16),
                             row(norm_moe), w_route, spread, tm=512)
    y_moe = _moe(h_moe, comb,
                 w_exp_gate.reshape(N_EXPERTS, D_MODEL, D_FF_EXPERT),
                 w_exp_up.reshape(N_EXPERTS, D_MODEL, D_FF_EXPERT),
                 w_exp_down.reshape(N_EXPERTS, D_FF_EXPERT, D_MODEL), tm=2048)
    return _ple(x1, y_moe, p, row(norm_ple), w_ple_gate.astype(BF16), w_ple_proj.astype(BF16),
                row(norm_final), tm=512)


def kernel(x, p, norm_mix, w_in, rwkv_mu, rwkv_w0, rwkv_w2, rwkv_a0, rwkv_a2, rwkv_g2, rwkv_k_k, rwkv_k_a, rwkv_r_k, rwkv_ln_w, rwkv_ln_b, w_proj_a, w_proj_b, w_out, norm_moe, w_router_group, w_router_expert, w_exp_gate, w_exp_up, w_exp_down, norm_ple, w_ple_gate, w_ple_proj, norm_final):
    bsz, seq, d = x.shape
    depth = w_in.shape[0]
    assert depth == 1, "the final norm is fused into the (single) layer"
    out = _layer(x.reshape(bsz * seq, d).astype(F32), p[0].reshape(bsz * seq, PLE_DIM), seq,
                 norm_mix[0], w_in[0], rwkv_mu[0], rwkv_w0[0], rwkv_w2[0], rwkv_a0[0], rwkv_a2[0],
                 rwkv_g2[0], rwkv_k_k[0], rwkv_k_a[0], rwkv_r_k[0], rwkv_ln_w[0], rwkv_ln_b[0],
                 w_proj_a[0], w_proj_b[0], w_out[0], norm_moe[0], w_router_group[0], w_router_expert[0],
                 w_exp_gate[0], w_exp_up[0], w_exp_down[0], norm_ple[0], w_ple_gate[0], w_ple_proj[0],
                 norm_final)
    return out.reshape(bsz, seq, d)
```

```python
import functools

import numpy as np
import jax
import jax.numpy as jnp
from jax import lax
from jax.experimental import pallas as pl
from jax.experimental.pallas import tpu as pltpu

F32 = jnp.float32
BF16 = jnp.bfloat16
HIGHEST = lax.Precision.HIGHEST

D_MODEL = 1024
PLE_DIM = 256
HEAD_DIM = 64
RWKV_HEADS = 8
RWKV_DIM = RWKV_HEADS * HEAD_DIM
W_LORA, A_LORA, G_LORA = 64, 64, 128
LORA_COLS = W_LORA + A_LORA + G_LORA
ATTN_GROUPS = ((128, 1), (512, 4), (2048, 16))
HEADS_PER_GROUP = 8
ATTN_HEADS = HEADS_PER_GROUP * len(ATTN_GROUPS)
ATTN_DIM = ATTN_HEADS * HEAD_DIM
ATTN_OUT_DIM = HEADS_PER_GROUP * HEAD_DIM
Q_BLOCK = 128
ATTN_TQ = 256
RWKV_COLS = 3 * RWKV_DIM + LORA_COLS
ATTN_COLS = 3 * ATTN_DIM
N_EXPERT_GROUPS = 4
EXPERTS_PER_GROUP = 8
N_EXPERTS = N_EXPERT_GROUPS * EXPERTS_PER_GROUP
D_FF_EXPERT = 256
NORM_EPS = 1e-6
RWKV_GN_EPS = 64e-5
LANES = 128
MASKED = -1e30
WKV_CHUNK = 64
VMEM_LIMIT = 48 * 1024 * 1024


def _mm(a, b):
    return jnp.dot(a.astype(BF16), b.astype(BF16), preferred_element_type=F32)


def _mm_nt(a, b):
    return lax.dot_general(a.astype(BF16), b.astype(BF16), (((1,), (1,)), ((), ())),
                           preferred_element_type=F32)


def _mm_tn(a, b):
    return lax.dot_general(a.astype(BF16), b.astype(BF16), (((0,), (0,)), ((), ())),
                           preferred_element_type=F32)


def _sigmoid(x):
    return 1.0 / (1.0 + jnp.exp(-x))


def _rms(x, gain):
    return x * lax.rsqrt(jnp.mean(x * x, axis=-1, keepdims=True) + NORM_EPS) * gain


def _params(*sem):
    return pltpu.CompilerParams(dimension_semantics=sem, vmem_limit_bytes=VMEM_LIMIT)


def _norm_matmul_kernel(x_ref, g_ref, w_ref, o_ref, h_ref, *, sigmoid):
    @pl.when(pl.program_id(1) == 0)
    def _():
        h_ref[...] = _rms(x_ref[...], g_ref[...]).astype(BF16)

    acc = jnp.dot(h_ref[...], w_ref[...], preferred_element_type=F32)
    if sigmoid:
        acc = _sigmoid(acc)
    o_ref[...] = acc.astype(o_ref.dtype)


def _norm_matmul(x, gain, w, out_dtype, *, tm, tn, sigmoid=False):
    t, d = x.shape
    n = w.shape[1]
    return pl.pallas_call(
        functools.partial(_norm_matmul_kernel, sigmoid=sigmoid),
        grid=(t // tm, n // tn),
        in_specs=[pl.BlockSpec((tm, d), lambda i, j: (i, 0)),
                  pl.BlockSpec((1, d), lambda i, j: (0, 0)),
                  pl.BlockSpec((d, tn), lambda i, j: (0, j))],
        out_specs=pl.BlockSpec((tm, tn), lambda i, j: (i, j)),
        out_shape=jax.ShapeDtypeStruct((t, n), out_dtype),
        scratch_shapes=[pltpu.VMEM((tm, d), BF16)],
        compiler_params=_params("parallel", "arbitrary"),
        name="norm_matmul",
    )(x, gain, w)


def _qkv_proj_kernel(x_ref, g_ref, w_ref, o0_ref, o1_ref, o2_ref, h_ref, acc_ref):
    j = pl.program_id(1)

    @pl.when(j == 0)
    def _():
        h_ref[...] = _rms(x_ref[...], g_ref[...]).astype(BF16)

    acc = jnp.dot(h_ref[...], w_ref[...], preferred_element_type=F32)
    chunks = acc.shape[1] // LANES
    col = lambda c: slice(c * LANES, (c + 1) * LANES)
    for grp, o_ref in enumerate((o0_ref, o1_ref, o2_ref)):
        dil = ATTN_GROUPS[grp][1]

        @pl.when(j == grp)
        def _(o_ref=o_ref, dil=dil):
            if dil == 1:
                o_ref[0, 0] = acc.astype(BF16)
                return
            for c in range(chunks):
                acc_ref[c] = acc[:, col(c)]
            n = acc.shape[0] // dil
            for r in range(dil):
                for c in range(chunks):
                    o_ref[0, r, :, col(c)] = acc_ref[c, pl.ds(r, n, stride=dil), :].astype(BF16)


def _qkv_proj(x, gain, w, seq, *, tm):
    t, d = x.shape
    tps = seq // tm
    tn = 3 * ATTN_OUT_DIM
    out_specs, out_shapes = [], []
    for _, dil in ATTN_GROUPS:
        out_specs.append(pl.BlockSpec((1, dil, tm // dil, tn), lambda i, j: (i // tps, 0, i % tps, 0)))
        out_shapes.append(jax.ShapeDtypeStruct((t // seq, dil, seq // dil, tn), BF16))
    return pl.pallas_call(
        _qkv_proj_kernel,
        grid=(t // tm, len(ATTN_GROUPS)),
        in_specs=[pl.BlockSpec((tm, d), lambda i, j: (i, 0)),
                  pl.BlockSpec((1, d), lambda i, j: (0, 0)),
                  pl.BlockSpec((d, tn), lambda i, j: (0, j))],
        out_specs=out_specs,
        out_shape=out_shapes,
        scratch_shapes=[pltpu.VMEM((tm, d), BF16), pltpu.VMEM((tn // LANES, tm, LANES), F32)],
        compiler_params=_params("parallel", "arbitrary"),
        name="qkv_proj",
    )(x, gain, w)


def _seg_sum(x, seg_ones):
    hi = x.astype(BF16)
    lo = (x - hi.astype(F32)).astype(BF16)
    return (jnp.dot(hi, seg_ones, preferred_element_type=F32)
            + jnp.dot(lo, seg_ones, preferred_element_type=F32))


def _rwkv_prep_kernel(z_ref, zp_ref, mu_ref, w0_ref, w2_ref, a0_ref, a2_ref, g2_ref, kk_ref, ka_ref,
                      seg_ref, r_o, lw_o, k_o, v_o, kk_o, b_o, g_o, *, tiles_per_seq):
    i = pl.program_id(0)
    z = z_ref[...]
    last = zp_ref[...][7:8, :]
    last = jnp.where(i % tiles_per_seq == 0, 0.0, last)
    row = lax.broadcasted_iota(jnp.int32, z.shape, 0)
    prev = jnp.where(row == 0, last, pltpu.roll(z, 1, axis=0))
    zs = z + (prev - z) * mu_ref[...]
    d = RWKV_DIM
    r, k, v = zs[:, :d], zs[:, d:2 * d], zs[:, 2 * d:3 * d]
    lora = zs[:, 3 * d:]
    lane = lax.broadcasted_iota(jnp.int32, lora.shape, 1)
    lora = jnp.where(lane < W_LORA, jnp.tanh(lora),
                     jnp.where(lane < W_LORA + A_LORA, lora, _sigmoid(lora)))
    w_lin = w0_ref[...] + jnp.dot(lora, w2_ref[...], precision=HIGHEST, preferred_element_type=F32)
    a_lin = a0_ref[...] + jnp.dot(lora, a2_ref[...], precision=HIGHEST, preferred_element_type=F32)
    g = jnp.dot(lora, g2_ref[...], precision=HIGHEST, preferred_element_type=F32)
    u = -w_lin
    softplus = jnp.maximum(u, 0.0) + jnp.log(1.0 + jnp.exp(-jnp.abs(u)))
    lw = -jnp.exp(-softplus - 0.5)
    a = _sigmoid(a_lin)
    kk = k * kk_ref[...]
    kk = kk * lax.rsqrt(_seg_sum(kk * kk, seg_ref[...]) + 1e-12)
    k = k * (1.0 + (a - 1.0) * ka_ref[...])
    r_o[...] = r
    lw_o[...] = lw
    k_o[...] = k
    v_o[...] = v
    kk_o[...] = kk
    b_o[...] = kk * a
    g_o[...] = g


def _rwkv_prep(z, seq, mu, w0, w2p, a0, a2p, g2p, k_k, k_a, seg_ones, *, tm):
    t = z.shape[0]
    d = RWKV_DIM
    row = lambda n: pl.BlockSpec((1, n), lambda i: (0, 0))
    full = lambda a: pl.BlockSpec(a.shape, lambda i: (0, 0))
    out = jax.ShapeDtypeStruct((t, d), F32)
    return pl.pallas_call(
        functools.partial(_rwkv_prep_kernel, tiles_per_seq=seq // tm),
        grid=(t // tm,),
        in_specs=[pl.BlockSpec((tm, RWKV_COLS), lambda i: (i, 0)),
                  pl.BlockSpec((8, RWKV_COLS), lambda i: (jnp.maximum(i * (tm // 8) - 1, 0), 0)),
                  row(RWKV_COLS), row(d), full(w2p), row(d), full(a2p), full(g2p), row(d), row(d),
                  full(seg_ones)],
        out_specs=[pl.BlockSpec((tm, d), lambda i: (i, 0))] * 7,
        out_shape=[out] * 7,
        compiler_params=_params("parallel"),
        name="rwkv_prep",
    )(z, z, mu, w0, w2p, a0, a2p, g2p, k_k, k_a, seg_ones)


def _bmm(a, b):
    return lax.dot_general(a.astype(BF16), b.astype(BF16), (((2,), (1,)), ((0,), (0,))),
                           preferred_element_type=F32)


def _bmm_nt(a, b):
    return lax.dot_general(a.astype(BF16), b.astype(BF16), (((2,), (2,)), ((0,), (0,))),
                           preferred_element_type=F32)


def _bmm_tn(a, b):
    return lax.dot_general(a.astype(BF16), b.astype(BF16), (((1,), (1,)), ((0,), (0,))),
                           preferred_element_type=F32)


def _wkv_kernel(r_ref, lw_ref, k_ref, v_ref, kk_ref, b_ref, g_ref, lnw_ref, lnb_ref, rk_ref, seg_ref,
                o_ref, s_ref, *, chunk):
    @pl.when(pl.program_id(1) == 0)
    def _():
        s_ref[...] = jnp.zeros_like(s_ref)

    tb, d = r_ref.shape[1:]
    nc, nh = tb // chunk, d // HEAD_DIM
    ii = lax.broadcasted_iota(jnp.int32, (chunk, chunk), 0)
    jj = lax.broadcasted_iota(jnp.int32, (chunk, chunk), 1)
    incl, strict = ii >= jj, ii > jj
    tri = incl.astype(F32)
    eye = (ii == jj).astype(F32)
    rows = lambda c: slice(c * chunk, (c + 1) * chunk)
    lanes = lambda h: slice(h * HEAD_DIM, (h + 1) * HEAD_DIM)

    def split(x):
        return jnp.stack([x[rows(c), lanes(h)] for c in range(nc) for h in range(nh)])

    r, lw, k, v, kk, b = r_ref[0], lw_ref[0], k_ref[0], v_ref[0], kk_ref[0], b_ref[0]
    cums = [jnp.dot(tri, lw[rows(c)], precision=HIGHEST, preferred_element_type=F32) for c in range(nc)]
    cum = jnp.concatenate(cums, axis=0)
    total = jnp.concatenate([jnp.broadcast_to(cs[chunk - 1:], (chunk, d)) for cs in cums], axis=0)
    g_inv = jnp.exp(-cum)
    g_tail = jnp.exp(total - cum)
    kq, rq = split(kk * jnp.exp(cum - lw)), split(r * jnp.exp(cum))
    bi, ki = split(b * g_inv), split(k * g_inv)
    bh, kh = split(b * g_tail), split(k * g_tail)
    vs = split(v)
    decay = jnp.stack([jnp.exp(cs[chunk - 1:, lanes(h)]) for cs in cums for h in range(nh)])

    a_b = jnp.where(strict, _bmm_nt(kq, bi), 0.0)
    a_k = jnp.where(strict, _bmm_nt(kq, ki), 0.0)
    a_rb = jnp.where(incl, _bmm_nt(rq, bi), 0.0)
    a_rk = jnp.where(incl, _bmm_nt(rq, ki), 0.0)
    inv = eye - a_b
    power = a_b
    for _ in range(int(np.log2(chunk)) - 1):
        power = _bmm(power, power)
        inv = inv + _bmm(inv, power)
    w = _bmm(inv, kq)
    u0 = -_bmm(inv, _bmm(a_k, vs))
    qh = rq - _bmm(a_rb, w)
    y0 = _bmm(a_rb, u0) + _bmm(a_rk, vs)
    mp = _bmm_tn(w, bh)
    s_add = _bmm_tn(u0, bh) + _bmm_tn(vs, kh)

    state = s_ref[...]
    for c in range(nc):
        sl = slice(c * nh, (c + 1) * nh)
        y = _bmm_nt(qh[sl], state) + y0[sl]
        for h in range(nh):
            o_ref[0, rows(c), lanes(h)] = y[h]
        state = state * decay[sl] - _bmm(state, mp[sl]) + s_add[sl]
    s_ref[...] = state

    seg = seg_ref[...]
    y = o_ref[0]
    yc = y - _seg_sum(y, seg) * (1.0 / HEAD_DIM)
    var = _seg_sum(yc * yc, seg) * (1.0 / HEAD_DIM)
    y = yc * lax.rsqrt(var + RWKV_GN_EPS) * lnw_ref[...] + lnb_ref[...]
    y = y + _seg_sum(r * k * rk_ref[...], seg) * v
    o_ref[0] = y * g_ref[0]


def _wkv(r, lw, k, v, kk, b, g, ln_w, ln_b, r_k, seg_ones, *, tb):
    bsz, seq, d = r.shape
    tok = pl.BlockSpec((1, tb, d), lambda bi, t: (bi, t, 0))
    par = pl.BlockSpec((1, d), lambda bi, t: (0, 0))
    return pl.pallas_call(
        functools.partial(_wkv_kernel, chunk=WKV_CHUNK),
        grid=(bsz, seq // tb),
        in_specs=[tok] * 7 + [par] * 3 + [pl.BlockSpec(seg_ones.shape, lambda bi, t: (0, 0))],
        out_specs=tok,
        out_shape=jax.ShapeDtypeStruct((bsz, seq, d), F32),
        scratch_shapes=[pltpu.VMEM((d // HEAD_DIM, HEAD_DIM, HEAD_DIM), F32)],
        compiler_params=_params("parallel", "arbitrary"),
        name="wkv_scan",
    )(r, lw, k, v, kk, b, g, ln_w, ln_b, r_k, seg_ones)


def _attn_kernel(q_ref, kp_ref, kc_ref, vp_ref, vc_ref, bias_ref, o_ref, l_ref):
    first = pl.program_id(2) == 0
    q = q_ref[0, 0] * (HEAD_DIM ** -0.5)
    kc, vc = kc_ref[0, 0], vc_ref[0, 0]
    subs = q.shape[0] // Q_BLOCK
    rows = lambda j: slice(j * Q_BLOCK, (j + 1) * Q_BLOCK)
    lanes = lambda h: slice(h * HEAD_DIM, (h + 1) * HEAD_DIM)
    k_prev = [kp_ref[0, 0]] + [kc[rows(j)] for j in range(subs - 1)]
    v_prev = [vp_ref[0, 0]] + [vc[rows(j)] for j in range(subs - 1)]
    work = [(j, h) for j in range(subs) for h in range(HEADS_PER_GROUP)]
    scores = []
    for j, h in work:
        bias = bias_ref[h]
        bias_prev = bias[:, :Q_BLOCK]
        if j == 0:
            bias_prev = jnp.where(first, MASKED, bias_prev)
        qh = q[rows(j), lanes(h)]
        scores.append((_mm_nt(qh, k_prev[j][:, lanes(h)]) + bias_prev,
                       _mm_nt(qh, kc[rows(j), lanes(h)]) + bias[:, Q_BLOCK:]))
    tops = [jnp.max(jnp.maximum(sp, sc), axis=-1, keepdims=True) for sp, sc in scores]
    probs = [(jnp.exp(sp - m), jnp.exp(sc - m)) for (sp, sc), m in zip(scores, tops)]
    dens = [jnp.sum(pp + pc, axis=-1, keepdims=True) for pp, pc in probs]
    accs = [_mm(pp, v_prev[j][:, lanes(h)]) + _mm(pc, vc[rows(j), lanes(h)])
            for (pp, pc), (j, h) in zip(probs, work)]
    lane = lax.broadcasted_iota(jnp.int32, (Q_BLOCK, LANES), 1)
    for j in range(subs):
        lse = jnp.zeros((Q_BLOCK, LANES), F32)
        for h in range(HEADS_PER_GROUP):
            i = j * HEADS_PER_GROUP + h
            o_ref[0, 0, rows(j), lanes(h)] = accs[i] * (1.0 / dens[i])
            lse = jnp.where(lane == h, tops[i] + jnp.log(dens[i]), lse)
        l_ref[0, 0, rows(j), :] = lse


def _attn_bias(group):
    window, dilation = ATTN_GROUPS[group]
    n_back = window // dilation
    heads = np.arange(group * HEADS_PER_GROUP, (group + 1) * HEADS_PER_GROUP)
    slopes = (2.0 ** (-8.0 * (heads + 1) / ATTN_HEADS)).astype(np.float32)
    delta = (np.arange(Q_BLOCK)[:, None] + Q_BLOCK) - np.arange(2 * Q_BLOCK)[None, :]
    valid = (delta >= 0) & (delta <= n_back)
    alibi = -slopes[:, None, None] * (dilation * delta).astype(np.float32)[None]
    return jnp.asarray(np.where(valid[None], alibi, np.float32(MASKED)).astype(np.float32))


def _attn_group(qkv, group):
    bsz, dil, comp, _ = qkv.shape
    assert comp % ATTN_TQ == 0
    sub = ATTN_TQ // Q_BLOCK
    col = lambda which: (lambda bi, r, n: (bi, r, n, which))
    col_prev = lambda which: (lambda bi, r, n: (bi, r, jnp.maximum(n * sub - 1, 0), which))
    cur, prev = (1, 1, ATTN_TQ, ATTN_OUT_DIM), (1, 1, Q_BLOCK, ATTN_OUT_DIM)
    bias = _attn_bias(group)
    return pl.pallas_call(
        _attn_kernel,
        grid=(bsz, dil, comp // ATTN_TQ),
        in_specs=[pl.BlockSpec(cur, col(0)),
                  pl.BlockSpec(prev, col_prev(1)), pl.BlockSpec(cur, col(1)),
                  pl.BlockSpec(prev, col_prev(2)), pl.BlockSpec(cur, col(2)),
                  pl.BlockSpec(bias.shape, lambda bi, r, n: (0, 0, 0))],
        out_specs=[pl.BlockSpec(cur, col(0)),
                   pl.BlockSpec((1, 1, ATTN_TQ, LANES), col(0))],
        out_shape=[jax.ShapeDtypeStruct((bsz, dil, comp, ATTN_OUT_DIM), F32),
                   jax.ShapeDtypeStruct((bsz, dil, comp, LANES), F32)],
        compiler_params=_params("parallel", "parallel", "parallel"),
        name=f"dilated_attn_{group}",
    )(qkv, qkv, qkv, qkv, qkv, bias)


def _route(logits):
    lane = lax.broadcasted_iota(jnp.int32, logits.shape, 1)
    neg = -jnp.inf
    first_max = lambda vals, mx: jnp.min(jnp.where(vals == mx, lane, LANES), axis=-1, keepdims=True)
    is_grp = (lane >= N_EXPERTS) & (lane < N_EXPERTS + N_EXPERT_GROUPS)
    gl = jnp.where(is_grp, logits, neg)
    gmax = jnp.max(gl, axis=-1, keepdims=True)
    grp_w = 1.0 / jnp.sum(jnp.exp(gl - gmax), axis=-1, keepdims=True)
    grp = first_max(gl, gmax) - N_EXPERTS
    in_grp = (lane >= grp * EXPERTS_PER_GROUP) & (lane < (grp + 1) * EXPERTS_PER_GROUP)
    el = jnp.where(in_grp, logits, neg)
    v1 = jnp.max(el, axis=-1, keepdims=True)
    i1 = first_max(el, v1)
    el2 = jnp.where(lane == i1, neg, el)
    v2 = jnp.max(el2, axis=-1, keepdims=True)
    i2 = first_max(el2, v2)
    e2 = jnp.exp(v2 - v1)
    w1 = grp_w / (1.0 + e2)
    return jnp.where(lane == i1, w1, 0.0) + jnp.where(lane == i2, w1 * e2, 0.0)


def _merge_kernel(x_ref, ya_ref, o0_ref, o1_ref, o2_ref, l0_ref, l1_ref, l2_ref, gt_ref,
                  wa_ref, wb_ref, wo_ref, gain_ref, wr_ref, ex_ref, x1_ref, h_ref, comb_ref,
                  o_scr, l_scr):
    outs, lses = [], []
    for grp, (o_ref, l_ref) in enumerate(((o0_ref, l0_ref), (o1_ref, l1_ref), (o2_ref, l2_ref))):
        dil = o_ref.shape[1]
        if dil == 1:
            outs.append(o_ref[0, 0])
            lses.append(l_ref[0, 0])
            continue
        n = o_ref.shape[2]
        chunks = o_ref.shape[3] // LANES
        for r in range(dil):
            l_scr[grp - 1, pl.ds(r, n, stride=dil), :] = l_ref[0, r]
            for c in range(chunks):
                o_scr[grp - 1, c, pl.ds(r, n, stride=dil), :] = o_ref[0, r, :, c * LANES:(c + 1) * LANES]
        outs.append(jnp.concatenate([o_scr[grp - 1, c] for c in range(chunks)], axis=1))
        lses.append(l_scr[grp - 1])
    l0, l1, l2 = lses
    m = jnp.maximum(jnp.maximum(l0, l1), l2)
    e0, e1, e2 = jnp.exp(l0 - m), jnp.exp(l1 - m), jnp.exp(l2 - m)
    inv = 1.0 / (e0 + e1 + e2)
    spread = ex_ref[...]
    yb = (_seg_sum(e0 * inv, spread) * outs[0] + _seg_sum(e1 * inv, spread) * outs[1]
          + _seg_sum(e2 * inv, spread) * outs[2])
    gates = gt_ref[...]
    merged = (gates[:, :D_MODEL] * _mm(ya_ref[...], wa_ref[...])
              + gates[:, D_MODEL:] * _mm(yb, wb_ref[...]))
    x1 = x_ref[...] + _mm(merged, wo_ref[...])
    x1_ref[...] = x1
    h = _rms(x1, gain_ref[...])
    h_ref[...] = h.astype(BF16)
    logits = jnp.dot(h, wr_ref[...], precision=HIGHEST, preferred_element_type=F32)
    comb_ref[...] = _route(logits)


def _merge(x, ya, outs, lses, gates, wa, wb, wo, gain, wr, spread, seq, *, tm):
    t = x.shape[0]
    tps = seq // tm
    tok = lambda n: pl.BlockSpec((tm, n), lambda i: (i, 0))
    full = lambda a: pl.BlockSpec(a.shape, lambda i: (0, 0))
    res = lambda a: pl.BlockSpec((1, a.shape[1], tm // a.shape[1], a.shape[3]),
                                 lambda i: (i // tps, 0, i % tps, 0))
    dilated = len(ATTN_GROUPS) - 1
    return pl.pallas_call(
        _merge_kernel,
        grid=(t // tm,),
        in_specs=[tok(D_MODEL), tok(RWKV_DIM)] + [res(a) for a in outs] + [res(a) for a in lses] + [
                  tok(2 * D_MODEL), full(wa), full(wb), full(wo), full(gain), full(wr), full(spread)],
        out_specs=[tok(D_MODEL), tok(D_MODEL), tok(LANES)],
        out_shape=[jax.ShapeDtypeStruct((t, D_MODEL), F32), jax.ShapeDtypeStruct((t, D_MODEL), BF16),
                   jax.ShapeDtypeStruct((t, LANES), F32)],
        scratch_shapes=[pltpu.VMEM((dilated, ATTN_OUT_DIM // LANES, tm, LANES), F32),
                        pltpu.VMEM((dilated, tm, LANES), F32)],
        compiler_params=_params("parallel"),
        name="merge_route",
    )(x, ya, *outs, *lses, gates, wa, wb, wo, gain, wr, spread)


def _moe_kernel(h_ref, comb_ref, wg_ref, wu_ref, wd_ref, o_ref):
    e = pl.program_id(1)

    @pl.when(e == 0)
    def _():
        o_ref[...] = jnp.zeros_like(o_ref)

    h = h_ref[...]
    comb = comb_ref[...]
    lane = lax.broadcasted_iota(jnp.int32, comb.shape, 1)
    weight = jnp.sum(jnp.where(lane == e, comb, 0.0), axis=-1, keepdims=True)
    gate = _mm(h, wg_ref[0])
    up = _mm(h, wu_ref[0])
    act = gate * _sigmoid(gate) * up * weight
    o_ref[...] += _mm(act, wd_ref[0])


def _moe(h, comb, wg, wu, wd, *, tm):
    t = h.shape[0]
    return pl.pallas_call(
        _moe_kernel,
        grid=(t // tm, N_EXPERTS),
        in_specs=[pl.BlockSpec((tm, D_MODEL), lambda i, e: (i, 0)),
                  pl.BlockSpec((tm, LANES), lambda i, e: (i, 0)),
                  pl.BlockSpec((1, D_MODEL, D_FF_EXPERT), lambda i, e: (e, 0, 0)),
                  pl.BlockSpec((1, D_MODEL, D_FF_EXPERT), lambda i, e: (e, 0, 0)),
                  pl.BlockSpec((1, D_FF_EXPERT, D_MODEL), lambda i, e: (e, 0, 0))],
        out_specs=pl.BlockSpec((tm, D_MODEL), lambda i, e: (i, 0)),
        out_shape=jax.ShapeDtypeStruct((t, D_MODEL), F32),
        compiler_params=_params("parallel", "arbitrary"),
        name="moe_dense",
    )(h, comb, wg, wu, wd)


def _ple_kernel(x_ref, y_ref, p_ref, gp_ref, wg_ref, wp_ref, gf_ref, o_ref):
    x = x_ref[...] + y_ref[...]
    gate = _sigmoid(_mm(_rms(x, gp_ref[...]), wg_ref[...]))
    x = x + gate * _mm(p_ref[...], wp_ref[...])
    o_ref[...] = _rms(x, gf_ref[...])


def _ple(x, y, p, gain_ple, w_gate, w_proj, gain_final, *, tm):
    t = x.shape[0]
    tok = lambda n: pl.BlockSpec((tm, n), lambda i: (i, 0))
    full = lambda a: pl.BlockSpec(a.shape, lambda i: (0, 0))
    return pl.pallas_call(
        _ple_kernel,
        grid=(t // tm,),
        in_specs=[tok(D_MODEL), tok(D_MODEL), tok(PLE_DIM), full(gain_ple), full(w_gate), full(w_proj),
                  full(gain_final)],
        out_specs=tok(D_MODEL),
        out_shape=jax.ShapeDtypeStruct((t, D_MODEL), F32),
        compiler_params=_params("parallel"),
        name="ple_final",
    )(x, y, p, gain_ple, w_gate, w_proj, gain_final)


def _pad_rows(w, start, total):
    return jnp.zeros((total, w.shape[1]), w.dtype).at[start:start + w.shape[0]].set(w)


def _layer(x, p, seq, norm_mix, w_in, rwkv_mu, rwkv_w0, rwkv_w2, rwkv_a0, rwkv_a2, rwkv_g2, rwkv_k_k,
           rwkv_k_a, rwkv_r_k, rwkv_ln_w, rwkv_ln_b, w_proj_a, w_proj_b, w_out, norm_moe, w_router_group,
           w_router_expert, w_exp_gate, w_exp_up, w_exp_down, norm_ple, w_ple_gate, w_ple_proj, norm_final):
    t = x.shape[0]
    bsz = t // seq
    row = lambda a: a.reshape(1, -1)
    w_in16 = w_in.astype(BF16)
    c0, c1 = RWKV_COLS, RWKV_COLS + ATTN_COLS
    z = _norm_matmul(x, row(norm_mix), w_in16[:, :c0], F32, tm=512, tn=RWKV_COLS // 2)
    w_qkv = w_in16[:, c0:c1].reshape(D_MODEL, 3, len(ATTN_GROUPS), ATTN_OUT_DIM)
    w_qkv = jnp.swapaxes(w_qkv, 1, 2).reshape(D_MODEL, ATTN_COLS)
    qkv = _qkv_proj(x, row(norm_mix), w_qkv, seq, tm=512)
    gates = _norm_matmul(x, row(norm_mix), w_in16[:, c1:], F32, tm=512, tn=D_MODEL, sigmoid=True)

    seg = np.arange(RWKV_DIM) // HEAD_DIM
    seg_ones = jnp.asarray(seg[:, None] == seg[None, :], dtype=BF16)
    r, lw, k, v, kk, b, g = _rwkv_prep(
        z, seq, row(rwkv_mu), row(rwkv_w0), _pad_rows(rwkv_w2, 0, LORA_COLS), row(rwkv_a0),
        _pad_rows(rwkv_a2, W_LORA, LORA_COLS), _pad_rows(rwkv_g2, W_LORA + A_LORA, LORA_COLS),
        row(rwkv_k_k), row(rwkv_k_a), seg_ones, tm=512)
    tok3 = lambda a: a.reshape(bsz, seq, RWKV_DIM)
    ya = _wkv(tok3(r), tok3(lw), tok3(k), tok3(v), tok3(kk), tok3(b), tok3(g), row(rwkv_ln_w),
              row(rwkv_ln_b), row(rwkv_r_k), seg_ones, tb=256).reshape(t, RWKV_DIM)

    attn = [_attn_group(qkv[grp], grp) for grp in range(len(ATTN_GROUPS))]

    head_of_lane = np.arange(ATTN_OUT_DIM) // HEAD_DIM
    spread = jnp.asarray(np.arange(LANES)[:, None] == head_of_lane[None, :], dtype=BF16)
    w_route = jnp.concatenate(
        [jnp.moveaxis(w_router_expert, 0, 1).reshape(D_MODEL, N_EXPERTS), w_router_group,
         jnp.zeros((D_MODEL, LANES - N_EXPERTS - N_EXPERT_GROUPS), F32)], axis=1)
    x1, h_moe, comb = _merge(x, ya, [o for o, _ in attn], [l for _, l in attn], gates,
                             w_proj_a.astype(BF16), w_proj_b.astype(BF16), w_out.astype(BF16),
                             row(norm_moe), w_route, spread, seq, tm=512)
    y_moe = _moe(h_moe, comb,
                 w_exp_gate.reshape(N_EXPERTS, D_MODEL, D_FF_EXPERT),
                 w_exp_up.reshape(N_EXPERTS, D_MODEL, D_FF_EXPERT),
                 w_exp_down.reshape(N_EXPERTS, D_FF_EXPERT, D_MODEL), tm=2048)
    return _ple(x1, y_moe, p, row(norm_ple), w_ple_gate.astype(BF16), w_ple_proj.astype(BF16),
                row(norm_final), tm=512)


def kernel(x, p, norm_mix, w_in, rwkv_mu, rwkv_w0, rwkv_w2, rwkv_a0, rwkv_a2, rwkv_g2, rwkv_k_k, rwkv_k_a, rwkv_r_k, rwkv_ln_w, rwkv_ln_b, w_proj_a, w_proj_b, w_out, norm_moe, w_router_group, w_router_expert, w_exp_gate, w_exp_up, w_exp_down, norm_ple, w_ple_gate, w_ple_proj, norm_final):
    bsz, seq, d = x.shape
    depth = w_in.shape[0]
    assert depth == 1, "the final norm is fused into the (single) layer"
    out = _layer(x.reshape(bsz * seq, d).astype(F32), p[0].reshape(bsz * seq, PLE_DIM), seq,
                 norm_mix[0], w_in[0], rwkv_mu[0], rwkv_w0[0], rwkv_w2[0], rwkv_a0[0], rwkv_a2[0],
                 rwkv_g2[0], rwkv_k_k[0], rwkv_k_a[0], rwkv_r_k[0], rwkv_ln_w[0], rwkv_ln_b[0],
                 w_proj_a[0], w_proj_b[0], w_out[0], norm_moe[0], w_router_group[0], w_router_expert[0],
                 w_exp_gate[0], w_exp_up[0], w_exp_down[0], norm_ple[0], w_ple_gate[0], w_ple_proj[0],
                 norm_final)
    return out.reshape(bsz, seq, d)
```

```python
import functools

import numpy as np
import jax
import jax.numpy as jnp
from jax import lax
from jax.experimental import pallas as pl
from jax.experimental.pallas import tpu as pltpu

F32 = jnp.float32
BF16 = jnp.bfloat16
HIGHEST = lax.Precision.HIGHEST

D_MODEL = 1024
PLE_DIM = 256
HEAD_DIM = 64
RWKV_HEADS = 8
RWKV_DIM = RWKV_HEADS * HEAD_DIM
W_LORA, A_LORA, G_LORA = 64, 64, 128
LORA_COLS = W_LORA + A_LORA + G_LORA
ATTN_GROUPS = ((128, 1), (512, 4), (2048, 16))
HEADS_PER_GROUP = 8
ATTN_HEADS = HEADS_PER_GROUP * len(ATTN_GROUPS)
ATTN_DIM = ATTN_HEADS * HEAD_DIM
ATTN_OUT_DIM = HEADS_PER_GROUP * HEAD_DIM
Q_BLOCK = 128
ATTN_TQ = 256
RWKV_COLS = 3 * RWKV_DIM + LORA_COLS
ATTN_COLS = 3 * ATTN_DIM
N_EXPERT_GROUPS = 4
EXPERTS_PER_GROUP = 8
N_EXPERTS = N_EXPERT_GROUPS * EXPERTS_PER_GROUP
D_FF_EXPERT = 256
NORM_EPS = 1e-6
RWKV_GN_EPS = 64e-5
LANES = 128
MASKED = -1e30
WKV_CHUNK = 64
VMEM_LIMIT = 48 * 1024 * 1024


def _mm(a, b):
    return jnp.dot(a.astype(BF16), b.astype(BF16), preferred_element_type=F32)


def _mm_nt(a, b):
    return lax.dot_general(a.astype(BF16), b.astype(BF16), (((1,), (1,)), ((), ())),
                           preferred_element_type=F32)


def _mm_tn(a, b):
    return lax.dot_general(a.astype(BF16), b.astype(BF16), (((0,), (0,)), ((), ())),
                           preferred_element_type=F32)


def _mm_split(a, b):
    a_hi, b_hi = a.astype(BF16), b.astype(BF16)
    a_lo = (a - a_hi.astype(F32)).astype(BF16)
    b_lo = (b - b_hi.astype(F32)).astype(BF16)
    dot = functools.partial(jnp.dot, preferred_element_type=F32)
    return dot(a_hi, b_hi) + (dot(a_hi, b_lo) + dot(a_lo, b_hi))


def _sigmoid(x):
    return 1.0 / (1.0 + jnp.exp(-x))


def _rms(x, gain):
    return x * lax.rsqrt(jnp.mean(x * x, axis=-1, keepdims=True) + NORM_EPS) * gain


def _params(*sem):
    return pltpu.CompilerParams(dimension_semantics=sem, vmem_limit_bytes=VMEM_LIMIT)


def _norm_matmul_kernel(x_ref, g_ref, w_ref, o_ref, h_ref, *, sigmoid):
    @pl.when(pl.program_id(1) == 0)
    def _():
        h_ref[...] = _rms(x_ref[...], g_ref[...]).astype(BF16)

    acc = jnp.dot(h_ref[...], w_ref[...], preferred_element_type=F32)
    if sigmoid:
        acc = _sigmoid(acc)
    o_ref[...] = acc.astype(o_ref.dtype)


def _norm_matmul(x, gain, w, out_dtype, *, tm, tn, sigmoid=False):
    t, d = x.shape
    n = w.shape[1]
    return pl.pallas_call(
        functools.partial(_norm_matmul_kernel, sigmoid=sigmoid),
        grid=(t // tm, n // tn),
        in_specs=[pl.BlockSpec((tm, d), lambda i, j: (i, 0)),
                  pl.BlockSpec((1, d), lambda i, j: (0, 0)),
                  pl.BlockSpec((d, tn), lambda i, j: (0, j))],
        out_specs=pl.BlockSpec((tm, tn), lambda i, j: (i, j)),
        out_shape=jax.ShapeDtypeStruct((t, n), out_dtype),
        scratch_shapes=[pltpu.VMEM((tm, d), BF16)],
        compiler_params=_params("parallel", "arbitrary"),
        name="norm_matmul",
    )(x, gain, w)


def _qkv_proj_kernel(x_ref, g_ref, w_ref, o0_ref, o1_ref, o2_ref, h_ref, acc_ref):
    j = pl.program_id(1)

    @pl.when(j == 0)
    def _():
        h_ref[...] = _rms(x_ref[...], g_ref[...]).astype(BF16)

    acc = jnp.dot(h_ref[...], w_ref[...], preferred_element_type=F32)
    chunks = acc.shape[1] // LANES
    col = lambda c: slice(c * LANES, (c + 1) * LANES)
    for grp, o_ref in enumerate((o0_ref, o1_ref, o2_ref)):
        dil = ATTN_GROUPS[grp][1]

        @pl.when(j == grp)
        def _(o_ref=o_ref, dil=dil):
            if dil == 1:
                o_ref[0, 0] = acc.astype(BF16)
                return
            for c in range(chunks):
                acc_ref[c] = acc[:, col(c)]
            n = acc.shape[0] // dil
            for r in range(dil):
                for c in range(chunks):
                    o_ref[0, r, :, col(c)] = acc_ref[c, pl.ds(r, n, stride=dil), :].astype(BF16)


def _qkv_proj(x, gain, w, seq, *, tm):
    t, d = x.shape
    tps = seq // tm
    tn = 3 * ATTN_OUT_DIM
    out_specs, out_shapes = [], []
    for _, dil in ATTN_GROUPS:
        out_specs.append(pl.BlockSpec((1, dil, tm // dil, tn), lambda i, j: (i // tps, 0, i % tps, 0)))
        out_shapes.append(jax.ShapeDtypeStruct((t // seq, dil, seq // dil, tn), BF16))
    return pl.pallas_call(
        _qkv_proj_kernel,
        grid=(t // tm, len(ATTN_GROUPS)),
        in_specs=[pl.BlockSpec((tm, d), lambda i, j: (i, 0)),
                  pl.BlockSpec((1, d), lambda i, j: (0, 0)),
                  pl.BlockSpec((d, tn), lambda i, j: (0, j))],
        out_specs=out_specs,
        out_shape=out_shapes,
        scratch_shapes=[pltpu.VMEM((tm, d), BF16), pltpu.VMEM((tn // LANES, tm, LANES), F32)],
        compiler_params=_params("parallel", "arbitrary"),
        name="qkv_proj",
    )(x, gain, w)


def _seg_sum(x, seg_ones):
    hi = x.astype(BF16)
    lo = (x - hi.astype(F32)).astype(BF16)
    return (jnp.dot(hi, seg_ones, preferred_element_type=F32)
            + jnp.dot(lo, seg_ones, preferred_element_type=F32))


def _rwkv_inputs(z, last, mu, w0, w2, a0, a2, g2, k_k, k_a, seg):
    row = lax.broadcasted_iota(jnp.int32, z.shape, 0)
    prev = jnp.where(row == 0, last, pltpu.roll(z, 1, axis=0))
    zs = z + (prev - z) * mu
    d = RWKV_DIM
    r, k, v = zs[:, :d], zs[:, d:2 * d], zs[:, 2 * d:3 * d]
    lora = zs[:, 3 * d:]
    lane = lax.broadcasted_iota(jnp.int32, lora.shape, 1)
    lora = jnp.where(lane < W_LORA, jnp.tanh(lora),
                     jnp.where(lane < W_LORA + A_LORA, lora, _sigmoid(lora)))
    w_lin = w0 + _mm_split(lora, w2)
    a_lin = a0 + _mm_split(lora, a2)
    g = _mm_split(lora, g2)
    u = -w_lin
    softplus = jnp.maximum(u, 0.0) + jnp.log(1.0 + jnp.exp(-jnp.abs(u)))
    lw = -jnp.exp(-softplus - 0.5)
    a = _sigmoid(a_lin)
    kk = k * k_k
    kk = kk * lax.rsqrt(_seg_sum(kk * kk, seg) + 1e-12)
    k = k * (1.0 + (a - 1.0) * k_a)
    return r, lw, k, v, kk, kk * a, g


def _bmm(a, b):
    return lax.dot_general(a.astype(BF16), b.astype(BF16), (((2,), (1,)), ((0,), (0,))),
                           preferred_element_type=F32)


def _bmm_nt(a, b):
    return lax.dot_general(a.astype(BF16), b.astype(BF16), (((2,), (2,)), ((0,), (0,))),
                           preferred_element_type=F32)


def _bmm_tn(a, b):
    return lax.dot_general(a.astype(BF16), b.astype(BF16), (((1,), (1,)), ((0,), (0,))),
                           preferred_element_type=F32)


def _rwkv_kernel(x_ref, gain_ref, wz_ref, mu_ref, w0_ref, w2_ref, a0_ref, a2_ref, g2_ref, kk_ref, ka_ref,
                 lnw_ref, lnb_ref, rk_ref, seg_ref, o_ref, s_ref, zlast_ref, *, chunk):
    @pl.when(pl.program_id(1) == 0)
    def _():
        s_ref[...] = jnp.zeros_like(s_ref)
        zlast_ref[...] = jnp.zeros_like(zlast_ref)

    tb = x_ref.shape[1]
    d = RWKV_DIM
    z = jnp.dot(_rms(x_ref[0], gain_ref[...]).astype(BF16), wz_ref[...], preferred_element_type=F32)
    last = zlast_ref[7:8, :]
    zlast_ref[...] = z[tb - 8:, :]
    r, lw, k, v, kk, b, g = _rwkv_inputs(z, last, mu_ref[...], w0_ref[...], w2_ref[...], a0_ref[...],
                                         a2_ref[...], g2_ref[...], kk_ref[...], ka_ref[...], seg_ref[...])
    nc, nh = tb // chunk, d // HEAD_DIM
    ii = lax.broadcasted_iota(jnp.int32, (chunk, chunk), 0)
    jj = lax.broadcasted_iota(jnp.int32, (chunk, chunk), 1)
    incl, strict = ii >= jj, ii > jj
    tri = incl.astype(F32)
    eye = (ii == jj).astype(F32)
    rows = lambda c: slice(c * chunk, (c + 1) * chunk)
    lanes = lambda h: slice(h * HEAD_DIM, (h + 1) * HEAD_DIM)

    def split(x):
        return jnp.stack([x[rows(c), lanes(h)] for c in range(nc) for h in range(nh)])

    cums =[jnp.dot(tri, lw[rows(c)], precision=HIGHEST, preferred_element_type=F32) for c in range(nc)]
    cum = jnp.concatenate(cums, axis=0)
    total = jnp.concatenate([jnp.broadcast_to(cs[chunk - 1:], (chunk, d)) for cs in cums], axis=0)
    g_inv = jnp.exp(-cum)
    g_tail = jnp.exp(total - cum)
    kq, rq = split(kk * jnp.exp(cum - lw)), split(r * jnp.exp(cum))
    bi, ki = split(b * g_inv), split(k * g_inv)
    bh, kh = split(b * g_tail), split(k * g_tail)
    vs = split(v)
    decay = jnp.stack([jnp.exp(cs[chunk - 1:, lanes(h)]) for cs in cums for h in range(nh)])

    a_b = jnp.where(strict, _bmm_nt(kq, bi), 0.0)
    a_k = jnp.where(strict, _bmm_nt(kq, ki), 0.0)
    a_rb = jnp.where(incl, _bmm_nt(rq, bi), 0.0)
    a_rk = jnp.where(incl, _bmm_nt(rq, ki), 0.0)
    inv = eye - a_b
    power = a_b
    for _ in range(int(np.log2(chunk)) - 1):
        power = _bmm(power, power)
        inv = inv + _bmm(inv, power)
    w = _bmm(inv, kq)
    u0 = -_bmm(inv, _bmm(a_k, vs))
    qh = rq - _bmm(a_rb, w)
    y0 = _bmm(a_rb, u0) + _bmm(a_rk, vs)
    mp = _bmm_tn(w, bh)
    s_add = _bmm_tn(u0, bh) + _bmm_tn(vs, kh)

    state = s_ref[...]
    for c in range(nc):
        sl = slice(c * nh, (c + 1) * nh)
        y = _bmm_nt(qh[sl], state) + y0[sl]
        for h in range(nh):
            o_ref[0, rows(c), lanes(h)] = y[h]
        state = state * decay[sl] - _bmm(state, mp[sl]) + s_add[sl]
    s_ref[...] = state

    seg = seg_ref[...]
    y = o_ref[0]
    yc = y - _seg_sum(y, seg) * (1.0 / HEAD_DIM)
    var = _seg_sum(yc * yc, seg) * (1.0 / HEAD_DIM)
    y = yc * lax.rsqrt(var + RWKV_GN_EPS) * lnw_ref[...] + lnb_ref[...]
    y = y + _seg_sum(r * k * rk_ref[...], seg) * v
    o_ref[0] = y * g


def _rwkv_branch(x, gain, wz, mu, w0, w2p, a0, a2p, g2p, k_k, k_a, ln_w, ln_b, r_k, seg_ones, *, tb):
    bsz, seq, dm = x.shape
    d = RWKV_DIM
    full = lambda a: pl.BlockSpec(a.shape, lambda bi, t: (0,) * a.ndim)
    params = (gain, wz, mu, w0, w2p, a0, a2p, g2p, k_k, k_a, ln_w, ln_b, r_k, seg_ones)
    return pl.pallas_call(
        functools.partial(_rwkv_kernel, chunk=WKV_CHUNK),
        grid=(bsz, seq // tb),
        in_specs=[pl.BlockSpec((1, tb, dm), lambda bi, t: (bi, t, 0))] + [full(a) for a in params],
        out_specs=pl.BlockSpec((1, tb, d), lambda bi, t: (bi, t, 0)),
        out_shape=jax.ShapeDtypeStruct((bsz, seq, d), F32),
        scratch_shapes=[pltpu.VMEM((d // HEAD_DIM, HEAD_DIM, HEAD_DIM), F32),
                        pltpu.VMEM((8, RWKV_COLS), F32)],
        compiler_params=_params("parallel", "arbitrary"),
        name="rwkv_branch",
    )(x, *params)


def _attn_kernel(q_ref, kp_ref, kc_ref, vp_ref, vc_ref, bias_ref, o_ref, l_ref):
    first = pl.program_id(2) == 0
    q = q_ref[0, 0] * (HEAD_DIM ** -0.5)
    kc, vc = kc_ref[0, 0], vc_ref[0, 0]
    subs = q.shape[0] // Q_BLOCK
    rows = lambda j: slice(j * Q_BLOCK, (j + 1) * Q_BLOCK)
    lanes = lambda h: slice(h * HEAD_DIM, (h + 1) * HEAD_DIM)
    k_prev = [kp_ref[0, 0]] + [kc[rows(j)] for j in range(subs - 1)]
    v_prev = [vp_ref[0, 0]] + [vc[rows(j)] for j in range(subs - 1)]
    work = [(j, h) for j in range(subs) for h in range(HEADS_PER_GROUP)]
    scores = []
    for j, h in work:
        bias = bias_ref[h]
        bias_prev = bias[:, :Q_BLOCK]
        if j == 0:
            bias_prev = jnp.where(first, MASKED, bias_prev)
        qh = q[rows(j), lanes(h)]
        scores.append((_mm_nt(qh, k_prev[j][:, lanes(h)]) + bias_prev,
                       _mm_nt(qh, kc[rows(j), lanes(h)]) + bias[:, Q_BLOCK:]))
    tops = [jnp.max(jnp.maximum(sp, sc), axis=-1, keepdims=True) for sp, sc in scores]
    probs = [(jnp.exp(sp - m), jnp.exp(sc - m)) for (sp, sc), m in zip(scores, tops)]
    dens = [jnp.sum(pp + pc, axis=-1, keepdims=True) for pp, pc in probs]
    accs = [_mm(pp, v_prev[j][:, lanes(h)]) + _mm(pc, vc[rows(j), lanes(h)])
            for (pp, pc), (j, h) in zip(probs, work)]
    lane = lax.broadcasted_iota(jnp.int32, (Q_BLOCK, LANES), 1)
    for j in range(subs):
        lse = jnp.zeros((Q_BLOCK, LANES), F32)
        for h in range(HEADS_PER_GROUP):
            i = j * HEADS_PER_GROUP + h
            o_ref[0, 0, rows(j), lanes(h)] = accs[i] * (1.0 / dens[i])
            lse = jnp.where(lane == h, tops[i] + jnp.log(dens[i]), lse)
        l_ref[0, 0, rows(j), :] = lse


def _attn_bias(group):
    window, dilation = ATTN_GROUPS[group]
    n_back = window // dilation
    heads = np.arange(group * HEADS_PER_GROUP, (group + 1) * HEADS_PER_GROUP)
    slopes = (2.0 ** (-8.0 * (heads + 1) / ATTN_HEADS)).astype(np.float32)
    delta = (np.arange(Q_BLOCK)[:, None] + Q_BLOCK) - np.arange(2 * Q_BLOCK)[None, :]
    valid = (delta >= 0) & (delta <= n_back)
    alibi = -slopes[:, None, None] * (dilation * delta).astype(np.float32)[None]
    return jnp.asarray(np.where(valid[None], alibi, np.float32(MASKED)).astype(np.float32))


def _attn_group(qkv, group):
    bsz, dil, comp, _ = qkv.shape
    assert comp % ATTN_TQ == 0
    sub = ATTN_TQ // Q_BLOCK
    col = lambda which: (lambda bi, r, n: (bi, r, n, which))
    col_prev = lambda which: (lambda bi, r, n: (bi, r, jnp.maximum(n * sub - 1, 0), which))
    cur, prev = (1, 1, ATTN_TQ, ATTN_OUT_DIM), (1, 1, Q_BLOCK, ATTN_OUT_DIM)
    bias = _attn_bias(group)
    return pl.pallas_call(
        _attn_kernel,
        grid=(bsz, dil, comp // ATTN_TQ),
        in_specs=[pl.BlockSpec(cur, col(0)),
                  pl.BlockSpec(prev, col_prev(1)), pl.BlockSpec(cur, col(1)),
                  pl.BlockSpec(prev, col_prev(2)), pl.BlockSpec(cur, col(2)),
                  pl.BlockSpec(bias.shape, lambda bi, r, n: (0, 0, 0))],
        out_specs=[pl.BlockSpec(cur, col(0)),
                   pl.BlockSpec((1, 1, ATTN_TQ, LANES), col(0))],
        out_shape=[jax.ShapeDtypeStruct((bsz, dil, comp, ATTN_OUT_DIM), F32),
                   jax.ShapeDtypeStruct((bsz, dil, comp, LANES), F32)],
        compiler_params=_params("parallel", "parallel", "parallel"),
        name=f"dilated_attn_{group}",
    )(qkv, qkv, qkv, qkv, qkv, bias)


def _route(logits):
    lane = lax.broadcasted_iota(jnp.int32, logits.shape, 1)
    neg = -jnp.inf
    first_max = lambda vals, mx: jnp.min(jnp.where(vals == mx, lane, LANES), axis=-1, keepdims=True)
    is_grp = (lane >= N_EXPERTS) & (lane < N_EXPERTS + N_EXPERT_GROUPS)
    gl = jnp.where(is_grp, logits, neg)
    gmax = jnp.max(gl, axis=-1, keepdims=True)
    grp_w = 1.0 / jnp.sum(jnp.exp(gl - gmax), axis=-1, keepdims=True)
    grp = first_max(gl, gmax) - N_EXPERTS
    in_grp = (lane >= grp * EXPERTS_PER_GROUP) & (lane < (grp + 1) * EXPERTS_PER_GROUP)
    el = jnp.where(in_grp, logits, neg)
    v1 = jnp.max(el, axis=-1, keepdims=True)
    i1 = first_max(el, v1)
    el2 = jnp.where(lane == i1, neg, el)
    v2 = jnp.max(el2, axis=-1, keepdims=True)
    i2 = first_max(el2, v2)
    e2 = jnp.exp(v2 - v1)
    w1 = grp_w / (1.0 + e2)
    return jnp.where(lane == i1, w1, 0.0) + jnp.where(lane == i2, w1 * e2, 0.0)


def _merge_kernel(x_ref, ya_ref, o0_ref, o1_ref, o2_ref, l0_ref, l1_ref, l2_ref, gt_ref,
                  wa_ref, wb_ref, wo_ref, gain_ref, wr_ref, ex_ref, x1_ref, h_ref, comb_ref,
                  o_scr, l_scr):
    outs, lses = [], []
    for grp, (o_ref, l_ref) in enumerate(((o0_ref, l0_ref), (o1_ref, l1_ref), (o2_ref, l2_ref))):
        dil = o_ref.shape[1]
        if dil == 1:
            outs.append(o_ref[0, 0])
            lses.append(l_ref[0, 0])
            continue
        n = o_ref.shape[2]
        chunks = o_ref.shape[3] // LANES
        for r in range(dil):
            l_scr[grp - 1, pl.ds(r, n, stride=dil), :] = l_ref[0, r]
            for c in range(chunks):
                o_scr[grp - 1, c, pl.ds(r, n, stride=dil), :] = o_ref[0, r, :, c * LANES:(c + 1) * LANES]
        outs.append(jnp.concatenate([o_scr[grp - 1, c] for c in range(chunks)], axis=1))
        lses.append(l_scr[grp - 1])
    l0, l1, l2 = lses
    m = jnp.maximum(jnp.maximum(l0, l1), l2)
    e0, e1, e2 = jnp.exp(l0 - m), jnp.exp(l1 - m), jnp.exp(l2 - m)
    inv = 1.0 / (e0 + e1 + e2)
    spread = ex_ref[...]
    yb = (_seg_sum(e0 * inv, spread) * outs[0] + _seg_sum(e1 * inv, spread) * outs[1]
          + _seg_sum(e2 * inv, spread) * outs[2])
    gates = gt_ref[...]
    merged = (gates[:, :D_MODEL] * _mm(ya_ref[...], wa_ref[...])
              + gates[:, D_MODEL:] * _mm(yb, wb_ref[...]))
    x1 = x_ref[...] + _mm(merged, wo_ref[...])
    x1_ref[...] = x1
    h = _rms(x1, gain_ref[...])
    h_ref[...] = h.astype(BF16)
    logits = _mm_split(h, wr_ref[...])
    comb_ref[...] = _route(logits)


def _merge(x, ya, outs, lses, gates, wa, wb, wo, gain, wr, spread, seq, *, tm):
    t = x.shape[0]
    tps = seq // tm
    tok = lambda n: pl.BlockSpec((tm, n), lambda i: (i, 0))
    full = lambda a: pl.BlockSpec(a.shape, lambda i: (0, 0))
    res = lambda a: pl.BlockSpec((1, a.shape[1], tm // a.shape[1], a.shape[3]),
                                 lambda i: (i // tps, 0, i % tps, 0))
    dilated = len(ATTN_GROUPS) - 1
    return pl.pallas_call(
        _merge_kernel,
        grid=(t // tm,),
        in_specs=[tok(D_MODEL), tok(RWKV_DIM)] + [res(a) for a in outs] + [res(a) for a in lses] + [
                  tok(2 * D_MODEL), full(wa), full(wb), full(wo), full(gain), full(wr), full(spread)],
        out_specs=[tok(D_MODEL), tok(D_MODEL), tok(LANES)],
        out_shape=[jax.ShapeDtypeStruct((t, D_MODEL), F32), jax.ShapeDtypeStruct((t, D_MODEL), BF16),
                   jax.ShapeDtypeStruct((t, LANES), F32)],
        scratch_shapes=[pltpu.VMEM((dilated, ATTN_OUT_DIM // LANES, tm, LANES), F32),
                        pltpu.VMEM((dilated, tm, LANES), F32)],
        compiler_params=_params("parallel"),
        name="merge_route",
    )(x, ya, *outs, *lses, gates, wa, wb, wo, gain, wr, spread)


def _moe_kernel(h_ref, comb_ref, wg_ref, wu_ref, wd_ref, o_ref):
    e = pl.program_id(1)

    @pl.when(e == 0)
    def _():
        o_ref[...] = jnp.zeros_like(o_ref)

    h = h_ref[...]
    comb = comb_ref[...]
    lane = lax.broadcasted_iota(jnp.int32, comb.shape, 1)
    weight = jnp.sum(jnp.where(lane == e, comb, 0.0), axis=-1, keepdims=True)
    gate = _mm(h, wg_ref[0])
    up = _mm(h, wu_ref[0])
    act = gate * _sigmoid(gate) * up * weight
    o_ref[...] += _mm(act, wd_ref[0])


def _moe(h, comb, wg, wu, wd, *, tm):
    t = h.shape[0]
    return pl.pallas_call(
        _moe_kernel,
        grid=(t // tm, N_EXPERTS),
        in_specs=[pl.BlockSpec((tm, D_MODEL), lambda i, e: (i, 0)),
                  pl.BlockSpec((tm, LANES), lambda i, e: (i, 0)),
                  pl.BlockSpec((1, D_MODEL, D_FF_EXPERT), lambda i, e: (e, 0, 0)),
                  pl.BlockSpec((1, D_MODEL, D_FF_EXPERT), lambda i, e: (e, 0, 0)),
                  pl.BlockSpec((1, D_FF_EXPERT, D_MODEL), lambda i, e: (e, 0, 0))],
        out_specs=pl.BlockSpec((tm, D_MODEL), lambda i, e: (i, 0)),
        out_shape=jax.ShapeDtypeStruct((t, D_MODEL), F32),
        compiler_params=_params("parallel", "arbitrary"),
        name="moe_dense",
    )(h, comb, wg, wu, wd)


def _ple_kernel(x_ref, y_ref, p_ref, gp_ref, wg_ref, wp_ref, gf_ref, o_ref):
    x = x_ref[...] + y_ref[...]
    gate = _sigmoid(_mm(_rms(x, gp_ref[...]), wg_ref[...]))
    x = x + gate * _mm(p_ref[...], wp_ref[...])
    o_ref[...] = _rms(x, gf_ref[...])


def _ple(x, y, p, gain_ple, w_gate, w_proj, gain_final, *, tm):
    t = x.shape[0]
    tok = lambda n: pl.BlockSpec((tm, n), lambda i: (i, 0))
    full = lambda a: pl.BlockSpec(a.shape, lambda i: (0, 0))
    return pl.pallas_call(
        _ple_kernel,
        grid=(t // tm,),
        in_specs=[tok(D_MODEL), tok(D_MODEL), tok(PLE_DIM), full(gain_ple), full(w_gate), full(w_proj),
                  full(gain_final)],
        out_specs=tok(D_MODEL),
        out_shape=jax.ShapeDtypeStruct((t, D_MODEL), F32),
        compiler_params=_params("parallel"),
        name="ple_final",
    )(x, y, p, gain_ple, w_gate, w_proj, gain_final)


def _pad_rows(w, start, total):
    return jnp.zeros((total, w.shape[1]), w.dtype).at[start:start + w.shape[0]].set(w)


def _layer(x, p, seq, norm_mix, w_in, rwkv_mu, rwkv_w0, rwkv_w2, rwkv_a0, rwkv_a2, rwkv_g2, rwkv_k_k,
           rwkv_k_a, rwkv_r_k, rwkv_ln_w, rwkv_ln_b, w_proj_a, w_proj_b, w_out, norm_moe, w_router_group,
           w_router_expert, w_exp_gate, w_exp_up, w_exp_down, norm_ple, w_ple_gate, w_ple_proj, norm_final):
    t = x.shape[0]
    bsz = t // seq
    row = lambda a: a.reshape(1, -1)
    w_in16 = w_in.astype(BF16)
    c0, c1 = RWKV_COLS, RWKV_COLS + ATTN_COLS
    w_qkv = w_in16[:, c0:c1].reshape(D_MODEL, 3, len(ATTN_GROUPS), ATTN_OUT_DIM)
    w_qkv = jnp.swapaxes(w_qkv, 1, 2).reshape(D_MODEL, ATTN_COLS)
    qkv = _qkv_proj(x, row(norm_mix), w_qkv, seq, tm=1024)
    gates = _norm_matmul(x, row(norm_mix), w_in16[:, c1:], BF16, tm=1024, tn=D_MODEL, sigmoid=True)

    seg = np.arange(RWKV_DIM) // HEAD_DIM
    seg_ones = jnp.asarray(seg[:, None] == seg[None, :], dtype=BF16)
    ya = _rwkv_branch(
        x.reshape(bsz, seq, D_MODEL), row(norm_mix), w_in16[:, :c0], row(rwkv_mu), row(rwkv_w0),
        _pad_rows(rwkv_w2, 0, LORA_COLS), row(rwkv_a0), _pad_rows(rwkv_a2, W_LORA, LORA_COLS),
        _pad_rows(rwkv_g2, W_LORA + A_LORA, LORA_COLS), row(rwkv_k_k), row(rwkv_k_a), row(rwkv_ln_w),
        row(rwkv_ln_b), row(rwkv_r_k), seg_ones, tb=256).reshape(t, RWKV_DIM)

    attn = [_attn_group(qkv[grp], grp) for grp in range(len(ATTN_GROUPS))]

    head_of_lane = np.arange(ATTN_OUT_DIM) // HEAD_DIM
    spread = jnp.asarray(np.arange(LANES)[:, None] == head_of_lane[None, :], dtype=BF16)
    w_route = jnp.concatenate(
        [jnp.moveaxis(w_router_expert, 0, 1).reshape(D_MODEL, N_EXPERTS), w_router_group,
         jnp.zeros((D_MODEL, LANES - N_EXPERTS - N_EXPERT_GROUPS), F32)], axis=1)
    x1, h_moe, comb = _merge(x, ya, [o for o, _ in attn], [l for _, l in attn], gates,
                             w_proj_a.astype(BF16), w_proj_b.astype(BF16), w_out.astype(BF16),
                             row(norm_moe), w_route, spread, seq, tm=512)
    y_moe = _moe(h_moe, comb,
                 w_exp_gate.reshape(N_EXPERTS, D_MODEL, D_FF_EXPERT),
                 w_exp_up.reshape(N_EXPERTS, D_MODEL, D_FF_EXPERT),
                 w_exp_down.reshape(N_EXPERTS, D_FF_EXPERT, D_MODEL), tm=2048)
    return _ple(x1, y_moe, p, row(norm_ple), w_ple_gate.astype(BF16), w_ple_proj.astype(BF16),
                row(norm_final), tm=512)


def kernel(x, p, norm_mix, w_in, rwkv_mu, rwkv_w0, rwkv_w2, rwkv_a0, rwkv_a2, rwkv_g2, rwkv_k_k, rwkv_k_a, rwkv_r_k, rwkv_ln_w, rwkv_ln_b, w_proj_a, w_proj_b, w_out, norm_moe, w_router_group, w_router_expert, w_exp_gate, w_exp_up, w_exp_down, norm_ple, w_ple_gate, w_ple_proj, norm_final):
    bsz, seq, d = x.shape
    depth = w_in.shape[0]
    assert depth == 1, "the final norm is fused into the (single) layer"
    out = _layer(x.reshape(bsz * seq, d).astype(F32), p[0].reshape(bsz * seq, PLE_DIM), seq,
                 norm_mix[0], w_in[0], rwkv_mu[0], rwkv_w0[0], rwkv_w2[0], rwkv_a0[0], rwkv_a2[0],
                 rwkv_g2[0], rwkv_k_k[0], rwkv_k_a[0], rwkv_r_k[0], rwkv_ln_w[0], rwkv_ln_b[0],
                 w_proj_a[0], w_proj_b[0], w_out[0], norm_moe[0], w_router_group[0], w_router_expert[0],
                 w_exp_gate[0], w_exp_up[0], w_exp_down[0], norm_ple[0], w_ple_gate[0], w_ple_proj[0],
                 norm_final)
    return out.reshape(bsz, seq, d)
```

```python
import functools

import numpy as np
import jax
import jax.numpy as jnp
from jax import lax
from jax.experimental import pallas as pl
from jax.experimental.pallas import tpu as pltpu

F32 = jnp.float32
BF16 = jnp.bfloat16
HIGHEST = lax.Precision.HIGHEST

D_MODEL = 1024
PLE_DIM = 256
HEAD_DIM = 64
RWKV_HEADS = 8
RWKV_DIM = RWKV_HEADS * HEAD_DIM
W_LORA, A_LORA, G_LORA = 64, 64, 128
LORA_COLS = W_LORA + A_LORA + G_LORA
ATTN_GROUPS = ((128, 1), (512, 4), (2048, 16))
HEADS_PER_GROUP = 8
ATTN_HEADS = HEADS_PER_GROUP * len(ATTN_GROUPS)
ATTN_DIM = ATTN_HEADS * HEAD_DIM
ATTN_OUT_DIM = HEADS_PER_GROUP * HEAD_DIM
Q_BLOCK = 128
ATTN_TQ = 256
RWKV_COLS = 3 * RWKV_DIM + LORA_COLS
ATTN_COLS = 3 * ATTN_DIM
N_EXPERT_GROUPS = 4
EXPERTS_PER_GROUP = 8
N_EXPERTS = N_EXPERT_GROUPS * EXPERTS_PER_GROUP
D_FF_EXPERT = 256
NORM_EPS = 1e-6
RWKV_GN_EPS = 64e-5
LANES = 128
MASKED = -1e30
WKV_CHUNK = 64
VMEM_LIMIT = 48 * 1024 * 1024
MOE_TILE = 256


def _mm(a, b):
    return jnp.dot(a.astype(BF16), b.astype(BF16), preferred_element_type=F32)


def _mm_nt(a, b):
    return lax.dot_general(a.astype(BF16), b.astype(BF16), (((1,), (1,)), ((), ())),
                           preferred_element_type=F32)


def _mm_tn(a, b):
    return lax.dot_general(a.astype(BF16), b.astype(BF16), (((0,), (0,)), ((), ())),
                           preferred_element_type=F32)


def _mm_split(a, b):
    a_hi, b_hi = a.astype(BF16), b.astype(BF16)
    a_lo = (a - a_hi.astype(F32)).astype(BF16)
    b_lo = (b - b_hi.astype(F32)).astype(BF16)
    dot = functools.partial(jnp.dot, preferred_element_type=F32)
    return dot(a_hi, b_hi) + (dot(a_hi, b_lo) + dot(a_lo, b_hi))


def _sigmoid(x):
    return 1.0 / (1.0 + jnp.exp(-x))


def _rms(x, gain):
    return x * lax.rsqrt(jnp.mean(x * x, axis=-1, keepdims=True) + NORM_EPS) * gain


def _params(*sem):
    return pltpu.CompilerParams(dimension_semantics=sem, vmem_limit_bytes=VMEM_LIMIT)


def _norm_matmul_kernel(x_ref, g_ref, w_ref, o_ref, h_ref, *, sigmoid):
    @pl.when(pl.program_id(1) == 0)
    def _():
        h_ref[...] = _rms(x_ref[...], g_ref[...]).astype(BF16)

    acc = jnp.dot(h_ref[...], w_ref[...], preferred_element_type=F32)
    if sigmoid:
        acc = _sigmoid(acc)
    o_ref[...] = acc.astype(o_ref.dtype)


def _norm_matmul(x, gain, w, out_dtype, *, tm, tn, sigmoid=False):
    t, d = x.shape
    n = w.shape[1]
    return pl.pallas_call(
        functools.partial(_norm_matmul_kernel, sigmoid=sigmoid),
        grid=(t // tm, n // tn),
        in_specs=[pl.BlockSpec((tm, d), lambda i, j: (i, 0)),
                  pl.BlockSpec((1, d), lambda i, j: (0, 0)),
                  pl.BlockSpec((d, tn), lambda i, j: (0, j))],
        out_specs=pl.BlockSpec((tm, tn), lambda i, j: (i, j)),
        out_shape=jax.ShapeDtypeStruct((t, n), out_dtype),
        scratch_shapes=[pltpu.VMEM((tm, d), BF16)],
        compiler_params=_params("parallel", "arbitrary"),
        name="norm_matmul",
    )(x, gain, w)


def _qkv_proj_kernel(x_ref, g_ref, w_ref, o0_ref, o1_ref, o2_ref, h_ref, acc_ref):
    j = pl.program_id(1)

    @pl.when(j == 0)
    def _():
        h_ref[...] = _rms(x_ref[...], g_ref[...]).astype(BF16)

    acc = jnp.dot(h_ref[...], w_ref[...], preferred_element_type=F32)
    chunks = acc.shape[1] // LANES
    col = lambda c: slice(c * LANES, (c + 1) * LANES)
    for grp, o_ref in enumerate((o0_ref, o1_ref, o2_ref)):
        dil = ATTN_GROUPS[grp][1]

        @pl.when(j == grp)
        def _(o_ref=o_ref, dil=dil):
            if dil == 1:
                o_ref[0, 0] = acc.astype(BF16)
                return
            for c in range(chunks):
                acc_ref[c] = acc[:, col(c)]
            n = acc.shape[0] // dil
            for r in range(dil):
                for c in range(chunks):
                    o_ref[0, r, :, col(c)] = acc_ref[c, pl.ds(r, n, stride=dil), :].astype(BF16)


def _qkv_proj(x, gain, w, seq, *, tm):
    t, d = x.shape
    tps = seq // tm
    tn = 3 * ATTN_OUT_DIM
    out_specs, out_shapes = [], []
    for _, dil in ATTN_GROUPS:
        out_specs.append(pl.BlockSpec((1, dil, tm // dil, tn), lambda i, j: (i // tps, 0, i % tps, 0)))
        out_shapes.append(jax.ShapeDtypeStruct((t // seq, dil, seq // dil, tn), BF16))
    return pl.pallas_call(
        _qkv_proj_kernel,
        grid=(t // tm, len(ATTN_GROUPS)),
        in_specs=[pl.BlockSpec((tm, d), lambda i, j: (i, 0)),
                  pl.BlockSpec((1, d), lambda i, j: (0, 0)),
                  pl.BlockSpec((d, tn), lambda i, j: (0, j))],
        out_specs=out_specs,
        out_shape=out_shapes,
        scratch_shapes=[pltpu.VMEM((tm, d), BF16), pltpu.VMEM((tn // LANES, tm, LANES), F32)],
        compiler_params=_params("parallel", "arbitrary"),
        name="qkv_proj",
    )(x, gain, w)


def _seg_sum(x, seg_ones):
    hi = x.astype(BF16)
    lo = (x - hi.astype(F32)).astype(BF16)
    return (jnp.dot(hi, seg_ones, preferred_element_type=F32)
            + jnp.dot(lo, seg_ones, preferred_element_type=F32))


def _rwkv_inputs(z, last, mu, w0, w2, a0, a2, g2, k_k, k_a, seg):
    row = lax.broadcasted_iota(jnp.int32, z.shape, 0)
    prev = jnp.where(row == 0, last, pltpu.roll(z, 1, axis=0))
    zs = z + (prev - z) * mu
    d = RWKV_DIM
    r, k, v = zs[:, :d], zs[:, d:2 * d], zs[:, 2 * d:3 * d]
    lora = zs[:, 3 * d:]
    lane = lax.broadcasted_iota(jnp.int32, lora.shape, 1)
    lora = jnp.where(lane < W_LORA, jnp.tanh(lora),
                     jnp.where(lane < W_LORA + A_LORA, lora, _sigmoid(lora)))
    w_lin = w0 + _mm_split(lora, w2)
    a_lin = a0 + _mm_split(lora, a2)
    g = _mm_split(lora, g2)
    u = -w_lin
    softplus = jnp.maximum(u, 0.0) + jnp.log(1.0 + jnp.exp(-jnp.abs(u)))
    lw = -jnp.exp(-softplus - 0.5)
    a = _sigmoid(a_lin)
    kk = k * k_k
    kk = kk * lax.rsqrt(_seg_sum(kk * kk, seg) + 1e-12)
    k = k * (1.0 + (a - 1.0) * k_a)
    return r, lw, k, v, kk, kk * a, g


def _bmm(a, b):
    return lax.dot_general(a.astype(BF16), b.astype(BF16), (((2,), (1,)), ((0,), (0,))),
                           preferred_element_type=F32)


def _bmm_nt(a, b):
    return lax.dot_general(a.astype(BF16), b.astype(BF16), (((2,), (2,)), ((0,), (0,))),
                           preferred_element_type=F32)


def _bmm_tn(a, b):
    return lax.dot_general(a.astype(BF16), b.astype(BF16), (((1,), (1,)), ((0,), (0,))),
                           preferred_element_type=F32)


def _rwkv_kernel(x_ref, gain_ref, wz_ref, mu_ref, w0_ref, w2_ref, a0_ref, a2_ref, g2_ref, kk_ref, ka_ref,
                 lnw_ref, lnb_ref, rk_ref, seg_ref, o_ref, s_ref, zlast_ref, *, chunk):
    @pl.when(pl.program_id(1) == 0)
    def _():
        s_ref[...] = jnp.zeros_like(s_ref)
        zlast_ref[...] = jnp.zeros_like(zlast_ref)

    tb = x_ref.shape[1]
    d = RWKV_DIM
    z = jnp.dot(_rms(x_ref[0], gain_ref[...]).astype(BF16), wz_ref[...], preferred_element_type=F32)
    last = zlast_ref[7:8, :]
    zlast_ref[...] = z[tb - 8:, :]
    r, lw, k, v, kk, b, g = _rwkv_inputs(z, last, mu_ref[...], w0_ref[...], w2_ref[...], a0_ref[...],
                                         a2_ref[...], g2_ref[...], kk_ref[...], ka_ref[...], seg_ref[...])
    nc, nh = tb // chunk, d // HEAD_DIM
    ii = lax.broadcasted_iota(jnp.int32, (chunk, chunk), 0)
    jj = lax.broadcasted_iota(jnp.int32, (chunk, chunk), 1)
    incl, strict = ii >= jj, ii > jj
    tri = incl.astype(F32)
    eye = (ii == jj).astype(F32)
    rows = lambda c: slice(c * chunk, (c + 1) * chunk)
    lanes = lambda h: slice(h * HEAD_DIM, (h + 1) * HEAD_DIM)

    def split(x):
        return jnp.stack([x[rows(c), lanes(h)] for c in range(nc) for h in range(nh)])

    cums =[jnp.dot(tri, lw[rows(c)], precision=HIGHEST, preferred_element_type=F32) for c in range(nc)]
    cum = jnp.concatenate(cums, axis=0)
    total = jnp.concatenate([jnp.broadcast_to(cs[chunk - 1:], (chunk, d)) for cs in cums], axis=0)
    g_inv = jnp.exp(-cum)
    g_tail = jnp.exp(total - cum)
    kq, rq = split(kk * jnp.exp(cum - lw)), split(r * jnp.exp(cum))
    bi, ki = split(b * g_inv), split(k * g_inv)
    bh, kh = split(b * g_tail), split(k * g_tail)
    vs = split(v)
    decay = jnp.stack([jnp.exp(cs[chunk - 1:, lanes(h)]) for cs in cums for h in range(nh)])

    a_b = jnp.where(strict, _bmm_nt(kq, bi), 0.0)
    a_k = jnp.where(strict, _bmm_nt(kq, ki), 0.0)
    a_rb = jnp.where(incl, _bmm_nt(rq, bi), 0.0)
    a_rk = jnp.where(incl, _bmm_nt(rq, ki), 0.0)
    inv = eye - a_b
    power = a_b
    for _ in range(int(np.log2(chunk)) - 1):
        power = _bmm(power, power)
        inv = inv + _bmm(inv, power)
    w = _bmm(inv, kq)
    u0 = -_bmm(inv, _bmm(a_k, vs))
    qh = rq - _bmm(a_rb, w)
    y0 = _bmm(a_rb, u0) + _bmm(a_rk, vs)
    mp = _bmm_tn(w, bh)
    s_add = _bmm_tn(u0, bh) + _bmm_tn(vs, kh)

    state = s_ref[...]
    for c in range(nc):
        sl = slice(c * nh, (c + 1) * nh)
        y = _bmm_nt(qh[sl], state) + y0[sl]
        for h in range(nh):
            o_ref[0, rows(c), lanes(h)] = y[h]
        state = state * decay[sl] - _bmm(state, mp[sl]) + s_add[sl]
    s_ref[...] = state

    seg = seg_ref[...]
    y = o_ref[0]
    yc = y - _seg_sum(y, seg) * (1.0 / HEAD_DIM)
    var = _seg_sum(yc * yc, seg) * (1.0 / HEAD_DIM)
    y = yc * lax.rsqrt(var + RWKV_GN_EPS) * lnw_ref[...] + lnb_ref[...]
    y = y + _seg_sum(r * k * rk_ref[...], seg) * v
    o_ref[0] = y * g


def _rwkv_branch(x, gain, wz, mu, w0, w2p, a0, a2p, g2p, k_k, k_a, ln_w, ln_b, r_k, seg_ones, *, tb):
    bsz, seq, dm = x.shape
    d = RWKV_DIM
    full = lambda a: pl.BlockSpec(a.shape, lambda bi, t: (0,) * a.ndim)
    params = (gain, wz, mu, w0, w2p, a0, a2p, g2p, k_k, k_a, ln_w, ln_b, r_k, seg_ones)
    return pl.pallas_call(
        functools.partial(_rwkv_kernel, chunk=WKV_CHUNK),
        grid=(bsz, seq // tb),
        in_specs=[pl.BlockSpec((1, tb, dm), lambda bi, t: (bi, t, 0))] + [full(a) for a in params],
        out_specs=pl.BlockSpec((1, tb, d), lambda bi, t: (bi, t, 0)),
        out_shape=jax.ShapeDtypeStruct((bsz, seq, d), F32),
        scratch_shapes=[pltpu.VMEM((d // HEAD_DIM, HEAD_DIM, HEAD_DIM), F32),
                        pltpu.VMEM((8, RWKV_COLS), F32)],
        compiler_params=_params("parallel", "arbitrary"),
        name="rwkv_branch",
    )(x, *params)


def _attn_kernel(q_ref, kp_ref, kc_ref, vp_ref, vc_ref, bias_ref, o_ref, l_ref):
    first = pl.program_id(2) == 0
    q = q_ref[0, 0] * (HEAD_DIM ** -0.5)
    kc, vc = kc_ref[0, 0], vc_ref[0, 0]
    subs = q.shape[0] // Q_BLOCK
    rows = lambda j: slice(j * Q_BLOCK, (j + 1) * Q_BLOCK)
    lanes = lambda h: slice(h * HEAD_DIM, (h + 1) * HEAD_DIM)
    k_prev = [kp_ref[0, 0]] + [kc[rows(j)] for j in range(subs - 1)]
    v_prev = [vp_ref[0, 0]] + [vc[rows(j)] for j in range(subs - 1)]
    work = [(j, h) for j in range(subs) for h in range(HEADS_PER_GROUP)]
    scores = []
    for j, h in work:
        bias = bias_ref[h]
        bias_prev = bias[:, :Q_BLOCK]
        if j == 0:
            bias_prev = jnp.where(first, MASKED, bias_prev)
        qh = q[rows(j), lanes(h)]
        scores.append((_mm_nt(qh, k_prev[j][:, lanes(h)]) + bias_prev,
                       _mm_nt(qh, kc[rows(j), lanes(h)]) + bias[:, Q_BLOCK:]))
    tops = [jnp.max(jnp.maximum(sp, sc), axis=-1, keepdims=True) for sp, sc in scores]
    probs = [(jnp.exp(sp - m), jnp.exp(sc - m)) for (sp, sc), m in zip(scores, tops)]
    dens = [jnp.sum(pp + pc, axis=-1, keepdims=True) for pp, pc in probs]
    accs = [_mm(pp, v_prev[j][:, lanes(h)]) + _mm(pc, vc[rows(j), lanes(h)])
            for (pp, pc), (j, h) in zip(probs, work)]
    lane = lax.broadcasted_iota(jnp.int32, (Q_BLOCK, LANES), 1)
    for j in range(subs):
        lse = jnp.zeros((Q_BLOCK, LANES), F32)
        for h in range(HEADS_PER_GROUP):
            i = j * HEADS_PER_GROUP + h
            o_ref[0, 0, rows(j), lanes(h)] = accs[i] * (1.0 / dens[i])
            lse = jnp.where(lane == h, tops[i] + jnp.log(dens[i]), lse)
        l_ref[0, 0, rows(j), :] = lse


def _attn_bias(group):
    window, dilation = ATTN_GROUPS[group]
    n_back = window // dilation
    heads = np.arange(group * HEADS_PER_GROUP, (group + 1) * HEADS_PER_GROUP)
    slopes = (2.0 ** (-8.0 * (heads + 1) / ATTN_HEADS)).astype(np.float32)
    delta = (np.arange(Q_BLOCK)[:, None] + Q_BLOCK) - np.arange(2 * Q_BLOCK)[None, :]
    valid = (delta >= 0) & (delta <= n_back)
    alibi = -slopes[:, None, None] * (dilation * delta).astype(np.float32)[None]
    return jnp.asarray(np.where(valid[None], alibi, np.float32(MASKED)).astype(np.float32))


def _attn_group(qkv, group):
    bsz, dil, comp, _ = qkv.shape
    assert comp % ATTN_TQ == 0
    sub = ATTN_TQ // Q_BLOCK
    col = lambda which: (lambda bi, r, n: (bi, r, n, which))
    col_prev = lambda which: (lambda bi, r, n: (bi, r, jnp.maximum(n * sub - 1, 0), which))
    cur, prev = (1, 1, ATTN_TQ, ATTN_OUT_DIM), (1, 1, Q_BLOCK, ATTN_OUT_DIM)
    bias = _attn_bias(group)
    return pl.pallas_call(
        _attn_kernel,
        grid=(bsz, dil, comp // ATTN_TQ),
        in_specs=[pl.BlockSpec(cur, col(0)),
                  pl.BlockSpec(prev, col_prev(1)), pl.BlockSpec(cur, col(1)),
                  pl.BlockSpec(prev, col_prev(2)), pl.BlockSpec(cur, col(2)),
                  pl.BlockSpec(bias.shape, lambda bi, r, n: (0, 0, 0))],
        out_specs=[pl.BlockSpec(cur, col(0)),
                   pl.BlockSpec((1, 1, ATTN_TQ, LANES), col(0))],
        out_shape=[jax.ShapeDtypeStruct((bsz, dil, comp, ATTN_OUT_DIM), F32),
                   jax.ShapeDtypeStruct((bsz, dil, comp, LANES), F32)],
        compiler_params=_params("parallel", "parallel", "parallel"),
        name=f"dilated_attn_{group}",
    )(qkv, qkv, qkv, qkv, qkv, bias)


def _route(logits):
    lane = lax.broadcasted_iota(jnp.int32, logits.shape, 1)
    neg = -jnp.inf
    first_max = lambda vals, mx: jnp.min(jnp.where(vals == mx, lane, LANES), axis=-1, keepdims=True)
    is_grp = (lane >= N_EXPERTS) & (lane < N_EXPERTS + N_EXPERT_GROUPS)
    gl = jnp.where(is_grp, logits, neg)
    gmax = jnp.max(gl, axis=-1, keepdims=True)
    grp_w = 1.0 / jnp.sum(jnp.exp(gl - gmax), axis=-1, keepdims=True)
    grp = first_max(gl, gmax) - N_EXPERTS
    in_grp = (lane >= grp * EXPERTS_PER_GROUP) & (lane < (grp + 1) * EXPERTS_PER_GROUP)
    el = jnp.where(in_grp, logits, neg)
    v1 = jnp.max(el, axis=-1, keepdims=True)
    i1 = first_max(el, v1)
    el2 = jnp.where(lane == i1, neg, el)
    v2 = jnp.max(el2, axis=-1, keepdims=True)
    i2 = first_max(el2, v2)
    e2 = jnp.exp(v2 - v1)
    w1 = grp_w / (1.0 + e2)
    return i1, i2, w1, w1 * e2


def _merge_kernel(x_ref, ya_ref, o0_ref, o1_ref, o2_ref, l0_ref, l1_ref, l2_ref, gt_ref,
                  wa_ref, wb_ref, wo_ref, gain_ref, wr_ref, ex_ref, tri_ref, x1_ref, h_ref, route_ref,
                  count_ref, o_scr, l_scr):
    outs, lses = [], []
    for grp, (o_ref, l_ref) in enumerate(((o0_ref, l0_ref), (o1_ref, l1_ref), (o2_ref, l2_ref))):
        dil = o_ref.shape[1]
        if dil == 1:
            outs.append(o_ref[0, 0])
            lses.append(l_ref[0, 0])
            continue
        n = o_ref.shape[2]
        chunks = o_ref.shape[3] // LANES
        for r in range(dil):
            l_scr[grp - 1, pl.ds(r, n, stride=dil), :] = l_ref[0, r]
            for c in range(chunks):
                o_scr[grp - 1, c, pl.ds(r, n, stride=dil), :] = o_ref[0, r, :, c * LANES:(c + 1) * LANES]
        outs.append(jnp.concatenate([o_scr[grp - 1, c] for c in range(chunks)], axis=1))
        lses.append(l_scr[grp - 1])
    l0, l1, l2 = lses
    m = jnp.maximum(jnp.maximum(l0, l1), l2)
    e0, e1, e2 = jnp.exp(l0 - m), jnp.exp(l1 - m), jnp.exp(l2 - m)
    inv = 1.0 / (e0 + e1 + e2)
    spread = ex_ref[...]
    yb = (_seg_sum(e0 * inv, spread) * outs[0] + _seg_sum(e1 * inv, spread) * outs[1]
          + _seg_sum(e2 * inv, spread) * outs[2])
    gates = gt_ref[...]
    merged = (gates[:, :D_MODEL] * _mm(ya_ref[...], wa_ref[...])
              + gates[:, D_MODEL:] * _mm(yb, wb_ref[...]))
    x1 = x_ref[...] + _mm(merged, wo_ref[...])
    x1_ref[...] = x1
    h = _rms(x1, gain_ref[...])
    h_ref[...] = h
    i1, i2, w1, w2 = _route(_mm_split(h, wr_ref[...]))

    @pl.when(pl.program_id(0) == 0)
    def _():
        count_ref[...] = jnp.zeros_like(count_ref)

    lane = lax.broadcasted_iota(jnp.int32, (x1.shape[0], LANES), 1)
    onehot = jnp.where((lane == i1) | (lane == i2), 1.0, 0.0)
    before = count_ref[0:1, :] + jnp.dot(tri_ref[...], onehot.astype(BF16), preferred_element_type=F32)
    rank1 = jnp.sum(jnp.where(lane == i1, before, 0.0), axis=-1, keepdims=True)
    rank2 = jnp.sum(jnp.where(lane == i2, before, 0.0), axis=-1, keepdims=True)
    last = x1.shape[0] - 1
    count_ref[...] = jnp.broadcast_to(before[last:, :] + onehot[last:, :], count_ref.shape)
    cols = (i1.astype(F32), i2.astype(F32), w1, w2, rank1, rank2)
    info = jnp.zeros(lane.shape, F32)
    for c, col in enumerate(cols):
        info = jnp.where(lane == c, col, info)
    route_ref[...] = info


def _merge(x, ya, outs, lses, gates, wa, wb, wo, gain, wr, spread, seq, *, tm):
    t = x.shape[0]
    tps = seq // tm
    tok = lambda n: pl.BlockSpec((tm, n), lambda i: (i, 0))
    full = lambda a: pl.BlockSpec(a.shape, lambda i: (0, 0))
    res = lambda a: pl.BlockSpec((1, a.shape[1], tm // a.shape[1], a.shape[3]),
                                 lambda i: (i // tps, 0, i % tps, 0))
    dilated = len(ATTN_GROUPS) - 1
    tri = jnp.asarray(np.tri(tm, k=-1), dtype=BF16)
    return pl.pallas_call(
        _merge_kernel,
        grid=(t // tm,),
        in_specs=[tok(D_MODEL), tok(RWKV_DIM)] + [res(a) for a in outs] + [res(a) for a in lses] + [
                  tok(2 * D_MODEL), full(wa), full(wb), full(wo), full(gain), full(wr), full(spread), full(tri)],
        out_specs=[tok(D_MODEL), tok(D_MODEL), tok(LANES), pl.BlockSpec((8, LANES), lambda i: (0, 0))],
        out_shape=[jax.ShapeDtypeStruct((t, D_MODEL), F32), jax.ShapeDtypeStruct((t, D_MODEL), F32),
                   jax.ShapeDtypeStruct((t, LANES), F32), jax.ShapeDtypeStruct((8, LANES), F32)],
        scratch_shapes=[pltpu.VMEM((dilated, ATTN_OUT_DIM // LANES, tm, LANES), F32),
                        pltpu.VMEM((dilated, tm, LANES), F32)],
        compiler_params=_params("arbitrary"),
        name="merge_route",
    )(x, ya, *outs, *lses, gates, wa, wb, wo, gain, wr, spread, tri)


ROUTE_E1, ROUTE_E2, ROUTE_W1, ROUTE_W2, ROUTE_R1, ROUTE_R2 = range(6)


def _moe_plan(route, counts, n_tiles):
    counts = counts[0, :N_EXPERTS].astype(jnp.int32)
    tiles = (counts + MOE_TILE - 1) // MOE_TILE
    tile_end = jnp.cumsum(tiles)
    row_start = (tile_end - tiles) * MOE_TILE
    col = lambda c: route[:, c].astype(jnp.int32)
    pos1 = row_start[col(ROUTE_E1)] + col(ROUTE_R1)
    pos2 = row_start[col(ROUTE_E2)] + col(ROUTE_R2)
    n_valid = tile_end[-1]
    tile = jnp.minimum(jnp.arange(n_tiles, dtype=jnp.int32), n_valid - 1)
    tile_expert = jnp.sum(tile[:, None] >= tile_end[None, :], axis=1).astype(jnp.int32)
    return pos1, pos2, tile_expert, n_valid.reshape(1)


def _dispatch_kernel(pos1_ref, pos2_ref, h_ref, xs_in_ref, xs_ref, sem):
    del xs_in_ref
    tm = h_ref.shape[0]
    base = pl.program_id(0) * tm

    def copies(j):
        src = h_ref.at[pl.ds(j, 1)]
        return (pltpu.make_async_copy(src, xs_ref.at[pl.ds(pos1_ref[base + j], 1)], sem.at[0]),
                pltpu.make_async_copy(src, xs_ref.at[pl.ds(pos2_ref[base + j], 1)], sem.at[1]))

    def start(j, carry):
        for cp in copies(j):
            cp.start()
        return carry

    def wait(j, carry):
        for cp in copies(j):
            cp.wait()
        return carry

    lax.fori_loop(0, tm, start, 0, unroll=8)
    lax.fori_loop(0, tm, wait, 0, unroll=8)


def _dispatch(h, pos1, pos2, n_rows, *, tm):
    t, d = h.shape
    zeros = jnp.zeros((n_rows, d), F32)
    return pl.pallas_call(
        _dispatch_kernel,
        grid_spec=pltpu.PrefetchScalarGridSpec(
            num_scalar_prefetch=2,
            grid=(t // tm,),
            in_specs=[pl.BlockSpec((tm, d), lambda i, p1, p2: (i, 0)),
                      pl.BlockSpec(memory_space=pl.ANY)],
            out_specs=pl.BlockSpec(memory_space=pl.ANY),
            scratch_shapes=[pltpu.SemaphoreType.DMA((2,))]),
        out_shape=jax.ShapeDtypeStruct((n_rows, d), F32),
        input_output_aliases={3: 0},
        compiler_params=_params("arbitrary"),
        name="moe_dispatch",
    )(pos1, pos2, h, zeros)


def _expert_kernel(te_ref, nv_ref, x_ref, wg_ref, wu_ref, wd_ref, o_ref):
    live = pl.program_id(0) < nv_ref[0]

    @pl.when(live)
    def _():
        x = x_ref[...]
        gate = _mm(x, wg_ref[0])
        up = _mm(x, wu_ref[0])
        o_ref[...] = _mm(gate * _sigmoid(gate) * up, wd_ref[0])

    @pl.when(jnp.logical_not(live))
    def _():
        o_ref[...] = jnp.zeros_like(o_ref)


def _experts(xs, tile_expert, n_valid, wg, wu, wd):
    n_rows, d = xs.shape
    weight = lambda shape: pl.BlockSpec((1,) + shape, lambda i, te, nv: (te[i], 0, 0))
    return pl.pallas_call(
        _expert_kernel,
        grid_spec=pltpu.PrefetchScalarGridSpec(
            num_scalar_prefetch=2,
            grid=(n_rows // MOE_TILE,),
            in_specs=[pl.BlockSpec((MOE_TILE, d), lambda i, te, nv: (jnp.minimum(i, nv[0] - 1), 0)),
                      weight((d, D_FF_EXPERT)), weight((d, D_FF_EXPERT)), weight((D_FF_EXPERT, d))],
            out_specs=pl.BlockSpec((MOE_TILE, d), lambda i, te, nv: (i, 0))),
        out_shape=jax.ShapeDtypeStruct((n_rows, d), F32),
        compiler_params=_params("arbitrary"),
        name="moe_experts",
    )(tile_expert, n_valid, xs, wg, wu, wd)


def _ple_kernel(pos1_ref, pos2_ref, x_ref, route_ref, p_ref, gp_ref, wg_ref, wp_ref, gf_ref, ys_ref, o_ref,
                y1_ref, y2_ref, sem):
    tm = x_ref.shape[0]
    base = pl.program_id(0) * tm

    def copies(j):
        return (pltpu.make_async_copy(ys_ref.at[pl.ds(pos1_ref[base + j], 1)], y1_ref.at[pl.ds(j, 1)], sem.at[0]),
                pltpu.make_async_copy(ys_ref.at[pl.ds(pos2_ref[base + j], 1)], y2_ref.at[pl.ds(j, 1)], sem.at[1]))

    def start(j, carry):
        for cp in copies(j):
            cp.start()
        return carry

    def wait(j, carry):
        for cp in copies(j):
            cp.wait()
        return carry

    lax.fori_loop(0, tm, start, 0, unroll=8)
    lax.fori_loop(0, tm, wait, 0, unroll=8)
    route = route_ref[...]
    w1, w2 = route[:, ROUTE_W1:ROUTE_W1 + 1], route[:, ROUTE_W2:ROUTE_W2 + 1]
    x = x_ref[...] + (w1 * y1_ref[...] + w2 * y2_ref[...])
    gate = _sigmoid(_mm(_rms(x, gp_ref[...]), wg_ref[...]))
    x = x + gate * _mm(p_ref[...], wp_ref[...])
    o_ref[...] = _rms(x, gf_ref[...])


def _ple(x, route, pos1, pos2, ys, p, gain_ple, w_gate, w_proj, gain_final, *, tm):
    t = x.shape[0]
    tok = lambda n: pl.BlockSpec((tm, n), lambda i, p1, p2: (i, 0))
    full = lambda a: pl.BlockSpec(a.shape, lambda i, p1, p2: (0, 0))
    return pl.pallas_call(
        _ple_kernel,
        grid_spec=pltpu.PrefetchScalarGridSpec(
            num_scalar_prefetch=2,
            grid=(t // tm,),
            in_specs=[tok(D_MODEL), tok(LANES), tok(PLE_DIM), full(gain_ple), full(w_gate), full(w_proj),
                      full(gain_final), pl.BlockSpec(memory_space=pl.ANY)],
            out_specs=tok(D_MODEL),
            scratch_shapes=[pltpu.VMEM((tm, D_MODEL), F32), pltpu.VMEM((tm, D_MODEL), F32),
                            pltpu.SemaphoreType.DMA((2,))]),
        out_shape=jax.ShapeDtypeStruct((t, D_MODEL), F32),
        compiler_params=_params("arbitrary"),
        name="ple_final",
    )(pos1, pos2, x, route, p, gain_ple, w_gate, w_proj, gain_final, ys)


def _pad_rows(w, start, total):
    return jnp.zeros((total, w.shape[1]), w.dtype).at[start:start + w.shape[0]].set(w)


def _layer(x, p, seq, norm_mix, w_in, rwkv_mu, rwkv_w0, rwkv_w2, rwkv_a0, rwkv_a2, rwkv_g2, rwkv_k_k,
           rwkv_k_a, rwkv_r_k, rwkv_ln_w, rwkv_ln_b, w_proj_a, w_proj_b, w_out, norm_moe, w_router_group,
           w_router_expert, w_exp_gate, w_exp_up, w_exp_down, norm_ple, w_ple_gate, w_ple_proj, norm_final):
    t = x.shape[0]
    bsz = t // seq
    row = lambda a: a.reshape(1, -1)
    w_in16 = w_in.astype(BF16)
    c0, c1 = RWKV_COLS, RWKV_COLS + ATTN_COLS
    w_qkv = w_in16[:, c0:c1].reshape(D_MODEL, 3, len(ATTN_GROUPS), ATTN_OUT_DIM)
    w_qkv = jnp.swapaxes(w_qkv, 1, 2).reshape(D_MODEL, ATTN_COLS)
    qkv = _qkv_proj(x, row(norm_mix), w_qkv, seq, tm=1024)
    gates = _norm_matmul(x, row(norm_mix), w_in16[:, c1:], BF16, tm=1024, tn=D_MODEL, sigmoid=True)

    seg = np.arange(RWKV_DIM) // HEAD_DIM
    seg_ones = jnp.asarray(seg[:, None] == seg[None, :], dtype=BF16)
    ya = _rwkv_branch(
        x.reshape(bsz, seq, D_MODEL), row(norm_mix), w_in16[:, :c0], row(rwkv_mu), row(rwkv_w0),
        _pad_rows(rwkv_w2, 0, LORA_COLS), row(rwkv_a0), _pad_rows(rwkv_a2, W_LORA, LORA_COLS),
        _pad_rows(rwkv_g2, W_LORA + A_LORA, LORA_COLS), row(rwkv_k_k), row(rwkv_k_a), row(rwkv_ln_w),
        row(rwkv_ln_b), row(rwkv_r_k), seg_ones, tb=256).reshape(t, RWKV_DIM)

    attn = [_attn_group(qkv[grp], grp) for grp in range(len(ATTN_GROUPS))]

    head_of_lane = np.arange(ATTN_OUT_DIM) // HEAD_DIM
    spread = jnp.asarray(np.arange(LANES)[:, None] == head_of_lane[None, :], dtype=BF16)
    w_route = jnp.concatenate(
        [jnp.moveaxis(w_router_expert, 0, 1).reshape(D_MODEL, N_EXPERTS), w_router_group,
         jnp.zeros((D_MODEL, LANES - N_EXPERTS - N_EXPERT_GROUPS), F32)], axis=1)
    x1, h_moe, route, counts = _merge(x, ya, [o for o, _ in attn], [l for _, l in attn], gates,
                             w_proj_a.astype(BF16), w_proj_b.astype(BF16), w_out.astype(BF16),
                             row(norm_moe), w_route, spread, seq, tm=512)
    n_tiles = 2 * t // MOE_TILE + N_EXPERTS
    pos1, pos2, tile_expert, n_valid = _moe_plan(route, counts, n_tiles)
    xs = _dispatch(h_moe, pos1, pos2, n_tiles * MOE_TILE, tm=512)
    ys = _experts(xs, tile_expert, n_valid,
                  w_exp_gate.reshape(N_EXPERTS, D_MODEL, D_FF_EXPERT),
                  w_exp_up.reshape(N_EXPERTS, D_MODEL, D_FF_EXPERT),
                  w_exp_down.reshape(N_EXPERTS, D_FF_EXPERT, D_MODEL))
    return _ple(x1, route, pos1, pos2, ys, p, row(norm_ple), w_ple_gate.astype(BF16),
                w_ple_proj.astype(BF16), row(norm_final), tm=512)


def kernel(x, p, norm_mix, w_in, rwkv_mu, rwkv_w0, rwkv_w2, rwkv_a0, rwkv_a2, rwkv_g2, rwkv_k_k, rwkv_k_a, rwkv_r_k, rwkv_ln_w, rwkv_ln_b, w_proj_a, w_proj_b, w_out, norm_moe, w_router_group, w_router_expert, w_exp_gate, w_exp_up, w_exp_down, norm_ple, w_ple_gate, w_ple_proj, norm_final):
    bsz, seq, d = x.shape
    depth = w_in.shape[0]
    assert depth == 1, "the final norm is fused into the (single) layer"
    out = _layer(x.reshape(bsz * seq, d).astype(F32), p[0].reshape(bsz * seq, PLE_DIM), seq,
                 norm_mix[0], w_in[0], rwkv_mu[0], rwkv_w0[0], rwkv_w2[0], rwkv_a0[0], rwkv_a2[0],
                 rwkv_g2[0], rwkv_k_k[0], rwkv_k_a[0], rwkv_r_k[0], rwkv_ln_w[0], rwkv_ln_b[0],
                 w_proj_a[0], w_proj_b[0], w_out[0], norm_moe[0], w_router_group[0], w_router_expert[0],
                 w_exp_gate[0], w_exp_up[0], w_exp_down[0], norm_ple[0], w_ple_gate[0], w_ple_proj[0],
                 norm_final)
    return out.reshape(bsz, seq, d)
```

```python
import functools

import numpy as np
import jax
import jax.numpy as jnp
from jax import lax
from jax.experimental import pallas as pl
from jax.experimental.pallas import tpu as pltpu

F32 = jnp.float32
BF16 = jnp.bfloat16
HIGHEST = lax.Precision.HIGHEST

D_MODEL = 1024
PLE_DIM = 256
HEAD_DIM = 64
RWKV_HEADS = 8
RWKV_DIM = RWKV_HEADS * HEAD_DIM
W_LORA, A_LORA, G_LORA = 64, 64, 128
LORA_COLS = W_LORA + A_LORA + G_LORA
ATTN_GROUPS = ((128, 1), (512, 4), (2048, 16))
HEADS_PER_GROUP = 8
ATTN_HEADS = HEADS_PER_GROUP * len(ATTN_GROUPS)
ATTN_DIM = ATTN_HEADS * HEAD_DIM
ATTN_OUT_DIM = HEADS_PER_GROUP * HEAD_DIM
Q_BLOCK = 128
ATTN_TQ = 256
RWKV_COLS = 3 * RWKV_DIM + LORA_COLS
ATTN_COLS = 3 * ATTN_DIM
N_EXPERT_GROUPS = 4
EXPERTS_PER_GROUP = 8
N_EXPERTS = N_EXPERT_GROUPS * EXPERTS_PER_GROUP
D_FF_EXPERT = 256
NORM_EPS = 1e-6
RWKV_GN_EPS = 64e-5
LANES = 128
MASKED = -1e30
WKV_CHUNK = 64
VMEM_LIMIT = 48 * 1024 * 1024
MOE_TILE = 256
ROW_TILE = 8


def _mm(a, b):
    return jnp.dot(a.astype(BF16), b.astype(BF16), preferred_element_type=F32)


def _mm_nt(a, b):
    return lax.dot_general(a.astype(BF16), b.astype(BF16), (((1,), (1,)), ((), ())),
                           preferred_element_type=F32)


def _mm_tn(a, b):
    return lax.dot_general(a.astype(BF16), b.astype(BF16), (((0,), (0,)), ((), ())),
                           preferred_element_type=F32)


def _mm_split(a, b):
    a_hi, b_hi = a.astype(BF16), b.astype(BF16)
    a_lo = (a - a_hi.astype(F32)).astype(BF16)
    b_lo = (b - b_hi.astype(F32)).astype(BF16)
    dot = functools.partial(jnp.dot, preferred_element_type=F32)
    return dot(a_hi, b_hi) + (dot(a_hi, b_lo) + dot(a_lo, b_hi))


def _sigmoid(x):
    return 1.0 / (1.0 + jnp.exp(-x))


def _rms(x, gain):
    return x * lax.rsqrt(jnp.mean(x * x, axis=-1, keepdims=True) + NORM_EPS) * gain


def _params(*sem):
    return pltpu.CompilerParams(dimension_semantics=sem, vmem_limit_bytes=VMEM_LIMIT)


def _norm_matmul_kernel(x_ref, g_ref, w_ref, o_ref, h_ref, *, sigmoid):
    @pl.when(pl.program_id(1) == 0)
    def _():
        h_ref[...] = _rms(x_ref[...], g_ref[...]).astype(BF16)

    acc = jnp.dot(h_ref[...], w_ref[...], preferred_element_type=F32)
    if sigmoid:
        acc = _sigmoid(acc)
    o_ref[...] = acc.astype(o_ref.dtype)


def _norm_matmul(x, gain, w, out_dtype, *, tm, tn, sigmoid=False):
    t, d = x.shape
    n = w.shape[1]
    return pl.pallas_call(
        functools.partial(_norm_matmul_kernel, sigmoid=sigmoid),
        grid=(t // tm, n // tn),
        in_specs=[pl.BlockSpec((tm, d), lambda i, j: (i, 0)),
                  pl.BlockSpec((1, d), lambda i, j: (0, 0)),
                  pl.BlockSpec((d, tn), lambda i, j: (0, j))],
        out_specs=pl.BlockSpec((tm, tn), lambda i, j: (i, j)),
        out_shape=jax.ShapeDtypeStruct((t, n), out_dtype),
        scratch_shapes=[pltpu.VMEM((tm, d), BF16)],
        compiler_params=_params("parallel", "arbitrary"),
        name="norm_matmul",
    )(x, gain, w)


def _qkv_proj_kernel(x_ref, g_ref, w_ref, o0_ref, o1_ref, o2_ref, h_ref, acc_ref):
    j = pl.program_id(1)

    @pl.when(j == 0)
    def _():
        h_ref[...] = _rms(x_ref[...], g_ref[...]).astype(BF16)

    acc = jnp.dot(h_ref[...], w_ref[...], preferred_element_type=F32)
    chunks = acc.shape[1] // LANES
    col = lambda c: slice(c * LANES, (c + 1) * LANES)
    for grp, o_ref in enumerate((o0_ref, o1_ref, o2_ref)):
        dil = ATTN_GROUPS[grp][1]

        @pl.when(j == grp)
        def _(o_ref=o_ref, dil=dil):
            if dil == 1:
                o_ref[0, 0] = acc.astype(BF16)
                return
            for c in range(chunks):
                acc_ref[c] = acc[:, col(c)]
            n = acc.shape[0] // dil
            for r in range(dil):
                for c in range(chunks):
                    o_ref[0, r, :, col(c)] = acc_ref[c, pl.ds(r, n, stride=dil), :].astype(BF16)


def _qkv_proj(x, gain, w, seq, *, tm):
    t, d = x.shape
    tps = seq // tm
    tn = 3 * ATTN_OUT_DIM
    out_specs, out_shapes = [], []
    for _, dil in ATTN_GROUPS:
        out_specs.append(pl.BlockSpec((1, dil, tm // dil, tn), lambda i, j: (i // tps, 0, i % tps, 0)))
        out_shapes.append(jax.ShapeDtypeStruct((t // seq, dil, seq // dil, tn), BF16))
    return pl.pallas_call(
        _qkv_proj_kernel,
        grid=(t // tm, len(ATTN_GROUPS)),
        in_specs=[pl.BlockSpec((tm, d), lambda i, j: (i, 0)),
                  pl.BlockSpec((1, d), lambda i, j: (0, 0)),
                  pl.BlockSpec((d, tn), lambda i, j: (0, j))],
        out_specs=out_specs,
        out_shape=out_shapes,
        scratch_shapes=[pltpu.VMEM((tm, d), BF16), pltpu.VMEM((tn // LANES, tm, LANES), F32)],
        compiler_params=_params("parallel", "arbitrary"),
        name="qkv_proj",
    )(x, gain, w)


def _seg_sum(x, seg_ones):
    hi = x.astype(BF16)
    lo = (x - hi.astype(F32)).astype(BF16)
    return (jnp.dot(hi, seg_ones, preferred_element_type=F32)
            + jnp.dot(lo, seg_ones, preferred_element_type=F32))


def _rwkv_inputs(z, last, mu, w0, w2, a0, a2, g2, k_k, k_a, seg):
    row = lax.broadcasted_iota(jnp.int32, z.shape, 0)
    prev = jnp.where(row == 0, last, pltpu.roll(z, 1, axis=0))
    zs = z + (prev - z) * mu
    d = RWKV_DIM
    r, k, v = zs[:, :d], zs[:, d:2 * d], zs[:, 2 * d:3 * d]
    lora = zs[:, 3 * d:]
    lane = lax.broadcasted_iota(jnp.int32, lora.shape, 1)
    lora = jnp.where(lane < W_LORA, jnp.tanh(lora),
                     jnp.where(lane < W_LORA + A_LORA, lora, _sigmoid(lora)))
    w_lin = w0 + _mm_split(lora, w2)
    a_lin = a0 + _mm_split(lora, a2)
    g = _mm_split(lora, g2)
    u = -w_lin
    softplus = jnp.maximum(u, 0.0) + jnp.log(1.0 + jnp.exp(-jnp.abs(u)))
    lw = -jnp.exp(-softplus - 0.5)
    a = _sigmoid(a_lin)
    kk = k * k_k
    kk = kk * lax.rsqrt(_seg_sum(kk * kk, seg) + 1e-12)
    k = k * (1.0 + (a - 1.0) * k_a)
    return r, lw, k, v, kk, kk * a, g


def _bmm(a, b):
    return lax.dot_general(a.astype(BF16), b.astype(BF16), (((2,), (1,)), ((0,), (0,))),
                           preferred_element_type=F32)


def _bmm_nt(a, b):
    return lax.dot_general(a.astype(BF16), b.astype(BF16), (((2,), (2,)), ((0,), (0,))),
                           preferred_element_type=F32)


def _bmm_tn(a, b):
    return lax.dot_general(a.astype(BF16), b.astype(BF16), (((1,), (1,)), ((0,), (0,))),
                           preferred_element_type=F32)


def _rwkv_kernel(x_ref, gain_ref, wz_ref, mu_ref, w0_ref, w2_ref, a0_ref, a2_ref, g2_ref, kk_ref, ka_ref,
                 lnw_ref, lnb_ref, rk_ref, seg_ref, o_ref, s_ref, zlast_ref, *, chunk):
    @pl.when(pl.program_id(1) == 0)
    def _():
        s_ref[...] = jnp.zeros_like(s_ref)
        zlast_ref[...] = jnp.zeros_like(zlast_ref)

    tb = x_ref.shape[1]
    d = RWKV_DIM
    z = jnp.dot(_rms(x_ref[0], gain_ref[...]).astype(BF16), wz_ref[...], preferred_element_type=F32)
    last = zlast_ref[7:8, :]
    zlast_ref[...] = z[tb - 8:, :]
    r, lw, k, v, kk, b, g = _rwkv_inputs(z, last, mu_ref[...], w0_ref[...], w2_ref[...], a0_ref[...],
                                         a2_ref[...], g2_ref[...], kk_ref[...], ka_ref[...], seg_ref[...])
    nc, nh = tb // chunk, d // HEAD_DIM
    ii = lax.broadcasted_iota(jnp.int32, (chunk, chunk), 0)
    jj = lax.broadcasted_iota(jnp.int32, (chunk, chunk), 1)
    incl, strict = ii >= jj, ii > jj
    tri = incl.astype(F32)
    eye = (ii == jj).astype(F32)
    rows = lambda c: slice(c * chunk, (c + 1) * chunk)
    lanes = lambda h: slice(h * HEAD_DIM, (h + 1) * HEAD_DIM)

    def split(x):
        return jnp.stack([x[rows(c), lanes(h)] for c in range(nc) for h in range(nh)])

    cums =[jnp.dot(tri, lw[rows(c)], precision=HIGHEST, preferred_element_type=F32) for c in range(nc)]
    cum = jnp.concatenate(cums, axis=0)
    total = jnp.concatenate([jnp.broadcast_to(cs[chunk - 1:], (chunk, d)) for cs in cums], axis=0)
    g_inv = jnp.exp(-cum)
    g_tail = jnp.exp(total - cum)
    kq, rq = split(kk * jnp.exp(cum - lw)), split(r * jnp.exp(cum))
    bi, ki = split(b * g_inv), split(k * g_inv)
    bh, kh = split(b * g_tail), split(k * g_tail)
    vs = split(v)
    decay = jnp.stack([jnp.exp(cs[chunk - 1:, lanes(h)]) for cs in cums for h in range(nh)])

    a_b = jnp.where(strict, _bmm_nt(kq, bi), 0.0)
    a_k = jnp.where(strict, _bmm_nt(kq, ki), 0.0)
    a_rb = jnp.where(incl, _bmm_nt(rq, bi), 0.0)
    a_rk = jnp.where(incl, _bmm_nt(rq, ki), 0.0)
    inv = eye - a_b
    power = a_b
    for _ in range(int(np.log2(chunk)) - 1):
        power = _bmm(power, power)
        inv = inv + _bmm(inv, power)
    w = _bmm(inv, kq)
    u0 = -_bmm(inv, _bmm(a_k, vs))
    qh = rq - _bmm(a_rb, w)
    y0 = _bmm(a_rb, u0) + _bmm(a_rk, vs)
    mp = _bmm_tn(w, bh)
    s_add = _bmm_tn(u0, bh) + _bmm_tn(vs, kh)

    state = s_ref[...]
    for c in range(nc):
        sl = slice(c * nh, (c + 1) * nh)
        y = _bmm_nt(qh[sl], state) + y0[sl]
        for h in range(nh):
            o_ref[0, rows(c), lanes(h)] = y[h]
        state = state * decay[sl] - _bmm(state, mp[sl]) + s_add[sl]
    s_ref[...] = state

    seg = seg_ref[...]
    y = o_ref[0]
    yc = y - _seg_sum(y, seg) * (1.0 / HEAD_DIM)
    var = _seg_sum(yc * yc, seg) * (1.0 / HEAD_DIM)
    y = yc * lax.rsqrt(var + RWKV_GN_EPS) * lnw_ref[...] + lnb_ref[...]
    y = y + _seg_sum(r * k * rk_ref[...], seg) * v
    o_ref[0] = y * g


def _rwkv_branch(x, gain, wz, mu, w0, w2p, a0, a2p, g2p, k_k, k_a, ln_w, ln_b, r_k, seg_ones, *, tb):
    bsz, seq, dm = x.shape
    d = RWKV_DIM
    full = lambda a: pl.BlockSpec(a.shape, lambda bi, t: (0,) * a.ndim)
    params = (gain, wz, mu, w0, w2p, a0, a2p, g2p, k_k, k_a, ln_w, ln_b, r_k, seg_ones)
    return pl.pallas_call(
        functools.partial(_rwkv_kernel, chunk=WKV_CHUNK),
        grid=(bsz, seq // tb),
        in_specs=[pl.BlockSpec((1, tb, dm), lambda bi, t: (bi, t, 0))] + [full(a) for a in params],
        out_specs=pl.BlockSpec((1, tb, d), lambda bi, t: (bi, t, 0)),
        out_shape=jax.ShapeDtypeStruct((bsz, seq, d), F32),
        scratch_shapes=[pltpu.VMEM((d // HEAD_DIM, HEAD_DIM, HEAD_DIM), F32),
                        pltpu.VMEM((8, RWKV_COLS), F32)],
        compiler_params=_params("parallel", "arbitrary"),
        name="rwkv_branch",
    )(x, *params)


def _attn_kernel(q_ref, kp_ref, kc_ref, vp_ref, vc_ref, bias_ref, o_ref, l_ref):
    first = pl.program_id(2) == 0
    q = q_ref[0, 0] * (HEAD_DIM ** -0.5)
    kc, vc = kc_ref[0, 0], vc_ref[0, 0]
    subs = q.shape[0] // Q_BLOCK
    rows = lambda j: slice(j * Q_BLOCK, (j + 1) * Q_BLOCK)
    lanes = lambda h: slice(h * HEAD_DIM, (h + 1) * HEAD_DIM)
    k_prev = [kp_ref[0, 0]] + [kc[rows(j)] for j in range(subs - 1)]
    v_prev = [vp_ref[0, 0]] + [vc[rows(j)] for j in range(subs - 1)]
    work = [(j, h) for j in range(subs) for h in range(HEADS_PER_GROUP)]
    scores = []
    for j, h in work:
        bias = bias_ref[h]
        bias_prev = bias[:, :Q_BLOCK]
        if j == 0:
            bias_prev = jnp.where(first, MASKED, bias_prev)
        qh = q[rows(j), lanes(h)]
        scores.append((_mm_nt(qh, k_prev[j][:, lanes(h)]) + bias_prev,
                       _mm_nt(qh, kc[rows(j), lanes(h)]) + bias[:, Q_BLOCK:]))
    tops = [jnp.max(jnp.maximum(sp, sc), axis=-1, keepdims=True) for sp, sc in scores]
    probs = [(jnp.exp(sp - m), jnp.exp(sc - m)) for (sp, sc), m in zip(scores, tops)]
    dens = [jnp.sum(pp + pc, axis=-1, keepdims=True) for pp, pc in probs]
    accs = [_mm(pp, v_prev[j][:, lanes(h)]) + _mm(pc, vc[rows(j), lanes(h)])
            for (pp, pc), (j, h) in zip(probs, work)]
    lane = lax.broadcasted_iota(jnp.int32, (Q_BLOCK, LANES), 1)
    for j in range(subs):
        lse = jnp.zeros((Q_BLOCK, LANES), F32)
        for h in range(HEADS_PER_GROUP):
            i = j * HEADS_PER_GROUP + h
            o_ref[0, 0, rows(j), lanes(h)] = accs[i] * (1.0 / dens[i])
            lse = jnp.where(lane == h, tops[i] + jnp.log(dens[i]), lse)
        l_ref[0, 0, rows(j), :] = lse


def _attn_bias(group):
    window, dilation = ATTN_GROUPS[group]
    n_back = window // dilation
    heads = np.arange(group * HEADS_PER_GROUP, (group + 1) * HEADS_PER_GROUP)
    slopes = (2.0 ** (-8.0 * (heads + 1) / ATTN_HEADS)).astype(np.float32)
    delta = (np.arange(Q_BLOCK)[:, None] + Q_BLOCK) - np.arange(2 * Q_BLOCK)[None, :]
    valid = (delta >= 0) & (delta <= n_back)
    alibi = -slopes[:, None, None] * (dilation * delta).astype(np.float32)[None]
    return jnp.asarray(np.where(valid[None], alibi, np.float32(MASKED)).astype(np.float32))


def _attn_group(qkv, group):
    bsz, dil, comp, _ = qkv.shape
    assert comp % ATTN_TQ == 0
    sub = ATTN_TQ // Q_BLOCK
    col = lambda which: (lambda bi, r, n: (bi, r, n, which))
    col_prev = lambda which: (lambda bi, r, n: (bi, r, jnp.maximum(n * sub - 1, 0), which))
    cur, prev = (1, 1, ATTN_TQ, ATTN_OUT_DIM), (1, 1, Q_BLOCK, ATTN_OUT_DIM)
    bias = _attn_bias(group)
    return pl.pallas_call(
        _attn_kernel,
        grid=(bsz, dil, comp // ATTN_TQ),
        in_specs=[pl.BlockSpec(cur, col(0)),
                  pl.BlockSpec(prev, col_prev(1)), pl.BlockSpec(cur, col(1)),
                  pl.BlockSpec(prev, col_prev(2)), pl.BlockSpec(cur, col(2)),
                  pl.BlockSpec(bias.shape, lambda bi, r, n: (0, 0, 0))],
        out_specs=[pl.BlockSpec(cur, col(0)),
                   pl.BlockSpec((1, 1, ATTN_TQ, LANES), col(0))],
        out_shape=[jax.ShapeDtypeStruct((bsz, dil, comp, ATTN_OUT_DIM), F32),
                   jax.ShapeDtypeStruct((bsz, dil, comp, LANES), F32)],
        compiler_params=_params("parallel", "parallel", "parallel"),
        name=f"dilated_attn_{group}",
    )(qkv, qkv, qkv, qkv, qkv, bias)


def _route(logits):
    lane = lax.broadcasted_iota(jnp.int32, logits.shape, 1)
    neg = -jnp.inf
    first_max = lambda vals, mx: jnp.min(jnp.where(vals == mx, lane, LANES), axis=-1, keepdims=True)
    is_grp = (lane >= N_EXPERTS) & (lane < N_EXPERTS + N_EXPERT_GROUPS)
    gl = jnp.where(is_grp, logits, neg)
    gmax = jnp.max(gl, axis=-1, keepdims=True)
    grp_w = 1.0 / jnp.sum(jnp.exp(gl - gmax), axis=-1, keepdims=True)
    grp = first_max(gl, gmax) - N_EXPERTS
    in_grp = (lane >= grp * EXPERTS_PER_GROUP) & (lane < (grp + 1) * EXPERTS_PER_GROUP)
    el = jnp.where(in_grp, logits, neg)
    v1 = jnp.max(el, axis=-1, keepdims=True)
    i1 = first_max(el, v1)
    el2 = jnp.where(lane == i1, neg, el)
    v2 = jnp.max(el2, axis=-1, keepdims=True)
    i2 = first_max(el2, v2)
    e2 = jnp.exp(v2 - v1)
    w1 = grp_w / (1.0 + e2)
    return i1, i2, w1, w1 * e2


def _merge_kernel(x_ref, ya_ref, o0_ref, o1_ref, o2_ref, l0_ref, l1_ref, l2_ref, gt_ref,
                  wa_ref, wb_ref, wo_ref, gain_ref, wr_ref, ex_ref, tri_ref, x1_ref, h_ref, route_ref,
                  count_ref, o_scr, l_scr):
    outs, lses = [], []
    for grp, (o_ref, l_ref) in enumerate(((o0_ref, l0_ref), (o1_ref, l1_ref), (o2_ref, l2_ref))):
        dil = o_ref.shape[1]
        if dil == 1:
            outs.append(o_ref[0, 0])
            lses.append(l_ref[0, 0])
            continue
        n = o_ref.shape[2]
        chunks = o_ref.shape[3] // LANES
        for r in range(dil):
            l_scr[grp - 1, pl.ds(r, n, stride=dil), :] = l_ref[0, r]
            for c in range(chunks):
                o_scr[grp - 1, c, pl.ds(r, n, stride=dil), :] = o_ref[0, r, :, c * LANES:(c + 1) * LANES]
        outs.append(jnp.concatenate([o_scr[grp - 1, c] for c in range(chunks)], axis=1))
        lses.append(l_scr[grp - 1])
    l0, l1, l2 = lses
    m = jnp.maximum(jnp.maximum(l0, l1), l2)
    e0, e1, e2 = jnp.exp(l0 - m), jnp.exp(l1 - m), jnp.exp(l2 - m)
    inv = 1.0 / (e0 + e1 + e2)
    spread = ex_ref[...]
    yb = (_seg_sum(e0 * inv, spread) * outs[0] + _seg_sum(e1 * inv, spread) * outs[1]
          + _seg_sum(e2 * inv, spread) * outs[2])
    gates = gt_ref[...]
    merged = (gates[:, :D_MODEL] * _mm(ya_ref[...], wa_ref[...])
              + gates[:, D_MODEL:] * _mm(yb, wb_ref[...]))
    x1 = x_ref[...] + _mm(merged, wo_ref[...])
    x1_ref[...] = x1
    h = _rms(x1, gain_ref[...])
    h_ref[...] = h
    i1, i2, w1, w2 = _route(_mm_split(h, wr_ref[...]))

    @pl.when(pl.program_id(0) == 0)
    def _():
        count_ref[...] = jnp.zeros_like(count_ref)

    lane = lax.broadcasted_iota(jnp.int32, (x1.shape[0], LANES), 1)
    onehot = jnp.where((lane == i1) | (lane == i2), 1.0, 0.0)
    before = count_ref[0:1, :] + jnp.dot(tri_ref[...], onehot.astype(BF16), preferred_element_type=F32)
    rank1 = jnp.sum(jnp.where(lane == i1, before, 0.0), axis=-1, keepdims=True)
    rank2 = jnp.sum(jnp.where(lane == i2, before, 0.0), axis=-1, keepdims=True)
    last = x1.shape[0] - 1
    count_ref[...] = jnp.broadcast_to(before[last:, :] + onehot[last:, :], count_ref.shape)
    cols = (i1.astype(F32), i2.astype(F32), w1, w2, rank1, rank2)
    info = jnp.zeros(lane.shape, F32)
    for c, col in enumerate(cols):
        info = jnp.where(lane == c, col, info)
    route_ref[...] = info


def _merge(x, ya, outs, lses, gates, wa, wb, wo, gain, wr, spread, seq, *, tm):
    t = x.shape[0]
    tps = seq // tm
    tok = lambda n: pl.BlockSpec((tm, n), lambda i: (i, 0))
    full = lambda a: pl.BlockSpec(a.shape, lambda i: (0, 0))
    res = lambda a: pl.BlockSpec((1, a.shape[1], tm // a.shape[1], a.shape[3]),
                                 lambda i: (i // tps, 0, i % tps, 0))
    dilated = len(ATTN_GROUPS) - 1
    tri = jnp.asarray(np.tri(tm, k=-1), dtype=BF16)
    return pl.pallas_call(
        _merge_kernel,
        grid=(t // tm,),
        in_specs=[tok(D_MODEL), tok(RWKV_DIM)] + [res(a) for a in outs] + [res(a) for a in lses] + [
                  tok(2 * D_MODEL), full(wa), full(wb), full(wo), full(gain), full(wr), full(spread), full(tri)],
        out_specs=[tok(D_MODEL), tok(D_MODEL), tok(LANES), pl.BlockSpec((8, LANES), lambda i: (0, 0))],
        out_shape=[jax.ShapeDtypeStruct((t, D_MODEL), F32), jax.ShapeDtypeStruct((t, D_MODEL), F32),
                   jax.ShapeDtypeStruct((t, LANES), F32), jax.ShapeDtypeStruct((8, LANES), F32)],
        scratch_shapes=[pltpu.VMEM((dilated, ATTN_OUT_DIM // LANES, tm, LANES), F32),
                        pltpu.VMEM((dilated, tm, LANES), F32)],
        compiler_params=_params("arbitrary"),
        name="merge_route",
    )(x, ya, *outs, *lses, gates, wa, wb, wo, gain, wr, spread, tri)


ROUTE_E1, ROUTE_E2, ROUTE_W1, ROUTE_W2, ROUTE_R1, ROUTE_R2 = range(6)


def _moe_plan(counts, n_tiles):
    counts = counts[0, :N_EXPERTS].astype(jnp.int32)
    tile_end = jnp.cumsum((counts + MOE_TILE - 1) // MOE_TILE)
    n_valid = tile_end[-1]
    tile = jnp.minimum(jnp.arange(n_tiles, dtype=jnp.int32), n_valid - 1)
    tile_expert = jnp.sum(tile[:, None] >= tile_end[None, :], axis=1).astype(jnp.int32)
    return tile_expert, n_valid.reshape(1)


def _row_tiles(ref, rows):
    return jnp.concatenate([ref[pl.ds(c, rows, stride=ROW_TILE), :] for c in range(ROW_TILE)], axis=1)


def _store_row_tiles(ref, x):
    for c in range(ROW_TILE):
        ref[pl.ds(c, x.shape[0], stride=ROW_TILE), :] = x[:, c * LANES:(c + 1) * LANES]


def _row(ref, r):
    return ref.at[pl.ds(pl.multiple_of(r * ROW_TILE, ROW_TILE), ROW_TILE)]


def _dispatch_kernel(route_ref, count_ref, upper_ref, h_ref, xs_in_ref, xs_ref, pos_ref, rows_ref, pos_vmem,
                     pos_smem, sem, pos_sem):
    del xs_in_ref
    tm = h_ref.shape[0]
    tiles = jnp.floor((count_ref[...] + (MOE_TILE - 1)) * (1.0 / MOE_TILE))
    first_row = jnp.dot(tiles.astype(BF16), upper_ref[...], preferred_element_type=F32)[0:1, :] * MOE_TILE
    route = route_ref[...]
    lane = lax.broadcasted_iota(jnp.int32, route.shape, 1)
    col = lambda c: route[:, c:c + 1]
    place = lambda e, r: jnp.sum(jnp.where(lane == col(e).astype(jnp.int32), first_row, 0.0), axis=-1,
                                 keepdims=True) + col(r)
    both = jnp.where(lane == 0, place(ROUTE_E1, ROUTE_R1), jnp.where(lane == 1, place(ROUTE_E2, ROUTE_R2), 0.0))
    pos = both.T[0:ROW_TILE, :].astype(jnp.int32)
    pos_ref[0] = pos[0:2]
    pos_vmem[...] = pos
    to_smem = pltpu.make_async_copy(pos_vmem, pos_smem, pos_sem)
    to_smem.start()
    _store_row_tiles(rows_ref, h_ref[...])
    to_smem.wait()

    def copies(j):
        src = _row(rows_ref, j)
        return (pltpu.make_async_copy(src, _row(xs_ref, pos_smem[0, j]), sem.at[0]),
                pltpu.make_async_copy(src, _row(xs_ref, pos_smem[1, j]), sem.at[1]))

    def start(j, carry):
        for cp in copies(j):
            cp.start()
        return carry

    def wait(j, carry):
        for cp in copies(j):
            cp.wait()
        return carry

    lax.fori_loop(0, tm, start, 0, unroll=8)
    lax.fori_loop(0, tm, wait, 0, unroll=8)


def _dispatch(h, route, counts, n_rows, *, tm):
    t, d = h.shape
    upper = jnp.asarray(np.triu(np.ones((LANES, LANES)), k=1), dtype=BF16)
    tok = lambda n: pl.BlockSpec((tm, n), lambda i: (i, 0))
    full = lambda a: pl.BlockSpec(a.shape, lambda i: (0, 0))
    return pl.pallas_call(
        _dispatch_kernel,
        grid=(t // tm,),
        in_specs=[tok(LANES), full(counts), full(upper), tok(d), pl.BlockSpec(memory_space=pl.ANY)],
        out_specs=[pl.BlockSpec(memory_space=pl.ANY), pl.BlockSpec((1, 2, tm), lambda i: (i, 0, 0))],
        out_shape=[jax.ShapeDtypeStruct((n_rows * ROW_TILE, LANES), F32),
                   jax.ShapeDtypeStruct((t // tm, 2, tm), jnp.int32)],
        scratch_shapes=[pltpu.VMEM((tm * ROW_TILE, LANES), F32), pltpu.VMEM((ROW_TILE, tm), jnp.int32),
                        pltpu.SMEM((ROW_TILE, tm), jnp.int32), pltpu.SemaphoreType.DMA((2,)),
                        pltpu.SemaphoreType.DMA],
        input_output_aliases={4: 0},
        compiler_params=_params("arbitrary"),
        name="moe_dispatch",
    )(route, counts, upper, h, jnp.zeros((n_rows * ROW_TILE, LANES), F32))


def _expert_kernel(te_ref, nv_ref, x_ref, wg_ref, wu_ref, wd_ref, o_ref):
    live = pl.program_id(0) < nv_ref[0]

    @pl.when(live)
    def _():
        x = _row_tiles(x_ref, MOE_TILE)
        gate = _mm(x, wg_ref[0])
        up = _mm(x, wu_ref[0])
        _store_row_tiles(o_ref, _mm(gate * _sigmoid(gate) * up, wd_ref[0]))

    @pl.when(jnp.logical_not(live))
    def _():
        o_ref[...] = jnp.zeros_like(o_ref)


def _experts(xs, tile_expert, n_valid, wg, wu, wd):
    d = wg.shape[1]
    weight = lambda shape: pl.BlockSpec((1,) + shape, lambda i, te, nv: (te[i], 0, 0))
    blk = (MOE_TILE * ROW_TILE, LANES)
    return pl.pallas_call(
        _expert_kernel,
        grid_spec=pltpu.PrefetchScalarGridSpec(
            num_scalar_prefetch=2,
            grid=(xs.shape[0] // blk[0],),
            in_specs=[pl.BlockSpec(blk, lambda i, te, nv: (jnp.minimum(i, nv[0] - 1), 0)),
                      weight((d, D_FF_EXPERT)), weight((d, D_FF_EXPERT)), weight((D_FF_EXPERT, d))],
            out_specs=pl.BlockSpec(blk, lambda i, te, nv: (i, 0))),
        out_shape=jax.ShapeDtypeStruct(xs.shape, F32),
        compiler_params=_params("arbitrary"),
        name="moe_experts",
    )(tile_expert, n_valid, xs, wg, wu, wd)


def _ple_kernel(pos_ref, x_ref, route_ref, p_ref, gp_ref, wg_ref, wp_ref, gf_ref, ys_ref, o_ref,
                y1_ref, y2_ref, sem):
    tm = x_ref.shape[0]
    base = pl.program_id(0) * (2 * tm)

    def copies(j):
        return (pltpu.make_async_copy(_row(ys_ref, pos_ref[base + j]), _row(y1_ref, j), sem.at[0]),
                pltpu.make_async_copy(_row(ys_ref, pos_ref[base + tm + j]), _row(y2_ref, j), sem.at[1]))

    def start(j, carry):
        for cp in copies(j):
            cp.start()
        return carry

    def wait(j, carry):
        for cp in copies(j):
            cp.wait()
        return carry

    lax.fori_loop(0, tm, start, 0, unroll=8)
    lax.fori_loop(0, tm, wait, 0, unroll=8)
    route = route_ref[...]
    w1, w2 = route[:, ROUTE_W1:ROUTE_W1 + 1], route[:, ROUTE_W2:ROUTE_W2 + 1]
    x = x_ref[...] + (w1 * _row_tiles(y1_ref, tm) + w2 * _row_tiles(y2_ref, tm))
    gate = _sigmoid(_mm(_rms(x, gp_ref[...]), wg_ref[...]))
    x = x + gate * _mm(p_ref[...], wp_ref[...])
    o_ref[...] = _rms(x, gf_ref[...])


def _ple(x, route, pos, ys, p, gain_ple, w_gate, w_proj, gain_final, *, tm):
    t = x.shape[0]
    tok = lambda n: pl.BlockSpec((tm, n), lambda i, ps: (i, 0))
    full = lambda a: pl.BlockSpec(a.shape, lambda i, ps: (0, 0))
    return pl.pallas_call(
        _ple_kernel,
        grid_spec=pltpu.PrefetchScalarGridSpec(
            num_scalar_prefetch=1,
            grid=(t // tm,),
            in_specs=[tok(D_MODEL), tok(LANES), tok(PLE_DIM), full(gain_ple), full(w_gate), full(w_proj),
                      full(gain_final), pl.BlockSpec(memory_space=pl.ANY)],
            out_specs=tok(D_MODEL),
            scratch_shapes=[pltpu.VMEM((tm * ROW_TILE, LANES), F32), pltpu.VMEM((tm * ROW_TILE, LANES), F32),
                            pltpu.SemaphoreType.DMA((2,))]),
        out_shape=jax.ShapeDtypeStruct((t, D_MODEL), F32),
        compiler_params=_params("arbitrary"),
        name="ple_final",
    )(pos, x, route, p, gain_ple, w_gate, w_proj, gain_final, ys)


def _pad_rows(w, start, total):
    return jnp.zeros((total, w.shape[1]), w.dtype).at[start:start + w.shape[0]].set(w)


def _layer(x, p, seq, norm_mix, w_in, rwkv_mu, rwkv_w0, rwkv_w2, rwkv_a0, rwkv_a2, rwkv_g2, rwkv_k_k,
           rwkv_k_a, rwkv_r_k, rwkv_ln_w, rwkv_ln_b, w_proj_a, w_proj_b, w_out, norm_moe, w_router_group,
           w_router_expert, w_exp_gate, w_exp_up, w_exp_down, norm_ple, w_ple_gate, w_ple_proj, norm_final):
    t = x.shape[0]
    bsz = t // seq
    row = lambda a: a.reshape(1, -1)
    w_in16 = w_in.astype(BF16)
    c0, c1 = RWKV_COLS, RWKV_COLS + ATTN_COLS
    w_qkv = w_in16[:, c0:c1].reshape(D_MODEL, 3, len(ATTN_GROUPS), ATTN_OUT_DIM)
    w_qkv = jnp.swapaxes(w_qkv, 1, 2).reshape(D_MODEL, ATTN_COLS)
    qkv = _qkv_proj(x, row(norm_mix), w_qkv, seq, tm=1024)
    gates = _norm_matmul(x, row(norm_mix), w_in16[:, c1:], BF16, tm=1024, tn=D_MODEL, sigmoid=True)

    seg = np.arange(RWKV_DIM) // HEAD_DIM
    seg_ones = jnp.asarray(seg[:, None] == seg[None, :], dtype=BF16)
    ya = _rwkv_branch(
        x.reshape(bsz, seq, D_MODEL), row(norm_mix), w_in16[:, :c0], row(rwkv_mu), row(rwkv_w0),
        _pad_rows(rwkv_w2, 0, LORA_COLS), row(rwkv_a0), _pad_rows(rwkv_a2, W_LORA, LORA_COLS),
        _pad_rows(rwkv_g2, W_LORA + A_LORA, LORA_COLS), row(rwkv_k_k), row(rwkv_k_a), row(rwkv_ln_w),
        row(rwkv_ln_b), row(rwkv_r_k), seg_ones, tb=256).reshape(t, RWKV_DIM)

    attn = [_attn_group(qkv[grp], grp) for grp in range(len(ATTN_GROUPS))]

    head_of_lane = np.arange(ATTN_OUT_DIM) // HEAD_DIM
    spread = jnp.asarray(np.arange(LANES)[:, None] == head_of_lane[None, :], dtype=BF16)
    w_route = jnp.concatenate(
        [jnp.moveaxis(w_router_expert, 0, 1).reshape(D_MODEL, N_EXPERTS), w_router_group,
         jnp.zeros((D_MODEL, LANES - N_EXPERTS - N_EXPERT_GROUPS), F32)], axis=1)
    x1, h_moe, route, counts = _merge(x, ya, [o for o, _ in attn], [l for _, l in attn], gates,
                             w_proj_a.astype(BF16), w_proj_b.astype(BF16), w_out.astype(BF16),
                             row(norm_moe), w_route, spread, seq, tm=512)
    n_tiles = 2 * t // MOE_TILE + N_EXPERTS
    tile_expert, n_valid = _moe_plan(counts, n_tiles)
    dispatch_tm = 512
    xs, pos = _dispatch(h_moe, route, counts, n_tiles * MOE_TILE, tm=dispatch_tm)
    ys = _experts(xs, tile_expert, n_valid,
                  w_exp_gate.reshape(N_EXPERTS, D_MODEL, D_FF_EXPERT),
                  w_exp_up.reshape(N_EXPERTS, D_MODEL, D_FF_EXPERT),
                  w_exp_down.reshape(N_EXPERTS, D_FF_EXPERT, D_MODEL))
    return _ple(x1, route, pos.reshape(-1), ys, p, row(norm_ple), w_ple_gate.astype(BF16),
                w_ple_proj.astype(BF16), row(norm_final), tm=dispatch_tm)


def kernel(x, p, norm_mix, w_in, rwkv_mu, rwkv_w0, rwkv_w2, rwkv_a0, rwkv_a2, rwkv_g2, rwkv_k_k, rwkv_k_a, rwkv_r_k, rwkv_ln_w, rwkv_ln_b, w_proj_a, w_proj_b, w_out, norm_moe, w_router_group, w_router_expert, w_exp_gate, w_exp_up, w_exp_down, norm_ple, w_ple_gate, w_ple_proj, norm_final):
    bsz, seq, d = x.shape
    depth = w_in.shape[0]
    assert depth == 1, "the final norm is fused into the (single) layer"
    out = _layer(x.reshape(bsz * seq, d).astype(F32), p[0].reshape(bsz * seq, PLE_DIM), seq,
                 norm_mix[0], w_in[0], rwkv_mu[0], rwkv_w0[0], rwkv_w2[0], rwkv_a0[0], rwkv_a2[0],
                 rwkv_g2[0], rwkv_k_k[0], rwkv_k_a[0], rwkv_r_k[0], rwkv_ln_w[0], rwkv_ln_b[0],
                 w_proj_a[0], w_proj_b[0], w_out[0], norm_moe[0], w_router_group[0], w_router_expert[0],
                 w_exp_gate[0], w_exp_up[0], w_exp_down[0], norm_ple[0], w_ple_gate[0], w_ple_proj[0],
                 norm_final)
    return out.reshape(bsz, seq, d)
```

```python
import functools

import numpy as np
import jax
import jax.numpy as jnp
from jax import lax
from jax.experimental import pallas as pl
from jax.experimental.pallas import tpu as pltpu

F32 = jnp.float32
BF16 = jnp.bfloat16
HIGHEST = lax.Precision.HIGHEST

D_MODEL = 1024
PLE_DIM = 256
HEAD_DIM = 64
RWKV_HEADS = 8
RWKV_DIM = RWKV_HEADS * HEAD_DIM
W_LORA, A_LORA, G_LORA = 64, 64, 128
LORA_COLS = W_LORA + A_LORA + G_LORA
ATTN_GROUPS = ((128, 1), (512, 4), (2048, 16))
HEADS_PER_GROUP = 8
ATTN_HEADS = HEADS_PER_GROUP * len(ATTN_GROUPS)
ATTN_DIM = ATTN_HEADS * HEAD_DIM
ATTN_OUT_DIM = HEADS_PER_GROUP * HEAD_DIM
Q_BLOCK = 128
ATTN_TQ = 256
RWKV_COLS = 3 * RWKV_DIM + LORA_COLS
ATTN_COLS = 3 * ATTN_DIM
N_EXPERT_GROUPS = 4
EXPERTS_PER_GROUP = 8
N_EXPERTS = N_EXPERT_GROUPS * EXPERTS_PER_GROUP
D_FF_EXPERT = 256
NORM_EPS = 1e-6
RWKV_GN_EPS = 64e-5
LANES = 128
MASKED = -1e30
WKV_CHUNK = 64
VMEM_LIMIT = 48 * 1024 * 1024
MOE_TILE = 256
ROW_TILE = 8


def _mm(a, b):
    return jnp.dot(a.astype(BF16), b.astype(BF16), preferred_element_type=F32)


def _mm_nt(a, b):
    return lax.dot_general(a.astype(BF16), b.astype(BF16), (((1,), (1,)), ((), ())),
                           preferred_element_type=F32)


def _mm_tn(a, b):
    return lax.dot_general(a.astype(BF16), b.astype(BF16), (((0,), (0,)), ((), ())),
                           preferred_element_type=F32)


def _mm_split(a, b):
    a_hi, b_hi = a.astype(BF16), b.astype(BF16)
    a_lo = (a - a_hi.astype(F32)).astype(BF16)
    b_lo = (b - b_hi.astype(F32)).astype(BF16)
    dot = functools.partial(jnp.dot, preferred_element_type=F32)
    return dot(a_hi, b_hi) + (dot(a_hi, b_lo) + dot(a_lo, b_hi))


def _sigmoid(x):
    return 1.0 / (1.0 + jnp.exp(-x))


def _rms(x, gain):
    return x * lax.rsqrt(jnp.mean(x * x, axis=-1, keepdims=True) + NORM_EPS) * gain


def _params(*sem):
    return pltpu.CompilerParams(dimension_semantics=sem, vmem_limit_bytes=VMEM_LIMIT)


def _norm_matmul_kernel(x_ref, g_ref, w_ref, o_ref, h_ref, *, sigmoid):
    @pl.when(pl.program_id(1) == 0)
    def _():
        h_ref[...] = _rms(x_ref[...], g_ref[...]).astype(BF16)

    acc = jnp.dot(h_ref[...], w_ref[...], preferred_element_type=F32)
    if sigmoid:
        acc = _sigmoid(acc)
    o_ref[...] = acc.astype(o_ref.dtype)


def _norm_matmul(x, gain, w, out_dtype, *, tm, tn, sigmoid=False):
    t, d = x.shape
    n = w.shape[1]
    return pl.pallas_call(
        functools.partial(_norm_matmul_kernel, sigmoid=sigmoid),
        grid=(t // tm, n // tn),
        in_specs=[pl.BlockSpec((tm, d), lambda i, j: (i, 0)),
                  pl.BlockSpec((1, d), lambda i, j: (0, 0)),
                  pl.BlockSpec((d, tn), lambda i, j: (0, j))],
        out_specs=pl.BlockSpec((tm, tn), lambda i, j: (i, j)),
        out_shape=jax.ShapeDtypeStruct((t, n), out_dtype),
        scratch_shapes=[pltpu.VMEM((tm, d), BF16)],
        compiler_params=_params("parallel", "arbitrary"),
        name="norm_matmul",
    )(x, gain, w)


def _qkv_proj_kernel(x_ref, g_ref, w_ref, o0_ref, o1_ref, o2_ref, h_ref, acc_ref):
    j = pl.program_id(1)

    @pl.when(j == 0)
    def _():
        h_ref[...] = _rms(x_ref[...], g_ref[...]).astype(BF16)

    acc = jnp.dot(h_ref[...], w_ref[...], preferred_element_type=F32)
    chunks = acc.shape[1] // LANES
    col = lambda c: slice(c * LANES, (c + 1) * LANES)
    for grp, o_ref in enumerate((o0_ref, o1_ref, o2_ref)):
        dil = ATTN_GROUPS[grp][1]

        @pl.when(j == grp)
        def _(o_ref=o_ref, dil=dil):
            if dil == 1:
                o_ref[0, 0] = acc.astype(BF16)
                return
            for c in range(chunks):
                acc_ref[c] = acc[:, col(c)]
            n = acc.shape[0] // dil
            for r in range(dil):
                for c in range(chunks):
                    o_ref[0, r, :, col(c)] = acc_ref[c, pl.ds(r, n, stride=dil), :].astype(BF16)


def _qkv_proj(x, gain, w, seq, *, tm):
    t, d = x.shape
    tps = seq // tm
    tn = 3 * ATTN_OUT_DIM
    out_specs, out_shapes = [], []
    for _, dil in ATTN_GROUPS:
        out_specs.append(pl.BlockSpec((1, dil, tm // dil, tn), lambda i, j: (i // tps, 0, i % tps, 0)))
        out_shapes.append(jax.ShapeDtypeStruct((t // seq, dil, seq // dil, tn), BF16))
    return pl.pallas_call(
        _qkv_proj_kernel,
        grid=(t // tm, len(ATTN_GROUPS)),
        in_specs=[pl.BlockSpec((tm, d), lambda i, j: (i, 0)),
                  pl.BlockSpec((1, d), lambda i, j: (0, 0)),
                  pl.BlockSpec((d, tn), lambda i, j: (0, j))],
        out_specs=out_specs,
        out_shape=out_shapes,
        scratch_shapes=[pltpu.VMEM((tm, d), BF16), pltpu.VMEM((tn // LANES, tm, LANES), F32)],
        compiler_params=_params("parallel", "arbitrary"),
        name="qkv_proj",
    )(x, gain, w)


def _seg_sum(x, seg_ones):
    hi = x.astype(BF16)
    lo = (x - hi.astype(F32)).astype(BF16)
    return (jnp.dot(hi, seg_ones, preferred_element_type=F32)
            + jnp.dot(lo, seg_ones, preferred_element_type=F32))


def _rwkv_inputs(z, last, mu, w0, w2, a0, a2, g2, k_k, k_a, seg):
    row = lax.broadcasted_iota(jnp.int32, z.shape, 0)
    prev = jnp.where(row == 0, last, pltpu.roll(z, 1, axis=0))
    zs = z + (prev - z) * mu
    d = RWKV_DIM
    r, k, v = zs[:, :d], zs[:, d:2 * d], zs[:, 2 * d:3 * d]
    lora = zs[:, 3 * d:]
    lane = lax.broadcasted_iota(jnp.int32, lora.shape, 1)
    lora = jnp.where(lane < W_LORA, jnp.tanh(lora),
                     jnp.where(lane < W_LORA + A_LORA, lora, _sigmoid(lora)))
    w_lin = w0 + _mm_split(lora, w2)
    a_lin = a0 + _mm_split(lora, a2)
    g = _mm_split(lora, g2)
    u = -w_lin
    softplus = jnp.maximum(u, 0.0) + jnp.log(1.0 + jnp.exp(-jnp.abs(u)))
    lw = -jnp.exp(-softplus - 0.5)
    a = _sigmoid(a_lin)
    kk = k * k_k
    kk = kk * lax.rsqrt(_seg_sum(kk * kk, seg) + 1e-12)
    k = k * (1.0 + (a - 1.0) * k_a)
    return r, lw, k, v, kk, kk * a, g


def _bmm(a, b):
    return lax.dot_general(a.astype(BF16), b.astype(BF16), (((2,), (1,)), ((0,), (0,))),
                           preferred_element_type=F32)


def _bmm_nt(a, b):
    return lax.dot_general(a.astype(BF16), b.astype(BF16), (((2,), (2,)), ((0,), (0,))),
                           preferred_element_type=F32)


def _bmm_tn(a, b):
    return lax.dot_general(a.astype(BF16), b.astype(BF16), (((1,), (1,)), ((0,), (0,))),
                           preferred_element_type=F32)


def _rwkv_kernel(x_ref, gain_ref, wz_ref, mu_ref, w0_ref, w2_ref, a0_ref, a2_ref, g2_ref, kk_ref, ka_ref,
                 lnw_ref, lnb_ref, rk_ref, seg_ref, o_ref, s_ref, zlast_ref, *, chunk):
    @pl.when(pl.program_id(1) == 0)
    def _():
        s_ref[...] = jnp.zeros_like(s_ref)
        zlast_ref[...] = jnp.zeros_like(zlast_ref)

    tb = x_ref.shape[1]
    d = RWKV_DIM
    z = jnp.dot(_rms(x_ref[0], gain_ref[...]).astype(BF16), wz_ref[...], preferred_element_type=F32)
    last = zlast_ref[7:8, :]
    zlast_ref[...] = z[tb - 8:, :]
    r, lw, k, v, kk, b, g = _rwkv_inputs(z, last, mu_ref[...], w0_ref[...], w2_ref[...], a0_ref[...],
                                         a2_ref[...], g2_ref[...], kk_ref[...], ka_ref[...], seg_ref[...])
    nc, nh = tb // chunk, d // HEAD_DIM
    ii = lax.broadcasted_iota(jnp.int32, (chunk, chunk), 0)
    jj = lax.broadcasted_iota(jnp.int32, (chunk, chunk), 1)
    incl, strict = ii >= jj, ii > jj
    tri = incl.astype(F32)
    eye = (ii == jj).astype(F32)
    rows = lambda c: slice(c * chunk, (c + 1) * chunk)
    lanes = lambda h: slice(h * HEAD_DIM, (h + 1) * HEAD_DIM)

    def split(x):
        return jnp.stack([x[rows(c), lanes(h)] for c in range(nc) for h in range(nh)])

    cums =[jnp.dot(tri, lw[rows(c)], precision=HIGHEST, preferred_element_type=F32) for c in range(nc)]
    cum = jnp.concatenate(cums, axis=0)
    total = jnp.concatenate([jnp.broadcast_to(cs[chunk - 1:], (chunk, d)) for cs in cums], axis=0)
    g_inv = jnp.exp(-cum)
    g_tail = jnp.exp(total - cum)
    kq, rq = split(kk * jnp.exp(cum - lw)), split(r * jnp.exp(cum))
    bi, ki = split(b * g_inv), split(k * g_inv)
    bh, kh = split(b * g_tail), split(k * g_tail)
    vs = split(v)
    decay = jnp.stack([jnp.exp(cs[chunk - 1:, lanes(h)]) for cs in cums for h in range(nh)])

    a_b = jnp.where(strict, _bmm_nt(kq, bi), 0.0)
    a_k = jnp.where(strict, _bmm_nt(kq, ki), 0.0)
    a_rb = jnp.where(incl, _bmm_nt(rq, bi), 0.0)
    a_rk = jnp.where(incl, _bmm_nt(rq, ki), 0.0)
    inv = eye - a_b
    power = a_b
    for _ in range(int(np.log2(chunk)) - 1):
        power = _bmm(power, power)
        inv = inv + _bmm(inv, power)
    w = _bmm(inv, kq)
    u0 = -_bmm(inv, _bmm(a_k, vs))
    qh = rq - _bmm(a_rb, w)
    y0 = _bmm(a_rb, u0) + _bmm(a_rk, vs)
    mp = _bmm_tn(w, bh)
    s_add = _bmm_tn(u0, bh) + _bmm_tn(vs, kh)

    state = s_ref[...]
    for c in range(nc):
        sl = slice(c * nh, (c + 1) * nh)
        y = _bmm_nt(qh[sl], state) + y0[sl]
        for h in range(nh):
            o_ref[0, rows(c), lanes(h)] = y[h]
        state = state * decay[sl] - _bmm(state, mp[sl]) + s_add[sl]
    s_ref[...] = state

    seg = seg_ref[...]
    y = o_ref[0]
    yc = y - _seg_sum(y, seg) * (1.0 / HEAD_DIM)
    var = _seg_sum(yc * yc, seg) * (1.0 / HEAD_DIM)
    y = yc * lax.rsqrt(var + RWKV_GN_EPS) * lnw_ref[...] + lnb_ref[...]
    y = y + _seg_sum(r * k * rk_ref[...], seg) * v
    o_ref[0] = y * g


def _rwkv_branch(x, gain, wz, mu, w0, w2p, a0, a2p, g2p, k_k, k_a, ln_w, ln_b, r_k, seg_ones, *, tb):
    bsz, seq, dm = x.shape
    d = RWKV_DIM
    full = lambda a: pl.BlockSpec(a.shape, lambda bi, t: (0,) * a.ndim)
    params = (gain, wz, mu, w0, w2p, a0, a2p, g2p, k_k, k_a, ln_w, ln_b, r_k, seg_ones)
    return pl.pallas_call(
        functools.partial(_rwkv_kernel, chunk=WKV_CHUNK),
        grid=(bsz, seq // tb),
        in_specs=[pl.BlockSpec((1, tb, dm), lambda bi, t: (bi, t, 0))] + [full(a) for a in params],
        out_specs=pl.BlockSpec((1, tb, d), lambda bi, t: (bi, t, 0)),
        out_shape=jax.ShapeDtypeStruct((bsz, seq, d), F32),
        scratch_shapes=[pltpu.VMEM((d // HEAD_DIM, HEAD_DIM, HEAD_DIM), F32),
                        pltpu.VMEM((8, RWKV_COLS), F32)],
        compiler_params=_params("parallel", "arbitrary"),
        name="rwkv_branch",
    )(x, *params)


def _attn_kernel(q_ref, kp_ref, kc_ref, vp_ref, vc_ref, bias_ref, o_ref, l_ref):
    first = pl.program_id(2) == 0
    q = q_ref[0, 0] * (HEAD_DIM ** -0.5)
    kc, vc = kc_ref[0, 0], vc_ref[0, 0]
    subs = q.shape[0] // Q_BLOCK
    rows = lambda j: slice(j * Q_BLOCK, (j + 1) * Q_BLOCK)
    lanes = lambda h: slice(h * HEAD_DIM, (h + 1) * HEAD_DIM)
    k_prev = [kp_ref[0, 0]] + [kc[rows(j)] for j in range(subs - 1)]
    v_prev = [vp_ref[0, 0]] + [vc[rows(j)] for j in range(subs - 1)]
    work = [(j, h) for j in range(subs) for h in range(HEADS_PER_GROUP)]
    scores = []
    for j, h in work:
        bias = bias_ref[h]
        bias_prev = bias[:, :Q_BLOCK]
        if j == 0:
            bias_prev = jnp.where(first, MASKED, bias_prev)
        qh = q[rows(j), lanes(h)]
        scores.append((_mm_nt(qh, k_prev[j][:, lanes(h)]) + bias_prev,
                       _mm_nt(qh, kc[rows(j), lanes(h)]) + bias[:, Q_BLOCK:]))
    tops = [jnp.max(jnp.maximum(sp, sc), axis=-1, keepdims=True) for sp, sc in scores]
    probs = [(jnp.exp(sp - m), jnp.exp(sc - m)) for (sp, sc), m in zip(scores, tops)]
    dens = [jnp.sum(pp + pc, axis=-1, keepdims=True) for pp, pc in probs]
    accs = [_mm(pp, v_prev[j][:, lanes(h)]) + _mm(pc, vc[rows(j), lanes(h)])
            for (pp, pc), (j, h) in zip(probs, work)]
    lane = lax.broadcasted_iota(jnp.int32, (Q_BLOCK, LANES), 1)
    for j in range(subs):
        lse = jnp.zeros((Q_BLOCK, LANES), F32)
        for h in range(HEADS_PER_GROUP):
            i = j * HEADS_PER_GROUP + h
            o_ref[0, 0, rows(j), lanes(h)] = accs[i] * (1.0 / dens[i])
            lse = jnp.where(lane == h, tops[i] + jnp.log(dens[i]), lse)
        l_ref[0, 0, rows(j), :] = lse


def _attn_bias(group):
    window, dilation = ATTN_GROUPS[group]
    n_back = window // dilation
    heads = np.arange(group * HEADS_PER_GROUP, (group + 1) * HEADS_PER_GROUP)
    slopes = (2.0 ** (-8.0 * (heads + 1) / ATTN_HEADS)).astype(np.float32)
    delta = (np.arange(Q_BLOCK)[:, None] + Q_BLOCK) - np.arange(2 * Q_BLOCK)[None, :]
    valid = (delta >= 0) & (delta <= n_back)
    alibi = -slopes[:, None, None] * (dilation * delta).astype(np.float32)[None]
    return jnp.asarray(np.where(valid[None], alibi, np.float32(MASKED)).astype(np.float32))


def _attn_group(qkv, group):
    bsz, dil, comp, _ = qkv.shape
    assert comp % ATTN_TQ == 0
    sub = ATTN_TQ // Q_BLOCK
    col = lambda which: (lambda bi, r, n: (bi, r, n, which))
    col_prev = lambda which: (lambda bi, r, n: (bi, r, jnp.maximum(n * sub - 1, 0), which))
    cur, prev = (1, 1, ATTN_TQ, ATTN_OUT_DIM), (1, 1, Q_BLOCK, ATTN_OUT_DIM)
    bias = _attn_bias(group)
    return pl.pallas_call(
        _attn_kernel,
        grid=(bsz, dil, comp // ATTN_TQ),
        in_specs=[pl.BlockSpec(cur, col(0)),
                  pl.BlockSpec(prev, col_prev(1)), pl.BlockSpec(cur, col(1)),
                  pl.BlockSpec(prev, col_prev(2)), pl.BlockSpec(cur, col(2)),
                  pl.BlockSpec(bias.shape, lambda bi, r, n: (0, 0, 0))],
        out_specs=[pl.BlockSpec(cur, col(0)),
                   pl.BlockSpec((1, 1, ATTN_TQ, LANES), col(0))],
        out_shape=[jax.ShapeDtypeStruct((bsz, dil, comp, ATTN_OUT_DIM), F32),
                   jax.ShapeDtypeStruct((bsz, dil, comp, LANES), F32)],
        compiler_params=_params("parallel", "parallel", "parallel"),
        name=f"dilated_attn_{group}",
    )(qkv, qkv, qkv, qkv, qkv, bias)


def _route(logits):
    lane = lax.broadcasted_iota(jnp.int32, logits.shape, 1)
    neg = -jnp.inf
    first_max = lambda vals, mx: jnp.min(jnp.where(vals == mx, lane, LANES), axis=-1, keepdims=True)
    is_grp = (lane >= N_EXPERTS) & (lane < N_EXPERTS + N_EXPERT_GROUPS)
    gl = jnp.where(is_grp, logits, neg)
    gmax = jnp.max(gl, axis=-1, keepdims=True)
    grp_w = 1.0 / jnp.sum(jnp.exp(gl - gmax), axis=-1, keepdims=True)
    grp = first_max(gl, gmax) - N_EXPERTS
    in_grp = (lane >= grp * EXPERTS_PER_GROUP) & (lane < (grp + 1) * EXPERTS_PER_GROUP)
    el = jnp.where(in_grp, logits, neg)
    v1 = jnp.max(el, axis=-1, keepdims=True)
    i1 = first_max(el, v1)
    el2 = jnp.where(lane == i1, neg, el)
    v2 = jnp.max(el2, axis=-1, keepdims=True)
    i2 = first_max(el2, v2)
    e2 = jnp.exp(v2 - v1)
    w1 = grp_w / (1.0 + e2)
    return i1, i2, w1, w1 * e2


def _merge_kernel(x_ref, ya_ref, o0_ref, o1_ref, o2_ref, l0_ref, l1_ref, l2_ref, gt_ref,
                  wa_ref, wb_ref, wo_ref, gain_ref, wr_ref, ex_ref, tri_ref, x1_ref, h_ref, route_ref,
                  count_ref, o_scr, l_scr):
    outs, lses = [], []
    for grp, (o_ref, l_ref) in enumerate(((o0_ref, l0_ref), (o1_ref, l1_ref), (o2_ref, l2_ref))):
        dil = o_ref.shape[1]
        if dil == 1:
            outs.append(o_ref[0, 0])
            lses.append(l_ref[0, 0])
            continue
        n = o_ref.shape[2]
        chunks = o_ref.shape[3] // LANES
        for r in range(dil):
            l_scr[grp - 1, pl.ds(r, n, stride=dil), :] = l_ref[0, r]
            for c in range(chunks):
                o_scr[grp - 1, c, pl.ds(r, n, stride=dil), :] = o_ref[0, r, :, c * LANES:(c + 1) * LANES]
        outs.append(jnp.concatenate([o_scr[grp - 1, c] for c in range(chunks)], axis=1))
        lses.append(l_scr[grp - 1])
    l0, l1, l2 = lses
    m = jnp.maximum(jnp.maximum(l0, l1), l2)
    e0, e1, e2 = jnp.exp(l0 - m), jnp.exp(l1 - m), jnp.exp(l2 - m)
    inv = 1.0 / (e0 + e1 + e2)
    spread = ex_ref[...]
    yb = (_seg_sum(e0 * inv, spread) * outs[0] + _seg_sum(e1 * inv, spread) * outs[1]
          + _seg_sum(e2 * inv, spread) * outs[2])
    gates = gt_ref[...]
    merged = (gates[:, :D_MODEL] * _mm(ya_ref[...], wa_ref[...])
              + gates[:, D_MODEL:] * _mm(yb, wb_ref[...]))
    x1 = x_ref[...] + _mm(merged, wo_ref[...])
    x1_ref[...] = x1
    h = _rms(x1, gain_ref[...])
    h_ref[...] = h
    i1, i2, w1, w2 = _route(_mm_split(h, wr_ref[...]))

    @pl.when(pl.program_id(0) == 0)
    def _():
        count_ref[...] = jnp.zeros_like(count_ref)

    lane = lax.broadcasted_iota(jnp.int32, (x1.shape[0], LANES), 1)
    onehot = jnp.where((lane == i1) | (lane == i2), 1.0, 0.0)
    before = count_ref[0:1, :] + jnp.dot(tri_ref[...], onehot.astype(BF16), preferred_element_type=F32)
    rank1 = jnp.sum(jnp.where(lane == i1, before, 0.0), axis=-1, keepdims=True)
    rank2 = jnp.sum(jnp.where(lane == i2, before, 0.0), axis=-1, keepdims=True)
    last = x1.shape[0] - 1
    count_ref[...] = jnp.broadcast_to(before[last:, :] + onehot[last:, :], count_ref.shape)
    cols = (i1.astype(F32), i2.astype(F32), w1, w2, rank1, rank2)
    info = jnp.zeros(lane.shape, F32)
    for c, col in enumerate(cols):
        info = jnp.where(lane == c, col, info)
    route_ref[...] = info


def _merge(x, ya, outs, lses, gates, wa, wb, wo, gain, wr, spread, seq, *, tm):
    t = x.shape[0]
    tps = seq // tm
    tok = lambda n: pl.BlockSpec((tm, n), lambda i: (i, 0))
    full = lambda a: pl.BlockSpec(a.shape, lambda i: (0, 0))
    res = lambda a: pl.BlockSpec((1, a.shape[1], tm // a.shape[1], a.shape[3]),
                                 lambda i: (i // tps, 0, i % tps, 0))
    dilated = len(ATTN_GROUPS) - 1
    tri = jnp.asarray(np.tri(tm, k=-1), dtype=BF16)
    return pl.pallas_call(
        _merge_kernel,
        grid=(t // tm,),
        in_specs=[tok(D_MODEL), tok(RWKV_DIM)] + [res(a) for a in outs] + [res(a) for a in lses] + [
                  tok(2 * D_MODEL), full(wa), full(wb), full(wo), full(gain), full(wr), full(spread), full(tri)],
        out_specs=[tok(D_MODEL), tok(D_MODEL), tok(LANES), pl.BlockSpec((8, LANES), lambda i: (0, 0))],
        out_shape=[jax.ShapeDtypeStruct((t, D_MODEL), F32), jax.ShapeDtypeStruct((t, D_MODEL), F32),
                   jax.ShapeDtypeStruct((t, LANES), F32), jax.ShapeDtypeStruct((8, LANES), F32)],
        scratch_shapes=[pltpu.VMEM((dilated, ATTN_OUT_DIM // LANES, tm, LANES), F32),
                        pltpu.VMEM((dilated, tm, LANES), F32)],
        compiler_params=_params("arbitrary"),
        name="merge_route",
    )(x, ya, *outs, *lses, gates, wa, wb, wo, gain, wr, spread, tri)


ROUTE_E1, ROUTE_E2, ROUTE_W1, ROUTE_W2, ROUTE_R1, ROUTE_R2 = range(6)


def _moe_plan(counts, n_tiles):
    counts = counts[0, :N_EXPERTS].astype(jnp.int32)
    tile_end = jnp.cumsum((counts + MOE_TILE - 1) // MOE_TILE)
    n_valid = tile_end[-1]
    tile = jnp.minimum(jnp.arange(n_tiles, dtype=jnp.int32), n_valid - 1)
    tile_expert = jnp.sum(tile[:, None] >= tile_end[None, :], axis=1).astype(jnp.int32)
    return tile_expert, n_valid.reshape(1)


def _row_tiles(ref, rows):
    return jnp.concatenate([ref[pl.ds(c, rows, stride=ROW_TILE), :] for c in range(ROW_TILE)], axis=1)


def _store_row_tiles(ref, x):
    for c in range(ROW_TILE):
        ref[pl.ds(c, x.shape[0], stride=ROW_TILE), :] = x[:, c * LANES:(c + 1) * LANES]


def _row(ref, r):
    return ref.at[pl.ds(pl.multiple_of(r * ROW_TILE, ROW_TILE), ROW_TILE)]


def _dispatch_kernel(route_ref, count_ref, upper_ref, h_ref, xs_in_ref, xs_ref, pos_ref, rows_ref, pos_vmem,
                     pos_smem, sem, pos_sem):
    del xs_in_ref
    tm = h_ref.shape[0]
    tiles = jnp.floor((count_ref[...] + (MOE_TILE - 1)) * (1.0 / MOE_TILE))
    first_row = jnp.dot(tiles.astype(BF16), upper_ref[...], preferred_element_type=F32)[0:1, :] * MOE_TILE
    route = route_ref[...]
    lane = lax.broadcasted_iota(jnp.int32, route.shape, 1)
    col = lambda c: route[:, c:c + 1]
    place = lambda e, r: jnp.sum(jnp.where(lane == col(e).astype(jnp.int32), first_row, 0.0), axis=-1,
                                 keepdims=True) + col(r)
    both = jnp.where(lane == 0, place(ROUTE_E1, ROUTE_R1), jnp.where(lane == 1, place(ROUTE_E2, ROUTE_R2), 0.0))
    pos = both.T[0:ROW_TILE, :].astype(jnp.int32)
    pos_ref[0] = pos[0:2]
    pos_vmem[...] = pos
    to_smem = pltpu.make_async_copy(pos_vmem, pos_smem, pos_sem)
    to_smem.start()
    _store_row_tiles(rows_ref, h_ref[...])
    to_smem.wait()

    def copies(j):
        src = _row(rows_ref, j)
        return (pltpu.make_async_copy(src, _row(xs_ref, pos_smem[0, j]), sem.at[0]),
                pltpu.make_async_copy(src, _row(xs_ref, pos_smem[1, j]), sem.at[1]))

    def start(j, carry):
        for stream, cp in enumerate(copies(j)):
            cp.start(priority=stream)
        return carry

    def wait(j, carry):
        for cp in copies(j):
            cp.wait()
        return carry

    lax.fori_loop(0, tm, start, 0, unroll=8)
    lax.fori_loop(0, tm, wait, 0, unroll=8)


def _dispatch(h, route, counts, n_rows, *, tm):
    t, d = h.shape
    upper = jnp.asarray(np.triu(np.ones((LANES, LANES)), k=1), dtype=BF16)
    tok = lambda n: pl.BlockSpec((tm, n), lambda i: (i, 0))
    full = lambda a: pl.BlockSpec(a.shape, lambda i: (0, 0))
    return pl.pallas_call(
        _dispatch_kernel,
        grid=(t // tm,),
        in_specs=[tok(LANES), full(counts), full(upper), tok(d), pl.BlockSpec(memory_space=pl.ANY)],
        out_specs=[pl.BlockSpec(memory_space=pl.ANY), pl.BlockSpec((1, 2, tm), lambda i: (i, 0, 0))],
        out_shape=[jax.ShapeDtypeStruct((n_rows * ROW_TILE, LANES), F32),
                   jax.ShapeDtypeStruct((t // tm, 2, tm), jnp.int32)],
        scratch_shapes=[pltpu.VMEM((tm * ROW_TILE, LANES), F32), pltpu.VMEM((ROW_TILE, tm), jnp.int32),
                        pltpu.SMEM((ROW_TILE, tm), jnp.int32), pltpu.SemaphoreType.DMA((2,)),
                        pltpu.SemaphoreType.DMA],
        input_output_aliases={4: 0},
        compiler_params=_params("arbitrary"),
        name="moe_dispatch",
    )(route, counts, upper, h, jnp.zeros((n_rows * ROW_TILE, LANES), F32))


def _expert_kernel(te_ref, nv_ref, x_ref, wg_ref, wu_ref, wd_ref, o_ref):
    live = pl.program_id(0) < nv_ref[0]

    @pl.when(live)
    def _():
        x = _row_tiles(x_ref, MOE_TILE)
        gate = _mm(x, wg_ref[0])
        up = _mm(x, wu_ref[0])
        _store_row_tiles(o_ref, _mm(gate * _sigmoid(gate) * up, wd_ref[0]))

    @pl.when(jnp.logical_not(live))
    def _():
        o_ref[...] = jnp.zeros_like(o_ref)


def _experts(xs, tile_expert, n_valid, wg, wu, wd):
    d = wg.shape[1]
    weight = lambda shape: pl.BlockSpec((1,) + shape, lambda i, te, nv: (te[i], 0, 0))
    blk = (MOE_TILE * ROW_TILE, LANES)
    return pl.pallas_call(
        _expert_kernel,
        grid_spec=pltpu.PrefetchScalarGridSpec(
            num_scalar_prefetch=2,
            grid=(xs.shape[0] // blk[0],),
            in_specs=[pl.BlockSpec(blk, lambda i, te, nv: (jnp.minimum(i, nv[0] - 1), 0)),
                      weight((d, D_FF_EXPERT)), weight((d, D_FF_EXPERT)), weight((D_FF_EXPERT, d))],
            out_specs=pl.BlockSpec(blk, lambda i, te, nv: (i, 0))),
        out_shape=jax.ShapeDtypeStruct(xs.shape, F32),
        compiler_params=_params("arbitrary"),
        name="moe_experts",
    )(tile_expert, n_valid, xs, wg, wu, wd)


def _ple_kernel(pos_ref, x_ref, route_ref, p_ref, gp_ref, wg_ref, wp_ref, gf_ref, ys_ref, o_ref,
                y_ref, sem):
    tm = x_ref.shape[0]
    step, steps = pl.program_id(0), pl.num_programs(0)

    def gather(tile, slot, start):
        base = tile * (2 * tm)

        def body(j, carry):
            for stream in range(2):
                cp = pltpu.make_async_copy(_row(ys_ref, pos_ref[base + stream * tm + j]),
                                           _row(y_ref.at[slot, stream], j), sem.at[slot, stream])
                if start:
                    cp.start(priority=stream)
                else:
                    cp.wait()
            return carry

        lax.fori_loop(0, tm, body, 0, unroll=8)

    slot = step % 2

    @pl.when(step == 0)
    def _():
        gather(0, 0, True)

    @pl.when(step + 1 < steps)
    def _():
        gather(step + 1, 1 - slot, True)

    gather(step, slot, False)
    route = route_ref[...]
    w1, w2 = route[:, ROUTE_W1:ROUTE_W1 + 1], route[:, ROUTE_W2:ROUTE_W2 + 1]
    x = x_ref[...] + (w1 * _row_tiles(y_ref.at[slot, 0], tm) + w2 * _row_tiles(y_ref.at[slot, 1], tm))
    gate = _sigmoid(_mm(_rms(x, gp_ref[...]), wg_ref[...]))
    x = x + gate * _mm(p_ref[...], wp_ref[...])
    o_ref[...] = _rms(x, gf_ref[...])


def _ple(x, route, pos, ys, p, gain_ple, w_gate, w_proj, gain_final, *, tm):
    t = x.shape[0]
    tok = lambda n: pl.BlockSpec((tm, n), lambda i, ps: (i, 0))
    full = lambda a: pl.BlockSpec(a.shape, lambda i, ps: (0, 0))
    return pl.pallas_call(
        _ple_kernel,
        grid_spec=pltpu.PrefetchScalarGridSpec(
            num_scalar_prefetch=1,
            grid=(t // tm,),
            in_specs=[tok(D_MODEL), tok(LANES), tok(PLE_DIM), full(gain_ple), full(w_gate), full(w_proj),
                      full(gain_final), pl.BlockSpec(memory_space=pl.ANY)],
            out_specs=tok(D_MODEL),
            scratch_shapes=[pltpu.VMEM((2, 2, tm * ROW_TILE, LANES), F32), pltpu.SemaphoreType.DMA((2, 2))]),
        out_shape=jax.ShapeDtypeStruct((t, D_MODEL), F32),
        compiler_params=_params("arbitrary"),
        name="ple_final",
    )(pos, x, route, p, gain_ple, w_gate, w_proj, gain_final, ys)


def _pad_rows(w, start, total):
    return jnp.zeros((total, w.shape[1]), w.dtype).at[start:start + w.shape[0]].set(w)


def _layer(x, p, seq, norm_mix, w_in, rwkv_mu, rwkv_w0, rwkv_w2, rwkv_a0, rwkv_a2, rwkv_g2, rwkv_k_k,
           rwkv_k_a, rwkv_r_k, rwkv_ln_w, rwkv_ln_b, w_proj_a, w_proj_b, w_out, norm_moe, w_router_group,
           w_router_expert, w_exp_gate, w_exp_up, w_exp_down, norm_ple, w_ple_gate, w_ple_proj, norm_final):
    t = x.shape[0]
    bsz = t // seq
    row = lambda a: a.reshape(1, -1)
    w_in16 = w_in.astype(BF16)
    c0, c1 = RWKV_COLS, RWKV_COLS + ATTN_COLS
    w_qkv = w_in16[:, c0:c1].reshape(D_MODEL, 3, len(ATTN_GROUPS), ATTN_OUT_DIM)
    w_qkv = jnp.swapaxes(w_qkv, 1, 2).reshape(D_MODEL, ATTN_COLS)
    qkv = _qkv_proj(x, row(norm_mix), w_qkv, seq, tm=1024)
    gates = _norm_matmul(x, row(norm_mix), w_in16[:, c1:], BF16, tm=1024, tn=D_MODEL, sigmoid=True)

    seg = np.arange(RWKV_DIM) // HEAD_DIM
    seg_ones = jnp.asarray(seg[:, None] == seg[None, :], dtype=BF16)
    ya = _rwkv_branch(
        x.reshape(bsz, seq, D_MODEL), row(norm_mix), w_in16[:, :c0], row(rwkv_mu), row(rwkv_w0),
        _pad_rows(rwkv_w2, 0, LORA_COLS), row(rwkv_a0), _pad_rows(rwkv_a2, W_LORA, LORA_COLS),
        _pad_rows(rwkv_g2, W_LORA + A_LORA, LORA_COLS), row(rwkv_k_k), row(rwkv_k_a), row(rwkv_ln_w),
        row(rwkv_ln_b), row(rwkv_r_k), seg_ones, tb=256).reshape(t, RWKV_DIM)

    attn = [_attn_group(qkv[grp], grp) for grp in range(len(ATTN_GROUPS))]

    head_of_lane = np.arange(ATTN_OUT_DIM) // HEAD_DIM
    spread = jnp.asarray(np.arange(LANES)[:, None] == head_of_lane[None, :], dtype=BF16)
    w_route = jnp.concatenate(
        [jnp.moveaxis(w_router_expert, 0, 1).reshape(D_MODEL, N_EXPERTS), w_router_group,
         jnp.zeros((D_MODEL, LANES - N_EXPERTS - N_EXPERT_GROUPS), F32)], axis=1)
    x1, h_moe, route, counts = _merge(x, ya, [o for o, _ in attn], [l for _, l in attn], gates,
                             w_proj_a.astype(BF16), w_proj_b.astype(BF16), w_out.astype(BF16),
                             row(norm_moe), w_route, spread, seq, tm=512)
    n_tiles = 2 * t // MOE_TILE + N_EXPERTS
    tile_expert, n_valid = _moe_plan(counts, n_tiles)
    dispatch_tm = 512
    xs, pos = _dispatch(h_moe, route, counts, n_tiles * MOE_TILE, tm=dispatch_tm)
    ys = _experts(xs, tile_expert, n_valid,
                  w_exp_gate.reshape(N_EXPERTS, D_MODEL, D_FF_EXPERT),
                  w_exp_up.reshape(N_EXPERTS, D_MODEL, D_FF_EXPERT),
                  w_exp_down.reshape(N_EXPERTS, D_FF_EXPERT, D_MODEL))
    return _ple(x1, route, pos.reshape(-1), ys, p, row(norm_ple), w_ple_gate.astype(BF16),
                w_ple_proj.astype(BF16), row(norm_final), tm=dispatch_tm)


def kernel(x, p, norm_mix, w_in, rwkv_mu, rwkv_w0, rwkv_w2, rwkv_a0, rwkv_a2, rwkv_g2, rwkv_k_k, rwkv_k_a, rwkv_r_k, rwkv_ln_w, rwkv_ln_b, w_proj_a, w_proj_b, w_out, norm_moe, w_router_group, w_router_expert, w_exp_gate, w_exp_up, w_exp_down, norm_ple, w_ple_gate, w_ple_proj, norm_final):
    bsz, seq, d = x.shape
    depth = w_in.shape[0]
    assert depth == 1, "the final norm is fused into the (single) layer"
    out = _layer(x.reshape(bsz * seq, d).astype(F32), p[0].reshape(bsz * seq, PLE_DIM), seq,
                 norm_mix[0], w_in[0], rwkv_mu[0], rwkv_w0[0], rwkv_w2[0], rwkv_a0[0], rwkv_a2[0],
                 rwkv_g2[0], rwkv_k_k[0], rwkv_k_a[0], rwkv_r_k[0], rwkv_ln_w[0], rwkv_ln_b[0],
                 w_proj_a[0], w_proj_b[0], w_out[0], norm_moe[0], w_router_group[0], w_router_expert[0],
                 w_exp_gate[0], w_exp_up[0], w_exp_down[0], norm_ple[0], w_ple_gate[0], w_ple_proj[0],
                 norm_final)
    return out.reshape(bsz, seq, d)
```

```python
import functools

import numpy as np
import jax
import jax.numpy as jnp
from jax import lax
from jax.experimental import pallas as pl
from jax.experimental.pallas import tpu as pltpu

F32 = jnp.float32
BF16 = jnp.bfloat16
HIGHEST = lax.Precision.HIGHEST

D_MODEL = 1024
PLE_DIM = 256
HEAD_DIM = 64
RWKV_HEADS = 8
RWKV_DIM = RWKV_HEADS * HEAD_DIM
W_LORA, A_LORA, G_LORA = 64, 64, 128
LORA_COLS = W_LORA + A_LORA + G_LORA
ATTN_GROUPS = ((128, 1), (512, 4), (2048, 16))
HEADS_PER_GROUP = 8
ATTN_HEADS = HEADS_PER_GROUP * len(ATTN_GROUPS)
ATTN_DIM = ATTN_HEADS * HEAD_DIM
ATTN_OUT_DIM = HEADS_PER_GROUP * HEAD_DIM
Q_BLOCK = 128
ATTN_TQ = 256
RWKV_COLS = 3 * RWKV_DIM + LORA_COLS
ATTN_COLS = 3 * ATTN_DIM
N_EXPERT_GROUPS = 4
EXPERTS_PER_GROUP = 8
N_EXPERTS = N_EXPERT_GROUPS * EXPERTS_PER_GROUP
D_FF_EXPERT = 256
NORM_EPS = 1e-6
RWKV_GN_EPS = 64e-5
LANES = 128
MASKED = -1e30
WKV_CHUNK = 64
VMEM_LIMIT = 48 * 1024 * 1024
MOE_TILE = 256
ROW_TILE = 8


def _mm(a, b):
    return jnp.dot(a.astype(BF16), b.astype(BF16), preferred_element_type=F32)


def _mm_nt(a, b):
    return lax.dot_general(a.astype(BF16), b.astype(BF16), (((1,), (1,)), ((), ())),
                           preferred_element_type=F32)


def _mm_tn(a, b):
    return lax.dot_general(a.astype(BF16), b.astype(BF16), (((0,), (0,)), ((), ())),
                           preferred_element_type=F32)


def _mm_split(a, b):
    a_hi, b_hi = a.astype(BF16), b.astype(BF16)
    a_lo = (a - a_hi.astype(F32)).astype(BF16)
    b_lo = (b - b_hi.astype(F32)).astype(BF16)
    dot = functools.partial(jnp.dot, preferred_element_type=F32)
    return dot(a_hi, b_hi) + (dot(a_hi, b_lo) + dot(a_lo, b_hi))


def _sigmoid(x):
    return 1.0 / (1.0 + jnp.exp(-x))


def _rms(x, gain):
    return x * lax.rsqrt(jnp.mean(x * x, axis=-1, keepdims=True) + NORM_EPS) * gain


def _params(*sem):
    return pltpu.CompilerParams(dimension_semantics=sem, vmem_limit_bytes=VMEM_LIMIT)


def _norm_matmul_kernel(x_ref, g_ref, w_ref, o_ref, h_ref, *, sigmoid):
    @pl.when(pl.program_id(1) == 0)
    def _():
        h_ref[...] = _rms(x_ref[...], g_ref[...]).astype(BF16)

    acc = jnp.dot(h_ref[...], w_ref[...], preferred_element_type=F32)
    if sigmoid:
        acc = _sigmoid(acc)
    o_ref[...] = acc.astype(o_ref.dtype)


def _norm_matmul(x, gain, w, out_dtype, *, tm, tn, sigmoid=False):
    t, d = x.shape
    n = w.shape[1]
    return pl.pallas_call(
        functools.partial(_norm_matmul_kernel, sigmoid=sigmoid),
        grid=(t // tm, n // tn),
        in_specs=[pl.BlockSpec((tm, d), lambda i, j: (i, 0)),
                  pl.BlockSpec((1, d), lambda i, j: (0, 0)),
                  pl.BlockSpec((d, tn), lambda i, j: (0, j))],
        out_specs=pl.BlockSpec((tm, tn), lambda i, j: (i, j)),
        out_shape=jax.ShapeDtypeStruct((t, n), out_dtype),
        scratch_shapes=[pltpu.VMEM((tm, d), BF16)],
        compiler_params=_params("parallel", "arbitrary"),
        name="norm_matmul",
    )(x, gain, w)


def _qkv_proj_kernel(x_ref, g_ref, w_ref, o0_ref, o1_ref, o2_ref, h_ref, acc_ref):
    j = pl.program_id(1)

    @pl.when(j == 0)
    def _():
        h_ref[...] = _rms(x_ref[...], g_ref[...]).astype(BF16)

    acc = jnp.dot(h_ref[...], w_ref[...], preferred_element_type=F32)
    chunks = acc.shape[1] // LANES
    col = lambda c: slice(c * LANES, (c + 1) * LANES)
    for grp, o_ref in enumerate((o0_ref, o1_ref, o2_ref)):
        dil = ATTN_GROUPS[grp][1]

        @pl.when(j == grp)
        def _(o_ref=o_ref, dil=dil):
            if dil == 1:
                o_ref[0, 0] = acc.astype(BF16)
                return
            for c in range(chunks):
                acc_ref[c] = acc[:, col(c)]
            n = acc.shape[0] // dil
            for r in range(dil):
                for c in range(chunks):
                    o_ref[0, r, :, col(c)] = acc_ref[c, pl.ds(r, n, stride=dil), :].astype(BF16)


def _qkv_proj(x, gain, w, seq, *, tm):
    t, d = x.shape
    tps = seq // tm
    tn = 3 * ATTN_OUT_DIM
    out_specs, out_shapes = [], []
    for _, dil in ATTN_GROUPS:
        out_specs.append(pl.BlockSpec((1, dil, tm // dil, tn), lambda i, j: (i // tps, 0, i % tps, 0)))
        out_shapes.append(jax.ShapeDtypeStruct((t // seq, dil, seq // dil, tn), BF16))
    return pl.pallas_call(
        _qkv_proj_kernel,
        grid=(t // tm, len(ATTN_GROUPS)),
        in_specs=[pl.BlockSpec((tm, d), lambda i, j: (i, 0)),
                  pl.BlockSpec((1, d), lambda i, j: (0, 0)),
                  pl.BlockSpec((d, tn), lambda i, j: (0, j))],
        out_specs=out_specs,
        out_shape=out_shapes,
        scratch_shapes=[pltpu.VMEM((tm, d), BF16), pltpu.VMEM((tn // LANES, tm, LANES), F32)],
        compiler_params=_params("parallel", "arbitrary"),
        name="qkv_proj",
    )(x, gain, w)


def _seg_sum(x, seg_ones):
    hi = x.astype(BF16)
    lo = (x - hi.astype(F32)).astype(BF16)
    return (jnp.dot(hi, seg_ones, preferred_element_type=F32)
            + jnp.dot(lo, seg_ones, preferred_element_type=F32))


def _rwkv_inputs(z, last, mu, w0, w2, a0, a2, g2, k_k, k_a, seg):
    row = lax.broadcasted_iota(jnp.int32, z.shape, 0)
    prev = jnp.where(row == 0, last, pltpu.roll(z, 1, axis=0))
    zs = z + (prev - z) * mu
    d = RWKV_DIM
    r, k, v = zs[:, :d], zs[:, d:2 * d], zs[:, 2 * d:3 * d]
    lora = zs[:, 3 * d:]
    lane = lax.broadcasted_iota(jnp.int32, lora.shape, 1)
    lora = jnp.where(lane < W_LORA, jnp.tanh(lora),
                     jnp.where(lane < W_LORA + A_LORA, lora, _sigmoid(lora)))
    w_lin = w0 + _mm_split(lora, w2)
    a_lin = a0 + _mm_split(lora, a2)
    g = _mm_split(lora, g2)
    u = -w_lin
    softplus = jnp.maximum(u, 0.0) + jnp.log(1.0 + jnp.exp(-jnp.abs(u)))
    lw = -jnp.exp(-softplus - 0.5)
    a = _sigmoid(a_lin)
    kk = k * k_k
    kk = kk * lax.rsqrt(_seg_sum(kk * kk, seg) + 1e-12)
    k = k * (1.0 + (a - 1.0) * k_a)
    return r, lw, k, v, kk, kk * a, g


def _bmm(a, b):
    return lax.dot_general(a.astype(BF16), b.astype(BF16), (((2,), (1,)), ((0,), (0,))),
                           preferred_element_type=F32)


def _bmm_nt(a, b):
    return lax.dot_general(a.astype(BF16), b.astype(BF16), (((2,), (2,)), ((0,), (0,))),
                           preferred_element_type=F32)


def _bmm_tn(a, b):
    return lax.dot_general(a.astype(BF16), b.astype(BF16), (((1,), (1,)), ((0,), (0,))),
                           preferred_element_type=F32)


def _rwkv_kernel(x_ref, gain_ref, wz_ref, mu_ref, w0_ref, w2_ref, a0_ref, a2_ref, g2_ref, kk_ref, ka_ref,
                 lnw_ref, lnb_ref, rk_ref, seg_ref, o_ref, s_ref, zlast_ref, *, chunk):
    @pl.when(pl.program_id(1) == 0)
    def _():
        s_ref[...] = jnp.zeros_like(s_ref)
        zlast_ref[...] = jnp.zeros_like(zlast_ref)

    tb = x_ref.shape[1]
    d = RWKV_DIM
    z = jnp.dot(_rms(x_ref[0], gain_ref[...]).astype(BF16), wz_ref[...], preferred_element_type=F32)
    last = zlast_ref[7:8, :]
    zlast_ref[...] = z[tb - 8:, :]
    r, lw, k, v, kk, b, g = _rwkv_inputs(z, last, mu_ref[...], w0_ref[...], w2_ref[...], a0_ref[...],
                                         a2_ref[...], g2_ref[...], kk_ref[...], ka_ref[...], seg_ref[...])
    nc, nh = tb // chunk, d // HEAD_DIM
    ii = lax.broadcasted_iota(jnp.int32, (chunk, chunk), 0)
    jj = lax.broadcasted_iota(jnp.int32, (chunk, chunk), 1)
    incl, strict = ii >= jj, ii > jj
    tri = incl.astype(F32)
    eye = (ii == jj).astype(F32)
    rows = lambda c: slice(c * chunk, (c + 1) * chunk)
    lanes = lambda h: slice(h * HEAD_DIM, (h + 1) * HEAD_DIM)

    def split(x):
        return jnp.stack([x[rows(c), lanes(h)] for c in range(nc) for h in range(nh)])

    cums =[jnp.dot(tri, lw[rows(c)], precision=HIGHEST, preferred_element_type=F32) for c in range(nc)]
    cum = jnp.concatenate(cums, axis=0)
    total = jnp.concatenate([jnp.broadcast_to(cs[chunk - 1:], (chunk, d)) for cs in cums], axis=0)
    g_inv = jnp.exp(-cum)
    g_tail = jnp.exp(total - cum)
    kq, rq = split(kk * jnp.exp(cum - lw)), split(r * jnp.exp(cum))
    bi, ki = split(b * g_inv), split(k * g_inv)
    bh, kh = split(b * g_tail), split(k * g_tail)
    vs = split(v)
    decay = jnp.stack([jnp.exp(cs[chunk - 1:, lanes(h)]) for cs in cums for h in range(nh)])

    qr = jnp.concatenate([kq, rq], axis=1)
    ab, ak = _bmm_nt(qr, bi), _bmm_nt(qr, ki)
    a_b, a_rb = jnp.where(strict, ab[:, :chunk], 0.0), jnp.where(incl, ab[:, chunk:], 0.0)
    a_k, a_rk = jnp.where(strict, ak[:, :chunk], 0.0), jnp.where(incl, ak[:, chunk:], 0.0)
    n = nc * nh
    is_b = lax.broadcasted_iota(jnp.int32, (chunk, 2 * chunk), 1) < chunk
    bt = jnp.concatenate([-a_b, jnp.broadcast_to(eye, (n, chunk, chunk))], axis=2)
    for _ in range(int(np.log2(chunk))):
        prod = _bmm(bt[:, :, :chunk], bt)
        bt = jnp.where(is_b, prod, bt + prod)
    inv = bt[:, :, chunk:]
    wu = _bmm(inv, jnp.concatenate([kq, -_bmm(a_k, vs)], axis=2))
    aw = _bmm(a_rb, wu)
    qh = rq - aw[:, :, :HEAD_DIM]
    y0 = aw[:, :, HEAD_DIM:] + _bmm(a_rk, vs)
    mu = _bmm_tn(wu, bh)
    mp = mu[:, :HEAD_DIM]
    s_add = mu[:, HEAD_DIM:] + _bmm_tn(vs, kh)

    state = s_ref[...]
    for c in range(nc):
        sl = slice(c * nh, (c + 1) * nh)
        y = _bmm_nt(qh[sl], state) + y0[sl]
        for h in range(nh):
            o_ref[0, rows(c), lanes(h)] = y[h]
        state = state * decay[sl] - _bmm(state, mp[sl]) + s_add[sl]
    s_ref[...] = state

    seg = seg_ref[...]
    y = o_ref[0]
    yc = y - _seg_sum(y, seg) * (1.0 / HEAD_DIM)
    var = _seg_sum(yc * yc, seg) * (1.0 / HEAD_DIM)
    y = yc * lax.rsqrt(var + RWKV_GN_EPS) * lnw_ref[...] + lnb_ref[...]
    y = y + _seg_sum(r * k * rk_ref[...], seg) * v
    o_ref[0] = y * g


def _rwkv_branch(x, gain, wz, mu, w0, w2p, a0, a2p, g2p, k_k, k_a, ln_w, ln_b, r_k, seg_ones, *, tb):
    bsz, seq, dm = x.shape
    d = RWKV_DIM
    full = lambda a: pl.BlockSpec(a.shape, lambda bi, t: (0,) * a.ndim)
    params = (gain, wz, mu, w0, w2p, a0, a2p, g2p, k_k, k_a, ln_w, ln_b, r_k, seg_ones)
    return pl.pallas_call(
        functools.partial(_rwkv_kernel, chunk=WKV_CHUNK),
        grid=(bsz, seq // tb),
        in_specs=[pl.BlockSpec((1, tb, dm), lambda bi, t: (bi, t, 0))] + [full(a) for a in params],
        out_specs=pl.BlockSpec((1, tb, d), lambda bi, t: (bi, t, 0)),
        out_shape=jax.ShapeDtypeStruct((bsz, seq, d), F32),
        scratch_shapes=[pltpu.VMEM((d // HEAD_DIM, HEAD_DIM, HEAD_DIM), F32),
                        pltpu.VMEM((8, RWKV_COLS), F32)],
        compiler_params=_params("parallel", "arbitrary"),
        name="rwkv_branch",
    )(x, *params)


def _attn_kernel(q_ref, kp_ref, kc_ref, vp_ref, vc_ref, bias_ref, o_ref, l_ref):
    first = pl.program_id(2) == 0
    q = q_ref[0, 0] * (HEAD_DIM ** -0.5)
    kc, vc = kc_ref[0, 0], vc_ref[0, 0]
    subs = q.shape[0] // Q_BLOCK
    rows = lambda j: slice(j * Q_BLOCK, (j + 1) * Q_BLOCK)
    lanes = lambda h: slice(h * HEAD_DIM, (h + 1) * HEAD_DIM)
    k_prev = [kp_ref[0, 0]] + [kc[rows(j)] for j in range(subs - 1)]
    v_prev = [vp_ref[0, 0]] + [vc[rows(j)] for j in range(subs - 1)]
    work = [(j, h) for j in range(subs) for h in range(HEADS_PER_GROUP)]
    scores = []
    for j, h in work:
        bias = bias_ref[h]
        bias_prev = bias[:, :Q_BLOCK]
        if j == 0:
            bias_prev = jnp.where(first, MASKED, bias_prev)
        qh = q[rows(j), lanes(h)]
        scores.append((_mm_nt(qh, k_prev[j][:, lanes(h)]) + bias_prev,
                       _mm_nt(qh, kc[rows(j), lanes(h)]) + bias[:, Q_BLOCK:]))
    tops = [jnp.max(jnp.maximum(sp, sc), axis=-1, keepdims=True) for sp, sc in scores]
    probs = [(jnp.exp(sp - m), jnp.exp(sc - m)) for (sp, sc), m in zip(scores, tops)]
    dens = [jnp.sum(pp + pc, axis=-1, keepdims=True) for pp, pc in probs]
    accs = [_mm(pp, v_prev[j][:, lanes(h)]) + _mm(pc, vc[rows(j), lanes(h)])
            for (pp, pc), (j, h) in zip(probs, work)]
    lane = lax.broadcasted_iota(jnp.int32, (Q_BLOCK, LANES), 1)
    for j in range(subs):
        lse = jnp.zeros((Q_BLOCK, LANES), F32)
        for h in range(HEADS_PER_GROUP):
            i = j * HEADS_PER_GROUP + h
            o_ref[0, 0, rows(j), lanes(h)] = accs[i] * (1.0 / dens[i])
            lse = jnp.where(lane == h, tops[i] + jnp.log(dens[i]), lse)
        l_ref[0, 0, rows(j), :] = lse


def _attn_bias(group):
    window, dilation = ATTN_GROUPS[group]
    n_back = window // dilation
    heads = np.arange(group * HEADS_PER_GROUP, (group + 1) * HEADS_PER_GROUP)
    slopes = (2.0 ** (-8.0 * (heads + 1) / ATTN_HEADS)).astype(np.float32)
    delta = (np.arange(Q_BLOCK)[:, None] + Q_BLOCK) - np.arange(2 * Q_BLOCK)[None, :]
    valid = (delta >= 0) & (delta <= n_back)
    alibi = -slopes[:, None, None] * (dilation * delta).astype(np.float32)[None]
    return jnp.asarray(np.where(valid[None], alibi, np.float32(MASKED)).astype(np.float32))


def _attn_group(qkv, group):
    bsz, dil, comp, _ = qkv.shape
    assert comp % ATTN_TQ == 0
    sub = ATTN_TQ // Q_BLOCK
    col = lambda which: (lambda bi, r, n: (bi, r, n, which))
    col_prev = lambda which: (lambda bi, r, n: (bi, r, jnp.maximum(n * sub - 1, 0), which))
    cur, prev = (1, 1, ATTN_TQ, ATTN_OUT_DIM), (1, 1, Q_BLOCK, ATTN_OUT_DIM)
    bias = _attn_bias(group)
    return pl.pallas_call(
        _attn_kernel,
        grid=(bsz, dil, comp // ATTN_TQ),
        in_specs=[pl.BlockSpec(cur, col(0)),
                  pl.BlockSpec(prev, col_prev(1)), pl.BlockSpec(cur, col(1)),
                  pl.BlockSpec(prev, col_prev(2)), pl.BlockSpec(cur, col(2)),
                  pl.BlockSpec(bias.shape, lambda bi, r, n: (0, 0, 0))],
        out_specs=[pl.BlockSpec(cur, col(0)),
                   pl.BlockSpec((1, 1, ATTN_TQ, LANES), col(0))],
        out_shape=[jax.ShapeDtypeStruct((bsz, dil, comp, ATTN_OUT_DIM), F32),
                   jax.ShapeDtypeStruct((bsz, dil, comp, LANES), F32)],
        compiler_params=_params("parallel", "parallel", "parallel"),
        name=f"dilated_attn_{group}",
    )(qkv, qkv, qkv, qkv, qkv, bias)


def _route(logits):
    lane = lax.broadcasted_iota(jnp.int32, logits.shape, 1)
    neg = -jnp.inf
    first_max = lambda vals, mx: jnp.min(jnp.where(vals == mx, lane, LANES), axis=-1, keepdims=True)
    is_grp = (lane >= N_EXPERTS) & (lane < N_EXPERTS + N_EXPERT_GROUPS)
    gl = jnp.where(is_grp, logits, neg)
    gmax = jnp.max(gl, axis=-1, keepdims=True)
    grp_w = 1.0 / jnp.sum(jnp.exp(gl - gmax), axis=-1, keepdims=True)
    grp = first_max(gl, gmax) - N_EXPERTS
    in_grp = (lane >= grp * EXPERTS_PER_GROUP) & (lane < (grp + 1) * EXPERTS_PER_GROUP)
    el = jnp.where(in_grp, logits, neg)
    v1 = jnp.max(el, axis=-1, keepdims=True)
    i1 = first_max(el, v1)
    el2 = jnp.where(lane == i1, neg, el)
    v2 = jnp.max(el2, axis=-1, keepdims=True)
    i2 = first_max(el2, v2)
    e2 = jnp.exp(v2 - v1)
    w1 = grp_w / (1.0 + e2)
    return i1, i2, w1, w1 * e2


def _merge_kernel(x_ref, ya_ref, o0_ref, o1_ref, o2_ref, l0_ref, l1_ref, l2_ref, gt_ref,
                  wa_ref, wb_ref, wo_ref, gain_ref, wr_ref, ex_ref, tri_ref, x1_ref, h_ref, route_ref,
                  count_ref, o_scr, l_scr):
    outs, lses = [], []
    for grp, (o_ref, l_ref) in enumerate(((o0_ref, l0_ref), (o1_ref, l1_ref), (o2_ref, l2_ref))):
        dil = o_ref.shape[1]
        if dil == 1:
            outs.append(o_ref[0, 0])
            lses.append(l_ref[0, 0])
            continue
        n = o_ref.shape[2]
        chunks = o_ref.shape[3] // LANES
        for r in range(dil):
            l_scr[grp - 1, pl.ds(r, n, stride=dil), :] = l_ref[0, r]
            for c in range(chunks):
                o_scr[grp - 1, c, pl.ds(r, n, stride=dil), :] = o_ref[0, r, :, c * LANES:(c + 1) * LANES]
        outs.append(jnp.concatenate([o_scr[grp - 1, c] for c in range(chunks)], axis=1))
        lses.append(l_scr[grp - 1])
    l0, l1, l2 = lses
    m = jnp.maximum(jnp.maximum(l0, l1), l2)
    e0, e1, e2 = jnp.exp(l0 - m), jnp.exp(l1 - m), jnp.exp(l2 - m)
    inv = 1.0 / (e0 + e1 + e2)
    spread = ex_ref[...]
    yb = (_seg_sum(e0 * inv, spread) * outs[0] + _seg_sum(e1 * inv, spread) * outs[1]
          + _seg_sum(e2 * inv, spread) * outs[2])
    gates = gt_ref[...]
    merged = (gates[:, :D_MODEL] * _mm(ya_ref[...], wa_ref[...])
              + gates[:, D_MODEL:] * _mm(yb, wb_ref[...]))
    x1 = x_ref[...] + _mm(merged, wo_ref[...])
    x1_ref[...] = x1
    h = _rms(x1, gain_ref[...])
    h_ref[...] = h
    i1, i2, w1, w2 = _route(_mm_split(h, wr_ref[...]))

    @pl.when(pl.program_id(0) == 0)
    def _():
        count_ref[...] = jnp.zeros_like(count_ref)

    lane = lax.broadcasted_iota(jnp.int32, (x1.shape[0], LANES), 1)
    onehot = jnp.where((lane == i1) | (lane == i2), 1.0, 0.0)
    before = count_ref[0:1, :] + jnp.dot(tri_ref[...], onehot.astype(BF16), preferred_element_type=F32)
    rank1 = jnp.sum(jnp.where(lane == i1, before, 0.0), axis=-1, keepdims=True)
    rank2 = jnp.sum(jnp.where(lane == i2, before, 0.0), axis=-1, keepdims=True)
    last = x1.shape[0] - 1
    count_ref[...] = jnp.broadcast_to(before[last:, :] + onehot[last:, :], count_ref.shape)
    cols = (i1.astype(F32), i2.astype(F32), w1, w2, rank1, rank2)
    info = jnp.zeros(lane.shape, F32)
    for c, col in enumerate(cols):
        info = jnp.where(lane == c, col, info)
    route_ref[...] = info


def _merge(x, ya, outs, lses, gates, wa, wb, wo, gain, wr, spread, seq, *, tm):
    t = x.shape[0]
    tps = seq // tm
    tok = lambda n: pl.BlockSpec((tm, n), lambda i: (i, 0))
    full = lambda a: pl.BlockSpec(a.shape, lambda i: (0, 0))
    res = lambda a: pl.BlockSpec((1, a.shape[1], tm // a.shape[1], a.shape[3]),
                                 lambda i: (i // tps, 0, i % tps, 0))
    dilated = len(ATTN_GROUPS) - 1
    tri = jnp.asarray(np.tri(tm, k=-1), dtype=BF16)
    return pl.pallas_call(
        _merge_kernel,
        grid=(t // tm,),
        in_specs=[tok(D_MODEL), tok(RWKV_DIM)] + [res(a) for a in outs] + [res(a) for a in lses] + [
                  tok(2 * D_MODEL), full(wa), full(wb), full(wo), full(gain), full(wr), full(spread), full(tri)],
        out_specs=[tok(D_MODEL), tok(D_MODEL), tok(LANES), pl.BlockSpec((8, LANES), lambda i: (0, 0))],
        out_shape=[jax.ShapeDtypeStruct((t, D_MODEL), F32), jax.ShapeDtypeStruct((t, D_MODEL), F32),
                   jax.ShapeDtypeStruct((t, LANES), F32), jax.ShapeDtypeStruct((8, LANES), F32)],
        scratch_shapes=[pltpu.VMEM((dilated, ATTN_OUT_DIM // LANES, tm, LANES), F32),
                        pltpu.VMEM((dilated, tm, LANES), F32)],
        compiler_params=_params("arbitrary"),
        name="merge_route",
    )(x, ya, *outs, *lses, gates, wa, wb, wo, gain, wr, spread, tri)


ROUTE_E1, ROUTE_E2, ROUTE_W1, ROUTE_W2, ROUTE_R1, ROUTE_R2 = range(6)


def _moe_plan(counts, n_tiles):
    counts = counts[0, :N_EXPERTS].astype(jnp.int32)
    tile_end = jnp.cumsum((counts + MOE_TILE - 1) // MOE_TILE)
    n_valid = tile_end[-1]
    tile = jnp.minimum(jnp.arange(n_tiles, dtype=jnp.int32), n_valid - 1)
    tile_expert = jnp.sum(tile[:, None] >= tile_end[None, :], axis=1).astype(jnp.int32)
    return tile_expert, n_valid.reshape(1)


def _row_tiles(ref, rows):
    return jnp.concatenate([ref[pl.ds(c, rows, stride=ROW_TILE), :] for c in range(ROW_TILE)], axis=1)


def _store_row_tiles(ref, x):
    for c in range(ROW_TILE):
        ref[pl.ds(c, x.shape[0], stride=ROW_TILE), :] = x[:, c * LANES:(c + 1) * LANES]


def _row(ref, r):
    return ref.at[pl.ds(pl.multiple_of(r * ROW_TILE, ROW_TILE), ROW_TILE)]


def _dispatch_kernel(route_ref, count_ref, upper_ref, h_ref, xs_in_ref, xs_ref, pos_ref, rows_ref, pos_vmem,
                     pos_smem, sem, pos_sem):
    del xs_in_ref
    tm = h_ref.shape[0]
    tiles = jnp.floor((count_ref[...] + (MOE_TILE - 1)) * (1.0 / MOE_TILE))
    first_row = jnp.dot(tiles.astype(BF16), upper_ref[...], preferred_element_type=F32)[0:1, :] * MOE_TILE
    route = route_ref[...]
    lane = lax.broadcasted_iota(jnp.int32, route.shape, 1)
    col = lambda c: route[:, c:c + 1]
    place = lambda e, r: jnp.sum(jnp.where(lane == col(e).astype(jnp.int32), first_row, 0.0), axis=-1,
                                 keepdims=True) + col(r)
    both = jnp.where(lane == 0, place(ROUTE_E1, ROUTE_R1), jnp.where(lane == 1, place(ROUTE_E2, ROUTE_R2), 0.0))
    pos = both.T[0:ROW_TILE, :].astype(jnp.int32)
    pos_ref[0] = pos[0:2]
    pos_vmem[...] = pos
    to_smem = pltpu.make_async_copy(pos_vmem, pos_smem, pos_sem)
    to_smem.start()
    _store_row_tiles(rows_ref, h_ref[...])
    to_smem.wait()

    def copies(j):
        src = _row(rows_ref, j)
        return (pltpu.make_async_copy(src, _row(xs_ref, pos_smem[0, j]), sem.at[0]),
                pltpu.make_async_copy(src, _row(xs_ref, pos_smem[1, j]), sem.at[1]))

    def start(j, carry):
        for stream, cp in enumerate(copies(j)):
            cp.start(priority=stream)
        return carry

    lax.fori_loop(0, tm, start, 0, unroll=8)
    for stream in range(2):
        pltpu.make_async_copy(rows_ref, rows_ref, sem.at[stream]).wait()


def _dispatch(h, route, counts, n_rows, *, tm):
    t, d = h.shape
    upper = jnp.asarray(np.triu(np.ones((LANES, LANES)), k=1), dtype=BF16)
    tok = lambda n: pl.BlockSpec((tm, n), lambda i: (i, 0))
    full = lambda a: pl.BlockSpec(a.shape, lambda i: (0, 0))
    return pl.pallas_call(
        _dispatch_kernel,
        grid=(t // tm,),
        in_specs=[tok(LANES), full(counts), full(upper), tok(d), pl.BlockSpec(memory_space=pl.ANY)],
        out_specs=[pl.BlockSpec(memory_space=pl.ANY), pl.BlockSpec((1, 2, tm), lambda i: (i, 0, 0))],
        out_shape=[jax.ShapeDtypeStruct((n_rows * ROW_TILE, LANES), F32),
                   jax.ShapeDtypeStruct((t // tm, 2, tm), jnp.int32)],
        scratch_shapes=[pltpu.VMEM((tm * ROW_TILE, LANES), F32), pltpu.VMEM((ROW_TILE, tm), jnp.int32),
                        pltpu.SMEM((ROW_TILE, tm), jnp.int32), pltpu.SemaphoreType.DMA((2,)),
                        pltpu.SemaphoreType.DMA],
        input_output_aliases={4: 0},
        compiler_params=_params("arbitrary"),
        name="moe_dispatch",
    )(route, counts, upper, h, jnp.zeros((n_rows * ROW_TILE, LANES), F32))


def _expert_kernel(te_ref, nv_ref, x_ref, wg_ref, wu_ref, wd_ref, o_ref):
    live = pl.program_id(0) < nv_ref[0]

    @pl.when(live)
    def _():
        x = _row_tiles(x_ref, MOE_TILE)
        gate = _mm(x, wg_ref[0])
        up = _mm(x, wu_ref[0])
        _store_row_tiles(o_ref, _mm(gate * _sigmoid(gate) * up, wd_ref[0]))

    @pl.when(jnp.logical_not(live))
    def _():
        o_ref[...] = jnp.zeros_like(o_ref)


def _experts(xs, tile_expert, n_valid, wg, wu, wd):
    d = wg.shape[1]
    weight = lambda shape: pl.BlockSpec((1,) + shape, lambda i, te, nv: (te[i], 0, 0))
    blk = (MOE_TILE * ROW_TILE, LANES)
    return pl.pallas_call(
        _expert_kernel,
        grid_spec=pltpu.PrefetchScalarGridSpec(
            num_scalar_prefetch=2,
            grid=(xs.shape[0] // blk[0],),
            in_specs=[pl.BlockSpec(blk, lambda i, te, nv: (jnp.minimum(i, nv[0] - 1), 0)),
                      weight((d, D_FF_EXPERT)), weight((d, D_FF_EXPERT)), weight((D_FF_EXPERT, d))],
            out_specs=pl.BlockSpec(blk, lambda i, te, nv: (i, 0))),
        out_shape=jax.ShapeDtypeStruct(xs.shape, F32),
        compiler_params=_params("arbitrary"),
        name="moe_experts",
    )(tile_expert, n_valid, xs, wg, wu, wd)


def _ple_kernel(pos_ref, x_ref, route_ref, p_ref, gp_ref, wg_ref, wp_ref, gf_ref, ys_ref, o_ref,
                y_ref, sem):
    tm = x_ref.shape[0]
    step, steps = pl.program_id(0), pl.num_programs(0)

    def gather(tile, slot):
        base = tile * (2 * tm)

        def body(j, carry):
            for stream in range(2):
                pltpu.make_async_copy(_row(ys_ref, pos_ref[base + stream * tm + j]),
                                      _row(y_ref.at[slot, stream], j), sem.at[slot, stream]
                                      ).start(priority=stream)
            return carry

        lax.fori_loop(0, tm, body, 0, unroll=8)

    slot = step % 2

    @pl.when(step == 0)
    def _():
        gather(0, 0)

    @pl.when(step + 1 < steps)
    def _():
        gather(step + 1, 1 - slot)

    for stream in range(2):
        buf = y_ref.at[slot, stream]
        pltpu.make_async_copy(buf, buf, sem.at[slot, stream]).wait()
    route = route_ref[...]
    w1, w2 = route[:, ROUTE_W1:ROUTE_W1 + 1], route[:, ROUTE_W2:ROUTE_W2 + 1]
    x = x_ref[...] + (w1 * _row_tiles(y_ref.at[slot, 0], tm) + w2 * _row_tiles(y_ref.at[slot, 1], tm))
    gate = _sigmoid(_mm(_rms(x, gp_ref[...]), wg_ref[...]))
    x = x + gate * _mm(p_ref[...], wp_ref[...])
    o_ref[...] = _rms(x, gf_ref[...])


def _ple(x, route, pos, ys, p, gain_ple, w_gate, w_proj, gain_final, *, tm):
    t = x.shape[0]
    tok = lambda n: pl.BlockSpec((tm, n), lambda i, ps: (i, 0))
    full = lambda a: pl.BlockSpec(a.shape, lambda i, ps: (0, 0))
    return pl.pallas_call(
        _ple_kernel,
        grid_spec=pltpu.PrefetchScalarGridSpec(
            num_scalar_prefetch=1,
            grid=(t // tm,),
            in_specs=[tok(D_MODEL), tok(LANES), tok(PLE_DIM), full(gain_ple), full(w_gate), full(w_proj),
                      full(gain_final), pl.BlockSpec(memory_space=pl.ANY)],
            out_specs=tok(D_MODEL),
            scratch_shapes=[pltpu.VMEM((2, 2, tm * ROW_TILE, LANES), F32), pltpu.SemaphoreType.DMA((2, 2))]),
        out_shape=jax.ShapeDtypeStruct((t, D_MODEL), F32),
        compiler_params=_params("arbitrary"),
        name="ple_final",
    )(pos, x, route, p, gain_ple, w_gate, w_proj, gain_final, ys)


def _pad_rows(w, start, total):
    return jnp.zeros((total, w.shape[1]), w.dtype).at[start:start + w.shape[0]].set(w)


def _layer(x, p, seq, norm_mix, w_in, rwkv_mu, rwkv_w0, rwkv_w2, rwkv_a0, rwkv_a2, rwkv_g2, rwkv_k_k,
           rwkv_k_a, rwkv_r_k, rwkv_ln_w, rwkv_ln_b, w_proj_a, w_proj_b, w_out, norm_moe, w_router_group,
           w_router_expert, w_exp_gate, w_exp_up, w_exp_down, norm_ple, w_ple_gate, w_ple_proj, norm_final):
    t = x.shape[0]
    bsz = t // seq
    row = lambda a: a.reshape(1, -1)
    w_in16 = w_in.astype(BF16)
    c0, c1 = RWKV_COLS, RWKV_COLS + ATTN_COLS
    w_qkv = w_in16[:, c0:c1].reshape(D_MODEL, 3, len(ATTN_GROUPS), ATTN_OUT_DIM)
    w_qkv = jnp.swapaxes(w_qkv, 1, 2).reshape(D_MODEL, ATTN_COLS)
    qkv = _qkv_proj(x, row(norm_mix), w_qkv, seq, tm=1024)
    gates = _norm_matmul(x, row(norm_mix), w_in16[:, c1:], BF16, tm=1024, tn=D_MODEL, sigmoid=True)

    seg = np.arange(RWKV_DIM) // HEAD_DIM
    seg_ones = jnp.asarray(seg[:, None] == seg[None, :], dtype=BF16)
    ya = _rwkv_branch(
        x.reshape(bsz, seq, D_MODEL), row(norm_mix), w_in16[:, :c0], row(rwkv_mu), row(rwkv_w0),
        _pad_rows(rwkv_w2, 0, LORA_COLS), row(rwkv_a0), _pad_rows(rwkv_a2, W_LORA, LORA_COLS),
        _pad_rows(rwkv_g2, W_LORA + A_LORA, LORA_COLS), row(rwkv_k_k), row(rwkv_k_a), row(rwkv_ln_w),
        row(rwkv_ln_b), row(rwkv_r_k), seg_ones, tb=256).reshape(t, RWKV_DIM)

    attn = [_attn_group(qkv[grp], grp) for grp in range(len(ATTN_GROUPS))]

    head_of_lane = np.arange(ATTN_OUT_DIM) // HEAD_DIM
    spread = jnp.asarray(np.arange(LANES)[:, None] == head_of_lane[None, :], dtype=BF16)
    w_route = jnp.concatenate(
        [jnp.moveaxis(w_router_expert, 0, 1).reshape(D_MODEL, N_EXPERTS), w_router_group,
         jnp.zeros((D_MODEL, LANES - N_EXPERTS - N_EXPERT_GROUPS), F32)], axis=1)
    x1, h_moe, route, counts = _merge(x, ya, [o for o, _ in attn], [l for _, l in attn], gates,
                             w_proj_a.astype(BF16), w_proj_b.astype(BF16), w_out.astype(BF16),
                             row(norm_moe), w_route, spread, seq, tm=512)
    n_tiles = 2 * t // MOE_TILE + N_EXPERTS
    tile_expert, n_valid = _moe_plan(counts, n_tiles)
    dispatch_tm = 512
    xs, pos = _dispatch(h_moe, route, counts, n_tiles * MOE_TILE, tm=dispatch_tm)
    ys = _experts(xs, tile_expert, n_valid,
                  w_exp_gate.reshape(N_EXPERTS, D_MODEL, D_FF_EXPERT),
                  w_exp_up.reshape(N_EXPERTS, D_MODEL, D_FF_EXPERT),
                  w_exp_down.reshape(N_EXPERTS, D_FF_EXPERT, D_MODEL))
    return _ple(x1, route, pos.reshape(-1), ys, p, row(norm_ple), w_ple_gate.astype(BF16),
                w_ple_proj.astype(BF16), row(norm_final), tm=dispatch_tm)


def kernel(x, p, norm_mix, w_in, rwkv_mu, rwkv_w0, rwkv_w2, rwkv_a0, rwkv_a2, rwkv_g2, rwkv_k_k, rwkv_k_a, rwkv_r_k, rwkv_ln_w, rwkv_ln_b, w_proj_a, w_proj_b, w_out, norm_moe, w_router_group, w_router_expert, w_exp_gate, w_exp_up, w_exp_down, norm_ple, w_ple_gate, w_ple_proj, norm_final):
    bsz, seq, d = x.shape
    depth = w_in.shape[0]
    assert depth == 1, "the final norm is fused into the (single) layer"
    out = _layer(x.reshape(bsz * seq, d).astype(F32), p[0].reshape(bsz * seq, PLE_DIM), seq,
                 norm_mix[0], w_in[0], rwkv_mu[0], rwkv_w0[0], rwkv_w2[0], rwkv_a0[0], rwkv_a2[0],
                 rwkv_g2[0], rwkv_k_k[0], rwkv_k_a[0], rwkv_r_k[0], rwkv_ln_w[0], rwkv_ln_b[0],
                 w_proj_a[0], w_proj_b[0], w_out[0], norm_moe[0], w_router_group[0], w_router_expert[0],
                 w_exp_gate[0], w_exp_up[0], w_exp_down[0], norm_ple[0], w_ple_gate[0], w_ple_proj[0],
                 norm_final)
    return out.reshape(bsz, seq, d)
```

```python
import functools

import numpy as np
import jax
import jax.numpy as jnp
from jax import lax
from jax.experimental import pallas as pl
from jax.experimental.pallas import tpu as pltpu

F32 = jnp.float32
BF16 = jnp.bfloat16
HIGHEST = lax.Precision.HIGHEST

D_MODEL = 1024
PLE_DIM = 256
HEAD_DIM = 64
RWKV_HEADS = 8
RWKV_DIM = RWKV_HEADS * HEAD_DIM
W_LORA, A_LORA, G_LORA = 64, 64, 128
LORA_COLS = W_LORA + A_LORA + G_LORA
ATTN_GROUPS = ((128, 1), (512, 4), (2048, 16))
HEADS_PER_GROUP = 8
ATTN_HEADS = HEADS_PER_GROUP * len(ATTN_GROUPS)
ATTN_DIM = ATTN_HEADS * HEAD_DIM
ATTN_OUT_DIM = HEADS_PER_GROUP * HEAD_DIM
Q_BLOCK = 128
ATTN_TQ = 256
RWKV_COLS = 3 * RWKV_DIM + LORA_COLS
ATTN_COLS = 3 * ATTN_DIM
N_EXPERT_GROUPS = 4
EXPERTS_PER_GROUP = 8
N_EXPERTS = N_EXPERT_GROUPS * EXPERTS_PER_GROUP
D_FF_EXPERT = 256
NORM_EPS = 1e-6
RWKV_GN_EPS = 64e-5
LANES = 128
MASKED = -1e30
WKV_CHUNK = 64
VMEM_LIMIT = 48 * 1024 * 1024
MOE_TILE = 256
ROW_TILE = 8


def _mm(a, b):
    return jnp.dot(a.astype(BF16), b.astype(BF16), preferred_element_type=F32)


def _mm_nt(a, b):
    return lax.dot_general(a.astype(BF16), b.astype(BF16), (((1,), (1,)), ((), ())),
                           preferred_element_type=F32)


def _mm_tn(a, b):
    return lax.dot_general(a.astype(BF16), b.astype(BF16), (((0,), (0,)), ((), ())),
                           preferred_element_type=F32)


def _mm_split(a, b):
    a_hi, b_hi = a.astype(BF16), b.astype(BF16)
    a_lo = (a - a_hi.astype(F32)).astype(BF16)
    b_lo = (b - b_hi.astype(F32)).astype(BF16)
    dot = functools.partial(jnp.dot, preferred_element_type=F32)
    return dot(a_hi, b_hi) + (dot(a_hi, b_lo) + dot(a_lo, b_hi))


def _sigmoid(x):
    return 1.0 / (1.0 + jnp.exp(-x))


def _rms(x, gain):
    return x * lax.rsqrt(jnp.mean(x * x, axis=-1, keepdims=True) + NORM_EPS) * gain


def _params(*sem):
    return pltpu.CompilerParams(dimension_semantics=sem, vmem_limit_bytes=VMEM_LIMIT)


def _norm_matmul_kernel(x_ref, g_ref, w_ref, o_ref, h_ref, *, sigmoid):
    @pl.when(pl.program_id(1) == 0)
    def _():
        h_ref[...] = _rms(x_ref[...], g_ref[...]).astype(BF16)

    acc = jnp.dot(h_ref[...], w_ref[...], preferred_element_type=F32)
    if sigmoid:
        acc = _sigmoid(acc)
    o_ref[...] = acc.astype(o_ref.dtype)


def _norm_matmul(x, gain, w, out_dtype, *, tm, tn, sigmoid=False):
    t, d = x.shape
    n = w.shape[1]
    return pl.pallas_call(
        functools.partial(_norm_matmul_kernel, sigmoid=sigmoid),
        grid=(t // tm, n // tn),
        in_specs=[pl.BlockSpec((tm, d), lambda i, j: (i, 0)),
                  pl.BlockSpec((1, d), lambda i, j: (0, 0)),
                  pl.BlockSpec((d, tn), lambda i, j: (0, j))],
        out_specs=pl.BlockSpec((tm, tn), lambda i, j: (i, j)),
        out_shape=jax.ShapeDtypeStruct((t, n), out_dtype),
        scratch_shapes=[pltpu.VMEM((tm, d), BF16)],
        compiler_params=_params("parallel", "arbitrary"),
        name="norm_matmul",
    )(x, gain, w)


def _qkv_proj_kernel(x_hbm, g_ref, w_ref, o_ref, x_buf, sem):
    tm = x_buf.shape[1]
    n_res, n_blk = pl.num_programs(1), pl.num_programs(2)
    step = (pl.program_id(0) * n_res + pl.program_id(1)) * n_blk + pl.program_id(2)
    steps = pl.num_programs(0) * n_res * n_blk

    def fetch(s, slot):
        b, r, i = s // (n_res * n_blk), (s // n_blk) % n_res, s % n_blk
        return pltpu.make_async_copy(x_hbm.at[b, pl.ds(i * tm, tm), r], x_buf.at[slot], sem.at[slot])

    slot = step % 2

    @pl.when(step == 0)
    def _():
        fetch(0, 0).start()

    @pl.when(step + 1 < steps)
    def _():
        fetch(step + 1, 1 - slot).start()

    fetch(step, slot).wait()
    h = _rms(x_buf[slot], g_ref[...]).astype(BF16)
    o_ref[0, 0] = jnp.dot(h, w_ref[...], preferred_element_type=F32).astype(BF16)


def _qkv_proj(x, gain, w, group, *, tm):
    bsz, seq, d = x.shape
    dil = ATTN_GROUPS[group][1]
    comp = seq // dil
    tm = min(tm, comp)
    tn = w.shape[1]
    return pl.pallas_call(
        _qkv_proj_kernel,
        grid=(bsz, dil, comp // tm),
        in_specs=[pl.BlockSpec(memory_space=pl.ANY),
                  pl.BlockSpec((1, d), lambda b, r, i: (0, 0)),
                  pl.BlockSpec((d, tn), lambda b, r, i: (0, 0))],
        out_specs=pl.BlockSpec((1, 1, tm, tn), lambda b, r, i: (b, r, i, 0)),
        out_shape=jax.ShapeDtypeStruct((bsz, dil, comp, tn), BF16),
        scratch_shapes=[pltpu.VMEM((2, tm, d), F32), pltpu.SemaphoreType.DMA((2,))],
        compiler_params=_params("arbitrary", "arbitrary", "arbitrary"),
        name=f"qkv_proj_{group}",
    )(x.reshape(bsz, comp, dil, d), gain, w)


def _seg_sum(x, seg_ones):
    hi = x.astype(BF16)
    lo = (x - hi.astype(F32)).astype(BF16)
    return (jnp.dot(hi, seg_ones, preferred_element_type=F32)
            + jnp.dot(lo, seg_ones, preferred_element_type=F32))


def _rwkv_inputs(z, last, mu, w0, w2, a0, a2, g2, k_k, k_a, seg):
    row = lax.broadcasted_iota(jnp.int32, z.shape, 0)
    prev = jnp.where(row == 0, last, pltpu.roll(z, 1, axis=0))
    zs = z + (prev - z) * mu
    d = RWKV_DIM
    r, k, v = zs[:, :d], zs[:, d:2 * d], zs[:, 2 * d:3 * d]
    lora = zs[:, 3 * d:]
    lane = lax.broadcasted_iota(jnp.int32, lora.shape, 1)
    lora = jnp.where(lane < W_LORA, jnp.tanh(lora),
                     jnp.where(lane < W_LORA + A_LORA, lora, _sigmoid(lora)))
    w_lin = w0 + _mm_split(lora, w2)
    a_lin = a0 + _mm_split(lora, a2)
    g = _mm_split(lora, g2)
    u = -w_lin
    softplus = jnp.maximum(u, 0.0) + jnp.log(1.0 + jnp.exp(-jnp.abs(u)))
    lw = -jnp.exp(-softplus - 0.5)
    a = _sigmoid(a_lin)
    kk = k * k_k
    kk = kk * lax.rsqrt(_seg_sum(kk * kk, seg) + 1e-12)
    k = k * (1.0 + (a - 1.0) * k_a)
    return r, lw, k, v, kk, kk * a, g


def _bmm(a, b):
    return lax.dot_general(a.astype(BF16), b.astype(BF16), (((2,), (1,)), ((0,), (0,))),
                           preferred_element_type=F32)


def _bmm_nt(a, b):
    return lax.dot_general(a.astype(BF16), b.astype(BF16), (((2,), (2,)), ((0,), (0,))),
                           preferred_element_type=F32)


def _bmm_tn(a, b):
    return lax.dot_general(a.astype(BF16), b.astype(BF16), (((1,), (1,)), ((0,), (0,))),
                           preferred_element_type=F32)


def _rwkv_kernel(x_ref, gain_ref, wz_ref, mu_ref, w0_ref, w2_ref, a0_ref, a2_ref, g2_ref, kk_ref, ka_ref,
                 lnw_ref, lnb_ref, rk_ref, seg_ref, o_ref, s_ref, zlast_ref, *, chunk):
    @pl.when(pl.program_id(1) == 0)
    def _():
        s_ref[...] = jnp.zeros_like(s_ref)
        zlast_ref[...] = jnp.zeros_like(zlast_ref)

    tb = x_ref.shape[1]
    d = RWKV_DIM
    z = jnp.dot(_rms(x_ref[0], gain_ref[...]).astype(BF16), wz_ref[...], preferred_element_type=F32)
    last = zlast_ref[7:8, :]
    zlast_ref[...] = z[tb - 8:, :]
    r, lw, k, v, kk, b, g = _rwkv_inputs(z, last, mu_ref[...], w0_ref[...], w2_ref[...], a0_ref[...],
                                         a2_ref[...], g2_ref[...], kk_ref[...], ka_ref[...], seg_ref[...])
    nc, nh = tb // chunk, d // HEAD_DIM
    ii = lax.broadcasted_iota(jnp.int32, (chunk, chunk), 0)
    jj = lax.broadcasted_iota(jnp.int32, (chunk, chunk), 1)
    incl, strict = ii >= jj, ii > jj
    tri = incl.astype(F32)
    eye = (ii == jj).astype(F32)
    rows = lambda c: slice(c * chunk, (c + 1) * chunk)
    lanes = lambda h: slice(h * HEAD_DIM, (h + 1) * HEAD_DIM)

    def split(x):
        return jnp.stack([x[rows(c), lanes(h)] for c in range(nc) for h in range(nh)])

    cums =[jnp.dot(tri, lw[rows(c)], precision=HIGHEST, preferred_element_type=F32) for c in range(nc)]
    cum = jnp.concatenate(cums, axis=0)
    total = jnp.concatenate([jnp.broadcast_to(cs[chunk - 1:], (chunk, d)) for cs in cums], axis=0)
    g_inv = jnp.exp(-cum)
    g_tail = jnp.exp(total - cum)
    kq, rq = split(kk * jnp.exp(cum - lw)), split(r * jnp.exp(cum))
    bi, ki = split(b * g_inv), split(k * g_inv)
    bh, kh = split(b * g_tail), split(k * g_tail)
    vs = split(v)
    decay = jnp.stack([jnp.exp(cs[chunk - 1:, lanes(h)]) for cs in cums for h in range(nh)])

    qr = jnp.concatenate([kq, rq], axis=1)
    ab, ak = _bmm_nt(qr, bi), _bmm_nt(qr, ki)
    a_b, a_rb = jnp.where(strict, ab[:, :chunk], 0.0), jnp.where(incl, ab[:, chunk:], 0.0)
    a_k, a_rk = jnp.where(strict, ak[:, :chunk], 0.0), jnp.where(incl, ak[:, chunk:], 0.0)
    n = nc * nh
    is_b = lax.broadcasted_iota(jnp.int32, (chunk, 2 * chunk), 1) < chunk
    bt = jnp.concatenate([-a_b, jnp.broadcast_to(eye, (n, chunk, chunk))], axis=2)
    for _ in range(int(np.log2(chunk))):
        prod = _bmm(bt[:, :, :chunk], bt)
        bt = jnp.where(is_b, prod, bt + prod)
    inv = bt[:, :, chunk:]
    wu = _bmm(inv, jnp.concatenate([kq, -_bmm(a_k, vs)], axis=2))
    aw = _bmm(a_rb, wu)
    qh = rq - aw[:, :, :HEAD_DIM]
    y0 = aw[:, :, HEAD_DIM:] + _bmm(a_rk, vs)
    mu = _bmm_tn(wu, bh)
    mp = mu[:, :HEAD_DIM]
    s_add = mu[:, HEAD_DIM:] + _bmm_tn(vs, kh)

    state = s_ref[...]
    for c in range(nc):
        sl = slice(c * nh, (c + 1) * nh)
        y = _bmm_nt(qh[sl], state) + y0[sl]
        for h in range(nh):
            o_ref[0, rows(c), lanes(h)] = y[h]
        state = state * decay[sl] - _bmm(state, mp[sl]) + s_add[sl]
    s_ref[...] = state

    seg = seg_ref[...]
    y = o_ref[0]
    yc = y - _seg_sum(y, seg) * (1.0 / HEAD_DIM)
    var = _seg_sum(yc * yc, seg) * (1.0 / HEAD_DIM)
    y = yc * lax.rsqrt(var + RWKV_GN_EPS) * lnw_ref[...] + lnb_ref[...]
    y = y + _seg_sum(r * k * rk_ref[...], seg) * v
    o_ref[0] = y * g


def _rwkv_branch(x, gain, wz, mu, w0, w2p, a0, a2p, g2p, k_k, k_a, ln_w, ln_b, r_k, seg_ones, *, tb):
    bsz, seq, dm = x.shape
    d = RWKV_DIM
    full = lambda a: pl.BlockSpec(a.shape, lambda bi, t: (0,) * a.ndim)
    params = (gain, wz, mu, w0, w2p, a0, a2p, g2p, k_k, k_a, ln_w, ln_b, r_k, seg_ones)
    return pl.pallas_call(
        functools.partial(_rwkv_kernel, chunk=WKV_CHUNK),
        grid=(bsz, seq // tb),
        in_specs=[pl.BlockSpec((1, tb, dm), lambda bi, t: (bi, t, 0))] + [full(a) for a in params],
        out_specs=pl.BlockSpec((1, tb, d), lambda bi, t: (bi, t, 0)),
        out_shape=jax.ShapeDtypeStruct((bsz, seq, d), F32),
        scratch_shapes=[pltpu.VMEM((d // HEAD_DIM, HEAD_DIM, HEAD_DIM), F32),
                        pltpu.VMEM((8, RWKV_COLS), F32)],
        compiler_params=_params("parallel", "arbitrary"),
        name="rwkv_branch",
    )(x, *params)


def _attn_kernel(q_ref, kp_ref, kc_ref, vp_ref, vc_ref, bias_ref, o_ref, l_ref):
    first = pl.program_id(2) == 0
    q = q_ref[0, 0] * (HEAD_DIM ** -0.5)
    kc, vc = kc_ref[0, 0], vc_ref[0, 0]
    subs = q.shape[0] // Q_BLOCK
    rows = lambda j: slice(j * Q_BLOCK, (j + 1) * Q_BLOCK)
    lanes = lambda h: slice(h * HEAD_DIM, (h + 1) * HEAD_DIM)
    k_prev = [kp_ref[0, 0]] + [kc[rows(j)] for j in range(subs - 1)]
    v_prev = [vp_ref[0, 0]] + [vc[rows(j)] for j in range(subs - 1)]
    work = [(j, h) for j in range(subs) for h in range(HEADS_PER_GROUP)]
    scores = []
    for j, h in work:
        bias = bias_ref[h]
        bias_prev = bias[:, :Q_BLOCK]
        if j == 0:
            bias_prev = jnp.where(first, MASKED, bias_prev)
        qh = q[rows(j), lanes(h)]
        scores.append((_mm_nt(qh, k_prev[j][:, lanes(h)]) + bias_prev,
                       _mm_nt(qh, kc[rows(j), lanes(h)]) + bias[:, Q_BLOCK:]))
    tops = [jnp.max(jnp.maximum(sp, sc), axis=-1, keepdims=True) for sp, sc in scores]
    probs = [(jnp.exp(sp - m), jnp.exp(sc - m)) for (sp, sc), m in zip(scores, tops)]
    dens = [jnp.sum(pp + pc, axis=-1, keepdims=True) for pp, pc in probs]
    accs = [_mm(pp, v_prev[j][:, lanes(h)]) + _mm(pc, vc[rows(j), lanes(h)])
            for (pp, pc), (j, h) in zip(probs, work)]
    lane = lax.broadcasted_iota(jnp.int32, (Q_BLOCK, LANES), 1)
    for j in range(subs):
        lse = jnp.zeros((Q_BLOCK, LANES), F32)
        for h in range(HEADS_PER_GROUP):
            i = j * HEADS_PER_GROUP + h
            o_ref[0, 0, rows(j), lanes(h)] = accs[i] * (1.0 / dens[i])
            lse = jnp.where(lane == h, tops[i] + jnp.log(dens[i]), lse)
        l_ref[0, 0, rows(j), :] = lse


def _attn_bias(group):
    window, dilation = ATTN_GROUPS[group]
    n_back = window // dilation
    heads = np.arange(group * HEADS_PER_GROUP, (group + 1) * HEADS_PER_GROUP)
    slopes = (2.0 ** (-8.0 * (heads + 1) / ATTN_HEADS)).astype(np.float32)
    delta = (np.arange(Q_BLOCK)[:, None] + Q_BLOCK) - np.arange(2 * Q_BLOCK)[None, :]
    valid = (delta >= 0) & (delta <= n_back)
    alibi = -slopes[:, None, None] * (dilation * delta).astype(np.float32)[None]
    return jnp.asarray(np.where(valid[None], alibi, np.float32(MASKED)).astype(np.float32))


def _attn_group(qkv, group):
    bsz, dil, comp, _ = qkv.shape
    assert comp % ATTN_TQ == 0
    sub = ATTN_TQ // Q_BLOCK
    col = lambda which: (lambda bi, r, n: (bi, r, n, which))
    col_prev = lambda which: (lambda bi, r, n: (bi, r, jnp.maximum(n * sub - 1, 0), which))
    cur, prev = (1, 1, ATTN_TQ, ATTN_OUT_DIM), (1, 1, Q_BLOCK, ATTN_OUT_DIM)
    bias = _attn_bias(group)
    return pl.pallas_call(
        _attn_kernel,
        grid=(bsz, dil, comp // ATTN_TQ),
        in_specs=[pl.BlockSpec(cur, col(0)),
                  pl.BlockSpec(prev, col_prev(1)), pl.BlockSpec(cur, col(1)),
                  pl.BlockSpec(prev, col_prev(2)), pl.BlockSpec(cur, col(2)),
                  pl.BlockSpec(bias.shape, lambda bi, r, n: (0, 0, 0))],
        out_specs=[pl.BlockSpec(cur, col(0)),
                   pl.BlockSpec((1, 1, ATTN_TQ, LANES), col(0))],
        out_shape=[jax.ShapeDtypeStruct((bsz, dil, comp, ATTN_OUT_DIM), F32),
                   jax.ShapeDtypeStruct((bsz, dil, comp, LANES), F32)],
        compiler_params=_params("parallel", "parallel", "parallel"),
        name=f"dilated_attn_{group}",
    )(qkv, qkv, qkv, qkv, qkv, bias)


def _route(logits):
    lane = lax.broadcasted_iota(jnp.int32, logits.shape, 1)
    neg = -jnp.inf
    first_max = lambda vals, mx: jnp.min(jnp.where(vals == mx, lane, LANES), axis=-1, keepdims=True)
    is_grp = (lane >= N_EXPERTS) & (lane < N_EXPERTS + N_EXPERT_GROUPS)
    gl = jnp.where(is_grp, logits, neg)
    gmax = jnp.max(gl, axis=-1, keepdims=True)
    grp_w = 1.0 / jnp.sum(jnp.exp(gl - gmax), axis=-1, keepdims=True)
    grp = first_max(gl, gmax) - N_EXPERTS
    in_grp = (lane >= grp * EXPERTS_PER_GROUP) & (lane < (grp + 1) * EXPERTS_PER_GROUP)
    el = jnp.where(in_grp, logits, neg)
    v1 = jnp.max(el, axis=-1, keepdims=True)
    i1 = first_max(el, v1)
    el2 = jnp.where(lane == i1, neg, el)
    v2 = jnp.max(el2, axis=-1, keepdims=True)
    i2 = first_max(el2, v2)
    e2 = jnp.exp(v2 - v1)
    w1 = grp_w / (1.0 + e2)
    return i1, i2, w1, w1 * e2


def _merge_kernel(x_ref, ya_ref, o0_ref, o1_ref, o2_ref, l0_ref, l1_ref, l2_ref, gt_ref,
                  wa_ref, wb_ref, wo_ref, gain_ref, wr_ref, ex_ref, tri_ref, x1_ref, h_ref, route_ref,
                  count_ref, o_scr, l_scr):
    outs, lses = [], []
    for grp, (o_ref, l_ref) in enumerate(((o0_ref, l0_ref), (o1_ref, l1_ref), (o2_ref, l2_ref))):
        dil = o_ref.shape[1]
        if dil == 1:
            outs.append(o_ref[0, 0])
            lses.append(l_ref[0, 0])
            continue
        n = o_ref.shape[2]
        chunks = o_ref.shape[3] // LANES
        for r in range(dil):
            l_scr[grp - 1, pl.ds(r, n, stride=dil), :] = l_ref[0, r]
            for c in range(chunks):
                o_scr[grp - 1, c, pl.ds(r, n, stride=dil), :] = o_ref[0, r, :, c * LANES:(c + 1) * LANES]
        outs.append(jnp.concatenate([o_scr[grp - 1, c] for c in range(chunks)], axis=1))
        lses.append(l_scr[grp - 1])
    l0, l1, l2 = lses
    m = jnp.maximum(jnp.maximum(l0, l1), l2)
    e0, e1, e2 = jnp.exp(l0 - m), jnp.exp(l1 - m), jnp.exp(l2 - m)
    inv = 1.0 / (e0 + e1 + e2)
    spread = ex_ref[...]
    yb = (_seg_sum(e0 * inv, spread) * outs[0] + _seg_sum(e1 * inv, spread) * outs[1]
          + _seg_sum(e2 * inv, spread) * outs[2])
    gates = gt_ref[...]
    merged = (gates[:, :D_MODEL] * _mm(ya_ref[...], wa_ref[...])
              + gates[:, D_MODEL:] * _mm(yb, wb_ref[...]))
    x1 = x_ref[...] + _mm(merged, wo_ref[...])
    x1_ref[...] = x1
    h = _rms(x1, gain_ref[...])
    h_ref[...] = h
    i1, i2, w1, w2 = _route(_mm_split(h, wr_ref[...]))

    @pl.when(pl.program_id(0) == 0)
    def _():
        count_ref[...] = jnp.zeros_like(count_ref)

    lane = lax.broadcasted_iota(jnp.int32, (x1.shape[0], LANES), 1)
    onehot = jnp.where((lane == i1) | (lane == i2), 1.0, 0.0)
    before = count_ref[0:1, :] + jnp.dot(tri_ref[...], onehot.astype(BF16), preferred_element_type=F32)
    rank1 = jnp.sum(jnp.where(lane == i1, before, 0.0), axis=-1, keepdims=True)
    rank2 = jnp.sum(jnp.where(lane == i2, before, 0.0), axis=-1, keepdims=True)
    last = x1.shape[0] - 1
    count_ref[...] = jnp.broadcast_to(before[last:, :] + onehot[last:, :], count_ref.shape)
    cols = (i1.astype(F32), i2.astype(F32), w1, w2, rank1, rank2)
    info = jnp.zeros(lane.shape, F32)
    for c, col in enumerate(cols):
        info = jnp.where(lane == c, col, info)
    route_ref[...] = info


def _merge(x, ya, outs, lses, gates, wa, wb, wo, gain, wr, spread, seq, *, tm):
    t = x.shape[0]
    tps = seq // tm
    tok = lambda n: pl.BlockSpec((tm, n), lambda i: (i, 0))
    full = lambda a: pl.BlockSpec(a.shape, lambda i: (0, 0))
    res = lambda a: pl.BlockSpec((1, a.shape[1], tm // a.shape[1], a.shape[3]),
                                 lambda i: (i // tps, 0, i % tps, 0))
    dilated = len(ATTN_GROUPS) - 1
    tri = jnp.asarray(np.tri(tm, k=-1), dtype=BF16)
    return pl.pallas_call(
        _merge_kernel,
        grid=(t // tm,),
        in_specs=[tok(D_MODEL), tok(RWKV_DIM)] + [res(a) for a in outs] + [res(a) for a in lses] + [
                  tok(2 * D_MODEL), full(wa), full(wb), full(wo), full(gain), full(wr), full(spread), full(tri)],
        out_specs=[tok(D_MODEL), tok(D_MODEL), tok(LANES), pl.BlockSpec((8, LANES), lambda i: (0, 0))],
        out_shape=[jax.ShapeDtypeStruct((t, D_MODEL), F32), jax.ShapeDtypeStruct((t, D_MODEL), F32),
                   jax.ShapeDtypeStruct((t, LANES), F32), jax.ShapeDtypeStruct((8, LANES), F32)],
        scratch_shapes=[pltpu.VMEM((dilated, ATTN_OUT_DIM // LANES, tm, LANES), F32),
                        pltpu.VMEM((dilated, tm, LANES), F32)],
        compiler_params=_params("arbitrary"),
        name="merge_route",
    )(x, ya, *outs, *lses, gates, wa, wb, wo, gain, wr, spread, tri)


ROUTE_E1, ROUTE_E2, ROUTE_W1, ROUTE_W2, ROUTE_R1, ROUTE_R2 = range(6)


def _moe_plan(counts, n_tiles):
    counts = counts[0, :N_EXPERTS].astype(jnp.int32)
    tile_end = jnp.cumsum((counts + MOE_TILE - 1) // MOE_TILE)
    n_valid = tile_end[-1]
    tile = jnp.minimum(jnp.arange(n_tiles, dtype=jnp.int32), n_valid - 1)
    tile_expert = jnp.sum(tile[:, None] >= tile_end[None, :], axis=1).astype(jnp.int32)
    return tile_expert, n_valid.reshape(1)


def _row_tiles(ref, rows):
    return jnp.concatenate([ref[pl.ds(c, rows, stride=ROW_TILE), :] for c in range(ROW_TILE)], axis=1)


def _store_row_tiles(ref, x):
    for c in range(ROW_TILE):
        ref[pl.ds(c, x.shape[0], stride=ROW_TILE), :] = x[:, c * LANES:(c + 1) * LANES]


def _row(ref, r):
    return ref.at[pl.ds(pl.multiple_of(r * ROW_TILE, ROW_TILE), ROW_TILE)]


def _dispatch_kernel(route_ref, count_ref, upper_ref, h_ref, xs_in_ref, xs_ref, pos_ref, rows_ref, pos_vmem,
                     pos_smem, sem, pos_sem):
    del xs_in_ref
    tm = h_ref.shape[0]
    tiles = jnp.floor((count_ref[...] + (MOE_TILE - 1)) * (1.0 / MOE_TILE))
    first_row = jnp.dot(tiles.astype(BF16), upper_ref[...], preferred_element_type=F32)[0:1, :] * MOE_TILE
    route = route_ref[...]
    lane = lax.broadcasted_iota(jnp.int32, route.shape, 1)
    col = lambda c: route[:, c:c + 1]
    place = lambda e, r: jnp.sum(jnp.where(lane == col(e).astype(jnp.int32), first_row, 0.0), axis=-1,
                                 keepdims=True) + col(r)
    both = jnp.where(lane == 0, place(ROUTE_E1, ROUTE_R1), jnp.where(lane == 1, place(ROUTE_E2, ROUTE_R2), 0.0))
    pos = both.T[0:ROW_TILE, :].astype(jnp.int32)
    pos_ref[0] = pos[0:2]
    pos_vmem[...] = pos
    to_smem = pltpu.make_async_copy(pos_vmem, pos_smem, pos_sem)
    to_smem.start()
    _store_row_tiles(rows_ref, h_ref[...])
    to_smem.wait()

    def copies(j):
        src = _row(rows_ref, j)
        return (pltpu.make_async_copy(src, _row(xs_ref, pos_smem[0, j]), sem.at[0]),
                pltpu.make_async_copy(src, _row(xs_ref, pos_smem[1, j]), sem.at[1]))

    def start(j, carry):
        for stream, cp in enumerate(copies(j)):
            cp.start(priority=stream)
        return carry

    lax.fori_loop(0, tm, start, 0, unroll=8)
    for stream in range(2):
        pltpu.make_async_copy(rows_ref, rows_ref, sem.at[stream]).wait()


def _dispatch(h, route, counts, n_rows, *, tm):
    t, d = h.shape
    upper = jnp.asarray(np.triu(np.ones((LANES, LANES)), k=1), dtype=BF16)
    tok = lambda n: pl.BlockSpec((tm, n), lambda i: (i, 0))
    full = lambda a: pl.BlockSpec(a.shape, lambda i: (0, 0))
    return pl.pallas_call(
        _dispatch_kernel,
        grid=(t // tm,),
        in_specs=[tok(LANES), full(counts), full(upper), tok(d), pl.BlockSpec(memory_space=pl.ANY)],
        out_specs=[pl.BlockSpec(memory_space=pl.ANY), pl.BlockSpec((1, 2, tm), lambda i: (i, 0, 0))],
        out_shape=[jax.ShapeDtypeStruct((n_rows * ROW_TILE, LANES), F32),
                   jax.ShapeDtypeStruct((t // tm, 2, tm), jnp.int32)],
        scratch_shapes=[pltpu.VMEM((tm * ROW_TILE, LANES), F32), pltpu.VMEM((ROW_TILE, tm), jnp.int32),
                        pltpu.SMEM((ROW_TILE, tm), jnp.int32), pltpu.SemaphoreType.DMA((2,)),
                        pltpu.SemaphoreType.DMA],
        input_output_aliases={4: 0},
        compiler_params=_params("arbitrary"),
        name="moe_dispatch",
    )(route, counts, upper, h, jnp.zeros((n_rows * ROW_TILE, LANES), F32))


def _expert_kernel(te_ref, nv_ref, x_ref, wg_ref, wu_ref, wd_ref, o_ref):
    live = pl.program_id(0) < nv_ref[0]

    @pl.when(live)
    def _():
        x = _row_tiles(x_ref, MOE_TILE)
        gate = _mm(x, wg_ref[0])
        up = _mm(x, wu_ref[0])
        _store_row_tiles(o_ref, _mm(gate * _sigmoid(gate) * up, wd_ref[0]))

    @pl.when(jnp.logical_not(live))
    def _():
        o_ref[...] = jnp.zeros_like(o_ref)


def _experts(xs, tile_expert, n_valid, wg, wu, wd):
    d = wg.shape[1]
    weight = lambda shape: pl.BlockSpec((1,) + shape, lambda i, te, nv: (te[i], 0, 0))
    blk = (MOE_TILE * ROW_TILE, LANES)
    return pl.pallas_call(
        _expert_kernel,
        grid_spec=pltpu.PrefetchScalarGridSpec(
            num_scalar_prefetch=2,
            grid=(xs.shape[0] // blk[0],),
            in_specs=[pl.BlockSpec(blk, lambda i, te, nv: (jnp.minimum(i, nv[0] - 1), 0)),
                      weight((d, D_FF_EXPERT)), weight((d, D_FF_EXPERT)), weight((D_FF_EXPERT, d))],
            out_specs=pl.BlockSpec(blk, lambda i, te, nv: (i, 0))),
        out_shape=jax.ShapeDtypeStruct(xs.shape, F32),
        compiler_params=_params("arbitrary"),
        name="moe_experts",
    )(tile_expert, n_valid, xs, wg, wu, wd)


def _ple_kernel(pos_ref, x_ref, route_ref, p_ref, gp_ref, wg_ref, wp_ref, gf_ref, ys_ref, o_ref,
                y_ref, sem):
    tm = x_ref.shape[0]
    step, steps = pl.program_id(0), pl.num_programs(0)

    def gather(tile, slot):
        base = tile * (2 * tm)

        def body(j, carry):
            for stream in range(2):
                pltpu.make_async_copy(_row(ys_ref, pos_ref[base + stream * tm + j]),
                                      _row(y_ref.at[slot, stream], j), sem.at[slot, stream]
                                      ).start(priority=stream)
            return carry

        lax.fori_loop(0, tm, body, 0, unroll=8)

    slot = step % 2

    @pl.when(step == 0)
    def _():
        gather(0, 0)

    @pl.when(step + 1 < steps)
    def _():
        gather(step + 1, 1 - slot)

    for stream in range(2):
        buf = y_ref.at[slot, stream]
        pltpu.make_async_copy(buf, buf, sem.at[slot, stream]).wait()
    route = route_ref[...]
    w1, w2 = route[:, ROUTE_W1:ROUTE_W1 + 1], route[:, ROUTE_W2:ROUTE_W2 + 1]
    x = x_ref[...] + (w1 * _row_tiles(y_ref.at[slot, 0], tm) + w2 * _row_tiles(y_ref.at[slot, 1], tm))
    gate = _sigmoid(_mm(_rms(x, gp_ref[...]), wg_ref[...]))
    x = x + gate * _mm(p_ref[...], wp_ref[...])
    o_ref[...] = _rms(x, gf_ref[...])


def _ple(x, route, pos, ys, p, gain_ple, w_gate, w_proj, gain_final, *, tm):
    t = x.shape[0]
    tok = lambda n: pl.BlockSpec((tm, n), lambda i, ps: (i, 0))
    full = lambda a: pl.BlockSpec(a.shape, lambda i, ps: (0, 0))
    return pl.pallas_call(
        _ple_kernel,
        grid_spec=pltpu.PrefetchScalarGridSpec(
            num_scalar_prefetch=1,
            grid=(t // tm,),
            in_specs=[tok(D_MODEL), tok(LANES), tok(PLE_DIM), full(gain_ple), full(w_gate), full(w_proj),
                      full(gain_final), pl.BlockSpec(memory_space=pl.ANY)],
            out_specs=tok(D_MODEL),
            scratch_shapes=[pltpu.VMEM((2, 2, tm * ROW_TILE, LANES), F32), pltpu.SemaphoreType.DMA((2, 2))]),
        out_shape=jax.ShapeDtypeStruct((t, D_MODEL), F32),
        compiler_params=_params("arbitrary"),
        name="ple_final",
    )(pos, x, route, p, gain_ple, w_gate, w_proj, gain_final, ys)


def _pad_rows(w, start, total):
    return jnp.zeros((total, w.shape[1]), w.dtype).at[start:start + w.shape[0]].set(w)


def _layer(x, p, seq, norm_mix, w_in, rwkv_mu, rwkv_w0, rwkv_w2, rwkv_a0, rwkv_a2, rwkv_g2, rwkv_k_k,
           rwkv_k_a, rwkv_r_k, rwkv_ln_w, rwkv_ln_b, w_proj_a, w_proj_b, w_out, norm_moe, w_router_group,
           w_router_expert, w_exp_gate, w_exp_up, w_exp_down, norm_ple, w_ple_gate, w_ple_proj, norm_final):
    t = x.shape[0]
    bsz = t // seq
    row = lambda a: a.reshape(1, -1)
    w_in16 = w_in.astype(BF16)
    c0, c1 = RWKV_COLS, RWKV_COLS + ATTN_COLS
    w_qkv = w_in16[:, c0:c1].reshape(D_MODEL, 3, len(ATTN_GROUPS), ATTN_OUT_DIM)
    w_qkv = jnp.swapaxes(w_qkv, 1, 2).reshape(D_MODEL, len(ATTN_GROUPS), 3 * ATTN_OUT_DIM)
    x3 = x.reshape(bsz, seq, D_MODEL)
    qkv = [_qkv_proj(x3, row(norm_mix), w_qkv[:, grp], grp, tm=1024) for grp in range(len(ATTN_GROUPS))]
    gates = _norm_matmul(x, row(norm_mix), w_in16[:, c1:], BF16, tm=1024, tn=D_MODEL, sigmoid=True)

    seg = np.arange(RWKV_DIM) // HEAD_DIM
    seg_ones = jnp.asarray(seg[:, None] == seg[None, :], dtype=BF16)
    ya = _rwkv_branch(
        x.reshape(bsz, seq, D_MODEL), row(norm_mix), w_in16[:, :c0], row(rwkv_mu), row(rwkv_w0),
        _pad_rows(rwkv_w2, 0, LORA_COLS), row(rwkv_a0), _pad_rows(rwkv_a2, W_LORA, LORA_COLS),
        _pad_rows(rwkv_g2, W_LORA + A_LORA, LORA_COLS), row(rwkv_k_k), row(rwkv_k_a), row(rwkv_ln_w),
        row(rwkv_ln_b), row(rwkv_r_k), seg_ones, tb=256).reshape(t, RWKV_DIM)

    attn = [_attn_group(qkv[grp], grp) for grp in range(len(ATTN_GROUPS))]

    head_of_lane = np.arange(ATTN_OUT_DIM) // HEAD_DIM
    spread = jnp.asarray(np.arange(LANES)[:, None] == head_of_lane[None, :], dtype=BF16)
    w_route = jnp.concatenate(
        [jnp.moveaxis(w_router_expert, 0, 1).reshape(D_MODEL, N_EXPERTS), w_router_group,
         jnp.zeros((D_MODEL, LANES - N_EXPERTS - N_EXPERT_GROUPS), F32)], axis=1)
    x1, h_moe, route, counts = _merge(x, ya, [o for o, _ in attn], [l for _, l in attn], gates,
                             w_proj_a.astype(BF16), w_proj_b.astype(BF16), w_out.astype(BF16),
                             row(norm_moe), w_route, spread, seq, tm=512)
    n_tiles = 2 * t // MOE_TILE + N_EXPERTS
    tile_expert, n_valid = _moe_plan(counts, n_tiles)
    dispatch_tm = 512
    xs, pos = _dispatch(h_moe, route, counts, n_tiles * MOE_TILE, tm=dispatch_tm)
    ys = _experts(xs, tile_expert, n_valid,
                  w_exp_gate.reshape(N_EXPERTS, D_MODEL, D_FF_EXPERT),
                  w_exp_up.reshape(N_EXPERTS, D_MODEL, D_FF_EXPERT),
                  w_exp_down.reshape(N_EXPERTS, D_FF_EXPERT, D_MODEL))
    return _ple(x1, route, pos.reshape(-1), ys, p, row(norm_ple), w_ple_gate.astype(BF16),
                w_ple_proj.astype(BF16), row(norm_final), tm=dispatch_tm)


def kernel(x, p, norm_mix, w_in, rwkv_mu, rwkv_w0, rwkv_w2, rwkv_a0, rwkv_a2, rwkv_g2, rwkv_k_k, rwkv_k_a, rwkv_r_k, rwkv_ln_w, rwkv_ln_b, w_proj_a, w_proj_b, w_out, norm_moe, w_router_group, w_router_expert, w_exp_gate, w_exp_up, w_exp_down, norm_ple, w_ple_gate, w_ple_proj, norm_final):
    bsz, seq, d = x.shape
    depth = w_in.shape[0]
    assert depth == 1, "the final norm is fused into the (single) layer"
    out = _layer(x.reshape(bsz * seq, d).astype(F32), p[0].reshape(bsz * seq, PLE_DIM), seq,
                 norm_mix[0], w_in[0], rwkv_mu[0], rwkv_w0[0], rwkv_w2[0], rwkv_a0[0], rwkv_a2[0],
                 rwkv_g2[0], rwkv_k_k[0], rwkv_k_a[0], rwkv_r_k[0], rwkv_ln_w[0], rwkv_ln_b[0],
                 w_proj_a[0], w_proj_b[0], w_out[0], norm_moe[0], w_router_group[0], w_router_expert[0],
                 w_exp_gate[0], w_exp_up[0], w_exp_down[0], norm_ple[0], w_ple_gate[0], w_ple_proj[0],
                 norm_final)
    return out.reshape(bsz, seq, d)
```

```python
import functools

import numpy as np
import jax
import jax.numpy as jnp
from jax import lax
from jax.experimental import pallas as pl
from jax.experimental.pallas import tpu as pltpu

F32 = jnp.float32
BF16 = jnp.bfloat16
HIGHEST = lax.Precision.HIGHEST

D_MODEL = 1024
PLE_DIM = 256
HEAD_DIM = 64
RWKV_HEADS = 8
RWKV_DIM = RWKV_HEADS * HEAD_DIM
W_LORA, A_LORA, G_LORA = 64, 64, 128
LORA_COLS = W_LORA + A_LORA + G_LORA
ATTN_GROUPS = ((128, 1), (512, 4), (2048, 16))
HEADS_PER_GROUP = 8
ATTN_HEADS = HEADS_PER_GROUP * len(ATTN_GROUPS)
ATTN_DIM = ATTN_HEADS * HEAD_DIM
ATTN_OUT_DIM = HEADS_PER_GROUP * HEAD_DIM
Q_BLOCK = 128
ATTN_TQ = 256
RWKV_COLS = 3 * RWKV_DIM + LORA_COLS
ATTN_COLS = 3 * ATTN_DIM
N_EXPERT_GROUPS = 4
EXPERTS_PER_GROUP = 8
N_EXPERTS = N_EXPERT_GROUPS * EXPERTS_PER_GROUP
D_FF_EXPERT = 256
NORM_EPS = 1e-6
RWKV_GN_EPS = 64e-5
LANES = 128
MASKED = -1e30
WKV_CHUNK = 64
VMEM_LIMIT = 48 * 1024 * 1024
MOE_TILE = 256
ROW_TILE = 8


def _mm(a, b):
    return jnp.dot(a.astype(BF16), b.astype(BF16), preferred_element_type=F32)


def _mm_nt(a, b):
    return lax.dot_general(a.astype(BF16), b.astype(BF16), (((1,), (1,)), ((), ())),
                           preferred_element_type=F32)


def _mm_tn(a, b):
    return lax.dot_general(a.astype(BF16), b.astype(BF16), (((0,), (0,)), ((), ())),
                           preferred_element_type=F32)


def _mm_split(a, b):
    a_hi, b_hi = a.astype(BF16), b.astype(BF16)
    a_lo = (a - a_hi.astype(F32)).astype(BF16)
    b_lo = (b - b_hi.astype(F32)).astype(BF16)
    dot = functools.partial(jnp.dot, preferred_element_type=F32)
    return dot(a_hi, b_hi) + (dot(a_hi, b_lo) + dot(a_lo, b_hi))


def _sigmoid(x):
    return 1.0 / (1.0 + jnp.exp(-x))


def _rms(x, gain):
    return x * lax.rsqrt(jnp.mean(x * x, axis=-1, keepdims=True) + NORM_EPS) * gain


def _params(*sem):
    return pltpu.CompilerParams(dimension_semantics=sem, vmem_limit_bytes=VMEM_LIMIT)


def _norm_matmul_kernel(x_ref, g_ref, w_ref, o_ref, h_ref, *, sigmoid):
    @pl.when(pl.program_id(1) == 0)
    def _():
        h_ref[...] = _rms(x_ref[...], g_ref[...]).astype(BF16)

    acc = jnp.dot(h_ref[...], w_ref[...], preferred_element_type=F32)
    if sigmoid:
        acc = _sigmoid(acc)
    o_ref[...] = acc.astype(o_ref.dtype)


def _norm_matmul(x, gain, w, out_dtype, *, tm, tn, sigmoid=False):
    t, d = x.shape
    n = w.shape[1]
    return pl.pallas_call(
        functools.partial(_norm_matmul_kernel, sigmoid=sigmoid),
        grid=(t // tm, n // tn),
        in_specs=[pl.BlockSpec((tm, d), lambda i, j: (i, 0)),
                  pl.BlockSpec((1, d), lambda i, j: (0, 0)),
                  pl.BlockSpec((d, tn), lambda i, j: (0, j))],
        out_specs=pl.BlockSpec((tm, tn), lambda i, j: (i, j)),
        out_shape=jax.ShapeDtypeStruct((t, n), out_dtype),
        scratch_shapes=[pltpu.VMEM((tm, d), BF16)],
        compiler_params=_params("parallel", "arbitrary"),
        name="norm_matmul",
    )(x, gain, w)


def _qkv_rows_kernel(x_ref, g_ref, w_ref, o_ref, stage_ref, *, dil):
    h = _rms(x_ref[0], g_ref[...])
    if dil == 1:
        o_ref[0, 0] = _mm(h, w_ref[...]).astype(BF16)
        return
    chunks = h.shape[1] // LANES
    for c in range(chunks):
        stage_ref[c] = h[:, c * LANES:(c + 1) * LANES]
    n = h.shape[0] // dil
    for r in range(dil):
        rows = jnp.concatenate([stage_ref[c, pl.ds(r, n, stride=dil), :] for c in range(chunks)], axis=1)
        o_ref[0, r] = _mm(rows, w_ref[...]).astype(BF16)


def _qkv_strided_kernel(x_hbm, g_ref, w_ref, o_ref, x_buf, sem):
    tm = x_buf.shape[1]
    n_res, n_blk = pl.num_programs(1), pl.num_programs(2)
    step = (pl.program_id(0) * n_res + pl.program_id(1)) * n_blk + pl.program_id(2)
    steps = pl.num_programs(0) * n_res * n_blk

    def fetch(s, slot):
        b, r, i = s // (n_res * n_blk), (s // n_blk) % n_res, s % n_blk
        return pltpu.make_async_copy(x_hbm.at[b, pl.ds(i * tm, tm), r], x_buf.at[slot], sem.at[slot])

    slot = step % 2

    @pl.when(step == 0)
    def _():
        fetch(0, 0).start()

    @pl.when(step + 1 < steps)
    def _():
        fetch(step + 1, 1 - slot).start()

    fetch(step, slot).wait()
    o_ref[0, 0] = _mm(_rms(x_buf[slot], g_ref[...]), w_ref[...]).astype(BF16)


def _qkv_proj(x, gain, w, group, *, tm):
    bsz, seq, d = x.shape
    dil = ATTN_GROUPS[group][1]
    comp = seq // dil
    tn = w.shape[1]
    out_shape = jax.ShapeDtypeStruct((bsz, dil, comp, tn), BF16)
    if dil % ROW_TILE:
        return pl.pallas_call(
            functools.partial(_qkv_rows_kernel, dil=dil),
            grid=(bsz, seq // tm),
            in_specs=[pl.BlockSpec((1, tm, d), lambda b, i: (b, i, 0)),
                      pl.BlockSpec((1, d), lambda b, i: (0, 0)),
                      pl.BlockSpec((d, tn), lambda b, i: (0, 0))],
            out_specs=pl.BlockSpec((1, dil, tm // dil, tn), lambda b, i: (b, 0, i, 0)),
            out_shape=out_shape,
            scratch_shapes=[pltpu.VMEM((d // LANES, tm, LANES), F32)],
            compiler_params=_params("parallel", "parallel"),
            name=f"qkv_proj_{group}",
        )(x, gain, w)
    tm = min(tm, comp)
    return pl.pallas_call(
        _qkv_strided_kernel,
        grid=(bsz, dil, comp // tm),
        in_specs=[pl.BlockSpec(memory_space=pl.ANY),
                  pl.BlockSpec((1, d), lambda b, r, i: (0, 0)),
                  pl.BlockSpec((d, tn), lambda b, r, i: (0, 0))],
        out_specs=pl.BlockSpec((1, 1, tm, tn), lambda b, r, i: (b, r, i, 0)),
        out_shape=out_shape,
        scratch_shapes=[pltpu.VMEM((2, tm, d), F32), pltpu.SemaphoreType.DMA((2,))],
        compiler_params=_params("arbitrary", "arbitrary", "arbitrary"),
        name=f"qkv_proj_{group}",
    )(x.reshape(bsz, comp, dil, d), gain, w)


def _seg_sum(x, seg_ones):
    hi = x.astype(BF16)
    lo = (x - hi.astype(F32)).astype(BF16)
    return (jnp.dot(hi, seg_ones, preferred_element_type=F32)
            + jnp.dot(lo, seg_ones, preferred_element_type=F32))


def _rwkv_inputs(z, last, mu, w0, w2, a0, a2, g2, k_k, k_a, seg):
    row = lax.broadcasted_iota(jnp.int32, z.shape, 0)
    prev = jnp.where(row == 0, last, pltpu.roll(z, 1, axis=0))
    zs = z + (prev - z) * mu
    d = RWKV_DIM
    r, k, v = zs[:, :d], zs[:, d:2 * d], zs[:, 2 * d:3 * d]
    lora = zs[:, 3 * d:]
    lane = lax.broadcasted_iota(jnp.int32, lora.shape, 1)
    lora = jnp.where(lane < W_LORA, jnp.tanh(lora),
                     jnp.where(lane < W_LORA + A_LORA, lora, _sigmoid(lora)))
    w_lin = w0 + _mm_split(lora, w2)
    a_lin = a0 + _mm_split(lora, a2)
    g = _mm_split(lora, g2)
    u = -w_lin
    softplus = jnp.maximum(u, 0.0) + jnp.log(1.0 + jnp.exp(-jnp.abs(u)))
    lw = -jnp.exp(-softplus - 0.5)
    a = _sigmoid(a_lin)
    kk = k * k_k
    kk = kk * lax.rsqrt(_seg_sum(kk * kk, seg) + 1e-12)
    k = k * (1.0 + (a - 1.0) * k_a)
    return r, lw, k, v, kk, kk * a, g


def _bmm(a, b):
    return lax.dot_general(a.astype(BF16), b.astype(BF16), (((2,), (1,)), ((0,), (0,))),
                           preferred_element_type=F32)


def _bmm_nt(a, b):
    return lax.dot_general(a.astype(BF16), b.astype(BF16), (((2,), (2,)), ((0,), (0,))),
                           preferred_element_type=F32)


def _bmm_tn(a, b):
    return lax.dot_general(a.astype(BF16), b.astype(BF16), (((1,), (1,)), ((0,), (0,))),
                           preferred_element_type=F32)


def _rwkv_kernel(x_ref, gain_ref, wz_ref, mu_ref, w0_ref, w2_ref, a0_ref, a2_ref, g2_ref, kk_ref, ka_ref,
                 lnw_ref, lnb_ref, rk_ref, seg_ref, o_ref, s_ref, zlast_ref, *, chunk):
    @pl.when(pl.program_id(1) == 0)
    def _():
        s_ref[...] = jnp.zeros_like(s_ref)
        zlast_ref[...] = jnp.zeros_like(zlast_ref)

    tb = x_ref.shape[1]
    d = RWKV_DIM
    z = jnp.dot(_rms(x_ref[0], gain_ref[...]).astype(BF16), wz_ref[...], preferred_element_type=F32)
    last = zlast_ref[7:8, :]
    zlast_ref[...] = z[tb - 8:, :]
    r, lw, k, v, kk, b, g = _rwkv_inputs(z, last, mu_ref[...], w0_ref[...], w2_ref[...], a0_ref[...],
                                         a2_ref[...], g2_ref[...], kk_ref[...], ka_ref[...], seg_ref[...])
    nc, nh = tb // chunk, d // HEAD_DIM
    ii = lax.broadcasted_iota(jnp.int32, (chunk, chunk), 0)
    jj = lax.broadcasted_iota(jnp.int32, (chunk, chunk), 1)
    incl, strict = ii >= jj, ii > jj
    tri = incl.astype(F32)
    eye = (ii == jj).astype(F32)
    rows = lambda c: slice(c * chunk, (c + 1) * chunk)
    lanes = lambda h: slice(h * HEAD_DIM, (h + 1) * HEAD_DIM)

    def split(x):
        return jnp.stack([x[rows(c), lanes(h)] for c in range(nc) for h in range(nh)])

    cums =[jnp.dot(tri, lw[rows(c)], precision=HIGHEST, preferred_element_type=F32) for c in range(nc)]
    cum = jnp.concatenate(cums, axis=0)
    total = jnp.concatenate([jnp.broadcast_to(cs[chunk - 1:], (chunk, d)) for cs in cums], axis=0)
    g_inv = jnp.exp(-cum)
    g_tail = jnp.exp(total - cum)
    kq, rq = split(kk * jnp.exp(cum - lw)), split(r * jnp.exp(cum))
    bi, ki = split(b * g_inv), split(k * g_inv)
    bh, kh = split(b * g_tail), split(k * g_tail)
    vs = split(v)
    decay = jnp.stack([jnp.exp(cs[chunk - 1:, lanes(h)]) for cs in cums for h in range(nh)])

    qr = jnp.concatenate([kq, rq], axis=1)
    ab, ak = _bmm_nt(qr, bi), _bmm_nt(qr, ki)
    a_b, a_rb = jnp.where(strict, ab[:, :chunk], 0.0), jnp.where(incl, ab[:, chunk:], 0.0)
    a_k, a_rk = jnp.where(strict, ak[:, :chunk], 0.0), jnp.where(incl, ak[:, chunk:], 0.0)
    n = nc * nh
    is_b = lax.broadcasted_iota(jnp.int32, (chunk, 2 * chunk), 1) < chunk
    bt = jnp.concatenate([-a_b, jnp.broadcast_to(eye, (n, chunk, chunk))], axis=2)
    for _ in range(int(np.log2(chunk))):
        prod = _bmm(bt[:, :, :chunk], bt)
        bt = jnp.where(is_b, prod, bt + prod)
    inv = bt[:, :, chunk:]
    wu = _bmm(inv, jnp.concatenate([kq, -_bmm(a_k, vs)], axis=2))
    aw = _bmm(a_rb, wu)
    qh = rq - aw[:, :, :HEAD_DIM]
    y0 = aw[:, :, HEAD_DIM:] + _bmm(a_rk, vs)
    mu = _bmm_tn(wu, bh)
    mp = mu[:, :HEAD_DIM]
    s_add = mu[:, HEAD_DIM:] + _bmm_tn(vs, kh)

    state = s_ref[...]
    for c in range(nc):
        sl = slice(c * nh, (c + 1) * nh)
        y = _bmm_nt(qh[sl], state) + y0[sl]
        for h in range(nh):
            o_ref[0, rows(c), lanes(h)] = y[h]
        state = state * decay[sl] - _bmm(state, mp[sl]) + s_add[sl]
    s_ref[...] = state

    seg = seg_ref[...]
    y = o_ref[0]
    yc = y - _seg_sum(y, seg) * (1.0 / HEAD_DIM)
    var = _seg_sum(yc * yc, seg) * (1.0 / HEAD_DIM)
    y = yc * lax.rsqrt(var + RWKV_GN_EPS) * lnw_ref[...] + lnb_ref[...]
    y = y + _seg_sum(r * k * rk_ref[...], seg) * v
    o_ref[0] = y * g


def _rwkv_branch(x, gain, wz, mu, w0, w2p, a0, a2p, g2p, k_k, k_a, ln_w, ln_b, r_k, seg_ones, *, tb):
    bsz, seq, dm = x.shape
    d = RWKV_DIM
    full = lambda a: pl.BlockSpec(a.shape, lambda bi, t: (0,) * a.ndim)
    params = (gain, wz, mu, w0, w2p, a0, a2p, g2p, k_k, k_a, ln_w, ln_b, r_k, seg_ones)
    return pl.pallas_call(
        functools.partial(_rwkv_kernel, chunk=WKV_CHUNK),
        grid=(bsz, seq // tb),
        in_specs=[pl.BlockSpec((1, tb, dm), lambda bi, t: (bi, t, 0))] + [full(a) for a in params],
        out_specs=pl.BlockSpec((1, tb, d), lambda bi, t: (bi, t, 0)),
        out_shape=jax.ShapeDtypeStruct((bsz, seq, d), F32),
        scratch_shapes=[pltpu.VMEM((d // HEAD_DIM, HEAD_DIM, HEAD_DIM), F32),
                        pltpu.VMEM((8, RWKV_COLS), F32)],
        compiler_params=_params("parallel", "arbitrary"),
        name="rwkv_branch",
    )(x, *params)


def _attn_kernel(q_ref, kp_ref, kc_ref, vp_ref, vc_ref, bias_ref, o_ref, l_ref):
    first = pl.program_id(2) == 0
    q = q_ref[0, 0] * (HEAD_DIM ** -0.5)
    kc, vc = kc_ref[0, 0], vc_ref[0, 0]
    subs = q.shape[0] // Q_BLOCK
    rows = lambda j: slice(j * Q_BLOCK, (j + 1) * Q_BLOCK)
    lanes = lambda h: slice(h * HEAD_DIM, (h + 1) * HEAD_DIM)
    k_prev = [kp_ref[0, 0]] + [kc[rows(j)] for j in range(subs - 1)]
    v_prev = [vp_ref[0, 0]] + [vc[rows(j)] for j in range(subs - 1)]
    work = [(j, h) for j in range(subs) for h in range(HEADS_PER_GROUP)]
    scores = []
    for j, h in work:
        bias = bias_ref[h]
        bias_prev = bias[:, :Q_BLOCK]
        if j == 0:
            bias_prev = jnp.where(first, MASKED, bias_prev)
        qh = q[rows(j), lanes(h)]
        scores.append((_mm_nt(qh, k_prev[j][:, lanes(h)]) + bias_prev,
                       _mm_nt(qh, kc[rows(j), lanes(h)]) + bias[:, Q_BLOCK:]))
    tops = [jnp.max(jnp.maximum(sp, sc), axis=-1, keepdims=True) for sp, sc in scores]
    probs = [(jnp.exp(sp - m), jnp.exp(sc - m)) for (sp, sc), m in zip(scores, tops)]
    dens = [jnp.sum(pp + pc, axis=-1, keepdims=True) for pp, pc in probs]
    accs = [_mm(pp, v_prev[j][:, lanes(h)]) + _mm(pc, vc[rows(j), lanes(h)])
            for (pp, pc), (j, h) in zip(probs, work)]
    lane = lax.broadcasted_iota(jnp.int32, (Q_BLOCK, LANES), 1)
    for j in range(subs):
        lse = jnp.zeros((Q_BLOCK, LANES), F32)
        for h in range(HEADS_PER_GROUP):
            i = j * HEADS_PER_GROUP + h
            o_ref[0, 0, rows(j), lanes(h)] = accs[i] * (1.0 / dens[i])
            lse = jnp.where(lane == h, tops[i] + jnp.log(dens[i]), lse)
        l_ref[0, 0, rows(j), :] = lse


def _attn_bias(group):
    window, dilation = ATTN_GROUPS[group]
    n_back = window // dilation
    heads = np.arange(group * HEADS_PER_GROUP, (group + 1) * HEADS_PER_GROUP)
    slopes = (2.0 ** (-8.0 * (heads + 1) / ATTN_HEADS)).astype(np.float32)
    delta = (np.arange(Q_BLOCK)[:, None] + Q_BLOCK) - np.arange(2 * Q_BLOCK)[None, :]
    valid = (delta >= 0) & (delta <= n_back)
    alibi = -slopes[:, None, None] * (dilation * delta).astype(np.float32)[None]
    return jnp.asarray(np.where(valid[None], alibi, np.float32(MASKED)).astype(np.float32))


def _attn_group(qkv, group):
    bsz, dil, comp, _ = qkv.shape
    assert comp % ATTN_TQ == 0
    sub = ATTN_TQ // Q_BLOCK
    col = lambda which: (lambda bi, r, n: (bi, r, n, which))
    col_prev = lambda which: (lambda bi, r, n: (bi, r, jnp.maximum(n * sub - 1, 0), which))
    cur, prev = (1, 1, ATTN_TQ, ATTN_OUT_DIM), (1, 1, Q_BLOCK, ATTN_OUT_DIM)
    bias = _attn_bias(group)
    return pl.pallas_call(
        _attn_kernel,
        grid=(bsz, dil, comp // ATTN_TQ),
        in_specs=[pl.BlockSpec(cur, col(0)),
                  pl.BlockSpec(prev, col_prev(1)), pl.BlockSpec(cur, col(1)),
                  pl.BlockSpec(prev, col_prev(2)), pl.BlockSpec(cur, col(2)),
                  pl.BlockSpec(bias.shape, lambda bi, r, n: (0, 0, 0))],
        out_specs=[pl.BlockSpec(cur, col(0)),
                   pl.BlockSpec((1, 1, ATTN_TQ, LANES), col(0))],
        out_shape=[jax.ShapeDtypeStruct((bsz, dil, comp, ATTN_OUT_DIM), F32),
                   jax.ShapeDtypeStruct((bsz, dil, comp, LANES), F32)],
        compiler_params=_params("parallel", "parallel", "parallel"),
        name=f"dilated_attn_{group}",
    )(qkv, qkv, qkv, qkv, qkv, bias)


def _route(logits):
    lane = lax.broadcasted_iota(jnp.int32, logits.shape, 1)
    neg = -jnp.inf
    first_max = lambda vals, mx: jnp.min(jnp.where(vals == mx, lane, LANES), axis=-1, keepdims=True)
    is_grp = (lane >= N_EXPERTS) & (lane < N_EXPERTS + N_EXPERT_GROUPS)
    gl = jnp.where(is_grp, logits, neg)
    gmax = jnp.max(gl, axis=-1, keepdims=True)
    grp_w = 1.0 / jnp.sum(jnp.exp(gl - gmax), axis=-1, keepdims=True)
    grp = first_max(gl, gmax) - N_EXPERTS
    in_grp = (lane >= grp * EXPERTS_PER_GROUP) & (lane < (grp + 1) * EXPERTS_PER_GROUP)
    el = jnp.where(in_grp, logits, neg)
    v1 = jnp.max(el, axis=-1, keepdims=True)
    i1 = first_max(el, v1)
    el2 = jnp.where(lane == i1, neg, el)
    v2 = jnp.max(el2, axis=-1, keepdims=True)
    i2 = first_max(el2, v2)
    e2 = jnp.exp(v2 - v1)
    w1 = grp_w / (1.0 + e2)
    return i1, i2, w1, w1 * e2


def _merge_kernel(x_ref, ya_ref, o0_ref, o1_ref, o2_ref, l0_ref, l1_ref, l2_ref, gt_ref,
                  wa_ref, wb_ref, wo_ref, gain_ref, wr_ref, ex_ref, tri_ref, x1_ref, h_ref, route_ref,
                  count_ref, o_scr, l_scr):
    outs, lses = [], []
    for grp, (o_ref, l_ref) in enumerate(((o0_ref, l0_ref), (o1_ref, l1_ref), (o2_ref, l2_ref))):
        dil = o_ref.shape[1]
        if dil == 1:
            outs.append(o_ref[0, 0])
            lses.append(l_ref[0, 0])
            continue
        n = o_ref.shape[2]
        chunks = o_ref.shape[3] // LANES
        for r in range(dil):
            l_scr[grp - 1, pl.ds(r, n, stride=dil), :] = l_ref[0, r]
            for c in range(chunks):
                o_scr[grp - 1, c, pl.ds(r, n, stride=dil), :] = o_ref[0, r, :, c * LANES:(c + 1) * LANES]
        outs.append(jnp.concatenate([o_scr[grp - 1, c] for c in range(chunks)], axis=1))
        lses.append(l_scr[grp - 1])
    l0, l1, l2 = lses
    m = jnp.maximum(jnp.maximum(l0, l1), l2)
    e0, e1, e2 = jnp.exp(l0 - m), jnp.exp(l1 - m), jnp.exp(l2 - m)
    inv = 1.0 / (e0 + e1 + e2)
    spread = ex_ref[...]
    yb = (_seg_sum(e0 * inv, spread) * outs[0] + _seg_sum(e1 * inv, spread) * outs[1]
          + _seg_sum(e2 * inv, spread) * outs[2])
    gates = gt_ref[...]
    merged = (gates[:, :D_MODEL] * _mm(ya_ref[...], wa_ref[...])
              + gates[:, D_MODEL:] * _mm(yb, wb_ref[...]))
    x1 = x_ref[...] + _mm(merged, wo_ref[...])
    x1_ref[...] = x1
    h = _rms(x1, gain_ref[...])
    h_ref[...] = h
    i1, i2, w1, w2 = _route(_mm_split(h, wr_ref[...]))

    @pl.when(pl.program_id(0) == 0)
    def _():
        count_ref[...] = jnp.zeros_like(count_ref)

    lane = lax.broadcasted_iota(jnp.int32, (x1.shape[0], LANES), 1)
    onehot = jnp.where((lane == i1) | (lane == i2), 1.0, 0.0)
    before = count_ref[0:1, :] + jnp.dot(tri_ref[...], onehot.astype(BF16), preferred_element_type=F32)
    rank1 = jnp.sum(jnp.where(lane == i1, before, 0.0), axis=-1, keepdims=True)
    rank2 = jnp.sum(jnp.where(lane == i2, before, 0.0), axis=-1, keepdims=True)
    last = x1.shape[0] - 1
    count_ref[...] = jnp.broadcast_to(before[last:, :] + onehot[last:, :], count_ref.shape)
    cols = (i1.astype(F32), i2.astype(F32), w1, w2, rank1, rank2)
    info = jnp.zeros(lane.shape, F32)
    for c, col in enumerate(cols):
        info = jnp.where(lane == c, col, info)
    route_ref[...] = info


def _merge(x, ya, outs, lses, gates, wa, wb, wo, gain, wr, spread, seq, *, tm):
    t = x.shape[0]
    tps = seq // tm
    tok = lambda n: pl.BlockSpec((tm, n), lambda i: (i, 0))
    full = lambda a: pl.BlockSpec(a.shape, lambda i: (0, 0))
    res = lambda a: pl.BlockSpec((1, a.shape[1], tm // a.shape[1], a.shape[3]),
                                 lambda i: (i // tps, 0, i % tps, 0))
    dilated = len(ATTN_GROUPS) - 1
    tri = jnp.asarray(np.tri(tm, k=-1), dtype=BF16)
    return pl.pallas_call(
        _merge_kernel,
        grid=(t // tm,),
        in_specs=[tok(D_MODEL), tok(RWKV_DIM)] + [res(a) for a in outs] + [res(a) for a in lses] + [
                  tok(2 * D_MODEL), full(wa), full(wb), full(wo), full(gain), full(wr), full(spread), full(tri)],
        out_specs=[tok(D_MODEL), tok(D_MODEL), tok(LANES), pl.BlockSpec((8, LANES), lambda i: (0, 0))],
        out_shape=[jax.ShapeDtypeStruct((t, D_MODEL), F32), jax.ShapeDtypeStruct((t, D_MODEL), F32),
                   jax.ShapeDtypeStruct((t, LANES), F32), jax.ShapeDtypeStruct((8, LANES), F32)],
        scratch_shapes=[pltpu.VMEM((dilated, ATTN_OUT_DIM // LANES, tm, LANES), F32),
                        pltpu.VMEM((dilated, tm, LANES), F32)],
        compiler_params=_params("arbitrary"),
        name="merge_route",
    )(x, ya, *outs, *lses, gates, wa, wb, wo, gain, wr, spread, tri)


ROUTE_E1, ROUTE_E2, ROUTE_W1, ROUTE_W2, ROUTE_R1, ROUTE_R2 = range(6)


def _moe_plan(counts, n_tiles):
    counts = counts[0, :N_EXPERTS].astype(jnp.int32)
    tile_end = jnp.cumsum((counts + MOE_TILE - 1) // MOE_TILE)
    n_valid = tile_end[-1]
    tile = jnp.minimum(jnp.arange(n_tiles, dtype=jnp.int32), n_valid - 1)
    tile_expert = jnp.sum(tile[:, None] >= tile_end[None, :], axis=1).astype(jnp.int32)
    return tile_expert, n_valid.reshape(1)


def _row_tiles(ref, rows):
    return jnp.concatenate([ref[pl.ds(c, rows, stride=ROW_TILE), :] for c in range(ROW_TILE)], axis=1)


def _store_row_tiles(ref, x):
    for c in range(ROW_TILE):
        ref[pl.ds(c, x.shape[0], stride=ROW_TILE), :] = x[:, c * LANES:(c + 1) * LANES]


def _row(ref, r):
    return ref.at[pl.ds(pl.multiple_of(r * ROW_TILE, ROW_TILE), ROW_TILE)]


def _dispatch_kernel(route_ref, count_ref, upper_ref, h_ref, xs_in_ref, xs_ref, pos_ref, rows_ref, pos_vmem,
                     pos_smem, sem, pos_sem):
    del xs_in_ref
    tm = h_ref.shape[0]
    tiles = jnp.floor((count_ref[...] + (MOE_TILE - 1)) * (1.0 / MOE_TILE))
    first_row = jnp.dot(tiles.astype(BF16), upper_ref[...], preferred_element_type=F32)[0:1, :] * MOE_TILE
    route = route_ref[...]
    lane = lax.broadcasted_iota(jnp.int32, route.shape, 1)
    col = lambda c: route[:, c:c + 1]
    place = lambda e, r: jnp.sum(jnp.where(lane == col(e).astype(jnp.int32), first_row, 0.0), axis=-1,
                                 keepdims=True) + col(r)
    both = jnp.where(lane == 0, place(ROUTE_E1, ROUTE_R1), jnp.where(lane == 1, place(ROUTE_E2, ROUTE_R2), 0.0))
    pos = both.T[0:ROW_TILE, :].astype(jnp.int32)
    pos_ref[0] = pos[0:2]
    pos_vmem[...] = pos
    to_smem = pltpu.make_async_copy(pos_vmem, pos_smem, pos_sem)
    to_smem.start()
    _store_row_tiles(rows_ref, h_ref[...])
    to_smem.wait()

    def copies(j):
        src = _row(rows_ref, j)
        return (pltpu.make_async_copy(src, _row(xs_ref, pos_smem[0, j]), sem.at[0]),
                pltpu.make_async_copy(src, _row(xs_ref, pos_smem[1, j]), sem.at[1]))

    def start(j, carry):
        for stream, cp in enumerate(copies(j)):
            cp.start(priority=stream)
        return carry

    lax.fori_loop(0, tm, start, 0, unroll=8)
    for stream in range(2):
        pltpu.make_async_copy(rows_ref, rows_ref, sem.at[stream]).wait()


def _dispatch(h, route, counts, n_rows, *, tm):
    t, d = h.shape
    upper = jnp.asarray(np.triu(np.ones((LANES, LANES)), k=1), dtype=BF16)
    tok = lambda n: pl.BlockSpec((tm, n), lambda i: (i, 0))
    full = lambda a: pl.BlockSpec(a.shape, lambda i: (0, 0))
    return pl.pallas_call(
        _dispatch_kernel,
        grid=(t // tm,),
        in_specs=[tok(LANES), full(counts), full(upper), tok(d), pl.BlockSpec(memory_space=pl.ANY)],
        out_specs=[pl.BlockSpec(memory_space=pl.ANY), pl.BlockSpec((1, 2, tm), lambda i: (i, 0, 0))],
        out_shape=[jax.ShapeDtypeStruct((n_rows * ROW_TILE, LANES), F32),
                   jax.ShapeDtypeStruct((t // tm, 2, tm), jnp.int32)],
        scratch_shapes=[pltpu.VMEM((tm * ROW_TILE, LANES), F32), pltpu.VMEM((ROW_TILE, tm), jnp.int32),
                        pltpu.SMEM((ROW_TILE, tm), jnp.int32), pltpu.SemaphoreType.DMA((2,)),
                        pltpu.SemaphoreType.DMA],
        input_output_aliases={4: 0},
        compiler_params=_params("arbitrary"),
        name="moe_dispatch",
    )(route, counts, upper, h, jnp.zeros((n_rows * ROW_TILE, LANES), F32))


def _expert_kernel(te_ref, nv_ref, x_ref, wg_ref, wu_ref, wd_ref, o_ref):
    live = pl.program_id(0) < nv_ref[0]

    @pl.when(live)
    def _():
        x = _row_tiles(x_ref, MOE_TILE)
        gate = _mm(x, wg_ref[0])
        up = _mm(x, wu_ref[0])
        _store_row_tiles(o_ref, _mm(gate * _sigmoid(gate) * up, wd_ref[0]))

    @pl.when(jnp.logical_not(live))
    def _():
        o_ref[...] = jnp.zeros_like(o_ref)


def _experts(xs, tile_expert, n_valid, wg, wu, wd):
    d = wg.shape[1]
    weight = lambda shape: pl.BlockSpec((1,) + shape, lambda i, te, nv: (te[i], 0, 0))
    blk = (MOE_TILE * ROW_TILE, LANES)
    return pl.pallas_call(
        _expert_kernel,
        grid_spec=pltpu.PrefetchScalarGridSpec(
            num_scalar_prefetch=2,
            grid=(xs.shape[0] // blk[0],),
            in_specs=[pl.BlockSpec(blk, lambda i, te, nv: (jnp.minimum(i, nv[0] - 1), 0)),
                      weight((d, D_FF_EXPERT)), weight((d, D_FF_EXPERT)), weight((D_FF_EXPERT, d))],
            out_specs=pl.BlockSpec(blk, lambda i, te, nv: (i, 0))),
        out_shape=jax.ShapeDtypeStruct(xs.shape, F32),
        compiler_params=_params("arbitrary"),
        name="moe_experts",
    )(tile_expert, n_valid, xs, wg, wu, wd)


def _ple_kernel(pos_ref, x_ref, route_ref, p_ref, gp_ref, wg_ref, wp_ref, gf_ref, ys_ref, o_ref,
                y_ref, sem):
    tm = x_ref.shape[0]
    step, steps = pl.program_id(0), pl.num_programs(0)

    def gather(tile, slot):
        base = tile * (2 * tm)

        def body(j, carry):
            for stream in range(2):
                pltpu.make_async_copy(_row(ys_ref, pos_ref[base + stream * tm + j]),
                                      _row(y_ref.at[slot, stream], j), sem.at[slot, stream]
                                      ).start(priority=stream)
            return carry

        lax.fori_loop(0, tm, body, 0, unroll=8)

    slot = step % 2

    @pl.when(step == 0)
    def _():
        gather(0, 0)

    @pl.when(step + 1 < steps)
    def _():
        gather(step + 1, 1 - slot)

    for stream in range(2):
        buf = y_ref.at[slot, stream]
        pltpu.make_async_copy(buf, buf, sem.at[slot, stream]).wait()
    route = route_ref[...]
    w1, w2 = route[:, ROUTE_W1:ROUTE_W1 + 1], route[:, ROUTE_W2:ROUTE_W2 + 1]
    x = x_ref[...] + (w1 * _row_tiles(y_ref.at[slot, 0], tm) + w2 * _row_tiles(y_ref.at[slot, 1], tm))
    gate = _sigmoid(_mm(_rms(x, gp_ref[...]), wg_ref[...]))
    x = x + gate * _mm(p_ref[...], wp_ref[...])
    o_ref[...] = _rms(x, gf_ref[...])


def _ple(x, route, pos, ys, p, gain_ple, w_gate, w_proj, gain_final, *, tm):
    t = x.shape[0]
    tok = lambda n: pl.BlockSpec((tm, n), lambda i, ps: (i, 0))
    full = lambda a: pl.BlockSpec(a.shape, lambda i, ps: (0, 0))
    return pl.pallas_call(
        _ple_kernel,
        grid_spec=pltpu.PrefetchScalarGridSpec(
            num_scalar_prefetch=1,
            grid=(t // tm,),
            in_specs=[tok(D_MODEL), tok(LANES), tok(PLE_DIM), full(gain_ple), full(w_gate), full(w_proj),
                      full(gain_final), pl.BlockSpec(memory_space=pl.ANY)],
            out_specs=tok(D_MODEL),
            scratch_shapes=[pltpu.VMEM((2, 2, tm * ROW_TILE, LANES), F32), pltpu.SemaphoreType.DMA((2, 2))]),
        out_shape=jax.ShapeDtypeStruct((t, D_MODEL), F32),
        compiler_params=_params("arbitrary"),
        name="ple_final",
    )(pos, x, route, p, gain_ple, w_gate, w_proj, gain_final, ys)


def _pad_rows(w, start, total):
    return jnp.zeros((total, w.shape[1]), w.dtype).at[start:start + w.shape[0]].set(w)


def _layer(x, p, seq, norm_mix, w_in, rwkv_mu, rwkv_w0, rwkv_w2, rwkv_a0, rwkv_a2, rwkv_g2, rwkv_k_k,
           rwkv_k_a, rwkv_r_k, rwkv_ln_w, rwkv_ln_b, w_proj_a, w_proj_b, w_out, norm_moe, w_router_group,
           w_router_expert, w_exp_gate, w_exp_up, w_exp_down, norm_ple, w_ple_gate, w_ple_proj, norm_final):
    t = x.shape[0]
    bsz = t // seq
    row = lambda a: a.reshape(1, -1)
    w_in16 = w_in.astype(BF16)
    c0, c1 = RWKV_COLS, RWKV_COLS + ATTN_COLS
    w_qkv = w_in16[:, c0:c1].reshape(D_MODEL, 3, len(ATTN_GROUPS), ATTN_OUT_DIM)
    w_qkv = jnp.swapaxes(w_qkv, 1, 2).reshape(D_MODEL, len(ATTN_GROUPS), 3 * ATTN_OUT_DIM)
    x3 = x.reshape(bsz, seq, D_MODEL)
    qkv = [_qkv_proj(x3, row(norm_mix), w_qkv[:, grp], grp, tm=1024) for grp in range(len(ATTN_GROUPS))]
    gates = _norm_matmul(x, row(norm_mix), w_in16[:, c1:], BF16, tm=1024, tn=D_MODEL, sigmoid=True)

    seg = np.arange(RWKV_DIM) // HEAD_DIM
    seg_ones = jnp.asarray(seg[:, None] == seg[None, :], dtype=BF16)
    ya = _rwkv_branch(
        x.reshape(bsz, seq, D_MODEL), row(norm_mix), w_in16[:, :c0], row(rwkv_mu), row(rwkv_w0),
        _pad_rows(rwkv_w2, 0, LORA_COLS), row(rwkv_a0), _pad_rows(rwkv_a2, W_LORA, LORA_COLS),
        _pad_rows(rwkv_g2, W_LORA + A_LORA, LORA_COLS), row(rwkv_k_k), row(rwkv_k_a), row(rwkv_ln_w),
        row(rwkv_ln_b), row(rwkv_r_k), seg_ones, tb=256).reshape(t, RWKV_DIM)

    attn = [_attn_group(qkv[grp], grp) for grp in range(len(ATTN_GROUPS))]

    head_of_lane = np.arange(ATTN_OUT_DIM) // HEAD_DIM
    spread = jnp.asarray(np.arange(LANES)[:, None] == head_of_lane[None, :], dtype=BF16)
    w_route = jnp.concatenate(
        [jnp.moveaxis(w_router_expert, 0, 1).reshape(D_MODEL, N_EXPERTS), w_router_group,
         jnp.zeros((D_MODEL, LANES - N_EXPERTS - N_EXPERT_GROUPS), F32)], axis=1)
    x1, h_moe, route, counts = _merge(x, ya, [o for o, _ in attn], [l for _, l in attn], gates,
                             w_proj_a.astype(BF16), w_proj_b.astype(BF16), w_out.astype(BF16),
                             row(norm_moe), w_route, spread, seq, tm=512)
    n_tiles = 2 * t // MOE_TILE + N_EXPERTS
    tile_expert, n_valid = _moe_plan(counts, n_tiles)
    dispatch_tm = 512
    xs, pos = _dispatch(h_moe, route, counts, n_tiles * MOE_TILE, tm=dispatch_tm)
    ys = _experts(xs, tile_expert, n_valid,
                  w_exp_gate.reshape(N_EXPERTS, D_MODEL, D_FF_EXPERT),
                  w_exp_up.reshape(N_EXPERTS, D_MODEL, D_FF_EXPERT),
                  w_exp_down.reshape(N_EXPERTS, D_FF_EXPERT, D_MODEL))
    return _ple(x1, route, pos.reshape(-1), ys, p, row(norm_ple), w_ple_gate.astype(BF16),
                w_ple_proj.astype(BF16), row(norm_final), tm=dispatch_tm)


def kernel(x, p, norm_mix, w_in, rwkv_mu, rwkv_w0, rwkv_w2, rwkv_a0, rwkv_a2, rwkv_g2, rwkv_k_k, rwkv_k_a, rwkv_r_k, rwkv_ln_w, rwkv_ln_b, w_proj_a, w_proj_b, w_out, norm_moe, w_router_group, w_router_expert, w_exp_gate, w_exp_up, w_exp_down, norm_ple, w_ple_gate, w_ple_proj, norm_final):
    bsz, seq, d = x.shape
    depth = w_in.shape[0]
    assert depth == 1, "the final norm is fused into the (single) layer"
    out = _layer(x.reshape(bsz * seq, d).astype(F32), p[0].reshape(bsz * seq, PLE_DIM), seq,
                 norm_mix[0], w_in[0], rwkv_mu[0], rwkv_w0[0], rwkv_w2[0], rwkv_a0[0], rwkv_a2[0],
                 rwkv_g2[0], rwkv_k_k[0], rwkv_k_a[0], rwkv_r_k[0], rwkv_ln_w[0], rwkv_ln_b[0],
                 w_proj_a[0], w_proj_b[0], w_out[0], norm_moe[0], w_router_group[0], w_router_expert[0],
                 w_exp_gate[0], w_exp_up[0], w_exp_down[0], norm_ple[0], w_ple_gate[0], w_ple_proj[0],
                 norm_final)
    return out.reshape(bsz, seq, d)
```

```python
import functools

import numpy as np
import jax
import jax.numpy as jnp
from jax import lax
from jax.experimental import pallas as pl
from jax.experimental.pallas import tpu as pltpu

F32 = jnp.float32
BF16 = jnp.bfloat16
HIGHEST = lax.Precision.HIGHEST

D_MODEL = 1024
PLE_DIM = 256
HEAD_DIM = 64
RWKV_HEADS = 8
RWKV_DIM = RWKV_HEADS * HEAD_DIM
W_LORA, A_LORA, G_LORA = 64, 64, 128
LORA_COLS = W_LORA + A_LORA + G_LORA
ATTN_GROUPS = ((128, 1), (512, 4), (2048, 16))
HEADS_PER_GROUP = 8
ATTN_HEADS = HEADS_PER_GROUP * len(ATTN_GROUPS)
ATTN_DIM = ATTN_HEADS * HEAD_DIM
ATTN_OUT_DIM = HEADS_PER_GROUP * HEAD_DIM
Q_BLOCK = 128
ATTN_TQ = 256
RWKV_COLS = 3 * RWKV_DIM + LORA_COLS
ATTN_COLS = 3 * ATTN_DIM
N_EXPERT_GROUPS = 4
EXPERTS_PER_GROUP = 8
N_EXPERTS = N_EXPERT_GROUPS * EXPERTS_PER_GROUP
D_FF_EXPERT = 256
NORM_EPS = 1e-6
RWKV_GN_EPS = 64e-5
LANES = 128
MASKED = -1e30
WKV_CHUNK = 64
VMEM_LIMIT = 48 * 1024 * 1024
MOE_TILE = 256
ROW_TILE = 8


def _mm(a, b):
    return jnp.dot(a.astype(BF16), b.astype(BF16), preferred_element_type=F32)


def _mm_nt(a, b):
    return lax.dot_general(a.astype(BF16), b.astype(BF16), (((1,), (1,)), ((), ())),
                           preferred_element_type=F32)


def _mm_tn(a, b):
    return lax.dot_general(a.astype(BF16), b.astype(BF16), (((0,), (0,)), ((), ())),
                           preferred_element_type=F32)


def _mm_split(a, b):
    a_hi, b_hi = a.astype(BF16), b.astype(BF16)
    a_lo = (a - a_hi.astype(F32)).astype(BF16)
    b_lo = (b - b_hi.astype(F32)).astype(BF16)
    dot = functools.partial(jnp.dot, preferred_element_type=F32)
    return dot(a_hi, b_hi) + (dot(a_hi, b_lo) + dot(a_lo, b_hi))


def _sigmoid(x):
    return 1.0 / (1.0 + jnp.exp(-x))


def _rms(x, gain):
    return x * lax.rsqrt(jnp.mean(x * x, axis=-1, keepdims=True) + NORM_EPS) * gain


def _params(*sem):
    return pltpu.CompilerParams(dimension_semantics=sem, vmem_limit_bytes=VMEM_LIMIT)


def _norm_matmul_kernel(x_ref, g_ref, w_ref, o_ref, h_ref, *, sigmoid):
    @pl.when(pl.program_id(1) == 0)
    def _():
        h_ref[...] = _rms(x_ref[...], g_ref[...]).astype(BF16)

    acc = jnp.dot(h_ref[...], w_ref[...], preferred_element_type=F32)
    if sigmoid:
        acc = _sigmoid(acc)
    o_ref[...] = acc.astype(o_ref.dtype)


def _norm_matmul(x, gain, w, out_dtype, *, tm, tn, sigmoid=False):
    t, d = x.shape
    n = w.shape[1]
    return pl.pallas_call(
        functools.partial(_norm_matmul_kernel, sigmoid=sigmoid),
        grid=(t // tm, n // tn),
        in_specs=[pl.BlockSpec((tm, d), lambda i, j: (i, 0)),
                  pl.BlockSpec((1, d), lambda i, j: (0, 0)),
                  pl.BlockSpec((d, tn), lambda i, j: (0, j))],
        out_specs=pl.BlockSpec((tm, tn), lambda i, j: (i, j)),
        out_shape=jax.ShapeDtypeStruct((t, n), out_dtype),
        scratch_shapes=[pltpu.VMEM((tm, d), BF16)],
        compiler_params=_params("parallel", "arbitrary"),
        name="norm_matmul",
    )(x, gain, w)


def _qkv_rows_kernel(x_ref, g_ref, w_ref, o_ref, stage_ref, *, dil):
    h = _rms(x_ref[0], g_ref[...])
    if dil == 1:
        o_ref[0, 0] = _mm(h, w_ref[...]).astype(BF16)
        return
    chunks = h.shape[1] // LANES
    for c in range(chunks):
        stage_ref[c] = h[:, c * LANES:(c + 1) * LANES]
    n = h.shape[0] // dil
    for r in range(dil):
        rows = jnp.concatenate([stage_ref[c, pl.ds(r, n, stride=dil), :] for c in range(chunks)], axis=1)
        o_ref[0, r] = _mm(rows, w_ref[...]).astype(BF16)


def _qkv_strided_kernel(x_hbm, g_ref, w_ref, o_ref, x_buf, sem):
    tm = x_buf.shape[1]
    n_res, n_blk = pl.num_programs(1), pl.num_programs(2)
    step = (pl.program_id(0) * n_res + pl.program_id(1)) * n_blk + pl.program_id(2)
    steps = pl.num_programs(0) * n_res * n_blk

    def fetch(s, slot):
        b, r, i = s // (n_res * n_blk), (s // n_blk) % n_res, s % n_blk
        return pltpu.make_async_copy(x_hbm.at[b, pl.ds(i * tm, tm), r], x_buf.at[slot], sem.at[slot])

    slot = step % 2

    @pl.when(step == 0)
    def _():
        fetch(0, 0).start()

    @pl.when(step + 1 < steps)
    def _():
        fetch(step + 1, 1 - slot).start()

    fetch(step, slot).wait()
    o_ref[0, 0] = _mm(_rms(x_buf[slot], g_ref[...]), w_ref[...]).astype(BF16)


def _qkv_proj(x, gain, w, group, *, tm):
    bsz, seq, d = x.shape
    dil = ATTN_GROUPS[group][1]
    comp = seq // dil
    tn = w.shape[1]
    out_shape = jax.ShapeDtypeStruct((bsz, dil, comp, tn), BF16)
    if dil % ROW_TILE:
        return pl.pallas_call(
            functools.partial(_qkv_rows_kernel, dil=dil),
            grid=(bsz, seq // tm),
            in_specs=[pl.BlockSpec((1, tm, d), lambda b, i: (b, i, 0)),
                      pl.BlockSpec((1, d), lambda b, i: (0, 0)),
                      pl.BlockSpec((d, tn), lambda b, i: (0, 0))],
            out_specs=pl.BlockSpec((1, dil, tm // dil, tn), lambda b, i: (b, 0, i, 0)),
            out_shape=out_shape,
            scratch_shapes=[pltpu.VMEM((d // LANES, tm, LANES), F32)],
            compiler_params=_params("parallel", "parallel"),
            name=f"qkv_proj_{group}",
        )(x, gain, w)
    tm = min(tm, comp)
    return pl.pallas_call(
        _qkv_strided_kernel,
        grid=(bsz, dil, comp // tm),
        in_specs=[pl.BlockSpec(memory_space=pl.ANY),
                  pl.BlockSpec((1, d), lambda b, r, i: (0, 0)),
                  pl.BlockSpec((d, tn), lambda b, r, i: (0, 0))],
        out_specs=pl.BlockSpec((1, 1, tm, tn), lambda b, r, i: (b, r, i, 0)),
        out_shape=out_shape,
        scratch_shapes=[pltpu.VMEM((2, tm, d), F32), pltpu.SemaphoreType.DMA((2,))],
        compiler_params=_params("arbitrary", "arbitrary", "arbitrary"),
        name=f"qkv_proj_{group}",
    )(x.reshape(bsz, comp, dil, d), gain, w)


def _seg_sum(x, seg_ones):
    hi = x.astype(BF16)
    lo = (x - hi.astype(F32)).astype(BF16)
    return (jnp.dot(hi, seg_ones, preferred_element_type=F32)
            + jnp.dot(lo, seg_ones, preferred_element_type=F32))


def _rwkv_inputs(z, last, mu, w0, w2, a0, a2, g2, k_k, k_a, seg):
    row = lax.broadcasted_iota(jnp.int32, z.shape, 0)
    prev = jnp.where(row == 0, last, pltpu.roll(z, 1, axis=0))
    zs = z + (prev - z) * mu
    d = RWKV_DIM
    r, k, v = zs[:, :d], zs[:, d:2 * d], zs[:, 2 * d:3 * d]
    lora = zs[:, 3 * d:]
    lane = lax.broadcasted_iota(jnp.int32, lora.shape, 1)
    lora = jnp.where(lane < W_LORA, jnp.tanh(lora),
                     jnp.where(lane < W_LORA + A_LORA, lora, _sigmoid(lora)))
    w_lin = w0 + _mm_split(lora, w2)
    a_lin = a0 + _mm_split(lora, a2)
    g = _mm_split(lora, g2)
    u = -w_lin
    softplus = jnp.maximum(u, 0.0) + jnp.log(1.0 + jnp.exp(-jnp.abs(u)))
    lw = -jnp.exp(-softplus - 0.5)
    a = _sigmoid(a_lin)
    kk = k * k_k
    kk = kk * lax.rsqrt(_seg_sum(kk * kk, seg) + 1e-12)
    k = k * (1.0 + (a - 1.0) * k_a)
    return r, lw, k, v, kk, kk * a, g


def _bmm(a, b):
    return lax.dot_general(a.astype(BF16), b.astype(BF16), (((2,), (1,)), ((0,), (0,))),
                           preferred_element_type=F32)


def _bmm_nt(a, b):
    return lax.dot_general(a.astype(BF16), b.astype(BF16), (((2,), (2,)), ((0,), (0,))),
                           preferred_element_type=F32)


def _bmm_tn(a, b):
    return lax.dot_general(a.astype(BF16), b.astype(BF16), (((1,), (1,)), ((0,), (0,))),
                           preferred_element_type=F32)


def _rwkv_kernel(x_ref, gain_ref, wz_ref, mu_ref, w0_ref, w2_ref, a0_ref, a2_ref, g2_ref, kk_ref, ka_ref,
                 lnw_ref, lnb_ref, rk_ref, seg_ref, o_ref, blank_ref, s_ref, zlast_ref, *, chunk):
    @pl.when(pl.program_id(1) == 0)
    def _():
        s_ref[...] = jnp.zeros_like(s_ref)
        zlast_ref[...] = jnp.zeros_like(zlast_ref)

    tb = x_ref.shape[1]
    d = RWKV_DIM
    z = jnp.dot(_rms(x_ref[0], gain_ref[...]).astype(BF16), wz_ref[...], preferred_element_type=F32)
    last = zlast_ref[7:8, :]
    zlast_ref[...] = z[tb - 8:, :]
    r, lw, k, v, kk, b, g = _rwkv_inputs(z, last, mu_ref[...], w0_ref[...], w2_ref[...], a0_ref[...],
                                         a2_ref[...], g2_ref[...], kk_ref[...], ka_ref[...], seg_ref[...])
    nc, nh = tb // chunk, d // HEAD_DIM
    ii = lax.broadcasted_iota(jnp.int32, (chunk, chunk), 0)
    jj = lax.broadcasted_iota(jnp.int32, (chunk, chunk), 1)
    incl, strict = ii >= jj, ii > jj
    tri = incl.astype(F32)
    eye = (ii == jj).astype(F32)
    rows = lambda c: slice(c * chunk, (c + 1) * chunk)
    lanes = lambda h: slice(h * HEAD_DIM, (h + 1) * HEAD_DIM)

    def split(x):
        return jnp.stack([x[rows(c), lanes(h)] for c in range(nc) for h in range(nh)])

    cums =[jnp.dot(tri, lw[rows(c)], precision=HIGHEST, preferred_element_type=F32) for c in range(nc)]
    cum = jnp.concatenate(cums, axis=0)
    total = jnp.concatenate([jnp.broadcast_to(cs[chunk - 1:], (chunk, d)) for cs in cums], axis=0)
    g_inv = jnp.exp(-cum)
    g_tail = jnp.exp(total - cum)
    kq, rq = split(kk * jnp.exp(cum - lw)), split(r * jnp.exp(cum))
    bi, ki = split(b * g_inv), split(k * g_inv)
    bh, kh = split(b * g_tail), split(k * g_tail)
    vs = split(v)
    decay = jnp.stack([jnp.exp(cs[chunk - 1:, lanes(h)]) for cs in cums for h in range(nh)])

    qr = jnp.concatenate([kq, rq], axis=1)
    ab, ak = _bmm_nt(qr, bi), _bmm_nt(qr, ki)
    a_b, a_rb = jnp.where(strict, ab[:, :chunk], 0.0), jnp.where(incl, ab[:, chunk:], 0.0)
    a_k, a_rk = jnp.where(strict, ak[:, :chunk], 0.0), jnp.where(incl, ak[:, chunk:], 0.0)
    n = nc * nh
    is_b = lax.broadcasted_iota(jnp.int32, (chunk, 2 * chunk), 1) < chunk
    bt = jnp.concatenate([-a_b, jnp.broadcast_to(eye, (n, chunk, chunk))], axis=2)
    for _ in range(int(np.log2(chunk))):
        prod = _bmm(bt[:, :, :chunk], bt)
        bt = jnp.where(is_b, prod, bt + prod)
    inv = bt[:, :, chunk:]
    wu = _bmm(inv, jnp.concatenate([kq, -_bmm(a_k, vs)], axis=2))
    aw = _bmm(a_rb, wu)
    qh = rq - aw[:, :, :HEAD_DIM]
    y0 = aw[:, :, HEAD_DIM:] + _bmm(a_rk, vs)
    mu = _bmm_tn(wu, bh)
    mp = mu[:, :HEAD_DIM]
    s_add = mu[:, HEAD_DIM:] + _bmm_tn(vs, kh)

    state = s_ref[...]
    for c in range(nc):
        sl = slice(c * nh, (c + 1) * nh)
        y = _bmm_nt(qh[sl], state) + y0[sl]
        for h in range(nh):
            o_ref[0, rows(c), lanes(h)] = y[h]
        state = state * decay[sl] - _bmm(state, mp[sl]) + s_add[sl]
    s_ref[...] = state

    seg = seg_ref[...]
    y = o_ref[0]
    yc = y - _seg_sum(y, seg) * (1.0 / HEAD_DIM)
    var = _seg_sum(yc * yc, seg) * (1.0 / HEAD_DIM)
    y = yc * lax.rsqrt(var + RWKV_GN_EPS) * lnw_ref[...] + lnb_ref[...]
    y = y + _seg_sum(r * k * rk_ref[...], seg) * v
    o_ref[0] = y * g
    blank_ref[...] = jnp.zeros_like(blank_ref)


def _rwkv_branch(x, gain, wz, mu, w0, w2p, a0, a2p, g2p, k_k, k_a, ln_w, ln_b, r_k, seg_ones, blank_rows,
                 *, tb):
    bsz, seq, dm = x.shape
    d = RWKV_DIM
    steps = seq // tb
    full = lambda a: pl.BlockSpec(a.shape, lambda bi, t: (0,) * a.ndim)
    params = (gain, wz, mu, w0, w2p, a0, a2p, g2p, k_k, k_a, ln_w, ln_b, r_k, seg_ones)
    blank_blk = blank_rows // (bsz * steps)
    return pl.pallas_call(
        functools.partial(_rwkv_kernel, chunk=WKV_CHUNK),
        grid=(bsz, steps),
        in_specs=[pl.BlockSpec((1, tb, dm), lambda bi, t: (bi, t, 0))] + [full(a) for a in params],
        out_specs=[pl.BlockSpec((1, tb, d), lambda bi, t: (bi, t, 0)),
                   pl.BlockSpec((blank_blk, LANES), lambda bi, t: (bi * steps + t, 0))],
        out_shape=[jax.ShapeDtypeStruct((bsz, seq, d), F32),
                   jax.ShapeDtypeStruct((blank_rows, LANES), F32)],
        scratch_shapes=[pltpu.VMEM((d // HEAD_DIM, HEAD_DIM, HEAD_DIM), F32),
                        pltpu.VMEM((8, RWKV_COLS), F32)],
        compiler_params=_params("parallel", "arbitrary"),
        name="rwkv_branch",
    )(x, *params)


def _attn_kernel(q_ref, kp_ref, kc_ref, vp_ref, vc_ref, bias_ref, o_ref, l_ref):
    first = pl.program_id(2) == 0
    q = q_ref[0, 0] * (HEAD_DIM ** -0.5)
    kc, vc = kc_ref[0, 0], vc_ref[0, 0]
    subs = q.shape[0] // Q_BLOCK
    rows = lambda j: slice(j * Q_BLOCK, (j + 1) * Q_BLOCK)
    lanes = lambda h: slice(h * HEAD_DIM, (h + 1) * HEAD_DIM)
    k_prev = [kp_ref[0, 0]] + [kc[rows(j)] for j in range(subs - 1)]
    v_prev = [vp_ref[0, 0]] + [vc[rows(j)] for j in range(subs - 1)]
    work = [(j, h) for j in range(subs) for h in range(HEADS_PER_GROUP)]
    scores = []
    for j, h in work:
        bias = bias_ref[h]
        bias_prev = bias[:, :Q_BLOCK]
        if j == 0:
            bias_prev = jnp.where(first, MASKED, bias_prev)
        qh = q[rows(j), lanes(h)]
        scores.append((_mm_nt(qh, k_prev[j][:, lanes(h)]) + bias_prev,
                       _mm_nt(qh, kc[rows(j), lanes(h)]) + bias[:, Q_BLOCK:]))
    tops = [jnp.max(jnp.maximum(sp, sc), axis=-1, keepdims=True) for sp, sc in scores]
    probs = [(jnp.exp(sp - m), jnp.exp(sc - m)) for (sp, sc), m in zip(scores, tops)]
    dens = [jnp.sum(pp + pc, axis=-1, keepdims=True) for pp, pc in probs]
    accs = [_mm(pp, v_prev[j][:, lanes(h)]) + _mm(pc, vc[rows(j), lanes(h)])
            for (pp, pc), (j, h) in zip(probs, work)]
    lane = lax.broadcasted_iota(jnp.int32, (Q_BLOCK, LANES), 1)
    for j in range(subs):
        lse = jnp.zeros((Q_BLOCK, LANES), F32)
        for h in range(HEADS_PER_GROUP):
            i = j * HEADS_PER_GROUP + h
            o_ref[0, 0, rows(j), lanes(h)] = accs[i] * (1.0 / dens[i])
            lse = jnp.where(lane == h, tops[i] + jnp.log(dens[i]), lse)
        l_ref[0, 0, rows(j), :] = lse


def _attn_bias(group):
    window, dilation = ATTN_GROUPS[group]
    n_back = window // dilation
    heads = np.arange(group * HEADS_PER_GROUP, (group + 1) * HEADS_PER_GROUP)
    slopes = (2.0 ** (-8.0 * (heads + 1) / ATTN_HEADS)).astype(np.float32)
    delta = (np.arange(Q_BLOCK)[:, None] + Q_BLOCK) - np.arange(2 * Q_BLOCK)[None, :]
    valid = (delta >= 0) & (delta <= n_back)
    alibi = -slopes[:, None, None] * (dilation * delta).astype(np.float32)[None]
    return jnp.asarray(np.where(valid[None], alibi, np.float32(MASKED)).astype(np.float32))


def _attn_group(qkv, group):
    bsz, dil, comp, _ = qkv.shape
    assert comp % ATTN_TQ == 0
    sub = ATTN_TQ // Q_BLOCK
    col = lambda which: (lambda bi, r, n: (bi, r, n, which))
    col_prev = lambda which: (lambda bi, r, n: (bi, r, jnp.maximum(n * sub - 1, 0), which))
    cur, prev = (1, 1, ATTN_TQ, ATTN_OUT_DIM), (1, 1, Q_BLOCK, ATTN_OUT_DIM)
    bias = _attn_bias(group)
    return pl.pallas_call(
        _attn_kernel,
        grid=(bsz, dil, comp // ATTN_TQ),
        in_specs=[pl.BlockSpec(cur, col(0)),
                  pl.BlockSpec(prev, col_prev(1)), pl.BlockSpec(cur, col(1)),
                  pl.BlockSpec(prev, col_prev(2)), pl.BlockSpec(cur, col(2)),
                  pl.BlockSpec(bias.shape, lambda bi, r, n: (0, 0, 0))],
        out_specs=[pl.BlockSpec(cur, col(0)),
                   pl.BlockSpec((1, 1, ATTN_TQ, LANES), col(0))],
        out_shape=[jax.ShapeDtypeStruct((bsz, dil, comp, ATTN_OUT_DIM), F32),
                   jax.ShapeDtypeStruct((bsz, dil, comp, LANES), F32)],
        compiler_params=_params("parallel", "parallel", "parallel"),
        name=f"dilated_attn_{group}",
    )(qkv, qkv, qkv, qkv, qkv, bias)


def _route(logits):
    lane = lax.broadcasted_iota(jnp.int32, logits.shape, 1)
    neg = -jnp.inf
    first_max = lambda vals, mx: jnp.min(jnp.where(vals == mx, lane, LANES), axis=-1, keepdims=True)
    is_grp = (lane >= N_EXPERTS) & (lane < N_EXPERTS + N_EXPERT_GROUPS)
    gl = jnp.where(is_grp, logits, neg)
    gmax = jnp.max(gl, axis=-1, keepdims=True)
    grp_w = 1.0 / jnp.sum(jnp.exp(gl - gmax), axis=-1, keepdims=True)
    grp = first_max(gl, gmax) - N_EXPERTS
    in_grp = (lane >= grp * EXPERTS_PER_GROUP) & (lane < (grp + 1) * EXPERTS_PER_GROUP)
    el = jnp.where(in_grp, logits, neg)
    v1 = jnp.max(el, axis=-1, keepdims=True)
    i1 = first_max(el, v1)
    el2 = jnp.where(lane == i1, neg, el)
    v2 = jnp.max(el2, axis=-1, keepdims=True)
    i2 = first_max(el2, v2)
    e2 = jnp.exp(v2 - v1)
    w1 = grp_w / (1.0 + e2)
    return i1, i2, w1, w1 * e2


def _merge_kernel(x_ref, ya_ref, o0_ref, o1_ref, o2_ref, l0_ref, l1_ref, l2_ref, gt_ref,
                  wa_ref, wb_ref, wo_ref, gain_ref, wr_ref, ex_ref, tri_ref, x1_ref, h_ref, route_ref,
                  count_ref, o_scr, l_scr):
    outs, lses = [], []
    for grp, (o_ref, l_ref) in enumerate(((o0_ref, l0_ref), (o1_ref, l1_ref), (o2_ref, l2_ref))):
        dil = o_ref.shape[1]
        if dil == 1:
            outs.append(o_ref[0, 0])
            lses.append(l_ref[0, 0])
            continue
        n = o_ref.shape[2]
        chunks = o_ref.shape[3] // LANES
        for r in range(dil):
            l_scr[grp - 1, pl.ds(r, n, stride=dil), :] = l_ref[0, r]
            for c in range(chunks):
                o_scr[grp - 1, c, pl.ds(r, n, stride=dil), :] = o_ref[0, r, :, c * LANES:(c + 1) * LANES]
        outs.append(jnp.concatenate([o_scr[grp - 1, c] for c in range(chunks)], axis=1))
        lses.append(l_scr[grp - 1])
    l0, l1, l2 = lses
    m = jnp.maximum(jnp.maximum(l0, l1), l2)
    e0, e1, e2 = jnp.exp(l0 - m), jnp.exp(l1 - m), jnp.exp(l2 - m)
    inv = 1.0 / (e0 + e1 + e2)
    spread = ex_ref[...]
    yb = (_seg_sum(e0 * inv, spread) * outs[0] + _seg_sum(e1 * inv, spread) * outs[1]
          + _seg_sum(e2 * inv, spread) * outs[2])
    gates = gt_ref[...]
    merged = (gates[:, :D_MODEL] * _mm(ya_ref[...], wa_ref[...])
              + gates[:, D_MODEL:] * _mm(yb, wb_ref[...]))
    x1 = x_ref[...] + _mm(merged, wo_ref[...])
    x1_ref[...] = x1
    h = _rms(x1, gain_ref[...])
    h_ref[...] = h
    i1, i2, w1, w2 = _route(_mm_split(h, wr_ref[...]))

    @pl.when(pl.program_id(0) == 0)
    def _():
        count_ref[...] = jnp.zeros_like(count_ref)

    lane = lax.broadcasted_iota(jnp.int32, (x1.shape[0], LANES), 1)
    onehot = jnp.where((lane == i1) | (lane == i2), 1.0, 0.0)
    before = count_ref[0:1, :] + jnp.dot(tri_ref[...], onehot.astype(BF16), preferred_element_type=F32)
    rank1 = jnp.sum(jnp.where(lane == i1, before, 0.0), axis=-1, keepdims=True)
    rank2 = jnp.sum(jnp.where(lane == i2, before, 0.0), axis=-1, keepdims=True)
    last = x1.shape[0] - 1
    count_ref[...] = jnp.broadcast_to(before[last:, :] + onehot[last:, :], count_ref.shape)
    cols = (i1.astype(F32), i2.astype(F32), w1, w2, rank1, rank2)
    info = jnp.zeros(lane.shape, F32)
    for c, col in enumerate(cols):
        info = jnp.where(lane == c, col, info)
    route_ref[...] = info


def _merge(x, ya, outs, lses, gates, wa, wb, wo, gain, wr, spread, seq, *, tm):
    t = x.shape[0]
    tps = seq // tm
    tok = lambda n: pl.BlockSpec((tm, n), lambda i: (i, 0))
    full = lambda a: pl.BlockSpec(a.shape, lambda i: (0, 0))
    res = lambda a: pl.BlockSpec((1, a.shape[1], tm // a.shape[1], a.shape[3]),
                                 lambda i: (i // tps, 0, i % tps, 0))
    dilated = len(ATTN_GROUPS) - 1
    tri = jnp.asarray(np.tri(tm, k=-1), dtype=BF16)
    return pl.pallas_call(
        _merge_kernel,
        grid=(t // tm,),
        in_specs=[tok(D_MODEL), tok(RWKV_DIM)] + [res(a) for a in outs] + [res(a) for a in lses] + [
                  tok(2 * D_MODEL), full(wa), full(wb), full(wo), full(gain), full(wr), full(spread), full(tri)],
        out_specs=[tok(D_MODEL), tok(D_MODEL), tok(LANES), pl.BlockSpec((8, LANES), lambda i: (0, 0))],
        out_shape=[jax.ShapeDtypeStruct((t, D_MODEL), F32), jax.ShapeDtypeStruct((t, D_MODEL), F32),
                   jax.ShapeDtypeStruct((t, LANES), F32), jax.ShapeDtypeStruct((8, LANES), F32)],
        scratch_shapes=[pltpu.VMEM((dilated, ATTN_OUT_DIM // LANES, tm, LANES), F32),
                        pltpu.VMEM((dilated, tm, LANES), F32)],
        compiler_params=_params("arbitrary"),
        name="merge_route",
    )(x, ya, *outs, *lses, gates, wa, wb, wo, gain, wr, spread, tri)


ROUTE_E1, ROUTE_E2, ROUTE_W1, ROUTE_W2, ROUTE_R1, ROUTE_R2 = range(6)


def _moe_plan(counts, n_tiles):
    counts = counts[0, :N_EXPERTS].astype(jnp.int32)
    tile_end = jnp.cumsum((counts + MOE_TILE - 1) // MOE_TILE)
    n_valid = tile_end[-1]
    tile = jnp.minimum(jnp.arange(n_tiles, dtype=jnp.int32), n_valid - 1)
    tile_expert = jnp.sum(tile[:, None] >= tile_end[None, :], axis=1).astype(jnp.int32)
    return tile_expert, n_valid.reshape(1)


def _row_tiles(ref, rows):
    return jnp.concatenate([ref[pl.ds(c, rows, stride=ROW_TILE), :] for c in range(ROW_TILE)], axis=1)


def _store_row_tiles(ref, x):
    for c in range(ROW_TILE):
        ref[pl.ds(c, x.shape[0], stride=ROW_TILE), :] = x[:, c * LANES:(c + 1) * LANES]


def _row(ref, r):
    return ref.at[pl.ds(pl.multiple_of(r * ROW_TILE, ROW_TILE), ROW_TILE)]


def _dispatch_kernel(route_ref, count_ref, upper_ref, h_ref, xs_in_ref, xs_ref, pos_ref, rows_ref, pos_vmem,
                     pos_smem, sem, pos_sem):
    del xs_in_ref
    tm = h_ref.shape[0]
    tiles = jnp.floor((count_ref[...] + (MOE_TILE - 1)) * (1.0 / MOE_TILE))
    first_row = jnp.dot(tiles.astype(BF16), upper_ref[...], preferred_element_type=F32)[0:1, :] * MOE_TILE
    route = route_ref[...]
    lane = lax.broadcasted_iota(jnp.int32, route.shape, 1)
    col = lambda c: route[:, c:c + 1]
    place = lambda e, r: jnp.sum(jnp.where(lane == col(e).astype(jnp.int32), first_row, 0.0), axis=-1,
                                 keepdims=True) + col(r)
    both = jnp.where(lane == 0, place(ROUTE_E1, ROUTE_R1), jnp.where(lane == 1, place(ROUTE_E2, ROUTE_R2), 0.0))
    pos = both.T[0:ROW_TILE, :].astype(jnp.int32)
    pos_ref[0] = pos[0:2]
    pos_vmem[...] = pos
    to_smem = pltpu.make_async_copy(pos_vmem, pos_smem, pos_sem)
    to_smem.start()
    _store_row_tiles(rows_ref, h_ref[...])
    to_smem.wait()

    def copies(j):
        src = _row(rows_ref, j)
        return (pltpu.make_async_copy(src, _row(xs_ref, pos_smem[0, j]), sem.at[0]),
                pltpu.make_async_copy(src, _row(xs_ref, pos_smem[1, j]), sem.at[1]))

    def start(j, carry):
        for stream, cp in enumerate(copies(j)):
            cp.start(priority=stream)
        return carry

    lax.fori_loop(0, tm, start, 0, unroll=8)
    for stream in range(2):
        pltpu.make_async_copy(rows_ref, rows_ref, sem.at[stream]).wait()


def _dispatch(h, route, counts, blank, *, tm):
    t, d = h.shape
    upper = jnp.asarray(np.triu(np.ones((LANES, LANES)), k=1), dtype=BF16)
    tok = lambda n: pl.BlockSpec((tm, n), lambda i: (i, 0))
    full = lambda a: pl.BlockSpec(a.shape, lambda i: (0, 0))
    return pl.pallas_call(
        _dispatch_kernel,
        grid=(t // tm,),
        in_specs=[tok(LANES), full(counts), full(upper), tok(d), pl.BlockSpec(memory_space=pl.ANY)],
        out_specs=[pl.BlockSpec(memory_space=pl.ANY), pl.BlockSpec((1, 2, tm), lambda i: (i, 0, 0))],
        out_shape=[jax.ShapeDtypeStruct(blank.shape, F32),
                   jax.ShapeDtypeStruct((t // tm, 2, tm), jnp.int32)],
        scratch_shapes=[pltpu.VMEM((tm * ROW_TILE, LANES), F32), pltpu.VMEM((ROW_TILE, tm), jnp.int32),
                        pltpu.SMEM((ROW_TILE, tm), jnp.int32), pltpu.SemaphoreType.DMA((2,)),
                        pltpu.SemaphoreType.DMA],
        input_output_aliases={4: 0},
        compiler_params=_params("arbitrary"),
        name="moe_dispatch",
    )(route, counts, upper, h, blank)


def _expert_kernel(te_ref, nv_ref, x_ref, wg_ref, wu_ref, wd_ref, o_ref):
    live = pl.program_id(0) < nv_ref[0]

    @pl.when(live)
    def _():
        x = _row_tiles(x_ref, MOE_TILE)
        gate = _mm(x, wg_ref[0])
        up = _mm(x, wu_ref[0])
        _store_row_tiles(o_ref, _mm(gate * _sigmoid(gate) * up, wd_ref[0]))

    @pl.when(jnp.logical_not(live))
    def _():
        o_ref[...] = jnp.zeros_like(o_ref)


def _experts(xs, tile_expert, n_valid, wg, wu, wd):
    d = wg.shape[1]
    weight = lambda shape: pl.BlockSpec((1,) + shape, lambda i, te, nv: (te[i], 0, 0))
    blk = (MOE_TILE * ROW_TILE, LANES)
    return pl.pallas_call(
        _expert_kernel,
        grid_spec=pltpu.PrefetchScalarGridSpec(
            num_scalar_prefetch=2,
            grid=(xs.shape[0] // blk[0],),
            in_specs=[pl.BlockSpec(blk, lambda i, te, nv: (jnp.minimum(i, nv[0] - 1), 0)),
                      weight((d, D_FF_EXPERT)), weight((d, D_FF_EXPERT)), weight((D_FF_EXPERT, d))],
            out_specs=pl.BlockSpec(blk, lambda i, te, nv: (i, 0))),
        out_shape=jax.ShapeDtypeStruct(xs.shape, F32),
        compiler_params=_params("arbitrary"),
        name="moe_experts",
    )(tile_expert, n_valid, xs, wg, wu, wd)


def _ple_kernel(pos_ref, x_ref, route_ref, p_ref, gp_ref, wg_ref, wp_ref, gf_ref, ys_ref, o_ref,
                y_ref, sem):
    tm = x_ref.shape[0]
    step, steps = pl.program_id(0), pl.num_programs(0)

    def gather(tile, slot, inline):
        base = tile * (2 * tm)

        def body(j, carry):
            for stream in range(2):
                pltpu.make_async_copy(_row(ys_ref, pos_ref[base + stream * tm + j]),
                                      _row(y_ref.at[slot, stream], j), sem.at[slot, stream]
                                      ).start(priority=stream)
            return carry

        if inline:
            for j in range(tm):
                body(j, 0)
        else:
            lax.fori_loop(0, tm, body, 0, unroll=8)

    def drain(slot):
        for stream in range(2):
            buf = y_ref.at[slot, stream]
            pltpu.make_async_copy(buf, buf, sem.at[slot, stream]).wait()

    slot = step % 2

    @pl.when(step == 0)
    def _():
        gather(0, 0, False)

    drain(slot)
    gather(jnp.minimum(step + 1, steps - 1), 1 - slot, True)
    route = route_ref[...]
    w1, w2 = route[:, ROUTE_W1:ROUTE_W1 + 1], route[:, ROUTE_W2:ROUTE_W2 + 1]
    x = x_ref[...] + (w1 * _row_tiles(y_ref.at[slot, 0], tm) + w2 * _row_tiles(y_ref.at[slot, 1], tm))
    gate = _sigmoid(_mm(_rms(x, gp_ref[...]), wg_ref[...]))
    x = x + gate * _mm(p_ref[...], wp_ref[...])
    o_ref[...] = _rms(x, gf_ref[...])

    @pl.when(step == steps - 1)
    def _():
        drain(1 - slot)


def _ple(x, route, pos, ys, p, gain_ple, w_gate, w_proj, gain_final, *, tm):
    t = x.shape[0]
    tok = lambda n: pl.BlockSpec((tm, n), lambda i, ps: (i, 0))
    full = lambda a: pl.BlockSpec(a.shape, lambda i, ps: (0, 0))
    return pl.pallas_call(
        _ple_kernel,
        grid_spec=pltpu.PrefetchScalarGridSpec(
            num_scalar_prefetch=1,
            grid=(t // tm,),
            in_specs=[tok(D_MODEL), tok(LANES), tok(PLE_DIM), full(gain_ple), full(w_gate), full(w_proj),
                      full(gain_final), pl.BlockSpec(memory_space=pl.ANY)],
            out_specs=tok(D_MODEL),
            scratch_shapes=[pltpu.VMEM((2, 2, tm * ROW_TILE, LANES), F32), pltpu.SemaphoreType.DMA((2, 2))]),
        out_shape=jax.ShapeDtypeStruct((t, D_MODEL), F32),
        compiler_params=_params("arbitrary"),
        name="ple_final",
    )(pos, x, route, p, gain_ple, w_gate, w_proj, gain_final, ys)


def _pad_rows(w, start, total):
    return jnp.zeros((total, w.shape[1]), w.dtype).at[start:start + w.shape[0]].set(w)


def _layer(x, p, seq, norm_mix, w_in, rwkv_mu, rwkv_w0, rwkv_w2, rwkv_a0, rwkv_a2, rwkv_g2, rwkv_k_k,
           rwkv_k_a, rwkv_r_k, rwkv_ln_w, rwkv_ln_b, w_proj_a, w_proj_b, w_out, norm_moe, w_router_group,
           w_router_expert, w_exp_gate, w_exp_up, w_exp_down, norm_ple, w_ple_gate, w_ple_proj, norm_final):
    t = x.shape[0]
    bsz = t // seq
    row = lambda a: a.reshape(1, -1)
    w_in16 = w_in.astype(BF16)
    c0, c1 = RWKV_COLS, RWKV_COLS + ATTN_COLS
    w_qkv = w_in16[:, c0:c1].reshape(D_MODEL, 3, len(ATTN_GROUPS), ATTN_OUT_DIM)
    w_qkv = jnp.swapaxes(w_qkv, 1, 2).reshape(D_MODEL, len(ATTN_GROUPS), 3 * ATTN_OUT_DIM)
    x3 = x.reshape(bsz, seq, D_MODEL)
    qkv = [_qkv_proj(x3, row(norm_mix), w_qkv[:, grp], grp, tm=1024) for grp in range(len(ATTN_GROUPS))]
    gates = _norm_matmul(x, row(norm_mix), w_in16[:, c1:], BF16, tm=1024, tn=D_MODEL, sigmoid=True)

    seg = np.arange(RWKV_DIM) // HEAD_DIM
    seg_ones = jnp.asarray(seg[:, None] == seg[None, :], dtype=BF16)
    n_tiles = 2 * t // MOE_TILE + N_EXPERTS
    ya, blank = _rwkv_branch(
        x3, row(norm_mix), w_in16[:, :c0], row(rwkv_mu), row(rwkv_w0),
        _pad_rows(rwkv_w2, 0, LORA_COLS), row(rwkv_a0), _pad_rows(rwkv_a2, W_LORA, LORA_COLS),
        _pad_rows(rwkv_g2, W_LORA + A_LORA, LORA_COLS), row(rwkv_k_k), row(rwkv_k_a), row(rwkv_ln_w),
        row(rwkv_ln_b), row(rwkv_r_k), seg_ones, n_tiles * MOE_TILE * ROW_TILE, tb=256)
    ya = ya.reshape(t, RWKV_DIM)

    attn = [_attn_group(qkv[grp], grp) for grp in range(len(ATTN_GROUPS))]

    head_of_lane = np.arange(ATTN_OUT_DIM) // HEAD_DIM
    spread = jnp.asarray(np.arange(LANES)[:, None] == head_of_lane[None, :], dtype=BF16)
    w_route = jnp.concatenate(
        [jnp.moveaxis(w_router_expert, 0, 1).reshape(D_MODEL, N_EXPERTS), w_router_group,
         jnp.zeros((D_MODEL, LANES - N_EXPERTS - N_EXPERT_GROUPS), F32)], axis=1)
    x1, h_moe, route, counts = _merge(x, ya, [o for o, _ in attn], [l for _, l in attn], gates,
                             w_proj_a.astype(BF16), w_proj_b.astype(BF16), w_out.astype(BF16),
                             row(norm_moe), w_route, spread, seq, tm=512)
    tile_expert, n_valid = _moe_plan(counts, n_tiles)
    dispatch_tm = 512
    xs, pos = _dispatch(h_moe, route, counts, blank, tm=dispatch_tm)
    ys = _experts(xs, tile_expert, n_valid,
                  w_exp_gate.reshape(N_EXPERTS, D_MODEL, D_FF_EXPERT),
                  w_exp_up.reshape(N_EXPERTS, D_MODEL, D_FF_EXPERT),
                  w_exp_down.reshape(N_EXPERTS, D_FF_EXPERT, D_MODEL))
    return _ple(x1, route, pos.reshape(-1), ys, p, row(norm_ple), w_ple_gate.astype(BF16),
                w_ple_proj.astype(BF16), row(norm_final), tm=dispatch_tm)


def kernel(x, p, norm_mix, w_in, rwkv_mu, rwkv_w0, rwkv_w2, rwkv_a0, rwkv_a2, rwkv_g2, rwkv_k_k, rwkv_k_a, rwkv_r_k, rwkv_ln_w, rwkv_ln_b, w_proj_a, w_proj_b, w_out, norm_moe, w_router_group, w_router_expert, w_exp_gate, w_exp_up, w_exp_down, norm_ple, w_ple_gate, w_ple_proj, norm_final):
    bsz, seq, d = x.shape
    depth = w_in.shape[0]
    assert depth == 1, "the final norm is fused into the (single) layer"
    out = _layer(x.reshape(bsz * seq, d).astype(F32), p[0].reshape(bsz * seq, PLE_DIM), seq,
                 norm_mix[0], w_in[0], rwkv_mu[0], rwkv_w0[0], rwkv_w2[0], rwkv_a0[0], rwkv_a2[0],
                 rwkv_g2[0], rwkv_k_k[0], rwkv_k_a[0], rwkv_r_k[0], rwkv_ln_w[0], rwkv_ln_b[0],
                 w_proj_a[0], w_proj_b[0], w_out[0], norm_moe[0], w_router_group[0], w_router_expert[0],
                 w_exp_gate[0], w_exp_up[0], w_exp_down[0], norm_ple[0], w_ple_gate[0], w_ple_proj[0],
                 norm_final)
    return out.reshape(bsz, seq, d)
```

```python
import functools

import numpy as np
import jax
import jax.numpy as jnp
from jax import lax
from jax.experimental import pallas as pl
from jax.experimental.pallas import tpu as pltpu

F32 = jnp.float32
BF16 = jnp.bfloat16
HIGHEST = lax.Precision.HIGHEST

D_MODEL = 1024
PLE_DIM = 256
HEAD_DIM = 64
RWKV_HEADS = 8
RWKV_DIM = RWKV_HEADS * HEAD_DIM
W_LORA, A_LORA, G_LORA = 64, 64, 128
LORA_COLS = W_LORA + A_LORA + G_LORA
ATTN_GROUPS = ((128, 1), (512, 4), (2048, 16))
HEADS_PER_GROUP = 8
ATTN_HEADS = HEADS_PER_GROUP * len(ATTN_GROUPS)
ATTN_DIM = ATTN_HEADS * HEAD_DIM
ATTN_OUT_DIM = HEADS_PER_GROUP * HEAD_DIM
Q_BLOCK = 128
ATTN_TQ = 256
RWKV_COLS = 3 * RWKV_DIM + LORA_COLS
ATTN_COLS = 3 * ATTN_DIM
N_EXPERT_GROUPS = 4
EXPERTS_PER_GROUP = 8
N_EXPERTS = N_EXPERT_GROUPS * EXPERTS_PER_GROUP
D_FF_EXPERT = 256
NORM_EPS = 1e-6
RWKV_GN_EPS = 64e-5
LANES = 128
MASKED = -1e30
WKV_CHUNK = 64
VMEM_LIMIT = 48 * 1024 * 1024
MOE_TILE = 512
ROW_TILE = 8


def _mm(a, b):
    return jnp.dot(a.astype(BF16), b.astype(BF16), preferred_element_type=F32)


def _mm_nt(a, b):
    return lax.dot_general(a.astype(BF16), b.astype(BF16), (((1,), (1,)), ((), ())),
                           preferred_element_type=F32)


def _mm_split(a, b):
    a_hi, b_hi = a.astype(BF16), b.astype(BF16)
    a_lo = (a - a_hi.astype(F32)).astype(BF16)
    b_lo = (b - b_hi.astype(F32)).astype(BF16)
    dot = functools.partial(jnp.dot, preferred_element_type=F32)
    return dot(a_hi, b_hi) + (dot(a_hi, b_lo) + dot(a_lo, b_hi))


def _sigmoid(x):
    return 1.0 / (1.0 + jnp.exp(-x))


def _rms(x, gain):
    return x * lax.rsqrt(jnp.mean(x * x, axis=-1, keepdims=True) + NORM_EPS) * gain


def _params(*sem):
    return pltpu.CompilerParams(dimension_semantics=sem, vmem_limit_bytes=VMEM_LIMIT)


def _norm_matmul_kernel(x_ref, g_ref, w_ref, o_ref, h_ref, *, sigmoid):
    @pl.when(pl.program_id(1) == 0)
    def _():
        h_ref[...] = _rms(x_ref[...], g_ref[...]).astype(BF16)

    acc = jnp.dot(h_ref[...], w_ref[...], preferred_element_type=F32)
    if sigmoid:
        acc = _sigmoid(acc)
    o_ref[...] = acc.astype(o_ref.dtype)


def _norm_matmul(x, gain, w, out_dtype, *, tm, tn, sigmoid=False):
    t, d = x.shape
    n = w.shape[1]
    return pl.pallas_call(
        functools.partial(_norm_matmul_kernel, sigmoid=sigmoid),
        grid=(t // tm, n // tn),
        in_specs=[pl.BlockSpec((tm, d), lambda i, j: (i, 0)),
                  pl.BlockSpec((1, d), lambda i, j: (0, 0)),
                  pl.BlockSpec((d, tn), lambda i, j: (0, j))],
        out_specs=pl.BlockSpec((tm, tn), lambda i, j: (i, j)),
        out_shape=jax.ShapeDtypeStruct((t, n), out_dtype),
        scratch_shapes=[pltpu.VMEM((tm, d), BF16)],
        compiler_params=_params("parallel", "arbitrary"),
        name="norm_matmul",
    )(x, gain, w)


def _qkv_rows_kernel(x_ref, g_ref, w_ref, o_ref, stage_ref, *, dil):
    h = _rms(x_ref[0], g_ref[...])
    if dil == 1:
        o_ref[0, 0] = _mm(h, w_ref[...]).astype(BF16)
        return
    chunks = h.shape[1] // LANES
    for c in range(chunks):
        stage_ref[c] = h[:, c * LANES:(c + 1) * LANES]
    n = h.shape[0] // dil
    for r in range(dil):
        rows = jnp.concatenate([stage_ref[c, pl.ds(r, n, stride=dil), :] for c in range(chunks)], axis=1)
        o_ref[0, r] = _mm(rows, w_ref[...]).astype(BF16)


def _qkv_strided_kernel(x_hbm, g_ref, w_ref, o_ref, x_buf, sem):
    tm = x_buf.shape[1]
    n_res, n_blk = pl.num_programs(1), pl.num_programs(2)
    step = (pl.program_id(0) * n_res + pl.program_id(1)) * n_blk + pl.program_id(2)
    steps = pl.num_programs(0) * n_res * n_blk

    def fetch(s, slot):
        b, r, i = s // (n_res * n_blk), (s // n_blk) % n_res, s % n_blk
        return pltpu.make_async_copy(x_hbm.at[b, pl.ds(i * tm, tm), r], x_buf.at[slot], sem.at[slot])

    slot = step % 2

    @pl.when(step == 0)
    def _():
        fetch(0, 0).start()

    @pl.when(step + 1 < steps)
    def _():
        fetch(step + 1, 1 - slot).start()

    fetch(step, slot).wait()
    o_ref[0, 0] = _mm(_rms(x_buf[slot], g_ref[...]), w_ref[...]).astype(BF16)


def _qkv_proj(x, gain, w, group, *, tm):
    bsz, seq, d = x.shape
    dil = ATTN_GROUPS[group][1]
    comp = seq // dil
    tn = w.shape[1]
    out_shape = jax.ShapeDtypeStruct((bsz, dil, comp, tn), BF16)
    if dil % ROW_TILE:
        return pl.pallas_call(
            functools.partial(_qkv_rows_kernel, dil=dil),
            grid=(bsz, seq // tm),
            in_specs=[pl.BlockSpec((1, tm, d), lambda b, i: (b, i, 0)),
                      pl.BlockSpec((1, d), lambda b, i: (0, 0)),
                      pl.BlockSpec((d, tn), lambda b, i: (0, 0))],
            out_specs=pl.BlockSpec((1, dil, tm // dil, tn), lambda b, i: (b, 0, i, 0)),
            out_shape=out_shape,
            scratch_shapes=[pltpu.VMEM((d // LANES, tm, LANES), F32)],
            compiler_params=_params("parallel", "parallel"),
            name=f"qkv_proj_{group}",
        )(x, gain, w)
    tm = min(tm, comp)
    return pl.pallas_call(
        _qkv_strided_kernel,
        grid=(bsz, dil, comp // tm),
        in_specs=[pl.BlockSpec(memory_space=pl.ANY),
                  pl.BlockSpec((1, d), lambda b, r, i: (0, 0)),
                  pl.BlockSpec((d, tn), lambda b, r, i: (0, 0))],
        out_specs=pl.BlockSpec((1, 1, tm, tn), lambda b, r, i: (b, r, i, 0)),
        out_shape=out_shape,
        scratch_shapes=[pltpu.VMEM((2, tm, d), F32), pltpu.SemaphoreType.DMA((2,))],
        compiler_params=_params("arbitrary", "arbitrary", "arbitrary"),
        name=f"qkv_proj_{group}",
    )(x.reshape(bsz, comp, dil, d), gain, w)


def _seg_sum(x, seg_ones):
    hi = x.astype(BF16)
    lo = (x - hi.astype(F32)).astype(BF16)
    return (jnp.dot(hi, seg_ones, preferred_element_type=F32)
            + jnp.dot(lo, seg_ones, preferred_element_type=F32))


def _rwkv_inputs(z, last, mu, w0, w2, a0, a2, g2, k_k, k_a, seg):
    row = lax.broadcasted_iota(jnp.int32, z.shape, 0)
    prev = jnp.where(row == 0, last, pltpu.roll(z, 1, axis=0))
    zs = z + (prev - z) * mu
    d = RWKV_DIM
    r, k, v = zs[:, :d], zs[:, d:2 * d], zs[:, 2 * d:3 * d]
    lora = zs[:, 3 * d:]
    lane = lax.broadcasted_iota(jnp.int32, lora.shape, 1)
    lora = jnp.where(lane < W_LORA, jnp.tanh(lora),
                     jnp.where(lane < W_LORA + A_LORA, lora, _sigmoid(lora)))
    w_lin = w0 + _mm_split(lora, w2)
    a_lin = a0 + _mm_split(lora, a2)
    g = _mm_split(lora, g2)
    u = -w_lin
    softplus = jnp.maximum(u, 0.0) + jnp.log(1.0 + jnp.exp(-jnp.abs(u)))
    lw = -jnp.exp(-softplus - 0.5)
    a = _sigmoid(a_lin)
    kk = k * k_k
    kk = kk * lax.rsqrt(_seg_sum(kk * kk, seg) + 1e-12)
    k = k * (1.0 + (a - 1.0) * k_a)
    return r, lw, k, v, kk, kk * a, g


def _bmm(a, b):
    return lax.dot_general(a.astype(BF16), b.astype(BF16), (((2,), (1,)), ((0,), (0,))),
                           preferred_element_type=F32)


def _bmm_nt(a, b):
    return lax.dot_general(a.astype(BF16), b.astype(BF16), (((2,), (2,)), ((0,), (0,))),
                           preferred_element_type=F32)


def _bmm_tn(a, b):
    return lax.dot_general(a.astype(BF16), b.astype(BF16), (((1,), (1,)), ((0,), (0,))),
                           preferred_element_type=F32)


def _rwkv_kernel(x_ref, gain_ref, wz_ref, mu_ref, w0_ref, w2_ref, a0_ref, a2_ref, g2_ref, kk_ref, ka_ref,
                 lnw_ref, lnb_ref, rk_ref, seg_ref, o_ref, blank_ref, s_ref, zlast_ref, *, chunk):
    @pl.when(pl.program_id(1) == 0)
    def _():
        s_ref[...] = jnp.zeros_like(s_ref)
        zlast_ref[...] = jnp.zeros_like(zlast_ref)

    tb = x_ref.shape[1]
    d = RWKV_DIM
    z = jnp.dot(_rms(x_ref[0], gain_ref[...]).astype(BF16), wz_ref[...], preferred_element_type=F32)
    last = zlast_ref[7:8, :]
    zlast_ref[...] = z[tb - 8:, :]
    r, lw, k, v, kk, b, g = _rwkv_inputs(z, last, mu_ref[...], w0_ref[...], w2_ref[...], a0_ref[...],
                                         a2_ref[...], g2_ref[...], kk_ref[...], ka_ref[...], seg_ref[...])
    nc, nh = tb // chunk, d // HEAD_DIM
    ii = lax.broadcasted_iota(jnp.int32, (chunk, chunk), 0)
    jj = lax.broadcasted_iota(jnp.int32, (chunk, chunk), 1)
    incl, strict = ii >= jj, ii > jj
    tri = incl.astype(F32)
    eye = (ii == jj).astype(F32)
    rows = lambda c: slice(c * chunk, (c + 1) * chunk)
    lanes = lambda h: slice(h * HEAD_DIM, (h + 1) * HEAD_DIM)

    def split(x):
        return jnp.stack([x[rows(c), lanes(h)] for c in range(nc) for h in range(nh)])

    cums = [jnp.dot(tri, lw[rows(c)], precision=HIGHEST, preferred_element_type=F32) for c in range(nc)]
    cum = jnp.concatenate(cums, axis=0)
    total = jnp.concatenate([jnp.broadcast_to(cs[chunk - 1:], (chunk, d)) for cs in cums], axis=0)
    g_inv = jnp.exp(-cum)
    g_tail = jnp.exp(total - cum)
    kq, rq = split(kk * jnp.exp(cum - lw)), split(r * jnp.exp(cum))
    bi, ki = split(b * g_inv), split(k * g_inv)
    bh, kh = split(b * g_tail), split(k * g_tail)
    vs = split(v)
    decay = jnp.stack([jnp.exp(cs[chunk - 1:, lanes(h)]) for cs in cums for h in range(nh)])

    qr = jnp.concatenate([kq, rq], axis=1)
    ab, ak = _bmm_nt(qr, bi), _bmm_nt(qr, ki)
    a_b, a_rb = jnp.where(strict, ab[:, :chunk], 0.0), jnp.where(incl, ab[:, chunk:], 0.0)
    a_k, a_rk = jnp.where(strict, ak[:, :chunk], 0.0), jnp.where(incl, ak[:, chunk:], 0.0)
    n = nc * nh
    is_b = lax.broadcasted_iota(jnp.int32, (chunk, 2 * chunk), 1) < chunk
    bt = jnp.concatenate([-a_b, jnp.broadcast_to(eye, (n, chunk, chunk))], axis=2)
    for _ in range(int(np.log2(chunk))):
        prod = _bmm(bt[:, :, :chunk], bt)
        bt = jnp.where(is_b, prod, bt + prod)
    inv = bt[:, :, chunk:]
    wu = _bmm(inv, jnp.concatenate([kq, -_bmm(a_k, vs)], axis=2))
    aw = _bmm(a_rb, wu)
    qh = rq - aw[:, :, :HEAD_DIM]
    y0 = aw[:, :, HEAD_DIM:] + _bmm(a_rk, vs)
    mu = _bmm_tn(wu, bh)
    mp = mu[:, :HEAD_DIM]
    s_add = mu[:, HEAD_DIM:] + _bmm_tn(vs, kh)

    state = s_ref[...]
    for c in range(nc):
        sl = slice(c * nh, (c + 1) * nh)
        y = _bmm_nt(qh[sl], state) + y0[sl]
        for h in range(nh):
            o_ref[0, rows(c), lanes(h)] = y[h]
        state = state * decay[sl] - _bmm(state, mp[sl]) + s_add[sl]
    s_ref[...] = state

    seg = seg_ref[...]
    y = o_ref[0]
    yc = y - _seg_sum(y, seg) * (1.0 / HEAD_DIM)
    var = _seg_sum(yc * yc, seg) * (1.0 / HEAD_DIM)
    y = yc * lax.rsqrt(var + RWKV_GN_EPS) * lnw_ref[...] + lnb_ref[...]
    y = y + _seg_sum(r * k * rk_ref[...], seg) * v
    o_ref[0] = y * g
    blank_ref[...] = jnp.zeros_like(blank_ref)


def _rwkv_branch(x, gain, wz, mu, w0, w2p, a0, a2p, g2p, k_k, k_a, ln_w, ln_b, r_k, seg_ones, blank_rows,
                 *, tb):
    bsz, seq, dm = x.shape
    d = RWKV_DIM
    steps = seq // tb
    full = lambda a: pl.BlockSpec(a.shape, lambda bi, t: (0,) * a.ndim)
    params = (gain, wz, mu, w0, w2p, a0, a2p, g2p, k_k, k_a, ln_w, ln_b, r_k, seg_ones)
    blank_blk = blank_rows // (bsz * steps)
    return pl.pallas_call(
        functools.partial(_rwkv_kernel, chunk=WKV_CHUNK),
        grid=(bsz, steps),
        in_specs=[pl.BlockSpec((1, tb, dm), lambda bi, t: (bi, t, 0))] + [full(a) for a in params],
        out_specs=[pl.BlockSpec((1, tb, d), lambda bi, t: (bi, t, 0)),
                   pl.BlockSpec((blank_blk, LANES), lambda bi, t: (bi * steps + t, 0))],
        out_shape=[jax.ShapeDtypeStruct((bsz, seq, d), F32),
                   jax.ShapeDtypeStruct((blank_rows, LANES), F32)],
        scratch_shapes=[pltpu.VMEM((d // HEAD_DIM, HEAD_DIM, HEAD_DIM), F32),
                        pltpu.VMEM((8, RWKV_COLS), F32)],
        compiler_params=_params("parallel", "arbitrary"),
        name="rwkv_branch",
    )(x, *params)


def _attn_kernel(q_ref, kp_ref, kc_ref, vp_ref, vc_ref, bias_ref, o_ref, l_ref):
    first = pl.program_id(2) == 0
    q = q_ref[0, 0] * (HEAD_DIM ** -0.5)
    kc, vc = kc_ref[0, 0], vc_ref[0, 0]
    subs = q.shape[0] // Q_BLOCK
    rows = lambda j: slice(j * Q_BLOCK, (j + 1) * Q_BLOCK)
    lanes = lambda h: slice(h * HEAD_DIM, (h + 1) * HEAD_DIM)
    k_prev = [kp_ref[0, 0]] + [kc[rows(j)] for j in range(subs - 1)]
    v_prev = [vp_ref[0, 0]] + [vc[rows(j)] for j in range(subs - 1)]
    work = [(j, h) for j in range(subs) for h in range(HEADS_PER_GROUP)]
    scores = []
    for j, h in work:
        bias = bias_ref[h]
        bias_prev = bias[:, :Q_BLOCK]
        if j == 0:
            bias_prev = jnp.where(first, MASKED, bias_prev)
        qh = q[rows(j), lanes(h)]
        scores.append((_mm_nt(qh, k_prev[j][:, lanes(h)]) + bias_prev,
                       _mm_nt(qh, kc[rows(j), lanes(h)]) + bias[:, Q_BLOCK:]))
    tops = [jnp.max(jnp.maximum(sp, sc), axis=-1, keepdims=True) for sp, sc in scores]
    probs = [(jnp.exp(sp - m), jnp.exp(sc - m)) for (sp, sc), m in zip(scores, tops)]
    dens = [jnp.sum(pp + pc, axis=-1, keepdims=True) for pp, pc in probs]
    accs = [_mm(pp, v_prev[j][:, lanes(h)]) + _mm(pc, vc[rows(j), lanes(h)])
            for (pp, pc), (j, h) in zip(probs, work)]
    lane = lax.broadcasted_iota(jnp.int32, (Q_BLOCK, LANES), 1)
    for j in range(subs):
        lse = jnp.zeros((Q_BLOCK, LANES), F32)
        for h in range(HEADS_PER_GROUP):
            i = j * HEADS_PER_GROUP + h
            o_ref[0, 0, rows(j), lanes(h)] = accs[i] * (1.0 / dens[i])
            lse = jnp.where(lane == h, tops[i] + jnp.log(dens[i]), lse)
        l_ref[0, 0, rows(j), :] = lse


def _attn_bias(group):
    window, dilation = ATTN_GROUPS[group]
    n_back = window // dilation
    heads = np.arange(group * HEADS_PER_GROUP, (group + 1) * HEADS_PER_GROUP)
    slopes = (2.0 ** (-8.0 * (heads + 1) / ATTN_HEADS)).astype(np.float32)
    delta = (np.arange(Q_BLOCK)[:, None] + Q_BLOCK) - np.arange(2 * Q_BLOCK)[None, :]
    valid = (delta >= 0) & (delta <= n_back)
    alibi = -slopes[:, None, None] * (dilation * delta).astype(np.float32)[None]
    return jnp.asarray(np.where(valid[None], alibi, np.float32(MASKED)).astype(np.float32))


def _attn_group(qkv, group):
    bsz, dil, comp, _ = qkv.shape
    assert comp % ATTN_TQ == 0
    sub = ATTN_TQ // Q_BLOCK
    col = lambda which: (lambda bi, r, n: (bi, r, n, which))
    col_prev = lambda which: (lambda bi, r, n: (bi, r, jnp.maximum(n * sub - 1, 0), which))
    cur, prev = (1, 1, ATTN_TQ, ATTN_OUT_DIM), (1, 1, Q_BLOCK, ATTN_OUT_DIM)
    bias = _attn_bias(group)
    return pl.pallas_call(
        _attn_kernel,
        grid=(bsz, dil, comp // ATTN_TQ),
        in_specs=[pl.BlockSpec(cur, col(0)),
                  pl.BlockSpec(prev, col_prev(1)), pl.BlockSpec(cur, col(1)),
                  pl.BlockSpec(prev, col_prev(2)), pl.BlockSpec(cur, col(2)),
                  pl.BlockSpec(bias.shape, lambda bi, r, n: (0, 0, 0))],
        out_specs=[pl.BlockSpec(cur, col(0)),
                   pl.BlockSpec((1, 1, ATTN_TQ, LANES), col(0))],
        out_shape=[jax.ShapeDtypeStruct((bsz, dil, comp, ATTN_OUT_DIM), F32),
                   jax.ShapeDtypeStruct((bsz, dil, comp, LANES), F32)],
        compiler_params=_params("parallel", "parallel", "parallel"),
        name=f"dilated_attn_{group}",
    )(qkv, qkv, qkv, qkv, qkv, bias)


def _route(logits):
    lane = lax.broadcasted_iota(jnp.int32, logits.shape, 1)
    neg = -jnp.inf
    first_max = lambda vals, mx: jnp.min(jnp.where(vals == mx, lane, LANES), axis=-1, keepdims=True)
    is_grp = (lane >= N_EXPERTS) & (lane < N_EXPERTS + N_EXPERT_GROUPS)
    gl = jnp.where(is_grp, logits, neg)
    gmax = jnp.max(gl, axis=-1, keepdims=True)
    grp_w = 1.0 / jnp.sum(jnp.exp(gl - gmax), axis=-1, keepdims=True)
    grp = first_max(gl, gmax) - N_EXPERTS
    in_grp = (lane >= grp * EXPERTS_PER_GROUP) & (lane < (grp + 1) * EXPERTS_PER_GROUP)
    el = jnp.where(in_grp, logits, neg)
    v1 = jnp.max(el, axis=-1, keepdims=True)
    i1 = first_max(el, v1)
    el2 = jnp.where(lane == i1, neg, el)
    v2 = jnp.max(el2, axis=-1, keepdims=True)
    i2 = first_max(el2, v2)
    e2 = jnp.exp(v2 - v1)
    w1 = grp_w / (1.0 + e2)
    return i1, i2, w1, w1 * e2


def _merge_kernel(x_ref, ya_ref, o0_ref, o1_ref, o2_ref, l0_ref, l1_ref, l2_ref, gt_ref,
                  wa_ref, wb_ref, wo_ref, gain_ref, wr_ref, ex_ref, tri_ref, x1_ref, h_ref, route_ref,
                  count_ref, o_scr, l_scr):
    outs, lses = [], []
    for grp, (o_ref, l_ref) in enumerate(((o0_ref, l0_ref), (o1_ref, l1_ref), (o2_ref, l2_ref))):
        dil = o_ref.shape[1]
        if dil == 1:
            outs.append(o_ref[0, 0])
            lses.append(l_ref[0, 0])
            continue
        n = o_ref.shape[2]
        chunks = o_ref.shape[3] // LANES
        for r in range(dil):
            l_scr[grp - 1, pl.ds(r, n, stride=dil), :] = l_ref[0, r]
            for c in range(chunks):
                o_scr[grp - 1, c, pl.ds(r, n, stride=dil), :] = o_ref[0, r, :, c * LANES:(c + 1) * LANES]
        outs.append(jnp.concatenate([o_scr[grp - 1, c] for c in range(chunks)], axis=1))
        lses.append(l_scr[grp - 1])
    l0, l1, l2 = lses
    m = jnp.maximum(jnp.maximum(l0, l1), l2)
    e0, e1, e2 = jnp.exp(l0 - m), jnp.exp(l1 - m), jnp.exp(l2 - m)
    inv = 1.0 / (e0 + e1 + e2)
    spread = ex_ref[...]
    yb = (_seg_sum(e0 * inv, spread) * outs[0] + _seg_sum(e1 * inv, spread) * outs[1]
          + _seg_sum(e2 * inv, spread) * outs[2])
    gates = gt_ref[...]
    merged = (gates[:, :D_MODEL] * _mm(ya_ref[...], wa_ref[...])
              + gates[:, D_MODEL:] * _mm(yb, wb_ref[...]))
    x1 = x_ref[...] + _mm(merged, wo_ref[...])
    x1_ref[...] = x1
    h = _rms(x1, gain_ref[...])
    h_ref[...] = h
    i1, i2, w1, w2 = _route(_mm_split(h, wr_ref[...]))

    @pl.when(pl.program_id(0) == 0)
    def _():
        count_ref[...] = jnp.zeros_like(count_ref)

    lane = lax.broadcasted_iota(jnp.int32, (x1.shape[0], LANES), 1)
    onehot = jnp.where((lane == i1) | (lane == i2), 1.0, 0.0)
    before = count_ref[0:1, :] + jnp.dot(tri_ref[...], onehot.astype(BF16), preferred_element_type=F32)
    rank1 = jnp.sum(jnp.where(lane == i1, before, 0.0), axis=-1, keepdims=True)
    rank2 = jnp.sum(jnp.where(lane == i2, before, 0.0), axis=-1, keepdims=True)
    last = x1.shape[0] - 1
    count_ref[...] = jnp.broadcast_to(before[last:, :] + onehot[last:, :], count_ref.shape)
    cols = (i1.astype(F32), i2.astype(F32), w1, w2, rank1, rank2)
    info = jnp.zeros(lane.shape, F32)
    for c, col in enumerate(cols):
        info = jnp.where(lane == c, col, info)
    route_ref[...] = info


def _merge(x, ya, outs, lses, gates, wa, wb, wo, gain, wr, spread, seq, *, tm):
    t = x.shape[0]
    tps = seq // tm
    tok = lambda n: pl.BlockSpec((tm, n), lambda i: (i, 0))
    full = lambda a: pl.BlockSpec(a.shape, lambda i: (0, 0))
    res = lambda a: pl.BlockSpec((1, a.shape[1], tm // a.shape[1], a.shape[3]),
                                 lambda i: (i // tps, 0, i % tps, 0))
    dilated = len(ATTN_GROUPS) - 1
    tri = jnp.asarray(np.tri(tm, k=-1), dtype=BF16)
    return pl.pallas_call(
        _merge_kernel,
        grid=(t // tm,),
        in_specs=[tok(D_MODEL), tok(RWKV_DIM)] + [res(a) for a in outs] + [res(a) for a in lses] + [
                  tok(2 * D_MODEL), full(wa), full(wb), full(wo), full(gain), full(wr), full(spread), full(tri)],
        out_specs=[tok(D_MODEL), tok(D_MODEL), tok(LANES), pl.BlockSpec((8, LANES), lambda i: (0, 0))],
        out_shape=[jax.ShapeDtypeStruct((t, D_MODEL), F32), jax.ShapeDtypeStruct((t, D_MODEL), F32),
                   jax.ShapeDtypeStruct((t, LANES), F32), jax.ShapeDtypeStruct((8, LANES), F32)],
        scratch_shapes=[pltpu.VMEM((dilated, ATTN_OUT_DIM // LANES, tm, LANES), F32),
                        pltpu.VMEM((dilated, tm, LANES), F32)],
        compiler_params=_params("arbitrary"),
        name="merge_route",
    )(x, ya, *outs, *lses, gates, wa, wb, wo, gain, wr, spread, tri)


ROUTE_E1, ROUTE_E2, ROUTE_W1, ROUTE_W2, ROUTE_R1, ROUTE_R2 = range(6)


def _moe_plan(counts, n_tiles):
    counts = counts[0, :N_EXPERTS].astype(jnp.int32)
    tile_end = jnp.cumsum((counts + MOE_TILE - 1) // MOE_TILE)
    n_valid = tile_end[-1]
    tile = jnp.minimum(jnp.arange(n_tiles, dtype=jnp.int32), n_valid - 1)
    tile_expert = jnp.sum(tile[:, None] >= tile_end[None, :], axis=1).astype(jnp.int32)
    return tile_expert, n_valid.reshape(1)


def _row_tiles(ref, rows):
    return jnp.concatenate([ref[pl.ds(c, rows, stride=ROW_TILE), :] for c in range(ROW_TILE)], axis=1)


def _store_row_tiles(ref, x):
    for c in range(ROW_TILE):
        ref[pl.ds(c, x.shape[0], stride=ROW_TILE), :] = x[:, c * LANES:(c + 1) * LANES]


def _row(ref, r):
    return ref.at[pl.ds(pl.multiple_of(r * ROW_TILE, ROW_TILE), ROW_TILE)]


def _dispatch_kernel(route_ref, count_ref, upper_ref, h_ref, xs_in_ref, xs_ref, pos_ref, rows_ref, pos_vmem,
                     pos_smem, sem, pos_sem):
    del xs_in_ref
    tm = h_ref.shape[0]
    tiles = jnp.floor((count_ref[...] + (MOE_TILE - 1)) * (1.0 / MOE_TILE))
    first_row = jnp.dot(tiles.astype(BF16), upper_ref[...], preferred_element_type=F32)[0:1, :] * MOE_TILE
    route = route_ref[...]
    lane = lax.broadcasted_iota(jnp.int32, route.shape, 1)
    col = lambda c: route[:, c:c + 1]
    place = lambda e, r: jnp.sum(jnp.where(lane == col(e).astype(jnp.int32), first_row, 0.0), axis=-1,
                                 keepdims=True) + col(r)
    both = jnp.where(lane == 0, place(ROUTE_E1, ROUTE_R1), jnp.where(lane == 1, place(ROUTE_E2, ROUTE_R2), 0.0))
    pos = both.T[0:ROW_TILE, :].astype(jnp.int32)
    pos_ref[0] = pos[0:2]
    pos_vmem[...] = pos
    to_smem = pltpu.make_async_copy(pos_vmem, pos_smem, pos_sem)
    to_smem.start()
    _store_row_tiles(rows_ref, h_ref[...])
    to_smem.wait()

    def copies(j):
        src = _row(rows_ref, j)
        return (pltpu.make_async_copy(src, _row(xs_ref, pos_smem[0, j]), sem.at[0]),
                pltpu.make_async_copy(src, _row(xs_ref, pos_smem[1, j]), sem.at[1]))

    def start(j, carry):
        for stream, cp in enumerate(copies(j)):
            cp.start(priority=stream)
        return carry

    lax.fori_loop(0, tm, start, 0, unroll=8)
    for stream in range(2):
        pltpu.make_async_copy(rows_ref, rows_ref, sem.at[stream]).wait()


def _dispatch(h, route, counts, blank, *, tm):
    t, d = h.shape
    upper = jnp.asarray(np.triu(np.ones((LANES, LANES)), k=1), dtype=BF16)
    tok = lambda n: pl.BlockSpec((tm, n), lambda i: (i, 0))
    full = lambda a: pl.BlockSpec(a.shape, lambda i: (0, 0))
    return pl.pallas_call(
        _dispatch_kernel,
        grid=(t // tm,),
        in_specs=[tok(LANES), full(counts), full(upper), tok(d), pl.BlockSpec(memory_space=pl.ANY)],
        out_specs=[pl.BlockSpec(memory_space=pl.ANY), pl.BlockSpec((1, 2, tm), lambda i: (i, 0, 0))],
        out_shape=[jax.ShapeDtypeStruct(blank.shape, F32),
                   jax.ShapeDtypeStruct((t // tm, 2, tm), jnp.int32)],
        scratch_shapes=[pltpu.VMEM((tm * ROW_TILE, LANES), F32), pltpu.VMEM((ROW_TILE, tm), jnp.int32),
                        pltpu.SMEM((ROW_TILE, tm), jnp.int32), pltpu.SemaphoreType.DMA((2,)),
                        pltpu.SemaphoreType.DMA],
        input_output_aliases={4: 0},
        compiler_params=_params("arbitrary"),
        name="moe_dispatch",
    )(route, counts, upper, h, blank)


def _expert_kernel(te_ref, nv_ref, x_ref, wg_ref, wu_ref, wd_ref, o_ref):
    live = pl.program_id(0) < nv_ref[0]

    @pl.when(live)
    def _():
        x = _row_tiles(x_ref, MOE_TILE)
        gate = _mm(x, wg_ref[0])
        up = _mm(x, wu_ref[0])
        _store_row_tiles(o_ref, _mm(gate * _sigmoid(gate) * up, wd_ref[0]))

    @pl.when(jnp.logical_not(live))
    def _():
        o_ref[...] = jnp.zeros_like(o_ref)


def _experts(xs, tile_expert, n_valid, wg, wu, wd):
    d = wg.shape[1]
    weight = lambda shape: pl.BlockSpec((1,) + shape, lambda i, te, nv: (te[i], 0, 0))
    blk = (MOE_TILE * ROW_TILE, LANES)
    return pl.pallas_call(
        _expert_kernel,
        grid_spec=pltpu.PrefetchScalarGridSpec(
            num_scalar_prefetch=2,
            grid=(xs.shape[0] // blk[0],),
            in_specs=[pl.BlockSpec(blk, lambda i, te, nv: (jnp.minimum(i, nv[0] - 1), 0)),
                      weight((d, D_FF_EXPERT)), weight((d, D_FF_EXPERT)), weight((D_FF_EXPERT, d))],
            out_specs=pl.BlockSpec(blk, lambda i, te, nv: (i, 0))),
        out_shape=jax.ShapeDtypeStruct(xs.shape, F32),
        compiler_params=_params("arbitrary"),
        name="moe_experts",
    )(tile_expert, n_valid, xs, wg, wu, wd)


def _ple_kernel(pos_ref, x_ref, route_ref, p_ref, gp_ref, wg_ref, wp_ref, gf_ref, ys_ref, o_ref,
                y_ref, sem):
    tm = x_ref.shape[0]
    step, steps = pl.program_id(0), pl.num_programs(0)

    def gather(tile, slot, inline):
        base = tile * (2 * tm)

        def body(j, carry):
            for stream in range(2):
                pltpu.make_async_copy(_row(ys_ref, pos_ref[base + stream * tm + j]),
                                      _row(y_ref.at[slot, stream], j), sem.at[slot, stream]
                                      ).start(priority=stream)
            return carry

        if inline:
            for j in range(tm):
                body(j, 0)
        else:
            lax.fori_loop(0, tm, body, 0, unroll=8)

    def drain(slot):
        for stream in range(2):
            buf = y_ref.at[slot, stream]
            pltpu.make_async_copy(buf, buf, sem.at[slot, stream]).wait()

    slot = step % 2

    @pl.when(step == 0)
    def _():
        gather(0, 0, False)

    drain(slot)
    gather(jnp.minimum(step + 1, steps - 1), 1 - slot, True)
    route = route_ref[...]
    w1, w2 = route[:, ROUTE_W1:ROUTE_W1 + 1], route[:, ROUTE_W2:ROUTE_W2 + 1]
    x = x_ref[...] + (w1 * _row_tiles(y_ref.at[slot, 0], tm) + w2 * _row_tiles(y_ref.at[slot, 1], tm))
    gate = _sigmoid(_mm(_rms(x, gp_ref[...]), wg_ref[...]))
    x = x + gate * _mm(p_ref[...], wp_ref[...])
    o_ref[...] = _rms(x, gf_ref[...])

    @pl.when(step == steps - 1)
    def _():
        drain(1 - slot)


def _ple(x, route, pos, ys, p, gain_ple, w_gate, w_proj, gain_final, *, tm):
    t = x.shape[0]
    tok = lambda n: pl.BlockSpec((tm, n), lambda i, ps: (i, 0))
    full = lambda a: pl.BlockSpec(a.shape, lambda i, ps: (0, 0))
    return pl.pallas_call(
        _ple_kernel,
        grid_spec=pltpu.PrefetchScalarGridSpec(
            num_scalar_prefetch=1,
            grid=(t // tm,),
            in_specs=[tok(D_MODEL), tok(LANES), tok(PLE_DIM), full(gain_ple), full(w_gate), full(w_proj),
                      full(gain_final), pl.BlockSpec(memory_space=pl.ANY)],
            out_specs=tok(D_MODEL),
            scratch_shapes=[pltpu.VMEM((2, 2, tm * ROW_TILE, LANES), F32), pltpu.SemaphoreType.DMA((2, 2))]),
        out_shape=jax.ShapeDtypeStruct((t, D_MODEL), F32),
        compiler_params=_params("arbitrary"),
        name="ple_final",
    )(pos, x, route, p, gain_ple, w_gate, w_proj, gain_final, ys)


def _pad_rows(w, start, total):
    return jnp.zeros((total, w.shape[1]), w.dtype).at[start:start + w.shape[0]].set(w)


def _layer(x, p, seq, norm_mix, w_in, rwkv_mu, rwkv_w0, rwkv_w2, rwkv_a0, rwkv_a2, rwkv_g2, rwkv_k_k,
           rwkv_k_a, rwkv_r_k, rwkv_ln_w, rwkv_ln_b, w_proj_a, w_proj_b, w_out, norm_moe, w_router_group,
           w_router_expert, w_exp_gate, w_exp_up, w_exp_down, norm_ple, w_ple_gate, w_ple_proj, norm_final):
    t = x.shape[0]
    bsz = t // seq
    row = lambda a: a.reshape(1, -1)
    c0, c1 = RWKV_COLS, RWKV_COLS + ATTN_COLS
    x3 = x.reshape(bsz, seq, D_MODEL)

    def w_group(grp):
        lo = lambda which: c0 + which * ATTN_DIM + grp * ATTN_OUT_DIM
        return jnp.concatenate([w_in[:, lo(which):lo(which) + ATTN_OUT_DIM] for which in range(3)],
                               axis=1).astype(BF16)

    qkv = [_qkv_proj(x3, row(norm_mix), w_group(grp), grp, tm=1024) for grp in range(len(ATTN_GROUPS))]
    gates = _norm_matmul(x, row(norm_mix), w_in[:, c1:].astype(BF16), BF16, tm=1024, tn=D_MODEL, sigmoid=True)

    seg = np.arange(RWKV_DIM) // HEAD_DIM
    seg_ones = jnp.asarray(seg[:, None] == seg[None, :], dtype=BF16)
    n_tiles = 2 * t // MOE_TILE + N_EXPERTS
    ya, blank = _rwkv_branch(
        x3, row(norm_mix), w_in[:, :c0].astype(BF16), row(rwkv_mu), row(rwkv_w0),
        _pad_rows(rwkv_w2, 0, LORA_COLS), row(rwkv_a0), _pad_rows(rwkv_a2, W_LORA, LORA_COLS),
        _pad_rows(rwkv_g2, W_LORA + A_LORA, LORA_COLS), row(rwkv_k_k), row(rwkv_k_a), row(rwkv_ln_w),
        row(rwkv_ln_b), row(rwkv_r_k), seg_ones, n_tiles * MOE_TILE * ROW_TILE, tb=256)
    ya = ya.reshape(t, RWKV_DIM)

    attn = [_attn_group(qkv[grp], grp) for grp in range(len(ATTN_GROUPS))]

    head_of_lane = np.arange(ATTN_OUT_DIM) // HEAD_DIM
    spread = jnp.asarray(np.arange(LANES)[:, None] == head_of_lane[None, :], dtype=BF16)
    w_route = jnp.concatenate(
        [jnp.moveaxis(w_router_expert, 0, 1).reshape(D_MODEL, N_EXPERTS), w_router_group,
         jnp.zeros((D_MODEL, LANES - N_EXPERTS - N_EXPERT_GROUPS), F32)], axis=1)
    x1, h_moe, route, counts = _merge(x, ya, [o for o, _ in attn], [l for _, l in attn], gates,
                             w_proj_a.astype(BF16), w_proj_b.astype(BF16), w_out.astype(BF16),
                             row(norm_moe), w_route, spread, seq, tm=512)
    tile_expert, n_valid = _moe_plan(counts, n_tiles)
    dispatch_tm = 512
    xs, pos = _dispatch(h_moe, route, counts, blank, tm=dispatch_tm)
    ys = _experts(xs, tile_expert, n_valid,
                  w_exp_gate.reshape(N_EXPERTS, D_MODEL, D_FF_EXPERT),
                  w_exp_up.reshape(N_EXPERTS, D_MODEL, D_FF_EXPERT),
                  w_exp_down.reshape(N_EXPERTS, D_FF_EXPERT, D_MODEL))
    return _ple(x1, route, pos.reshape(-1), ys, p, row(norm_ple), w_ple_gate.astype(BF16),
                w_ple_proj.astype(BF16), row(norm_final), tm=dispatch_tm)


def kernel(x, p, norm_mix, w_in, rwkv_mu, rwkv_w0, rwkv_w2, rwkv_a0, rwkv_a2, rwkv_g2, rwkv_k_k, rwkv_k_a, rwkv_r_k, rwkv_ln_w, rwkv_ln_b, w_proj_a, w_proj_b, w_out, norm_moe, w_router_group, w_router_expert, w_exp_gate, w_exp_up, w_exp_down, norm_ple, w_ple_gate, w_ple_proj, norm_final):
    bsz, seq, d = x.shape
    depth = w_in.shape[0]
    assert depth == 1, "the final norm is fused into the (single) layer"
    out = _layer(x.reshape(bsz * seq, d).astype(F32), p[0].reshape(bsz * seq, PLE_DIM), seq,
                 norm_mix[0], w_in[0], rwkv_mu[0], rwkv_w0[0], rwkv_w2[0], rwkv_a0[0], rwkv_a2[0],
                 rwkv_g2[0], rwkv_k_k[0], rwkv_k_a[0], rwkv_r_k[0], rwkv_ln_w[0], rwkv_ln_b[0],
                 w_proj_a[0], w_proj_b[0], w_out[0], norm_moe[0], w_router_group[0], w_router_expert[0],
                 w_exp_gate[0], w_exp_up[0], w_exp_down[0], norm_ple[0], w_ple_gate[0], w_ple_proj[0],
                 norm_final)
    return out.reshape(bsz, seq, d)
```

```python
import functools

import numpy as np
import jax
import jax.numpy as jnp
from jax import lax
from jax.experimental import pallas as pl
from jax.experimental.pallas import tpu as pltpu

F32 = jnp.float32
BF16 = jnp.bfloat16
HIGHEST = lax.Precision.HIGHEST

D_MODEL = 1024
PLE_DIM = 256
HEAD_DIM = 64
RWKV_HEADS = 8
RWKV_DIM = RWKV_HEADS * HEAD_DIM
W_LORA, A_LORA, G_LORA = 64, 64, 128
LORA_COLS = W_LORA + A_LORA + G_LORA
ATTN_GROUPS = ((128, 1), (512, 4), (2048, 16))
HEADS_PER_GROUP = 8
ATTN_HEADS = HEADS_PER_GROUP * len(ATTN_GROUPS)
ATTN_DIM = ATTN_HEADS * HEAD_DIM
ATTN_OUT_DIM = HEADS_PER_GROUP * HEAD_DIM
Q_BLOCK = 128
ATTN_TQ = 512
RWKV_COLS = 3 * RWKV_DIM + LORA_COLS
ATTN_COLS = 3 * ATTN_DIM
N_EXPERT_GROUPS = 4
EXPERTS_PER_GROUP = 8
N_EXPERTS = N_EXPERT_GROUPS * EXPERTS_PER_GROUP
D_FF_EXPERT = 256
NORM_EPS = 1e-6
RWKV_GN_EPS = 64e-5
LANES = 128
MASKED = -1e30
WKV_CHUNK = 64
VMEM_LIMIT = 48 * 1024 * 1024
MOE_TILE = 512
ROW_TILE = 8


def _mm(a, b):
    return jnp.dot(a.astype(BF16), b.astype(BF16), preferred_element_type=F32)


def _mm_nt(a, b):
    return lax.dot_general(a.astype(BF16), b.astype(BF16), (((1,), (1,)), ((), ())),
                           preferred_element_type=F32)


def _mm_split(a, b):
    a_hi, b_hi = a.astype(BF16), b.astype(BF16)
    a_lo = (a - a_hi.astype(F32)).astype(BF16)
    b_lo = (b - b_hi.astype(F32)).astype(BF16)
    dot = functools.partial(jnp.dot, preferred_element_type=F32)
    return dot(a_hi, b_hi) + (dot(a_hi, b_lo) + dot(a_lo, b_hi))


def _sigmoid(x):
    return 1.0 / (1.0 + jnp.exp(-x))


def _rms(x, gain):
    return x * lax.rsqrt(jnp.mean(x * x, axis=-1, keepdims=True) + NORM_EPS) * gain


def _params(*sem):
    return pltpu.CompilerParams(dimension_semantics=sem, vmem_limit_bytes=VMEM_LIMIT)


def _norm_matmul_kernel(x_ref, g_ref, w_ref, o_ref, h_ref, *, sigmoid):
    @pl.when(pl.program_id(1) == 0)
    def _():
        h_ref[...] = _rms(x_ref[...], g_ref[...]).astype(BF16)

    acc = jnp.dot(h_ref[...], w_ref[...], preferred_element_type=F32)
    if sigmoid:
        acc = _sigmoid(acc)
    o_ref[...] = acc.astype(o_ref.dtype)


def _norm_matmul(x, gain, w, out_dtype, *, tm, tn, sigmoid=False):
    t, d = x.shape
    n = w.shape[1]
    return pl.pallas_call(
        functools.partial(_norm_matmul_kernel, sigmoid=sigmoid),
        grid=(t // tm, n // tn),
        in_specs=[pl.BlockSpec((tm, d), lambda i, j: (i, 0)),
                  pl.BlockSpec((1, d), lambda i, j: (0, 0)),
                  pl.BlockSpec((d, tn), lambda i, j: (0, j))],
        out_specs=pl.BlockSpec((tm, tn), lambda i, j: (i, j)),
        out_shape=jax.ShapeDtypeStruct((t, n), out_dtype),
        scratch_shapes=[pltpu.VMEM((tm, d), BF16)],
        compiler_params=_params("parallel", "arbitrary"),
        name="norm_matmul",
    )(x, gain, w)


def _qkv_rows_kernel(x_ref, g_ref, w_ref, o_ref, stage_ref, *, dil):
    h = _rms(x_ref[0], g_ref[...])
    if dil == 1:
        o_ref[0, 0] = _mm(h, w_ref[...]).astype(BF16)
        return
    chunks = h.shape[1] // LANES
    for c in range(chunks):
        stage_ref[c] = h[:, c * LANES:(c + 1) * LANES]
    n = h.shape[0] // dil
    for r in range(dil):
        rows = jnp.concatenate([stage_ref[c, pl.ds(r, n, stride=dil), :] for c in range(chunks)], axis=1)
        o_ref[0, r] = _mm(rows, w_ref[...]).astype(BF16)


def _qkv_strided_kernel(x_hbm, g_ref, w_ref, o_ref, x_buf, sem):
    tm = x_buf.shape[1]
    n_res, n_blk = pl.num_programs(1), pl.num_programs(2)
    step = (pl.program_id(0) * n_res + pl.program_id(1)) * n_blk + pl.program_id(2)
    steps = pl.num_programs(0) * n_res * n_blk

    def fetch(s, slot):
        b, r, i = s // (n_res * n_blk), (s // n_blk) % n_res, s % n_blk
        return pltpu.make_async_copy(x_hbm.at[b, pl.ds(i * tm, tm), r], x_buf.at[slot], sem.at[slot])

    slot = step % 2

    @pl.when(step == 0)
    def _():
        fetch(0, 0).start()

    @pl.when(step + 1 < steps)
    def _():
        fetch(step + 1, 1 - slot).start()

    fetch(step, slot).wait()
    o_ref[0, 0] = _mm(_rms(x_buf[slot], g_ref[...]), w_ref[...]).astype(BF16)


def _qkv_proj(x, gain, w, group, *, tm):
    bsz, seq, d = x.shape
    dil = ATTN_GROUPS[group][1]
    comp = seq // dil
    tn = w.shape[1]
    out_shape = jax.ShapeDtypeStruct((bsz, dil, comp, tn), BF16)
    if dil % ROW_TILE:
        return pl.pallas_call(
            functools.partial(_qkv_rows_kernel, dil=dil),
            grid=(bsz, seq // tm),
            in_specs=[pl.BlockSpec((1, tm, d), lambda b, i: (b, i, 0)),
                      pl.BlockSpec((1, d), lambda b, i: (0, 0)),
                      pl.BlockSpec((d, tn), lambda b, i: (0, 0))],
            out_specs=pl.BlockSpec((1, dil, tm // dil, tn), lambda b, i: (b, 0, i, 0)),
            out_shape=out_shape,
            scratch_shapes=[pltpu.VMEM((d // LANES, tm, LANES), F32)],
            compiler_params=_params("parallel", "parallel"),
            name=f"qkv_proj_{group}",
        )(x, gain, w)
    tm = min(tm, comp)
    return pl.pallas_call(
        _qkv_strided_kernel,
        grid=(bsz, dil, comp // tm),
        in_specs=[pl.BlockSpec(memory_space=pl.ANY),
                  pl.BlockSpec((1, d), lambda b, r, i: (0, 0)),
                  pl.BlockSpec((d, tn), lambda b, r, i: (0, 0))],
        out_specs=pl.BlockSpec((1, 1, tm, tn), lambda b, r, i: (b, r, i, 0)),
        out_shape=out_shape,
        scratch_shapes=[pltpu.VMEM((2, tm, d), F32), pltpu.SemaphoreType.DMA((2,))],
        compiler_params=_params("arbitrary", "arbitrary", "arbitrary"),
        name=f"qkv_proj_{group}",
    )(x.reshape(bsz, comp, dil, d), gain, w)


def _seg_sum(x, seg_ones):
    hi = x.astype(BF16)
    lo = (x - hi.astype(F32)).astype(BF16)
    return (jnp.dot(hi, seg_ones, preferred_element_type=F32)
            + jnp.dot(lo, seg_ones, preferred_element_type=F32))


def _rwkv_inputs(z, last, mu, w0, w2, a0, a2, g2, k_k, k_a, seg):
    row = lax.broadcasted_iota(jnp.int32, z.shape, 0)
    prev = jnp.where(row == 0, last, pltpu.roll(z, 1, axis=0))
    zs = z + (prev - z) * mu
    d = RWKV_DIM
    r, k, v = zs[:, :d], zs[:, d:2 * d], zs[:, 2 * d:3 * d]
    lora = zs[:, 3 * d:]
    lane = lax.broadcasted_iota(jnp.int32, lora.shape, 1)
    lora = jnp.where(lane < W_LORA, jnp.tanh(lora),
                     jnp.where(lane < W_LORA + A_LORA, lora, _sigmoid(lora)))
    w_lin = w0 + _mm_split(lora, w2)
    a_lin = a0 + _mm_split(lora, a2)
    g = _mm_split(lora, g2)
    u = -w_lin
    softplus = jnp.maximum(u, 0.0) + jnp.log(1.0 + jnp.exp(-jnp.abs(u)))
    lw = -jnp.exp(-softplus - 0.5)
    a = _sigmoid(a_lin)
    kk = k * k_k
    kk = kk * lax.rsqrt(_seg_sum(kk * kk, seg) + 1e-12)
    k = k * (1.0 + (a - 1.0) * k_a)
    return r, lw, k, v, kk, kk * a, g


def _bmm(a, b):
    return lax.dot_general(a.astype(BF16), b.astype(BF16), (((2,), (1,)), ((0,), (0,))),
                           preferred_element_type=F32)


def _bmm_nt(a, b):
    return lax.dot_general(a.astype(BF16), b.astype(BF16), (((2,), (2,)), ((0,), (0,))),
                           preferred_element_type=F32)


def _bmm_tn(a, b):
    return lax.dot_general(a.astype(BF16), b.astype(BF16), (((1,), (1,)), ((0,), (0,))),
                           preferred_element_type=F32)


def _rwkv_kernel(x_ref, gain_ref, wz_ref, mu_ref, w0_ref, w2_ref, a0_ref, a2_ref, g2_ref, kk_ref, ka_ref,
                 lnw_ref, lnb_ref, rk_ref, seg_ref, o_ref, blank_ref, s_ref, zlast_ref, *, chunk):
    @pl.when(pl.program_id(1) == 0)
    def _():
        s_ref[...] = jnp.zeros_like(s_ref)
        zlast_ref[...] = jnp.zeros_like(zlast_ref)

    tb = x_ref.shape[1]
    d = RWKV_DIM
    z = jnp.dot(_rms(x_ref[0], gain_ref[...]).astype(BF16), wz_ref[...], preferred_element_type=F32)
    last = zlast_ref[7:8, :]
    zlast_ref[...] = z[tb - 8:, :]
    r, lw, k, v, kk, b, g = _rwkv_inputs(z, last, mu_ref[...], w0_ref[...], w2_ref[...], a0_ref[...],
                                         a2_ref[...], g2_ref[...], kk_ref[...], ka_ref[...], seg_ref[...])
    nc, nh = tb // chunk, d // HEAD_DIM
    ii = lax.broadcasted_iota(jnp.int32, (chunk, chunk), 0)
    jj = lax.broadcasted_iota(jnp.int32, (chunk, chunk), 1)
    incl, strict = ii >= jj, ii > jj
    tri = incl.astype(F32)
    eye = (ii == jj).astype(F32)
    rows = lambda c: slice(c * chunk, (c + 1) * chunk)
    lanes = lambda h: slice(h * HEAD_DIM, (h + 1) * HEAD_DIM)

    def split(x):
        return jnp.stack([x[rows(c), lanes(h)] for c in range(nc) for h in range(nh)])

    cums = [jnp.dot(tri, lw[rows(c)], precision=HIGHEST, preferred_element_type=F32) for c in range(nc)]
    cum = jnp.concatenate(cums, axis=0)
    total = jnp.concatenate([jnp.broadcast_to(cs[chunk - 1:], (chunk, d)) for cs in cums], axis=0)
    g_inv = jnp.exp(-cum)
    g_tail = jnp.exp(total - cum)
    kq, rq = split(kk * jnp.exp(cum - lw)), split(r * jnp.exp(cum))
    bi, ki = split(b * g_inv), split(k * g_inv)
    bh, kh = split(b * g_tail), split(k * g_tail)
    vs = split(v)
    decay = jnp.stack([jnp.exp(cs[chunk - 1:, lanes(h)]) for cs in cums for h in range(nh)])

    qr = jnp.concatenate([kq, rq], axis=1)
    ab, ak = _bmm_nt(qr, bi), _bmm_nt(qr, ki)
    a_b, a_rb = jnp.where(strict, ab[:, :chunk], 0.0), jnp.where(incl, ab[:, chunk:], 0.0)
    a_k, a_rk = jnp.where(strict, ak[:, :chunk], 0.0), jnp.where(incl, ak[:, chunk:], 0.0)
    n = nc * nh
    is_b = lax.broadcasted_iota(jnp.int32, (chunk, 2 * chunk), 1) < chunk
    bt = jnp.concatenate([-a_b, jnp.broadcast_to(eye, (n, chunk, chunk))], axis=2)
    for _ in range(int(np.log2(chunk))):
        prod = _bmm(bt[:, :, :chunk], bt)
        bt = jnp.where(is_b, prod, bt + prod)
    inv = bt[:, :, chunk:]
    wu = _bmm(inv, jnp.concatenate([kq, -_bmm(a_k, vs)], axis=2))
    aw = _bmm(a_rb, wu)
    qh = rq - aw[:, :, :HEAD_DIM]
    y0 = aw[:, :, HEAD_DIM:] + _bmm(a_rk, vs)
    mu = _bmm_tn(wu, bh)
    mp = mu[:, :HEAD_DIM]
    s_add = mu[:, HEAD_DIM:] + _bmm_tn(vs, kh)

    state = s_ref[...]
    for c in range(nc):
        sl = slice(c * nh, (c + 1) * nh)
        y = _bmm_nt(qh[sl], state) + y0[sl]
        for h in range(nh):
            o_ref[0, rows(c), lanes(h)] = y[h]
        state = state * decay[sl] - _bmm(state, mp[sl]) + s_add[sl]
    s_ref[...] = state

    seg = seg_ref[...]
    y = o_ref[0]
    yc = y - _seg_sum(y, seg) * (1.0 / HEAD_DIM)
    var = _seg_sum(yc * yc, seg) * (1.0 / HEAD_DIM)
    y = yc * lax.rsqrt(var + RWKV_GN_EPS) * lnw_ref[...] + lnb_ref[...]
    y = y + _seg_sum(r * k * rk_ref[...], seg) * v
    o_ref[0] = y * g
    blank_ref[...] = jnp.zeros_like(blank_ref)


def _rwkv_branch(x, gain, wz, mu, w0, w2p, a0, a2p, g2p, k_k, k_a, ln_w, ln_b, r_k, seg_ones, blank_rows,
                 *, tb):
    bsz, seq, dm = x.shape
    d = RWKV_DIM
    steps = seq // tb
    full = lambda a: pl.BlockSpec(a.shape, lambda bi, t: (0,) * a.ndim)
    params = (gain, wz, mu, w0, w2p, a0, a2p, g2p, k_k, k_a, ln_w, ln_b, r_k, seg_ones)
    blank_blk = blank_rows // (bsz * steps)
    return pl.pallas_call(
        functools.partial(_rwkv_kernel, chunk=WKV_CHUNK),
        grid=(bsz, steps),
        in_specs=[pl.BlockSpec((1, tb, dm), lambda bi, t: (bi, t, 0))] + [full(a) for a in params],
        out_specs=[pl.BlockSpec((1, tb, d), lambda bi, t: (bi, t, 0)),
                   pl.BlockSpec((blank_blk, LANES), lambda bi, t: (bi * steps + t, 0))],
        out_shape=[jax.ShapeDtypeStruct((bsz, seq, d), F32),
                   jax.ShapeDtypeStruct((blank_rows, LANES), F32)],
        scratch_shapes=[pltpu.VMEM((d // HEAD_DIM, HEAD_DIM, HEAD_DIM), F32),
                        pltpu.VMEM((8, RWKV_COLS), F32)],
        compiler_params=_params("parallel", "arbitrary"),
        name="rwkv_branch",
    )(x, *params)


def _attn_kernel(q_ref, kp_ref, kc_ref, vp_ref, vc_ref, bias_ref, o_ref, l_ref):
    first = pl.program_id(2) == 0
    q = q_ref[0, 0] * (HEAD_DIM ** -0.5)
    kc, vc = kc_ref[0, 0], vc_ref[0, 0]
    subs = q.shape[0] // Q_BLOCK
    rows = lambda j: slice(j * Q_BLOCK, (j + 1) * Q_BLOCK)
    lanes = lambda h: slice(h * HEAD_DIM, (h + 1) * HEAD_DIM)
    k_prev = [kp_ref[0, 0]] + [kc[rows(j)] for j in range(subs - 1)]
    v_prev = [vp_ref[0, 0]] + [vc[rows(j)] for j in range(subs - 1)]
    work = [(j, h) for j in range(subs) for h in range(HEADS_PER_GROUP)]
    scores = []
    for j, h in work:
        bias = bias_ref[h]
        bias_prev = bias[:, :Q_BLOCK]
        if j == 0:
            bias_prev = jnp.where(first, MASKED, bias_prev)
        qh = q[rows(j), lanes(h)]
        scores.append((_mm_nt(qh, k_prev[j][:, lanes(h)]) + bias_prev,
                       _mm_nt(qh, kc[rows(j), lanes(h)]) + bias[:, Q_BLOCK:]))
    tops = [jnp.max(jnp.maximum(sp, sc), axis=-1, keepdims=True) for sp, sc in scores]
    probs = [(jnp.exp(sp - m), jnp.exp(sc - m)) for (sp, sc), m in zip(scores, tops)]
    dens = [jnp.sum(pp + pc, axis=-1, keepdims=True) for pp, pc in probs]
    accs = [_mm(pp, v_prev[j][:, lanes(h)]) + _mm(pc, vc[rows(j), lanes(h)])
            for (pp, pc), (j, h) in zip(probs, work)]
    lane = lax.broadcasted_iota(jnp.int32, (Q_BLOCK, LANES), 1)
    for j in range(subs):
        lse = jnp.zeros((Q_BLOCK, LANES), F32)
        for h in range(HEADS_PER_GROUP):
            i = j * HEADS_PER_GROUP + h
            o_ref[0, 0, rows(j), lanes(h)] = accs[i] * (1.0 / dens[i])
            lse = jnp.where(lane == h, tops[i] + jnp.log(dens[i]), lse)
        l_ref[0, 0, rows(j), :] = lse


def _attn_bias(group):
    window, dilation = ATTN_GROUPS[group]
    n_back = window // dilation
    heads = np.arange(group * HEADS_PER_GROUP, (group + 1) * HEADS_PER_GROUP)
    slopes = (2.0 ** (-8.0 * (heads + 1) / ATTN_HEADS)).astype(np.float32)
    delta = (np.arange(Q_BLOCK)[:, None] + Q_BLOCK) - np.arange(2 * Q_BLOCK)[None, :]
    valid = (delta >= 0) & (delta <= n_back)
    alibi = -slopes[:, None, None] * (dilation * delta).astype(np.float32)[None]
    return jnp.asarray(np.where(valid[None], alibi, np.float32(MASKED)).astype(np.float32))


def _attn_group(qkv, group):
    bsz, dil, comp, _ = qkv.shape
    tq = min(ATTN_TQ, comp)
    assert comp % tq == 0 and tq % Q_BLOCK == 0
    sub = tq // Q_BLOCK
    col = lambda which: (lambda bi, r, n: (bi, r, n, which))
    col_prev = lambda which: (lambda bi, r, n: (bi, r, jnp.maximum(n * sub - 1, 0), which))
    cur, prev = (1, 1, tq, ATTN_OUT_DIM), (1, 1, Q_BLOCK, ATTN_OUT_DIM)
    bias = _attn_bias(group)
    return pl.pallas_call(
        _attn_kernel,
        grid=(bsz, dil, comp // tq),
        in_specs=[pl.BlockSpec(cur, col(0)),
                  pl.BlockSpec(prev, col_prev(1)), pl.BlockSpec(cur, col(1)),
                  pl.BlockSpec(prev, col_prev(2)), pl.BlockSpec(cur, col(2)),
                  pl.BlockSpec(bias.shape, lambda bi, r, n: (0, 0, 0))],
        out_specs=[pl.BlockSpec(cur, col(0)),
                   pl.BlockSpec((1, 1, tq, LANES), col(0))],
        out_shape=[jax.ShapeDtypeStruct((bsz, dil, comp, ATTN_OUT_DIM), F32),
                   jax.ShapeDtypeStruct((bsz, dil, comp, LANES), F32)],
        compiler_params=_params("parallel", "parallel", "parallel"),
        name=f"dilated_attn_{group}",
    )(qkv, qkv, qkv, qkv, qkv, bias)


def _route(logits):
    lane = lax.broadcasted_iota(jnp.int32, logits.shape, 1)
    neg = -jnp.inf
    first_max = lambda vals, mx: jnp.min(jnp.where(vals == mx, lane, LANES), axis=-1, keepdims=True)
    is_grp = (lane >= N_EXPERTS) & (lane < N_EXPERTS + N_EXPERT_GROUPS)
    gl = jnp.where(is_grp, logits, neg)
    gmax = jnp.max(gl, axis=-1, keepdims=True)
    grp_w = 1.0 / jnp.sum(jnp.exp(gl - gmax), axis=-1, keepdims=True)
    grp = first_max(gl, gmax) - N_EXPERTS
    in_grp = (lane >= grp * EXPERTS_PER_GROUP) & (lane < (grp + 1) * EXPERTS_PER_GROUP)
    el = jnp.where(in_grp, logits, neg)
    v1 = jnp.max(el, axis=-1, keepdims=True)
    i1 = first_max(el, v1)
    el2 = jnp.where(lane == i1, neg, el)
    v2 = jnp.max(el2, axis=-1, keepdims=True)
    i2 = first_max(el2, v2)
    e2 = jnp.exp(v2 - v1)
    w1 = grp_w / (1.0 + e2)
    return i1, i2, w1, w1 * e2


def _merge_kernel(x_ref, ya_ref, o0_ref, o1_ref, o2_ref, l0_ref, l1_ref, l2_ref, gt_ref,
                  wa_ref, wb_ref, wo_ref, gain_ref, wr_ref, ex_ref, tri_ref, x1_ref, h_ref, route_ref,
                  count_ref, o_scr, l_scr):
    outs, lses = [], []
    for grp, (o_ref, l_ref) in enumerate(((o0_ref, l0_ref), (o1_ref, l1_ref), (o2_ref, l2_ref))):
        dil = o_ref.shape[1]
        if dil == 1:
            outs.append(o_ref[0, 0])
            lses.append(l_ref[0, 0])
            continue
        n = o_ref.shape[2]
        chunks = o_ref.shape[3] // LANES
        for r in range(dil):
            l_scr[grp - 1, pl.ds(r, n, stride=dil), :] = l_ref[0, r]
            for c in range(chunks):
                o_scr[grp - 1, c, pl.ds(r, n, stride=dil), :] = o_ref[0, r, :, c * LANES:(c + 1) * LANES]
        outs.append(jnp.concatenate([o_scr[grp - 1, c] for c in range(chunks)], axis=1))
        lses.append(l_scr[grp - 1])
    l0, l1, l2 = lses
    m = jnp.maximum(jnp.maximum(l0, l1), l2)
    e0, e1, e2 = jnp.exp(l0 - m), jnp.exp(l1 - m), jnp.exp(l2 - m)
    inv = 1.0 / (e0 + e1 + e2)
    spread = ex_ref[...]
    yb = (_seg_sum(e0 * inv, spread) * outs[0] + _seg_sum(e1 * inv, spread) * outs[1]
          + _seg_sum(e2 * inv, spread) * outs[2])
    gates = gt_ref[...]
    merged = (gates[:, :D_MODEL] * _mm(ya_ref[...], wa_ref[...])
              + gates[:, D_MODEL:] * _mm(yb, wb_ref[...]))
    x1 = x_ref[...] + _mm(merged, wo_ref[...])
    x1_ref[...] = x1
    h = _rms(x1, gain_ref[...])
    h_ref[...] = h
    i1, i2, w1, w2 = _route(_mm_split(h, wr_ref[...]))

    @pl.when(pl.program_id(0) == 0)
    def _():
        count_ref[...] = jnp.zeros_like(count_ref)

    lane = lax.broadcasted_iota(jnp.int32, (x1.shape[0], LANES), 1)
    onehot = jnp.where((lane == i1) | (lane == i2), 1.0, 0.0)
    before = count_ref[0:1, :] + jnp.dot(tri_ref[...], onehot.astype(BF16), preferred_element_type=F32)
    rank1 = jnp.sum(jnp.where(lane == i1, before, 0.0), axis=-1, keepdims=True)
    rank2 = jnp.sum(jnp.where(lane == i2, before, 0.0), axis=-1, keepdims=True)
    last = x1.shape[0] - 1
    count_ref[...] = jnp.broadcast_to(before[last:, :] + onehot[last:, :], count_ref.shape)
    cols = (i1.astype(F32), i2.astype(F32), w1, w2, rank1, rank2)
    info = jnp.zeros(lane.shape, F32)
    for c, col in enumerate(cols):
        info = jnp.where(lane == c, col, info)
    route_ref[...] = info


def _merge(x, ya, outs, lses, gates, wa, wb, wo, gain, wr, spread, seq, *, tm):
    t = x.shape[0]
    tps = seq // tm
    tok = lambda n: pl.BlockSpec((tm, n), lambda i: (i, 0))
    full = lambda a: pl.BlockSpec(a.shape, lambda i: (0, 0))
    res = lambda a: pl.BlockSpec((1, a.shape[1], tm // a.shape[1], a.shape[3]),
                                 lambda i: (i // tps, 0, i % tps, 0))
    dilated = len(ATTN_GROUPS) - 1
    tri = jnp.asarray(np.tri(tm, k=-1), dtype=BF16)
    return pl.pallas_call(
        _merge_kernel,
        grid=(t // tm,),
        in_specs=[tok(D_MODEL), tok(RWKV_DIM)] + [res(a) for a in outs] + [res(a) for a in lses] + [
                  tok(2 * D_MODEL), full(wa), full(wb), full(wo), full(gain), full(wr), full(spread), full(tri)],
        out_specs=[tok(D_MODEL), tok(D_MODEL), tok(LANES), pl.BlockSpec((8, LANES), lambda i: (0, 0))],
        out_shape=[jax.ShapeDtypeStruct((t, D_MODEL), F32), jax.ShapeDtypeStruct((t, D_MODEL), F32),
                   jax.ShapeDtypeStruct((t, LANES), F32), jax.ShapeDtypeStruct((8, LANES), F32)],
        scratch_shapes=[pltpu.VMEM((dilated, ATTN_OUT_DIM // LANES, tm, LANES), F32),
                        pltpu.VMEM((dilated, tm, LANES), F32)],
        compiler_params=_params("arbitrary"),
        name="merge_route",
    )(x, ya, *outs, *lses, gates, wa, wb, wo, gain, wr, spread, tri)


ROUTE_E1, ROUTE_E2, ROUTE_W1, ROUTE_W2, ROUTE_R1, ROUTE_R2 = range(6)


def _moe_plan(counts, n_tiles):
    counts = counts[0, :N_EXPERTS].astype(jnp.int32)
    tile_end = jnp.cumsum((counts + MOE_TILE - 1) // MOE_TILE)
    n_valid = tile_end[-1]
    tile = jnp.minimum(jnp.arange(n_tiles, dtype=jnp.int32), n_valid - 1)
    tile_expert = jnp.sum(tile[:, None] >= tile_end[None, :], axis=1).astype(jnp.int32)
    return tile_expert, n_valid.reshape(1)


def _row_tiles(ref, rows, tile=ROW_TILE):
    return jnp.concatenate([ref[pl.ds(c, rows, stride=tile), :] for c in range(tile)], axis=1)


def _store_row_tiles(ref, x, tile=ROW_TILE):
    for c in range(tile):
        ref[pl.ds(c, x.shape[0], stride=tile), :] = x[:, c * LANES:(c + 1) * LANES]


def _row(ref, r, tile=ROW_TILE):
    return ref.at[pl.ds(pl.multiple_of(r * tile, tile), tile)]


def _dispatch_kernel(route_ref, count_ref, upper_ref, h_ref, xs_in_ref, xs_ref, pos_ref, rows_ref, pos_vmem,
                     pos_smem, sem, pos_sem):
    del xs_in_ref
    tm = h_ref.shape[0]
    tiles = jnp.floor((count_ref[...] + (MOE_TILE - 1)) * (1.0 / MOE_TILE))
    first_row = jnp.dot(tiles.astype(BF16), upper_ref[...], preferred_element_type=F32)[0:1, :] * MOE_TILE
    route = route_ref[...]
    lane = lax.broadcasted_iota(jnp.int32, route.shape, 1)
    col = lambda c: route[:, c:c + 1]
    place = lambda e, r: jnp.sum(jnp.where(lane == col(e).astype(jnp.int32), first_row, 0.0), axis=-1,
                                 keepdims=True) + col(r)
    both = jnp.where(lane == 0, place(ROUTE_E1, ROUTE_R1), jnp.where(lane == 1, place(ROUTE_E2, ROUTE_R2), 0.0))
    pos = both.T[0:ROW_TILE, :].astype(jnp.int32)
    pos_ref[0] = pos[0:2]
    pos_vmem[...] = pos
    to_smem = pltpu.make_async_copy(pos_vmem, pos_smem, pos_sem)
    to_smem.start()
    _store_row_tiles(rows_ref, h_ref[...])
    to_smem.wait()

    def copies(j):
        src = _row(rows_ref, j)
        return (pltpu.make_async_copy(src, _row(xs_ref, pos_smem[0, j]), sem.at[0]),
                pltpu.make_async_copy(src, _row(xs_ref, pos_smem[1, j]), sem.at[1]))

    def start(j, carry):
        for stream, cp in enumerate(copies(j)):
            cp.start(priority=stream)
        return carry

    lax.fori_loop(0, tm, start, 0, unroll=8)
    for stream in range(2):
        pltpu.make_async_copy(rows_ref, rows_ref, sem.at[stream]).wait()


def _dispatch(h, route, counts, blank, *, tm):
    t, d = h.shape
    upper = jnp.asarray(np.triu(np.ones((LANES, LANES)), k=1), dtype=BF16)
    tok = lambda n: pl.BlockSpec((tm, n), lambda i: (i, 0))
    full = lambda a: pl.BlockSpec(a.shape, lambda i: (0, 0))
    return pl.pallas_call(
        _dispatch_kernel,
        grid=(t // tm,),
        in_specs=[tok(LANES), full(counts), full(upper), tok(d), pl.BlockSpec(memory_space=pl.ANY)],
        out_specs=[pl.BlockSpec(memory_space=pl.ANY), pl.BlockSpec((1, 2, tm), lambda i: (i, 0, 0))],
        out_shape=[jax.ShapeDtypeStruct(blank.shape, blank.dtype),
                   jax.ShapeDtypeStruct((t // tm, 2, tm), jnp.int32)],
        scratch_shapes=[pltpu.VMEM((tm * ROW_TILE, LANES), F32), pltpu.VMEM((ROW_TILE, tm), jnp.int32),
                        pltpu.SMEM((ROW_TILE, tm), jnp.int32), pltpu.SemaphoreType.DMA((2,)),
                        pltpu.SemaphoreType.DMA],
        input_output_aliases={4: 0},
        compiler_params=_params("arbitrary"),
        name="moe_dispatch",
    )(route, counts, upper, h, blank)


def _expert_kernel(te_ref, nv_ref, x_ref, wg_ref, wu_ref, wd_ref, o_ref):
    live = pl.program_id(0) < nv_ref[0]

    @pl.when(live)
    def _():
        x = _row_tiles(x_ref, MOE_TILE)
        gate = _mm(x, wg_ref[0])
        up = _mm(x, wu_ref[0])
        _store_row_tiles(o_ref, _mm(gate * _sigmoid(gate) * up, wd_ref[0]))

    @pl.when(jnp.logical_not(live))
    def _():
        o_ref[...] = jnp.zeros_like(o_ref)


def _experts(xs, tile_expert, n_valid, wg, wu, wd):
    d = wg.shape[1]
    weight = lambda shape: pl.BlockSpec((1,) + shape, lambda i, te, nv: (te[i], 0, 0))
    blk = (MOE_TILE * ROW_TILE, LANES)
    return pl.pallas_call(
        _expert_kernel,
        grid_spec=pltpu.PrefetchScalarGridSpec(
            num_scalar_prefetch=2,
            grid=(xs.shape[0] // blk[0],),
            in_specs=[pl.BlockSpec(blk, lambda i, te, nv: (jnp.minimum(i, nv[0] - 1), 0)),
                      weight((d, D_FF_EXPERT)), weight((d, D_FF_EXPERT)), weight((D_FF_EXPERT, d))],
            out_specs=pl.BlockSpec(blk, lambda i, te, nv: (i, 0))),
        out_shape=jax.ShapeDtypeStruct(xs.shape, F32),
        compiler_params=_params("arbitrary"),
        name="moe_experts",
    )(tile_expert, n_valid, xs, wg, wu, wd)


def _ple_kernel(pos_ref, x_ref, route_ref, p_ref, gp_ref, wg_ref, wp_ref, gf_ref, ys_ref, o_ref,
                y_ref, sem):
    tm = x_ref.shape[0]
    step, steps = pl.program_id(0), pl.num_programs(0)

    def gather(tile, slot, inline):
        base = tile * (2 * tm)

        def body(j, carry):
            for stream in range(2):
                pltpu.make_async_copy(_row(ys_ref, pos_ref[base + stream * tm + j]),
                                      _row(y_ref.at[slot, stream], j), sem.at[slot, stream]
                                      ).start(priority=stream)
            return carry

        if inline:
            for j in range(tm):
                body(j, 0)
        else:
            lax.fori_loop(0, tm, body, 0, unroll=8)

    def drain(slot):
        for stream in range(2):
            buf = y_ref.at[slot, stream]
            pltpu.make_async_copy(buf, buf, sem.at[slot, stream]).wait()

    slot = step % 2

    @pl.when(step == 0)
    def _():
        gather(0, 0, False)

    drain(slot)
    gather(jnp.minimum(step + 1, steps - 1), 1 - slot, True)
    route = route_ref[...]
    w1, w2 = route[:, ROUTE_W1:ROUTE_W1 + 1], route[:, ROUTE_W2:ROUTE_W2 + 1]
    x = x_ref[...] + (w1 * _row_tiles(y_ref.at[slot, 0], tm) + w2 * _row_tiles(y_ref.at[slot, 1], tm))
    gate = _sigmoid(_mm(_rms(x, gp_ref[...]), wg_ref[...]))
    x = x + gate * _mm(p_ref[...], wp_ref[...])
    o_ref[...] = _rms(x, gf_ref[...])

    @pl.when(step == steps - 1)
    def _():
        drain(1 - slot)


def _ple(x, route, pos, ys, p, gain_ple, w_gate, w_proj, gain_final, *, tm):
    t = x.shape[0]
    tok = lambda n: pl.BlockSpec((tm, n), lambda i, ps: (i, 0))
    full = lambda a: pl.BlockSpec(a.shape, lambda i, ps: (0, 0))
    return pl.pallas_call(
        _ple_kernel,
        grid_spec=pltpu.PrefetchScalarGridSpec(
            num_scalar_prefetch=1,
            grid=(t // tm,),
            in_specs=[tok(D_MODEL), tok(LANES), tok(PLE_DIM), full(gain_ple), full(w_gate), full(w_proj),
                      full(gain_final), pl.BlockSpec(memory_space=pl.ANY)],
            out_specs=tok(D_MODEL),
            scratch_shapes=[pltpu.VMEM((2, 2, tm * ROW_TILE, LANES), F32), pltpu.SemaphoreType.DMA((2, 2))]),
        out_shape=jax.ShapeDtypeStruct((t, D_MODEL), F32),
        compiler_params=_params("arbitrary"),
        name="ple_final",
    )(pos, x, route, p, gain_ple, w_gate, w_proj, gain_final, ys)


def _pad_rows(w, start, total):
    return jnp.zeros((total, w.shape[1]), w.dtype).at[start:start + w.shape[0]].set(w)


def _layer(x, p, seq, norm_mix, w_in, rwkv_mu, rwkv_w0, rwkv_w2, rwkv_a0, rwkv_a2, rwkv_g2, rwkv_k_k,
           rwkv_k_a, rwkv_r_k, rwkv_ln_w, rwkv_ln_b, w_proj_a, w_proj_b, w_out, norm_moe, w_router_group,
           w_router_expert, w_exp_gate, w_exp_up, w_exp_down, norm_ple, w_ple_gate, w_ple_proj, norm_final):
    t = x.shape[0]
    bsz = t // seq
    row = lambda a: a.reshape(1, -1)
    c0, c1 = RWKV_COLS, RWKV_COLS + ATTN_COLS
    x3 = x.reshape(bsz, seq, D_MODEL)

    def w_group(grp):
        lo = lambda which: c0 + which * ATTN_DIM + grp * ATTN_OUT_DIM
        return jnp.concatenate([w_in[:, lo(which):lo(which) + ATTN_OUT_DIM] for which in range(3)],
                               axis=1).astype(BF16)

    qkv = [_qkv_proj(x3, row(norm_mix), w_group(grp), grp, tm=1024) for grp in range(len(ATTN_GROUPS))]
    gates = _norm_matmul(x, row(norm_mix), w_in[:, c1:].astype(BF16), BF16, tm=1024, tn=D_MODEL, sigmoid=True)

    seg = np.arange(RWKV_DIM) // HEAD_DIM
    seg_ones = jnp.asarray(seg[:, None] == seg[None, :], dtype=BF16)
    n_tiles = 2 * t // MOE_TILE + N_EXPERTS
    ya, blank = _rwkv_branch(
        x3, row(norm_mix), w_in[:, :c0].astype(BF16), row(rwkv_mu), row(rwkv_w0),
        _pad_rows(rwkv_w2, 0, LORA_COLS), row(rwkv_a0), _pad_rows(rwkv_a2, W_LORA, LORA_COLS),
        _pad_rows(rwkv_g2, W_LORA + A_LORA, LORA_COLS), row(rwkv_k_k), row(rwkv_k_a), row(rwkv_ln_w),
        row(rwkv_ln_b), row(rwkv_r_k), seg_ones, n_tiles * MOE_TILE * ROW_TILE, tb=256)
    ya = ya.reshape(t, RWKV_DIM)

    attn = [_attn_group(qkv[grp], grp) for grp in range(len(ATTN_GROUPS))]

    head_of_lane = np.arange(ATTN_OUT_DIM) // HEAD_DIM
    spread = jnp.asarray(np.arange(LANES)[:, None] == head_of_lane[None, :], dtype=BF16)
    w_route = jnp.concatenate(
        [jnp.moveaxis(w_router_expert, 0, 1).reshape(D_MODEL, N_EXPERTS), w_router_group,
         jnp.zeros((D_MODEL, LANES - N_EXPERTS - N_EXPERT_GROUPS), F32)], axis=1)
    x1, h_moe, route, counts = _merge(x, ya, [o for o, _ in attn], [l for _, l in attn], gates,
                             w_proj_a.astype(BF16), w_proj_b.astype(BF16), w_out.astype(BF16),
                             row(norm_moe), w_route, spread, seq, tm=512)
    tile_expert, n_valid = _moe_plan(counts, n_tiles)
    dispatch_tm = 512
    xs, pos = _dispatch(h_moe, route, counts, blank, tm=dispatch_tm)
    ys = _experts(xs, tile_expert, n_valid,
                  w_exp_gate.reshape(N_EXPERTS, D_MODEL, D_FF_EXPERT),
                  w_exp_up.reshape(N_EXPERTS, D_MODEL, D_FF_EXPERT),
                  w_exp_down.reshape(N_EXPERTS, D_FF_EXPERT, D_MODEL))
    return _ple(x1, route, pos.reshape(-1), ys, p, row(norm_ple), w_ple_gate.astype(BF16),
                w_ple_proj.astype(BF16), row(norm_final), tm=dispatch_tm)


def kernel(x, p, norm_mix, w_in, rwkv_mu, rwkv_w0, rwkv_w2, rwkv_a0, rwkv_a2, rwkv_g2, rwkv_k_k, rwkv_k_a, rwkv_r_k, rwkv_ln_w, rwkv_ln_b, w_proj_a, w_proj_b, w_out, norm_moe, w_router_group, w_router_expert, w_exp_gate, w_exp_up, w_exp_down, norm_ple, w_ple_gate, w_ple_proj, norm_final):
    bsz, seq, d = x.shape
    depth = w_in.shape[0]
    assert depth == 1, "the final norm is fused into the (single) layer"
    out = _layer(x.reshape(bsz * seq, d).astype(F32), p[0].reshape(bsz * seq, PLE_DIM), seq,
                 norm_mix[0], w_in[0], rwkv_mu[0], rwkv_w0[0], rwkv_w2[0], rwkv_a0[0], rwkv_a2[0],
                 rwkv_g2[0], rwkv_k_k[0], rwkv_k_a[0], rwkv_r_k[0], rwkv_ln_w[0], rwkv_ln_b[0],
                 w_proj_a[0], w_proj_b[0], w_out[0], norm_moe[0], w_router_group[0], w_router_expert[0],
                 w_exp_gate[0], w_exp_up[0], w_exp_down[0], norm_ple[0], w_ple_gate[0], w_ple_proj[0],
                 norm_final)
    return out.reshape(bsz, seq, d)
```

```python
import functools

import numpy as np
import jax
import jax.numpy as jnp
from jax import lax
from jax.experimental import pallas as pl
from jax.experimental.pallas import tpu as pltpu

F32 = jnp.float32
BF16 = jnp.bfloat16
HIGHEST = lax.Precision.HIGHEST

D_MODEL = 1024
PLE_DIM = 256
HEAD_DIM = 64
RWKV_HEADS = 8
RWKV_DIM = RWKV_HEADS * HEAD_DIM
W_LORA, A_LORA, G_LORA = 64, 64, 128
LORA_COLS = W_LORA + A_LORA + G_LORA
ATTN_GROUPS = ((128, 1), (512, 4), (2048, 16))
HEADS_PER_GROUP = 8
ATTN_HEADS = HEADS_PER_GROUP * len(ATTN_GROUPS)
ATTN_DIM = ATTN_HEADS * HEAD_DIM
ATTN_OUT_DIM = HEADS_PER_GROUP * HEAD_DIM
Q_BLOCK = 128
ATTN_TQ = 512
RWKV_COLS = 3 * RWKV_DIM + LORA_COLS
ATTN_COLS = 3 * ATTN_DIM
N_EXPERT_GROUPS = 4
EXPERTS_PER_GROUP = 8
N_EXPERTS = N_EXPERT_GROUPS * EXPERTS_PER_GROUP
D_FF_EXPERT = 256
NORM_EPS = 1e-6
RWKV_GN_EPS = 64e-5
LANES = 128
MASKED = -1e30
WKV_CHUNK = 64
VMEM_LIMIT = 48 * 1024 * 1024
MOE_TILE = 512
ROW_TILE = 8


def _mm(a, b):
    return jnp.dot(a.astype(BF16), b.astype(BF16), preferred_element_type=F32)


def _mm_nt(a, b):
    return lax.dot_general(a.astype(BF16), b.astype(BF16), (((1,), (1,)), ((), ())),
                           preferred_element_type=F32)


def _mm_split(a, b):
    a_hi, b_hi = a.astype(BF16), b.astype(BF16)
    a_lo = (a - a_hi.astype(F32)).astype(BF16)
    b_lo = (b - b_hi.astype(F32)).astype(BF16)
    dot = functools.partial(jnp.dot, preferred_element_type=F32)
    return dot(a_hi, b_hi) + (dot(a_hi, b_lo) + dot(a_lo, b_hi))


def _sigmoid(x):
    return 1.0 / (1.0 + jnp.exp(-x))


def _rms(x, gain):
    return x * lax.rsqrt(jnp.mean(x * x, axis=-1, keepdims=True) + NORM_EPS) * gain


def _params(*sem):
    return pltpu.CompilerParams(dimension_semantics=sem, vmem_limit_bytes=VMEM_LIMIT)


def _qkv_rows_kernel(x_ref, g_ref, w_ref, o_ref, stage_ref, *, dil):
    h = _rms(x_ref[0], g_ref[...])
    if dil == 1:
        o_ref[0, 0] = _mm(h, w_ref[...]).astype(BF16)
        return
    chunks = h.shape[1] // LANES
    for c in range(chunks):
        stage_ref[c] = h[:, c * LANES:(c + 1) * LANES]
    n = h.shape[0] // dil
    for r in range(dil):
        rows = jnp.concatenate([stage_ref[c, pl.ds(r, n, stride=dil), :] for c in range(chunks)], axis=1)
        o_ref[0, r] = _mm(rows, w_ref[...]).astype(BF16)


def _qkv_strided_kernel(x_hbm, g_ref, w_ref, o_ref, x_buf, sem):
    tm = x_buf.shape[1]
    n_res, n_blk = pl.num_programs(1), pl.num_programs(2)
    step = (pl.program_id(0) * n_res + pl.program_id(1)) * n_blk + pl.program_id(2)
    steps = pl.num_programs(0) * n_res * n_blk

    def fetch(s, slot):
        b, r, i = s // (n_res * n_blk), (s // n_blk) % n_res, s % n_blk
        return pltpu.make_async_copy(x_hbm.at[b, pl.ds(i * tm, tm), r], x_buf.at[slot], sem.at[slot])

    slot = step % 2

    @pl.when(step == 0)
    def _():
        fetch(0, 0).start()

    @pl.when(step + 1 < steps)
    def _():
        fetch(step + 1, 1 - slot).start()

    fetch(step, slot).wait()
    o_ref[0, 0] = _mm(_rms(x_buf[slot], g_ref[...]), w_ref[...]).astype(BF16)


def _qkv_proj(x, gain, w, group, *, tm):
    bsz, seq, d = x.shape
    dil = ATTN_GROUPS[group][1]
    comp = seq // dil
    tn = w.shape[1]
    out_shape = jax.ShapeDtypeStruct((bsz, dil, comp, tn), BF16)
    if dil % ROW_TILE:
        return pl.pallas_call(
            functools.partial(_qkv_rows_kernel, dil=dil),
            grid=(bsz, seq // tm),
            in_specs=[pl.BlockSpec((1, tm, d), lambda b, i: (b, i, 0)),
                      pl.BlockSpec((1, d), lambda b, i: (0, 0)),
                      pl.BlockSpec((d, tn), lambda b, i: (0, 0))],
            out_specs=pl.BlockSpec((1, dil, tm // dil, tn), lambda b, i: (b, 0, i, 0)),
            out_shape=out_shape,
            scratch_shapes=[pltpu.VMEM((d // LANES, tm, LANES), F32)],
            compiler_params=_params("parallel", "parallel"),
            name=f"qkv_proj_{group}",
        )(x, gain, w)
    tm = min(tm, comp)
    return pl.pallas_call(
        _qkv_strided_kernel,
        grid=(bsz, dil, comp // tm),
        in_specs=[pl.BlockSpec(memory_space=pl.ANY),
                  pl.BlockSpec((1, d), lambda b, r, i: (0, 0)),
                  pl.BlockSpec((d, tn), lambda b, r, i: (0, 0))],
        out_specs=pl.BlockSpec((1, 1, tm, tn), lambda b, r, i: (b, r, i, 0)),
        out_shape=out_shape,
        scratch_shapes=[pltpu.VMEM((2, tm, d), F32), pltpu.SemaphoreType.DMA((2,))],
        compiler_params=_params("arbitrary", "arbitrary", "arbitrary"),
        name=f"qkv_proj_{group}",
    )(x.reshape(bsz, comp, dil, d), gain, w)


def _seg_sum(x, seg_ones):
    hi = x.astype(BF16)
    lo = (x - hi.astype(F32)).astype(BF16)
    return (jnp.dot(hi, seg_ones, preferred_element_type=F32)
            + jnp.dot(lo, seg_ones, preferred_element_type=F32))


def _rwkv_inputs(z, last, mu, w0, w2, a0, a2, g2, k_k, k_a, seg):
    row = lax.broadcasted_iota(jnp.int32, z.shape, 0)
    prev = jnp.where(row == 0, last, pltpu.roll(z, 1, axis=0))
    zs = z + (prev - z) * mu
    d = RWKV_DIM
    r, k, v = zs[:, :d], zs[:, d:2 * d], zs[:, 2 * d:3 * d]
    lora = zs[:, 3 * d:]
    lane = lax.broadcasted_iota(jnp.int32, lora.shape, 1)
    lora = jnp.where(lane < W_LORA, jnp.tanh(lora),
                     jnp.where(lane < W_LORA + A_LORA, lora, _sigmoid(lora)))
    w_lin = w0 + _mm_split(lora, w2)
    a_lin = a0 + _mm_split(lora, a2)
    g = _mm_split(lora, g2)
    u = -w_lin
    softplus = jnp.maximum(u, 0.0) + jnp.log(1.0 + jnp.exp(-jnp.abs(u)))
    lw = -jnp.exp(-softplus - 0.5)
    a = _sigmoid(a_lin)
    kk = k * k_k
    kk = kk * lax.rsqrt(_seg_sum(kk * kk, seg) + 1e-12)
    k = k * (1.0 + (a - 1.0) * k_a)
    return r, lw, k, v, kk, kk * a, g


def _bmm(a, b):
    return lax.dot_general(a.astype(BF16), b.astype(BF16), (((2,), (1,)), ((0,), (0,))),
                           preferred_element_type=F32)


def _bmm_nt(a, b):
    return lax.dot_general(a.astype(BF16), b.astype(BF16), (((2,), (2,)), ((0,), (0,))),
                           preferred_element_type=F32)


def _bmm_tn(a, b):
    return lax.dot_general(a.astype(BF16), b.astype(BF16), (((1,), (1,)), ((0,), (0,))),
                           preferred_element_type=F32)


def _rwkv_kernel(x_ref, gain_ref, wz_ref, mu_ref, w0_ref, w2_ref, a0_ref, a2_ref, g2_ref, kk_ref, ka_ref,
                 lnw_ref, lnb_ref, rk_ref, seg_ref, o_ref, blank_ref, s_ref, zlast_ref, *, chunk):
    @pl.when(pl.program_id(1) == 0)
    def _():
        s_ref[...] = jnp.zeros_like(s_ref)
        zlast_ref[...] = jnp.zeros_like(zlast_ref)

    tb = x_ref.shape[1]
    d = RWKV_DIM
    z = jnp.dot(_rms(x_ref[0], gain_ref[...]).astype(BF16), wz_ref[...], preferred_element_type=F32)
    last = zlast_ref[7:8, :]
    zlast_ref[...] = z[tb - 8:, :]
    r, lw, k, v, kk, b, g = _rwkv_inputs(z, last, mu_ref[...], w0_ref[...], w2_ref[...], a0_ref[...],
                                         a2_ref[...], g2_ref[...], kk_ref[...], ka_ref[...], seg_ref[...])
    nc, nh = tb // chunk, d // HEAD_DIM
    ii = lax.broadcasted_iota(jnp.int32, (chunk, chunk), 0)
    jj = lax.broadcasted_iota(jnp.int32, (chunk, chunk), 1)
    incl, strict = ii >= jj, ii > jj
    tri = incl.astype(F32)
    eye = (ii == jj).astype(F32)
    rows = lambda c: slice(c * chunk, (c + 1) * chunk)
    lanes = lambda h: slice(h * HEAD_DIM, (h + 1) * HEAD_DIM)

    def split(x):
        return jnp.stack([x[rows(c), lanes(h)] for c in range(nc) for h in range(nh)])

    cums = [jnp.dot(tri, lw[rows(c)], precision=HIGHEST, preferred_element_type=F32) for c in range(nc)]
    cum = jnp.concatenate(cums, axis=0)
    total = jnp.concatenate([jnp.broadcast_to(cs[chunk - 1:], (chunk, d)) for cs in cums], axis=0)
    g_inv = jnp.exp(-cum)
    g_tail = jnp.exp(total - cum)
    kq, rq = split(kk * jnp.exp(cum - lw)), split(r * jnp.exp(cum))
    bi, ki = split(b * g_inv), split(k * g_inv)
    bh, kh = split(b * g_tail), split(k * g_tail)
    vs = split(v)
    decay = jnp.stack([jnp.exp(cs[chunk - 1:, lanes(h)]) for cs in cums for h in range(nh)])

    qr = jnp.concatenate([kq, rq], axis=1)
    ab, ak = _bmm_nt(qr, bi), _bmm_nt(qr, ki)
    a_b, a_rb = jnp.where(strict, ab[:, :chunk], 0.0), jnp.where(incl, ab[:, chunk:], 0.0)
    a_k, a_rk = jnp.where(strict, ak[:, :chunk], 0.0), jnp.where(incl, ak[:, chunk:], 0.0)
    n = nc * nh
    is_b = lax.broadcasted_iota(jnp.int32, (chunk, 2 * chunk), 1) < chunk
    bt = jnp.concatenate([-a_b, jnp.broadcast_to(eye, (n, chunk, chunk))], axis=2)
    for _ in range(int(np.log2(chunk))):
        prod = _bmm(bt[:, :, :chunk], bt)
        bt = jnp.where(is_b, prod, bt + prod)
    inv = bt[:, :, chunk:]
    wu = _bmm(inv, jnp.concatenate([kq, -_bmm(a_k, vs)], axis=2))
    aw = _bmm(a_rb, wu)
    qh = rq - aw[:, :, :HEAD_DIM]
    y0 = aw[:, :, HEAD_DIM:] + _bmm(a_rk, vs)
    mu = _bmm_tn(wu, bh)
    mp = mu[:, :HEAD_DIM]
    s_add = mu[:, HEAD_DIM:] + _bmm_tn(vs, kh)

    state = s_ref[...]
    for c in range(nc):
        sl = slice(c * nh, (c + 1) * nh)
        y = _bmm_nt(qh[sl], state) + y0[sl]
        for h in range(nh):
            o_ref[0, rows(c), lanes(h)] = y[h]
        state = state * decay[sl] - _bmm(state, mp[sl]) + s_add[sl]
    s_ref[...] = state

    seg = seg_ref[...]
    y = o_ref[0]
    yc = y - _seg_sum(y, seg) * (1.0 / HEAD_DIM)
    var = _seg_sum(yc * yc, seg) * (1.0 / HEAD_DIM)
    y = yc * lax.rsqrt(var + RWKV_GN_EPS) * lnw_ref[...] + lnb_ref[...]
    y = y + _seg_sum(r * k * rk_ref[...], seg) * v
    o_ref[0] = y * g
    blank_ref[...] = jnp.zeros_like(blank_ref)


def _rwkv_branch(x, gain, wz, mu, w0, w2p, a0, a2p, g2p, k_k, k_a, ln_w, ln_b, r_k, seg_ones, blank_rows,
                 *, tb):
    bsz, seq, dm = x.shape
    d = RWKV_DIM
    steps = seq // tb
    full = lambda a: pl.BlockSpec(a.shape, lambda bi, t: (0,) * a.ndim)
    params = (gain, wz, mu, w0, w2p, a0, a2p, g2p, k_k, k_a, ln_w, ln_b, r_k, seg_ones)
    blank_blk = blank_rows // (bsz * steps)
    return pl.pallas_call(
        functools.partial(_rwkv_kernel, chunk=WKV_CHUNK),
        grid=(bsz, steps),
        in_specs=[pl.BlockSpec((1, tb, dm), lambda bi, t: (bi, t, 0))] + [full(a) for a in params],
        out_specs=[pl.BlockSpec((1, tb, d), lambda bi, t: (bi, t, 0)),
                   pl.BlockSpec((blank_blk, LANES), lambda bi, t: (bi * steps + t, 0))],
        out_shape=[jax.ShapeDtypeStruct((bsz, seq, d), F32),
                   jax.ShapeDtypeStruct((blank_rows, LANES), F32)],
        scratch_shapes=[pltpu.VMEM((d // HEAD_DIM, HEAD_DIM, HEAD_DIM), F32),
                        pltpu.VMEM((8, RWKV_COLS), F32)],
        compiler_params=_params("parallel", "arbitrary"),
        name="rwkv_branch",
    )(x, *params)


def _attn_kernel(q_ref, kp_ref, kc_ref, vp_ref, vc_ref, bias_ref, o_ref, l_ref):
    first = pl.program_id(2) == 0
    q = q_ref[0, 0] * (HEAD_DIM ** -0.5)
    kc, vc = kc_ref[0, 0], vc_ref[0, 0]
    subs = q.shape[0] // Q_BLOCK
    rows = lambda j: slice(j * Q_BLOCK, (j + 1) * Q_BLOCK)
    lanes = lambda h: slice(h * HEAD_DIM, (h + 1) * HEAD_DIM)
    k_prev = [kp_ref[0, 0]] + [kc[rows(j)] for j in range(subs - 1)]
    v_prev = [vp_ref[0, 0]] + [vc[rows(j)] for j in range(subs - 1)]
    work = [(j, h) for j in range(subs) for h in range(HEADS_PER_GROUP)]
    scores = []
    for j, h in work:
        bias = bias_ref[h]
        bias_prev = bias[:, :Q_BLOCK]
        if j == 0:
            bias_prev = jnp.where(first, MASKED, bias_prev)
        qh = q[rows(j), lanes(h)]
        scores.append((_mm_nt(qh, k_prev[j][:, lanes(h)]) + bias_prev,
                       _mm_nt(qh, kc[rows(j), lanes(h)]) + bias[:, Q_BLOCK:]))
    tops = [jnp.max(jnp.maximum(sp, sc), axis=-1, keepdims=True) for sp, sc in scores]
    probs = [(jnp.exp(sp - m), jnp.exp(sc - m)) for (sp, sc), m in zip(scores, tops)]
    dens = [jnp.sum(pp + pc, axis=-1, keepdims=True) for pp, pc in probs]
    accs = [_mm(pp, v_prev[j][:, lanes(h)]) + _mm(pc, vc[rows(j), lanes(h)])
            for (pp, pc), (j, h) in zip(probs, work)]
    lane = lax.broadcasted_iota(jnp.int32, (Q_BLOCK, LANES), 1)
    for j in range(subs):
        lse = jnp.zeros((Q_BLOCK, LANES), F32)
        for h in range(HEADS_PER_GROUP):
            i = j * HEADS_PER_GROUP + h
            o_ref[0, 0, rows(j), lanes(h)] = accs[i] * (1.0 / dens[i])
            lse = jnp.where(lane == h, tops[i] + jnp.log(dens[i]), lse)
        l_ref[0, 0, rows(j), :] = lse


def _attn_bias(group):
    window, dilation = ATTN_GROUPS[group]
    n_back = window // dilation
    heads = np.arange(group * HEADS_PER_GROUP, (group + 1) * HEADS_PER_GROUP)
    slopes = (2.0 ** (-8.0 * (heads + 1) / ATTN_HEADS)).astype(np.float32)
    delta = (np.arange(Q_BLOCK)[:, None] + Q_BLOCK) - np.arange(2 * Q_BLOCK)[None, :]
    valid = (delta >= 0) & (delta <= n_back)
    alibi = -slopes[:, None, None] * (dilation * delta).astype(np.float32)[None]
    return jnp.asarray(np.where(valid[None], alibi, np.float32(MASKED)).astype(np.float32))


def _attn_group(qkv, group):
    bsz, dil, comp, _ = qkv.shape
    tq = min(ATTN_TQ, comp)
    assert comp % tq == 0 and tq % Q_BLOCK == 0
    sub = tq // Q_BLOCK
    col = lambda which: (lambda bi, r, n: (bi, r, n, which))
    col_prev = lambda which: (lambda bi, r, n: (bi, r, jnp.maximum(n * sub - 1, 0), which))
    cur, prev = (1, 1, tq, ATTN_OUT_DIM), (1, 1, Q_BLOCK, ATTN_OUT_DIM)
    bias = _attn_bias(group)
    return pl.pallas_call(
        _attn_kernel,
        grid=(bsz, dil, comp // tq),
        in_specs=[pl.BlockSpec(cur, col(0)),
                  pl.BlockSpec(prev, col_prev(1)), pl.BlockSpec(cur, col(1)),
                  pl.BlockSpec(prev, col_prev(2)), pl.BlockSpec(cur, col(2)),
                  pl.BlockSpec(bias.shape, lambda bi, r, n: (0, 0, 0))],
        out_specs=[pl.BlockSpec(cur, col(0)),
                   pl.BlockSpec((1, 1, tq, LANES), col(0))],
        out_shape=[jax.ShapeDtypeStruct((bsz, dil, comp, ATTN_OUT_DIM), F32),
                   jax.ShapeDtypeStruct((bsz, dil, comp, LANES), F32)],
        compiler_params=_params("parallel", "parallel", "parallel"),
        name=f"dilated_attn_{group}",
    )(qkv, qkv, qkv, qkv, qkv, bias)


def _route(logits):
    lane = lax.broadcasted_iota(jnp.int32, logits.shape, 1)
    neg = -jnp.inf
    first_max = lambda vals, mx: jnp.min(jnp.where(vals == mx, lane, LANES), axis=-1, keepdims=True)
    is_grp = (lane >= N_EXPERTS) & (lane < N_EXPERTS + N_EXPERT_GROUPS)
    gl = jnp.where(is_grp, logits, neg)
    gmax = jnp.max(gl, axis=-1, keepdims=True)
    grp_w = 1.0 / jnp.sum(jnp.exp(gl - gmax), axis=-1, keepdims=True)
    grp = first_max(gl, gmax) - N_EXPERTS
    in_grp = (lane >= grp * EXPERTS_PER_GROUP) & (lane < (grp + 1) * EXPERTS_PER_GROUP)
    el = jnp.where(in_grp, logits, neg)
    v1 = jnp.max(el, axis=-1, keepdims=True)
    i1 = first_max(el, v1)
    el2 = jnp.where(lane == i1, neg, el)
    v2 = jnp.max(el2, axis=-1, keepdims=True)
    i2 = first_max(el2, v2)
    e2 = jnp.exp(v2 - v1)
    w1 = grp_w / (1.0 + e2)
    return i1, i2, w1, w1 * e2


def _merge_kernel(x_ref, ya_ref, o0_ref, o1_ref, o2_ref, l0_ref, l1_ref, l2_ref, gmix_ref, wgt_ref,
                  wa_ref, wb_ref, wo_ref, gain_ref, wr_ref, ex_ref, tri_ref, x1_ref, h_ref, route_ref,
                  count_ref, o_scr, l_scr):
    outs, lses = [], []
    for grp, (o_ref, l_ref) in enumerate(((o0_ref, l0_ref), (o1_ref, l1_ref), (o2_ref, l2_ref))):
        dil = o_ref.shape[1]
        if dil == 1:
            outs.append(o_ref[0, 0])
            lses.append(l_ref[0, 0])
            continue
        n = o_ref.shape[2]
        chunks = o_ref.shape[3] // LANES
        for r in range(dil):
            l_scr[grp - 1, pl.ds(r, n, stride=dil), :] = l_ref[0, r]
            for c in range(chunks):
                o_scr[grp - 1, c, pl.ds(r, n, stride=dil), :] = o_ref[0, r, :, c * LANES:(c + 1) * LANES]
        outs.append(jnp.concatenate([o_scr[grp - 1, c] for c in range(chunks)], axis=1))
        lses.append(l_scr[grp - 1])
    def mix(rows):
        l0, l1, l2 = (l[rows] for l in lses)
        m = jnp.maximum(jnp.maximum(l0, l1), l2)
        e0, e1, e2 = jnp.exp(l0 - m), jnp.exp(l1 - m), jnp.exp(l2 - m)
        inv = 1.0 / (e0 + e1 + e2)
        spread = ex_ref[...]
        yb = (_seg_sum(e0 * inv, spread) * outs[0][rows] + _seg_sum(e1 * inv, spread) * outs[1][rows]
              + _seg_sum(e2 * inv, spread) * outs[2][rows])
        x = x_ref[rows, :]
        gates = _sigmoid(_mm(_rms(x, gmix_ref[...]), wgt_ref[...]))
        merged = (gates[:, :D_MODEL] * _mm(ya_ref[rows, :], wa_ref[...])
                  + gates[:, D_MODEL:] * _mm(yb, wb_ref[...]))
        x1 = x + _mm(merged, wo_ref[...])
        h = _rms(x1, gain_ref[...])
        return (x1, h) + _route(_mm_split(h, wr_ref[...]))

    half = x_ref.shape[0] // 2
    parts = [mix(slice(0, half)), mix(slice(half, 2 * half))]
    x1, h, i1, i2, w1, w2 = (jnp.concatenate(both, axis=0) for both in zip(*parts))
    x1_ref[...] = x1
    h_ref[...] = h

    @pl.when(pl.program_id(0) == 0)
    def _():
        count_ref[...] = jnp.zeros_like(count_ref)

    lane = lax.broadcasted_iota(jnp.int32, (x1.shape[0], LANES), 1)
    onehot = jnp.where((lane == i1) | (lane == i2), 1.0, 0.0)
    before = count_ref[0:1, :] + jnp.dot(tri_ref[...], onehot.astype(BF16), preferred_element_type=F32)
    rank1 = jnp.sum(jnp.where(lane == i1, before, 0.0), axis=-1, keepdims=True)
    rank2 = jnp.sum(jnp.where(lane == i2, before, 0.0), axis=-1, keepdims=True)
    last = x1.shape[0] - 1
    count_ref[...] = jnp.broadcast_to(before[last:, :] + onehot[last:, :], count_ref.shape)
    cols = (i1.astype(F32), i2.astype(F32), w1, w2, rank1, rank2)
    info = jnp.zeros(lane.shape, F32)
    for c, col in enumerate(cols):
        info = jnp.where(lane == c, col, info)
    route_ref[...] = info


def _merge(x, ya, outs, lses, gain_mix, w_gates, wa, wb, wo, gain, wr, spread, seq, *, tm):
    t = x.shape[0]
    tps = seq // tm
    tok = lambda n: pl.BlockSpec((tm, n), lambda i: (i, 0))
    full = lambda a: pl.BlockSpec(a.shape, lambda i: (0, 0))
    res = lambda a: pl.BlockSpec((1, a.shape[1], tm // a.shape[1], a.shape[3]),
                                 lambda i: (i // tps, 0, i % tps, 0))
    dilated = len(ATTN_GROUPS) - 1
    tri = jnp.asarray(np.tri(tm, k=-1), dtype=BF16)
    return pl.pallas_call(
        _merge_kernel,
        grid=(t // tm,),
        in_specs=[tok(D_MODEL), tok(RWKV_DIM)] + [res(a) for a in outs] + [res(a) for a in lses] + [
                  full(gain_mix), full(w_gates), full(wa), full(wb), full(wo), full(gain), full(wr), full(spread),
                  full(tri)],
        out_specs=[tok(D_MODEL), tok(D_MODEL), tok(LANES), pl.BlockSpec((8, LANES), lambda i: (0, 0))],
        out_shape=[jax.ShapeDtypeStruct((t, D_MODEL), F32), jax.ShapeDtypeStruct((t, D_MODEL), F32),
                   jax.ShapeDtypeStruct((t, LANES), F32), jax.ShapeDtypeStruct((8, LANES), F32)],
        scratch_shapes=[pltpu.VMEM((dilated, ATTN_OUT_DIM // LANES, tm, LANES), F32),
                        pltpu.VMEM((dilated, tm, LANES), F32)],
        compiler_params=_params("arbitrary"),
        name="merge_route",
    )(x, ya, *outs, *lses, gain_mix, w_gates, wa, wb, wo, gain, wr, spread, tri)


ROUTE_E1, ROUTE_E2, ROUTE_W1, ROUTE_W2, ROUTE_R1, ROUTE_R2 = range(6)


def _moe_plan(counts, n_tiles):
    counts = counts[0, :N_EXPERTS].astype(jnp.int32)
    tile_end = jnp.cumsum((counts + MOE_TILE - 1) // MOE_TILE)
    n_valid = tile_end[-1]
    tile = jnp.minimum(jnp.arange(n_tiles, dtype=jnp.int32), n_valid - 1)
    tile_expert = jnp.sum(tile[:, None] >= tile_end[None, :], axis=1).astype(jnp.int32)
    return tile_expert, n_valid.reshape(1)


def _row_tiles(ref, rows, tile=ROW_TILE):
    return jnp.concatenate([ref[pl.ds(c, rows, stride=tile), :] for c in range(tile)], axis=1)


def _store_row_tiles(ref, x, tile=ROW_TILE):
    for c in range(tile):
        ref[pl.ds(c, x.shape[0], stride=tile), :] = x[:, c * LANES:(c + 1) * LANES]


def _row(ref, r, tile=ROW_TILE):
    return ref.at[pl.ds(pl.multiple_of(r * tile, tile), tile)]


def _dispatch_kernel(route_ref, count_ref, upper_ref, h_ref, xs_in_ref, xs_ref, pos_ref, rows_ref, pos_vmem,
                     pos_smem, sem, pos_sem):
    del xs_in_ref
    tm = h_ref.shape[0]
    tiles = jnp.floor((count_ref[...] + (MOE_TILE - 1)) * (1.0 / MOE_TILE))
    first_row = jnp.dot(tiles.astype(BF16), upper_ref[...], preferred_element_type=F32)[0:1, :] * MOE_TILE
    route = route_ref[...]
    lane = lax.broadcasted_iota(jnp.int32, route.shape, 1)
    col = lambda c: route[:, c:c + 1]
    place = lambda e, r: jnp.sum(jnp.where(lane == col(e).astype(jnp.int32), first_row, 0.0), axis=-1,
                                 keepdims=True) + col(r)
    both = jnp.where(lane == 0, place(ROUTE_E1, ROUTE_R1), jnp.where(lane == 1, place(ROUTE_E2, ROUTE_R2), 0.0))
    pos = both.T[0:ROW_TILE, :].astype(jnp.int32)
    pos_ref[0] = pos[0:2]
    pos_vmem[...] = pos
    to_smem = pltpu.make_async_copy(pos_vmem, pos_smem, pos_sem)
    to_smem.start()
    _store_row_tiles(rows_ref, h_ref[...])
    to_smem.wait()

    def copies(j):
        src = _row(rows_ref, j)
        return (pltpu.make_async_copy(src, _row(xs_ref, pos_smem[0, j]), sem.at[0]),
                pltpu.make_async_copy(src, _row(xs_ref, pos_smem[1, j]), sem.at[1]))

    def start(j, carry):
        for stream, cp in enumerate(copies(j)):
            cp.start(priority=stream)
        return carry

    lax.fori_loop(0, tm, start, 0, unroll=8)
    for stream in range(2):
        pltpu.make_async_copy(rows_ref, rows_ref, sem.at[stream]).wait()


def _dispatch(h, route, counts, blank, *, tm):
    t, d = h.shape
    upper = jnp.asarray(np.triu(np.ones((LANES, LANES)), k=1), dtype=BF16)
    tok = lambda n: pl.BlockSpec((tm, n), lambda i: (i, 0))
    full = lambda a: pl.BlockSpec(a.shape, lambda i: (0, 0))
    return pl.pallas_call(
        _dispatch_kernel,
        grid=(t // tm,),
        in_specs=[tok(LANES), full(counts), full(upper), tok(d), pl.BlockSpec(memory_space=pl.ANY)],
        out_specs=[pl.BlockSpec(memory_space=pl.ANY), pl.BlockSpec((1, 2, tm), lambda i: (i, 0, 0))],
        out_shape=[jax.ShapeDtypeStruct(blank.shape, blank.dtype),
                   jax.ShapeDtypeStruct((t // tm, 2, tm), jnp.int32)],
        scratch_shapes=[pltpu.VMEM((tm * ROW_TILE, LANES), F32), pltpu.VMEM((ROW_TILE, tm), jnp.int32),
                        pltpu.SMEM((ROW_TILE, tm), jnp.int32), pltpu.SemaphoreType.DMA((2,)),
                        pltpu.SemaphoreType.DMA],
        input_output_aliases={4: 0},
        compiler_params=_params("arbitrary"),
        name="moe_dispatch",
    )(route, counts, upper, h, blank)


def _expert_kernel(te_ref, nv_ref, x_ref, wg_ref, wu_ref, wd_ref, o_ref):
    live = pl.program_id(0) < nv_ref[0]

    @pl.when(live)
    def _():
        x = _row_tiles(x_ref, MOE_TILE)
        gate = _mm(x, wg_ref[0])
        up = _mm(x, wu_ref[0])
        _store_row_tiles(o_ref, _mm(gate * _sigmoid(gate) * up, wd_ref[0]))

    @pl.when(jnp.logical_not(live))
    def _():
        o_ref[...] = jnp.zeros_like(o_ref)


def _experts(xs, tile_expert, n_valid, wg, wu, wd):
    d = wg.shape[1]
    weight = lambda shape: pl.BlockSpec((1,) + shape, lambda i, te, nv: (te[i], 0, 0))
    blk = (MOE_TILE * ROW_TILE, LANES)
    return pl.pallas_call(
        _expert_kernel,
        grid_spec=pltpu.PrefetchScalarGridSpec(
            num_scalar_prefetch=2,
            grid=(xs.shape[0] // blk[0],),
            in_specs=[pl.BlockSpec(blk, lambda i, te, nv: (jnp.minimum(i, nv[0] - 1), 0)),
                      weight((d, D_FF_EXPERT)), weight((d, D_FF_EXPERT)), weight((D_FF_EXPERT, d))],
            out_specs=pl.BlockSpec(blk, lambda i, te, nv: (i, 0))),
        out_shape=jax.ShapeDtypeStruct(xs.shape, F32),
        compiler_params=_params("arbitrary"),
        name="moe_experts",
    )(tile_expert, n_valid, xs, wg, wu, wd)


def _ple_kernel(pos_ref, x_ref, route_ref, p_ref, gp_ref, wg_ref, wp_ref, gf_ref, ys_ref, o_ref,
                y_ref, sem):
    tm = x_ref.shape[0]
    step, steps = pl.program_id(0), pl.num_programs(0)

    def gather(tile, slot, inline):
        base = tile * (2 * tm)

        def body(j, carry):
            for stream in range(2):
                pltpu.make_async_copy(_row(ys_ref, pos_ref[base + stream * tm + j]),
                                      _row(y_ref.at[slot, stream], j), sem.at[slot, stream]
                                      ).start(priority=stream)
            return carry

        if inline:
            for j in range(tm):
                body(j, 0)
        else:
            lax.fori_loop(0, tm, body, 0, unroll=8)

    def drain(slot):
        for stream in range(2):
            buf = y_ref.at[slot, stream]
            pltpu.make_async_copy(buf, buf, sem.at[slot, stream]).wait()

    slot = step % 2

    @pl.when(step == 0)
    def _():
        gather(0, 0, False)

    drain(slot)
    gather(jnp.minimum(step + 1, steps - 1), 1 - slot, True)
    route = route_ref[...]
    w1, w2 = route[:, ROUTE_W1:ROUTE_W1 + 1], route[:, ROUTE_W2:ROUTE_W2 + 1]
    x = x_ref[...] + (w1 * _row_tiles(y_ref.at[slot, 0], tm) + w2 * _row_tiles(y_ref.at[slot, 1], tm))
    gate = _sigmoid(_mm(_rms(x, gp_ref[...]), wg_ref[...]))
    x = x + gate * _mm(p_ref[...], wp_ref[...])
    o_ref[...] = _rms(x, gf_ref[...])

    @pl.when(step == steps - 1)
    def _():
        drain(1 - slot)


def _ple(x, route, pos, ys, p, gain_ple, w_gate, w_proj, gain_final, *, tm):
    t = x.shape[0]
    tok = lambda n: pl.BlockSpec((tm, n), lambda i, ps: (i, 0))
    full = lambda a: pl.BlockSpec(a.shape, lambda i, ps: (0, 0))
    return pl.pallas_call(
        _ple_kernel,
        grid_spec=pltpu.PrefetchScalarGridSpec(
            num_scalar_prefetch=1,
            grid=(t // tm,),
            in_specs=[tok(D_MODEL), tok(LANES), tok(PLE_DIM), full(gain_ple), full(w_gate), full(w_proj),
                      full(gain_final), pl.BlockSpec(memory_space=pl.ANY)],
            out_specs=tok(D_MODEL),
            scratch_shapes=[pltpu.VMEM((2, 2, tm * ROW_TILE, LANES), F32), pltpu.SemaphoreType.DMA((2, 2))]),
        out_shape=jax.ShapeDtypeStruct((t, D_MODEL), F32),
        compiler_params=_params("arbitrary"),
        name="ple_final",
    )(pos, x, route, p, gain_ple, w_gate, w_proj, gain_final, ys)


def _pad_rows(w, start, total):
    return jnp.zeros((total, w.shape[1]), w.dtype).at[start:start + w.shape[0]].set(w)


def _layer(x, p, seq, norm_mix, w_in, rwkv_mu, rwkv_w0, rwkv_w2, rwkv_a0, rwkv_a2, rwkv_g2, rwkv_k_k,
           rwkv_k_a, rwkv_r_k, rwkv_ln_w, rwkv_ln_b, w_proj_a, w_proj_b, w_out, norm_moe, w_router_group,
           w_router_expert, w_exp_gate, w_exp_up, w_exp_down, norm_ple, w_ple_gate, w_ple_proj, norm_final):
    t = x.shape[0]
    bsz = t // seq
    row = lambda a: a.reshape(1, -1)
    c0, c1 = RWKV_COLS, RWKV_COLS + ATTN_COLS
    x3 = x.reshape(bsz, seq, D_MODEL)

    def w_group(grp):
        lo = lambda which: c0 + which * ATTN_DIM + grp * ATTN_OUT_DIM
        return jnp.concatenate([w_in[:, lo(which):lo(which) + ATTN_OUT_DIM] for which in range(3)],
                               axis=1).astype(BF16)

    qkv = [_qkv_proj(x3, row(norm_mix), w_group(grp), grp, tm=1024) for grp in range(len(ATTN_GROUPS))]

    seg = np.arange(RWKV_DIM) // HEAD_DIM
    seg_ones = jnp.asarray(seg[:, None] == seg[None, :], dtype=BF16)
    n_tiles = 2 * t // MOE_TILE + N_EXPERTS
    ya, blank = _rwkv_branch(
        x3, row(norm_mix), w_in[:, :c0].astype(BF16), row(rwkv_mu), row(rwkv_w0),
        _pad_rows(rwkv_w2, 0, LORA_COLS), row(rwkv_a0), _pad_rows(rwkv_a2, W_LORA, LORA_COLS),
        _pad_rows(rwkv_g2, W_LORA + A_LORA, LORA_COLS), row(rwkv_k_k), row(rwkv_k_a), row(rwkv_ln_w),
        row(rwkv_ln_b), row(rwkv_r_k), seg_ones, n_tiles * MOE_TILE * ROW_TILE, tb=256)
    ya = ya.reshape(t, RWKV_DIM)

    attn = [_attn_group(qkv[grp], grp) for grp in range(len(ATTN_GROUPS))]

    head_of_lane = np.arange(ATTN_OUT_DIM) // HEAD_DIM
    spread = jnp.asarray(np.arange(LANES)[:, None] == head_of_lane[None, :], dtype=BF16)
    w_route = jnp.concatenate(
        [jnp.moveaxis(w_router_expert, 0, 1).reshape(D_MODEL, N_EXPERTS), w_router_group,
         jnp.zeros((D_MODEL, LANES - N_EXPERTS - N_EXPERT_GROUPS), F32)], axis=1)
    x1, h_moe, route, counts = _merge(x, ya, [o for o, _ in attn], [l for _, l in attn],
                             row(norm_mix), w_in[:, c1:].astype(BF16),
                             w_proj_a.astype(BF16), w_proj_b.astype(BF16), w_out.astype(BF16),
                             row(norm_moe), w_route, spread, seq, tm=512)
    tile_expert, n_valid = _moe_plan(counts, n_tiles)
    dispatch_tm = 512
    xs, pos = _dispatch(h_moe, route, counts, blank, tm=dispatch_tm)
    ys = _experts(xs, tile_expert, n_valid,
                  w_exp_gate.reshape(N_EXPERTS, D_MODEL, D_FF_EXPERT),
                  w_exp_up.reshape(N_EXPERTS, D_MODEL, D_FF_EXPERT),
                  w_exp_down.reshape(N_EXPERTS, D_FF_EXPERT, D_MODEL))
    return _ple(x1, route, pos.reshape(-1), ys, p, row(norm_ple), w_ple_gate.astype(BF16),
                w_ple_proj.astype(BF16), row(norm_final), tm=dispatch_tm)


def kernel(x, p, norm_mix, w_in, rwkv_mu, rwkv_w0, rwkv_w2, rwkv_a0, rwkv_a2, rwkv_g2, rwkv_k_k, rwkv_k_a, rwkv_r_k, rwkv_ln_w, rwkv_ln_b, w_proj_a, w_proj_b, w_out, norm_moe, w_router_group, w_router_expert, w_exp_gate, w_exp_up, w_exp_down, norm_ple, w_ple_gate, w_ple_proj, norm_final):
    bsz, seq, d = x.shape
    depth = w_in.shape[0]
    assert depth == 1, "the final norm is fused into the (single) layer"
    out = _layer(x.reshape(bsz * seq, d).astype(F32), p[0].reshape(bsz * seq, PLE_DIM), seq,
                 norm_mix[0], w_in[0], rwkv_mu[0], rwkv_w0[0], rwkv_w2[0], rwkv_a0[0], rwkv_a2[0],
                 rwkv_g2[0], rwkv_k_k[0], rwkv_k_a[0], rwkv_r_k[0], rwkv_ln_w[0], rwkv_ln_b[0],
                 w_proj_a[0], w_proj_b[0], w_out[0], norm_moe[0], w_router_group[0], w_router_expert[0],
                 w_exp_gate[0], w_exp_up[0], w_exp_down[0], norm_ple[0], w_ple_gate[0], w_ple_proj[0],
                 norm_final)
    return out.reshape(bsz, seq, d)
```

```python
import functools

import numpy as np
import jax
import jax.numpy as jnp
from jax import lax
from jax.experimental import pallas as pl
from jax.experimental.pallas import tpu as pltpu

F32 = jnp.float32
BF16 = jnp.bfloat16
HIGHEST = lax.Precision.HIGHEST

D_MODEL = 1024
PLE_DIM = 256
HEAD_DIM = 64
RWKV_HEADS = 8
RWKV_DIM = RWKV_HEADS * HEAD_DIM
W_LORA, A_LORA, G_LORA = 64, 64, 128
LORA_COLS = W_LORA + A_LORA + G_LORA
ATTN_GROUPS = ((128, 1), (512, 4), (2048, 16))
HEADS_PER_GROUP = 8
ATTN_HEADS = HEADS_PER_GROUP * len(ATTN_GROUPS)
ATTN_DIM = ATTN_HEADS * HEAD_DIM
ATTN_OUT_DIM = HEADS_PER_GROUP * HEAD_DIM
Q_BLOCK = 128
ATTN_TQ = 512
RWKV_COLS = 3 * RWKV_DIM + LORA_COLS
ATTN_COLS = 3 * ATTN_DIM
N_EXPERT_GROUPS = 4
EXPERTS_PER_GROUP = 8
N_EXPERTS = N_EXPERT_GROUPS * EXPERTS_PER_GROUP
D_FF_EXPERT = 256
NORM_EPS = 1e-6
RWKV_GN_EPS = 64e-5
LANES = 128
MASKED = -1e30
WKV_CHUNK = 64
VMEM_LIMIT = 48 * 1024 * 1024
MOE_TILE = 512
ROW_TILE = 8


def _mm(a, b):
    return jnp.dot(a.astype(BF16), b.astype(BF16), preferred_element_type=F32)


def _mm_nt(a, b):
    return lax.dot_general(a.astype(BF16), b.astype(BF16), (((1,), (1,)), ((), ())),
                           preferred_element_type=F32)


def _mm_split(a, b):
    a_hi, b_hi = a.astype(BF16), b.astype(BF16)
    a_lo = (a - a_hi.astype(F32)).astype(BF16)
    b_lo = (b - b_hi.astype(F32)).astype(BF16)
    dot = functools.partial(jnp.dot, preferred_element_type=F32)
    return dot(a_hi, b_hi) + (dot(a_hi, b_lo) + dot(a_lo, b_hi))


def _sigmoid(x):
    return 1.0 / (1.0 + jnp.exp(-x))


def _rms(x, gain):
    return x * lax.rsqrt(jnp.mean(x * x, axis=-1, keepdims=True) + NORM_EPS) * gain


def _params(*sem):
    return pltpu.CompilerParams(dimension_semantics=sem, vmem_limit_bytes=VMEM_LIMIT)


def _qkv_rows_kernel(x_ref, g_ref, w_ref, o_ref, stage_ref, *, dil):
    h = _rms(x_ref[0], g_ref[...])
    if dil == 1:
        o_ref[0, 0] = _mm(h, w_ref[...]).astype(BF16)
        return
    chunks = h.shape[1] // LANES
    for c in range(chunks):
        stage_ref[c] = h[:, c * LANES:(c + 1) * LANES]
    n = h.shape[0] // dil
    for r in range(dil):
        rows = jnp.concatenate([stage_ref[c, pl.ds(r, n, stride=dil), :] for c in range(chunks)], axis=1)
        o_ref[0, r] = _mm(rows, w_ref[...]).astype(BF16)


def _qkv_strided_kernel(x_hbm, g_ref, w_ref, o_ref, x_buf, sem):
    tm = x_buf.shape[1]
    n_res, n_blk = pl.num_programs(1), pl.num_programs(2)
    step = (pl.program_id(0) * n_res + pl.program_id(1)) * n_blk + pl.program_id(2)
    steps = pl.num_programs(0) * n_res * n_blk

    def fetch(s, slot):
        b, r, i = s // (n_res * n_blk), (s // n_blk) % n_res, s % n_blk
        return pltpu.make_async_copy(x_hbm.at[b, pl.ds(i * tm, tm), r], x_buf.at[slot], sem.at[slot])

    slot = step % 2

    @pl.when(step == 0)
    def _():
        fetch(0, 0).start()

    @pl.when(step + 1 < steps)
    def _():
        fetch(step + 1, 1 - slot).start()

    fetch(step, slot).wait()
    o_ref[0, 0] = _mm(_rms(x_buf[slot], g_ref[...]), w_ref[...]).astype(BF16)


def _qkv_proj(x, gain, w, group, *, tm):
    bsz, seq, d = x.shape
    dil = ATTN_GROUPS[group][1]
    comp = seq // dil
    tn = w.shape[1]
    out_shape = jax.ShapeDtypeStruct((bsz, dil, comp, tn), BF16)
    if dil % ROW_TILE:
        return pl.pallas_call(
            functools.partial(_qkv_rows_kernel, dil=dil),
            grid=(bsz, seq // tm),
            in_specs=[pl.BlockSpec((1, tm, d), lambda b, i: (b, i, 0)),
                      pl.BlockSpec((1, d), lambda b, i: (0, 0)),
                      pl.BlockSpec((d, tn), lambda b, i: (0, 0))],
            out_specs=pl.BlockSpec((1, dil, tm // dil, tn), lambda b, i: (b, 0, i, 0)),
            out_shape=out_shape,
            scratch_shapes=[pltpu.VMEM((d // LANES, tm, LANES), F32)],
            compiler_params=_params("parallel", "parallel"),
            name=f"qkv_proj_{group}",
        )(x, gain, w)
    tm = min(tm, comp)
    return pl.pallas_call(
        _qkv_strided_kernel,
        grid=(bsz, dil, comp // tm),
        in_specs=[pl.BlockSpec(memory_space=pl.ANY),
                  pl.BlockSpec((1, d), lambda b, r, i: (0, 0)),
                  pl.BlockSpec((d, tn), lambda b, r, i: (0, 0))],
        out_specs=pl.BlockSpec((1, 1, tm, tn), lambda b, r, i: (b, r, i, 0)),
        out_shape=out_shape,
        scratch_shapes=[pltpu.VMEM((2, tm, d), F32), pltpu.SemaphoreType.DMA((2,))],
        compiler_params=_params("arbitrary", "arbitrary", "arbitrary"),
        name=f"qkv_proj_{group}",
    )(x.reshape(bsz, comp, dil, d), gain, w)


def _seg_sum(x, seg_ones):
    hi = x.astype(BF16)
    lo = (x - hi.astype(F32)).astype(BF16)
    return (jnp.dot(hi, seg_ones, preferred_element_type=F32)
            + jnp.dot(lo, seg_ones, preferred_element_type=F32))


def _rwkv_inputs(z, last, mu, w0, w2, a0, a2, g2, k_k, k_a, seg):
    row = lax.broadcasted_iota(jnp.int32, z.shape, 0)
    prev = jnp.where(row == 0, last, pltpu.roll(z, 1, axis=0))
    zs = z + (prev - z) * mu
    d = RWKV_DIM
    r, k, v = zs[:, :d], zs[:, d:2 * d], zs[:, 2 * d:3 * d]
    lora = zs[:, 3 * d:]
    lane = lax.broadcasted_iota(jnp.int32, lora.shape, 1)
    lora = jnp.where(lane < W_LORA, jnp.tanh(lora),
                     jnp.where(lane < W_LORA + A_LORA, lora, _sigmoid(lora)))
    w_lin = w0 + _mm_split(lora, w2)
    a_lin = a0 + _mm_split(lora, a2)
    g = _mm_split(lora, g2)
    u = -w_lin
    softplus = jnp.maximum(u, 0.0) + jnp.log(1.0 + jnp.exp(-jnp.abs(u)))
    lw = -jnp.exp(-softplus - 0.5)
    a = _sigmoid(a_lin)
    kk = k * k_k
    kk = kk * lax.rsqrt(_seg_sum(kk * kk, seg) + 1e-12)
    k = k * (1.0 + (a - 1.0) * k_a)
    return r, lw, k, v, kk, kk * a, g


def _bmm(a, b):
    return lax.dot_general(a.astype(BF16), b.astype(BF16), (((2,), (1,)), ((0,), (0,))),
                           preferred_element_type=F32)


def _bmm_nt(a, b):
    return lax.dot_general(a.astype(BF16), b.astype(BF16), (((2,), (2,)), ((0,), (0,))),
                           preferred_element_type=F32)


def _bmm_tn(a, b):
    return lax.dot_general(a.astype(BF16), b.astype(BF16), (((1,), (1,)), ((0,), (0,))),
                           preferred_element_type=F32)


def _rwkv_kernel(x_ref, gain_ref, wz_ref, mu_ref, w0_ref, w2_ref, a0_ref, a2_ref, g2_ref, kk_ref, ka_ref,
                 lnw_ref, lnb_ref, rk_ref, seg_ref, o_ref, blank_ref, s_ref, zlast_ref, *, chunk):
    @pl.when(pl.program_id(1) == 0)
    def _():
        s_ref[...] = jnp.zeros_like(s_ref)
        zlast_ref[...] = jnp.zeros_like(zlast_ref)

    tb = x_ref.shape[1]
    d = RWKV_DIM
    z = jnp.dot(_rms(x_ref[0], gain_ref[...]).astype(BF16), wz_ref[...], preferred_element_type=F32)
    last = zlast_ref[7:8, :]
    zlast_ref[...] = z[tb - 8:, :]
    r, lw, k, v, kk, b, g = _rwkv_inputs(z, last, mu_ref[...], w0_ref[...], w2_ref[...], a0_ref[...],
                                         a2_ref[...], g2_ref[...], kk_ref[...], ka_ref[...], seg_ref[...])
    nc, nh = tb // chunk, d // HEAD_DIM
    ii = lax.broadcasted_iota(jnp.int32, (chunk, chunk), 0)
    jj = lax.broadcasted_iota(jnp.int32, (chunk, chunk), 1)
    incl, strict = ii >= jj, ii > jj
    tri = incl.astype(F32)
    eye = (ii == jj).astype(F32)
    rows = lambda c: slice(c * chunk, (c + 1) * chunk)
    lanes = lambda h: slice(h * HEAD_DIM, (h + 1) * HEAD_DIM)

    def split(x):
        return jnp.stack([x[rows(c), lanes(h)] for c in range(nc) for h in range(nh)])

    cums = [jnp.dot(tri, lw[rows(c)], precision=HIGHEST, preferred_element_type=F32) for c in range(nc)]
    cum = jnp.concatenate(cums, axis=0)
    total = jnp.concatenate([jnp.broadcast_to(cs[chunk - 1:], (chunk, d)) for cs in cums], axis=0)
    g_inv = jnp.exp(-cum)
    g_tail = jnp.exp(total - cum)
    kq, rq = split(kk * jnp.exp(cum - lw)), split(r * jnp.exp(cum))
    bi, ki = split(b * g_inv), split(k * g_inv)
    bh, kh = split(b * g_tail), split(k * g_tail)
    vs = split(v)
    decay = jnp.stack([jnp.exp(cs[chunk - 1:, lanes(h)]) for cs in cums for h in range(nh)])

    qr = jnp.concatenate([kq, rq], axis=1)
    ab, ak = _bmm_nt(qr, bi), _bmm_nt(qr, ki)
    a_b, a_rb = jnp.where(strict, ab[:, :chunk], 0.0), jnp.where(incl, ab[:, chunk:], 0.0)
    a_k, a_rk = jnp.where(strict, ak[:, :chunk], 0.0), jnp.where(incl, ak[:, chunk:], 0.0)
    n = nc * nh
    is_b = lax.broadcasted_iota(jnp.int32, (chunk, 2 * chunk), 1) < chunk
    bt = jnp.concatenate([-a_b, jnp.broadcast_to(eye, (n, chunk, chunk))], axis=2)
    for _ in range(int(np.log2(chunk))):
        prod = _bmm(bt[:, :, :chunk], bt)
        bt = jnp.where(is_b, prod, bt + prod)
    inv = bt[:, :, chunk:]
    wu = _bmm(inv, jnp.concatenate([kq, -_bmm(a_k, vs)], axis=2))
    aw = _bmm(a_rb, wu)
    qh = rq - aw[:, :, :HEAD_DIM]
    y0 = aw[:, :, HEAD_DIM:] + _bmm(a_rk, vs)
    mu = _bmm_tn(wu, bh)
    mp = mu[:, :HEAD_DIM]
    s_add = mu[:, HEAD_DIM:] + _bmm_tn(vs, kh)

    state = s_ref[...]
    for c in range(nc):
        sl = slice(c * nh, (c + 1) * nh)
        y = _bmm_nt(qh[sl], state) + y0[sl]
        for h in range(nh):
            o_ref[0, rows(c), lanes(h)] = y[h]
        state = state * decay[sl] - _bmm(state, mp[sl]) + s_add[sl]
    s_ref[...] = state

    seg = seg_ref[...]
    y = o_ref[0]
    yc = y - _seg_sum(y, seg) * (1.0 / HEAD_DIM)
    var = _seg_sum(yc * yc, seg) * (1.0 / HEAD_DIM)
    y = yc * lax.rsqrt(var + RWKV_GN_EPS) * lnw_ref[...] + lnb_ref[...]
    y = y + _seg_sum(r * k * rk_ref[...], seg) * v
    o_ref[0] = y * g
    blank_ref[...] = jnp.zeros_like(blank_ref)


def _rwkv_branch(x, gain, wz, mu, w0, w2p, a0, a2p, g2p, k_k, k_a, ln_w, ln_b, r_k, seg_ones, blank_rows,
                 *, tb):
    bsz, seq, dm = x.shape
    d = RWKV_DIM
    steps = seq // tb
    full = lambda a: pl.BlockSpec(a.shape, lambda bi, t: (0,) * a.ndim)
    params = (gain, wz, mu, w0, w2p, a0, a2p, g2p, k_k, k_a, ln_w, ln_b, r_k, seg_ones)
    blank_blk = blank_rows // (bsz * steps)
    return pl.pallas_call(
        functools.partial(_rwkv_kernel, chunk=WKV_CHUNK),
        grid=(bsz, steps),
        in_specs=[pl.BlockSpec((1, tb, dm), lambda bi, t: (bi, t, 0))] + [full(a) for a in params],
        out_specs=[pl.BlockSpec((1, tb, d), lambda bi, t: (bi, t, 0)),
                   pl.BlockSpec((blank_blk, LANES), lambda bi, t: (bi * steps + t, 0))],
        out_shape=[jax.ShapeDtypeStruct((bsz, seq, d), F32),
                   jax.ShapeDtypeStruct((blank_rows, LANES), F32)],
        scratch_shapes=[pltpu.VMEM((d // HEAD_DIM, HEAD_DIM, HEAD_DIM), F32),
                        pltpu.VMEM((8, RWKV_COLS), F32)],
        compiler_params=_params("parallel", "arbitrary"),
        name="rwkv_branch",
    )(x, *params)


def _attn_kernel(q_ref, kp_ref, kc_ref, vp_ref, vc_ref, bias_ref, o_ref, l_ref):
    first = pl.program_id(2) == 0
    q = q_ref[0, 0] * (HEAD_DIM ** -0.5)
    kc, vc = kc_ref[0, 0], vc_ref[0, 0]
    subs = q.shape[0] // Q_BLOCK
    rows = lambda j: slice(j * Q_BLOCK, (j + 1) * Q_BLOCK)
    lanes = lambda h: slice(h * HEAD_DIM, (h + 1) * HEAD_DIM)
    k_prev = [kp_ref[0, 0]] + [kc[rows(j)] for j in range(subs - 1)]
    v_prev = [vp_ref[0, 0]] + [vc[rows(j)] for j in range(subs - 1)]
    work = [(j, h) for j in range(subs) for h in range(HEADS_PER_GROUP)]
    scores = []
    for j, h in work:
        bias = bias_ref[h]
        bias_prev = bias[:, :Q_BLOCK]
        if j == 0:
            bias_prev = jnp.where(first, MASKED, bias_prev)
        qh = q[rows(j), lanes(h)]
        scores.append((_mm_nt(qh, k_prev[j][:, lanes(h)]) + bias_prev,
                       _mm_nt(qh, kc[rows(j), lanes(h)]) + bias[:, Q_BLOCK:]))
    tops = [jnp.max(jnp.maximum(sp, sc), axis=-1, keepdims=True) for sp, sc in scores]
    probs = [(jnp.exp(sp - m), jnp.exp(sc - m)) for (sp, sc), m in zip(scores, tops)]
    dens = [jnp.sum(pp + pc, axis=-1, keepdims=True) for pp, pc in probs]
    accs = [_mm(pp, v_prev[j][:, lanes(h)]) + _mm(pc, vc[rows(j), lanes(h)])
            for (pp, pc), (j, h) in zip(probs, work)]
    lane = lax.broadcasted_iota(jnp.int32, (Q_BLOCK, LANES), 1)
    for j in range(subs):
        lse = jnp.zeros((Q_BLOCK, LANES), F32)
        for h in range(HEADS_PER_GROUP):
            i = j * HEADS_PER_GROUP + h
            o_ref[0, 0, rows(j), lanes(h)] = accs[i] * (1.0 / dens[i])
            lse = jnp.where(lane == h, tops[i] + jnp.log(dens[i]), lse)
        l_ref[0, 0, rows(j), :] = lse


def _attn_bias(group):
    window, dilation = ATTN_GROUPS[group]
    n_back = window // dilation
    heads = np.arange(group * HEADS_PER_GROUP, (group + 1) * HEADS_PER_GROUP)
    slopes = (2.0 ** (-8.0 * (heads + 1) / ATTN_HEADS)).astype(np.float32)
    delta = (np.arange(Q_BLOCK)[:, None] + Q_BLOCK) - np.arange(2 * Q_BLOCK)[None, :]
    valid = (delta >= 0) & (delta <= n_back)
    alibi = -slopes[:, None, None] * (dilation * delta).astype(np.float32)[None]
    return jnp.asarray(np.where(valid[None], alibi, np.float32(MASKED)).astype(np.float32))


def _attn_group(qkv, group):
    bsz, dil, comp, _ = qkv.shape
    tq = min(ATTN_TQ, comp)
    assert comp % tq == 0 and tq % Q_BLOCK == 0
    sub = tq // Q_BLOCK
    col = lambda which: (lambda bi, r, n: (bi, r, n, which))
    col_prev = lambda which: (lambda bi, r, n: (bi, r, jnp.maximum(n * sub - 1, 0), which))
    cur, prev = (1, 1, tq, ATTN_OUT_DIM), (1, 1, Q_BLOCK, ATTN_OUT_DIM)
    bias = _attn_bias(group)
    return pl.pallas_call(
        _attn_kernel,
        grid=(bsz, dil, comp // tq),
        in_specs=[pl.BlockSpec(cur, col(0)),
                  pl.BlockSpec(prev, col_prev(1)), pl.BlockSpec(cur, col(1)),
                  pl.BlockSpec(prev, col_prev(2)), pl.BlockSpec(cur, col(2)),
                  pl.BlockSpec(bias.shape, lambda bi, r, n: (0, 0, 0))],
        out_specs=[pl.BlockSpec(cur, col(0)),
                   pl.BlockSpec((1, 1, tq, LANES), col(0))],
        out_shape=[jax.ShapeDtypeStruct((bsz, dil, comp, ATTN_OUT_DIM), F32),
                   jax.ShapeDtypeStruct((bsz, dil, comp, LANES), F32)],
        compiler_params=_params("parallel", "parallel", "parallel"),
        name=f"dilated_attn_{group}",
    )(qkv, qkv, qkv, qkv, qkv, bias)


def _route(logits):
    lane = lax.broadcasted_iota(jnp.int32, logits.shape, 1)
    neg = -jnp.inf
    first_max = lambda vals, mx: jnp.min(jnp.where(vals == mx, lane, LANES), axis=-1, keepdims=True)
    is_grp = (lane >= N_EXPERTS) & (lane < N_EXPERTS + N_EXPERT_GROUPS)
    gl = jnp.where(is_grp, logits, neg)
    gmax = jnp.max(gl, axis=-1, keepdims=True)
    grp_w = 1.0 / jnp.sum(jnp.exp(gl - gmax), axis=-1, keepdims=True)
    grp = first_max(gl, gmax) - N_EXPERTS
    in_grp = (lane >= grp * EXPERTS_PER_GROUP) & (lane < (grp + 1) * EXPERTS_PER_GROUP)
    el = jnp.where(in_grp, logits, neg)
    v1 = jnp.max(el, axis=-1, keepdims=True)
    i1 = first_max(el, v1)
    el2 = jnp.where(lane == i1, neg, el)
    v2 = jnp.max(el2, axis=-1, keepdims=True)
    i2 = first_max(el2, v2)
    e2 = jnp.exp(v2 - v1)
    w1 = grp_w / (1.0 + e2)
    return i1, i2, w1, w1 * e2


def _merge_kernel(x_ref, ya_ref, o0_ref, o1_ref, o2_ref, l0_ref, l1_ref, l2_ref, gmix_ref, wgt_ref,
                  wa_ref, wb_ref, wo_ref, gain_ref, wr_ref, ex_ref, tri_ref, x1_ref, h_ref, route_ref,
                  count_ref, o_scr, l_scr):
    outs, lses = [], []
    for grp, (o_ref, l_ref) in enumerate(((o0_ref, l0_ref), (o1_ref, l1_ref), (o2_ref, l2_ref))):
        dil = o_ref.shape[1]
        if dil == 1:
            outs.append(o_ref[0, 0])
            lses.append(l_ref[0, 0])
            continue
        n = o_ref.shape[2]
        chunks = o_ref.shape[3] // LANES
        for r in range(dil):
            l_scr[grp - 1, pl.ds(r, n, stride=dil), :] = l_ref[0, r]
            for c in range(chunks):
                o_scr[grp - 1, c, pl.ds(r, n, stride=dil), :] = o_ref[0, r, :, c * LANES:(c + 1) * LANES]
        outs.append(jnp.concatenate([o_scr[grp - 1, c] for c in range(chunks)], axis=1))
        lses.append(l_scr[grp - 1])
    def mix(rows):
        l0, l1, l2 = (l[rows] for l in lses)
        m = jnp.maximum(jnp.maximum(l0, l1), l2)
        e0, e1, e2 = jnp.exp(l0 - m), jnp.exp(l1 - m), jnp.exp(l2 - m)
        inv = 1.0 / (e0 + e1 + e2)
        spread = ex_ref[...]
        yb = (_seg_sum(e0 * inv, spread) * outs[0][rows] + _seg_sum(e1 * inv, spread) * outs[1][rows]
              + _seg_sum(e2 * inv, spread) * outs[2][rows])
        x = x_ref[rows, :]
        gates = _sigmoid(_mm(_rms(x, gmix_ref[...]), wgt_ref[...]))
        merged = (gates[:, :D_MODEL] * _mm(ya_ref[rows, :], wa_ref[...])
                  + gates[:, D_MODEL:] * _mm(yb, wb_ref[...]))
        x1 = x + _mm(merged, wo_ref[...])
        h = _rms(x1, gain_ref[...])
        return (x1, h) + _route(_mm_split(h, wr_ref[...]))

    half = x_ref.shape[0] // 2
    parts = [mix(slice(0, half)), mix(slice(half, 2 * half))]
    x1, h, i1, i2, w1, w2 = (jnp.concatenate(both, axis=0) for both in zip(*parts))
    x1_ref[...] = x1
    h_ref[...] = h

    @pl.when(pl.program_id(0) == 0)
    def _():
        count_ref[...] = jnp.zeros_like(count_ref)

    lane = lax.broadcasted_iota(jnp.int32, (x1.shape[0], LANES), 1)
    onehot = jnp.where((lane == i1) | (lane == i2), 1.0, 0.0)
    before = count_ref[0:1, :] + jnp.dot(tri_ref[...], onehot.astype(BF16), preferred_element_type=F32)
    rank1 = jnp.sum(jnp.where(lane == i1, before, 0.0), axis=-1, keepdims=True)
    rank2 = jnp.sum(jnp.where(lane == i2, before, 0.0), axis=-1, keepdims=True)
    last = x1.shape[0] - 1
    count_ref[...] = jnp.broadcast_to(before[last:, :] + onehot[last:, :], count_ref.shape)
    cols = (i1.astype(F32), i2.astype(F32), w1, w2, rank1, rank2)
    info = jnp.zeros(lane.shape, F32)
    for c, col in enumerate(cols):
        info = jnp.where(lane == c, col, info)
    route_ref[...] = info


def _merge(x, ya, outs, lses, gain_mix, w_gates, wa, wb, wo, gain, wr, spread, seq, *, tm):
    t = x.shape[0]
    tps = seq // tm
    tok = lambda n: pl.BlockSpec((tm, n), lambda i: (i, 0))
    full = lambda a: pl.BlockSpec(a.shape, lambda i: (0, 0))
    res = lambda a: pl.BlockSpec((1, a.shape[1], tm // a.shape[1], a.shape[3]),
                                 lambda i: (i // tps, 0, i % tps, 0))
    dilated = len(ATTN_GROUPS) - 1
    tri = jnp.asarray(np.tri(tm, k=-1), dtype=BF16)
    return pl.pallas_call(
        _merge_kernel,
        grid=(t // tm,),
        in_specs=[tok(D_MODEL), tok(RWKV_DIM)] + [res(a) for a in outs] + [res(a) for a in lses] + [
                  full(gain_mix), full(w_gates), full(wa), full(wb), full(wo), full(gain), full(wr), full(spread),
                  full(tri)],
        out_specs=[tok(D_MODEL), tok(D_MODEL), tok(LANES), pl.BlockSpec((8, LANES), lambda i: (0, 0))],
        out_shape=[jax.ShapeDtypeStruct((t, D_MODEL), F32), jax.ShapeDtypeStruct((t, D_MODEL), F32),
                   jax.ShapeDtypeStruct((t, LANES), F32), jax.ShapeDtypeStruct((8, LANES), F32)],
        scratch_shapes=[pltpu.VMEM((dilated, ATTN_OUT_DIM // LANES, tm, LANES), F32),
                        pltpu.VMEM((dilated, tm, LANES), F32)],
        compiler_params=_params("arbitrary"),
        name="merge_route",
    )(x, ya, *outs, *lses, gain_mix, w_gates, wa, wb, wo, gain, wr, spread, tri)


ROUTE_E1, ROUTE_E2, ROUTE_W1, ROUTE_W2, ROUTE_R1, ROUTE_R2 = range(6)


def _moe_plan(counts, n_tiles):
    counts = counts[0, :N_EXPERTS].astype(jnp.int32)
    tile_end = jnp.cumsum((counts + MOE_TILE - 1) // MOE_TILE)
    n_valid = tile_end[-1]
    tile = jnp.minimum(jnp.arange(n_tiles, dtype=jnp.int32), n_valid - 1)
    tile_expert = jnp.sum(tile[:, None] >= tile_end[None, :], axis=1).astype(jnp.int32)
    return tile_expert, n_valid.reshape(1)


def _row_tiles(ref, rows, tile=ROW_TILE):
    return jnp.concatenate([ref[pl.ds(c, rows, stride=tile), :] for c in range(tile)], axis=1)


def _store_row_tiles(ref, x, tile=ROW_TILE):
    for c in range(tile):
        ref[pl.ds(c, x.shape[0], stride=tile), :] = x[:, c * LANES:(c + 1) * LANES]


def _row(ref, r, tile=ROW_TILE):
    return ref.at[pl.ds(pl.multiple_of(r * tile, tile), tile)]


def _dispatch_kernel(route_ref, count_ref, upper_ref, h_ref, xs_in_ref, xs_ref, pos_ref, rows_ref, pos_vmem,
                     pos_smem, sem, pos_sem):
    del xs_in_ref
    tm = h_ref.shape[0]
    tiles = jnp.floor((count_ref[...] + (MOE_TILE - 1)) * (1.0 / MOE_TILE))
    first_row = jnp.dot(tiles.astype(BF16), upper_ref[...], preferred_element_type=F32)[0:1, :] * MOE_TILE
    route = route_ref[...]
    lane = lax.broadcasted_iota(jnp.int32, route.shape, 1)
    col = lambda c: route[:, c:c + 1]
    place = lambda e, r: jnp.sum(jnp.where(lane == col(e).astype(jnp.int32), first_row, 0.0), axis=-1,
                                 keepdims=True) + col(r)
    both = jnp.where(lane == 0, place(ROUTE_E1, ROUTE_R1), jnp.where(lane == 1, place(ROUTE_E2, ROUTE_R2), 0.0))
    pos = both.T[0:ROW_TILE, :].astype(jnp.int32)
    pos_ref[0] = pos[0:2]
    pos_vmem[...] = pos
    to_smem = pltpu.make_async_copy(pos_vmem, pos_smem, pos_sem)
    to_smem.start()
    _store_row_tiles(rows_ref, h_ref[...])
    to_smem.wait()

    def copies(j):
        src = _row(rows_ref, j)
        return (pltpu.make_async_copy(src, _row(xs_ref, pos_smem[0, j]), sem.at[0]),
                pltpu.make_async_copy(src, _row(xs_ref, pos_smem[1, j]), sem.at[1]))

    def start(j, carry):
        for stream, cp in enumerate(copies(j)):
            cp.start(priority=stream)
        return carry

    lax.fori_loop(0, tm, start, 0, unroll=8)
    for stream in range(2):
        pltpu.make_async_copy(rows_ref, rows_ref, sem.at[stream]).wait()


def _dispatch(h, route, counts, blank, *, tm):
    t, d = h.shape
    upper = jnp.asarray(np.triu(np.ones((LANES, LANES)), k=1), dtype=BF16)
    tok = lambda n: pl.BlockSpec((tm, n), lambda i: (i, 0))
    full = lambda a: pl.BlockSpec(a.shape, lambda i: (0, 0))
    return pl.pallas_call(
        _dispatch_kernel,
        grid=(t // tm,),
        in_specs=[tok(LANES), full(counts), full(upper), tok(d), pl.BlockSpec(memory_space=pl.ANY)],
        out_specs=[pl.BlockSpec(memory_space=pl.ANY), pl.BlockSpec((1, 2, tm), lambda i: (i, 0, 0))],
        out_shape=[jax.ShapeDtypeStruct(blank.shape, blank.dtype),
                   jax.ShapeDtypeStruct((t // tm, 2, tm), jnp.int32)],
        scratch_shapes=[pltpu.VMEM((tm * ROW_TILE, LANES), F32), pltpu.VMEM((ROW_TILE, tm), jnp.int32),
                        pltpu.SMEM((ROW_TILE, tm), jnp.int32), pltpu.SemaphoreType.DMA((2,)),
                        pltpu.SemaphoreType.DMA],
        input_output_aliases={4: 0},
        compiler_params=_params("arbitrary"),
        name="moe_dispatch",
    )(route, counts, upper, h, blank)


def _expert_kernel(te_ref, nv_ref, x_ref, wg_ref, wu_ref, wd_ref, o_ref):
    live = pl.program_id(0) < nv_ref[0]

    @pl.when(live)
    def _():
        x = _row_tiles(x_ref, MOE_TILE)
        gate = _mm(x, wg_ref[0])
        up = _mm(x, wu_ref[0])
        _store_row_tiles(o_ref, _mm(gate * _sigmoid(gate) * up, wd_ref[0]))

    @pl.when(jnp.logical_not(live))
    def _():
        o_ref[...] = jnp.zeros_like(o_ref)


def _experts(xs, tile_expert, n_valid, wg, wu, wd):
    d = wg.shape[1]
    weight = lambda shape: pl.BlockSpec((1,) + shape, lambda i, te, nv: (te[i], 0, 0))
    blk = (MOE_TILE * ROW_TILE, LANES)
    return pl.pallas_call(
        _expert_kernel,
        grid_spec=pltpu.PrefetchScalarGridSpec(
            num_scalar_prefetch=2,
            grid=(xs.shape[0] // blk[0],),
            in_specs=[pl.BlockSpec(blk, lambda i, te, nv: (jnp.minimum(i, nv[0] - 1), 0)),
                      weight((d, D_FF_EXPERT)), weight((d, D_FF_EXPERT)), weight((D_FF_EXPERT, d))],
            out_specs=pl.BlockSpec(blk, lambda i, te, nv: (i, 0))),
        out_shape=jax.ShapeDtypeStruct(xs.shape, F32),
        compiler_params=_params("arbitrary"),
        name="moe_experts",
    )(tile_expert, n_valid, xs, wg, wu, wd)


def _ple_kernel(pos_ref, x_ref, route_ref, p_ref, gp_ref, wg_ref, wp_ref, gf_ref, ys_ref, o_ref,
                y_ref, sem):
    tm = x_ref.shape[0]
    step, steps = pl.program_id(0), pl.num_programs(0)

    def gather(tile, slot, inline):
        base = tile * (2 * tm)

        def body(j, carry):
            for stream in range(2):
                pltpu.make_async_copy(_row(ys_ref, pos_ref[base + stream * tm + j]),
                                      _row(y_ref.at[slot, stream], j), sem.at[slot, stream]
                                      ).start(priority=stream)
            return carry

        if inline:
            for j in range(tm):
                body(j, 0)
        else:
            lax.fori_loop(0, tm, body, 0, unroll=8)

    def drain(slot):
        for stream in range(2):
            buf = y_ref.at[slot, stream]
            pltpu.make_async_copy(buf, buf, sem.at[slot, stream]).wait()

    slot = step % 2

    @pl.when(step == 0)
    def _():
        gather(0, 0, False)

    drain(slot)
    gather(jnp.minimum(step + 1, steps - 1), 1 - slot, True)
    route = route_ref[...]
    w1, w2 = route[:, ROUTE_W1:ROUTE_W1 + 1], route[:, ROUTE_W2:ROUTE_W2 + 1]
    x = x_ref[...] + (w1 * _row_tiles(y_ref.at[slot, 0], tm) + w2 * _row_tiles(y_ref.at[slot, 1], tm))
    gate = _sigmoid(_mm(_rms(x, gp_ref[...]), wg_ref[...]))
    x = x + gate * _mm(p_ref[...], wp_ref[...])
    o_ref[...] = _rms(x, gf_ref[...])

    @pl.when(step == steps - 1)
    def _():
        drain(1 - slot)


def _ple(x, route, pos, ys, p, gain_ple, w_gate, w_proj, gain_final, *, tm):
    t = x.shape[0]
    tok = lambda n: pl.BlockSpec((tm, n), lambda i, ps: (i, 0))
    full = lambda a: pl.BlockSpec(a.shape, lambda i, ps: (0, 0))
    return pl.pallas_call(
        _ple_kernel,
        grid_spec=pltpu.PrefetchScalarGridSpec(
            num_scalar_prefetch=1,
            grid=(t // tm,),
            in_specs=[tok(D_MODEL), tok(LANES), tok(PLE_DIM), full(gain_ple), full(w_gate), full(w_proj),
                      full(gain_final), pl.BlockSpec(memory_space=pl.ANY)],
            out_specs=tok(D_MODEL),
            scratch_shapes=[pltpu.VMEM((2, 2, tm * ROW_TILE, LANES), F32), pltpu.SemaphoreType.DMA((2, 2))]),
        out_shape=jax.ShapeDtypeStruct((t, D_MODEL), F32),
        compiler_params=_params("arbitrary"),
        name="ple_final",
    )(pos, x, route, p, gain_ple, w_gate, w_proj, gain_final, ys)


def _pad_rows(w, start, total):
    return jnp.zeros((total, w.shape[1]), w.dtype).at[start:start + w.shape[0]].set(w)


def _layer(x, p, seq, norm_mix, w_in, rwkv_mu, rwkv_w0, rwkv_w2, rwkv_a0, rwkv_a2, rwkv_g2, rwkv_k_k,
           rwkv_k_a, rwkv_r_k, rwkv_ln_w, rwkv_ln_b, w_proj_a, w_proj_b, w_out, norm_moe, w_router_group,
           w_router_expert, w_exp_gate, w_exp_up, w_exp_down, norm_ple, w_ple_gate, w_ple_proj, norm_final):
    t = x.shape[0]
    bsz = t // seq
    row = lambda a: a.reshape(1, -1)
    c0, c1 = RWKV_COLS, RWKV_COLS + ATTN_COLS
    x3 = x.reshape(bsz, seq, D_MODEL)

    def w_group(grp):
        lo = lambda which: c0 + which * ATTN_DIM + grp * ATTN_OUT_DIM
        return jnp.concatenate([w_in[:, lo(which):lo(which) + ATTN_OUT_DIM] for which in range(3)],
                               axis=1).astype(BF16)

    qkv = [_qkv_proj(x3, row(norm_mix), w_group(grp), grp, tm=1024) for grp in range(len(ATTN_GROUPS))]

    seg = np.arange(RWKV_DIM) // HEAD_DIM
    seg_ones = jnp.asarray(seg[:, None] == seg[None, :], dtype=BF16)
    n_tiles = 2 * t // MOE_TILE + N_EXPERTS
    ya, blank = _rwkv_branch(
        x3, row(norm_mix), w_in[:, :c0].astype(BF16), row(rwkv_mu), row(rwkv_w0),
        _pad_rows(rwkv_w2, 0, LORA_COLS), row(rwkv_a0), _pad_rows(rwkv_a2, W_LORA, LORA_COLS),
        _pad_rows(rwkv_g2, W_LORA + A_LORA, LORA_COLS), row(rwkv_k_k), row(rwkv_k_a), row(rwkv_ln_w),
        row(rwkv_ln_b), row(rwkv_r_k), seg_ones, n_tiles * MOE_TILE * ROW_TILE, tb=256)
    ya = ya.reshape(t, RWKV_DIM)

    attn = [_attn_group(qkv[grp], grp) for grp in range(len(ATTN_GROUPS))]

    head_of_lane = np.arange(ATTN_OUT_DIM) // HEAD_DIM
    spread = jnp.asarray(np.arange(LANES)[:, None] == head_of_lane[None, :], dtype=BF16)
    w_route = jnp.concatenate(
        [jnp.moveaxis(w_router_expert, 0, 1).reshape(D_MODEL, N_EXPERTS), w_router_group,
         jnp.zeros((D_MODEL, LANES - N_EXPERTS - N_EXPERT_GROUPS), F32)], axis=1)
    x1, h_moe, route, counts = _merge(x, ya, [o for o, _ in attn], [l for _, l in attn],
                             row(norm_mix), w_in[:, c1:].astype(BF16),
                             w_proj_a.astype(BF16), w_proj_b.astype(BF16), w_out.astype(BF16),
                             row(norm_moe), w_route, spread, seq, tm=512)
    tile_expert, n_valid = _moe_plan(counts, n_tiles)
    dispatch_tm = 1024
    xs, pos = _dispatch(h_moe, route, counts, blank, tm=dispatch_tm)
    ys = _experts(xs, tile_expert, n_valid,
                  w_exp_gate.reshape(N_EXPERTS, D_MODEL, D_FF_EXPERT),
                  w_exp_up.reshape(N_EXPERTS, D_MODEL, D_FF_EXPERT),
                  w_exp_down.reshape(N_EXPERTS, D_FF_EXPERT, D_MODEL))
    return _ple(x1, route, pos.reshape(-1), ys, p, row(norm_ple), w_ple_gate.astype(BF16),
                w_ple_proj.astype(BF16), row(norm_final), tm=dispatch_tm)


def kernel(x, p, norm_mix, w_in, rwkv_mu, rwkv_w0, rwkv_w2, rwkv_a0, rwkv_a2, rwkv_g2, rwkv_k_k, rwkv_k_a, rwkv_r_k, rwkv_ln_w, rwkv_ln_b, w_proj_a, w_proj_b, w_out, norm_moe, w_router_group, w_router_expert, w_exp_gate, w_exp_up, w_exp_down, norm_ple, w_ple_gate, w_ple_proj, norm_final):
    bsz, seq, d = x.shape
    depth = w_in.shape[0]
    assert depth == 1, "the final norm is fused into the (single) layer"
    out = _layer(x.reshape(bsz * seq, d).astype(F32), p[0].reshape(bsz * seq, PLE_DIM), seq,
                 norm_mix[0], w_in[0], rwkv_mu[0], rwkv_w0[0], rwkv_w2[0], rwkv_a0[0], rwkv_a2[0],
                 rwkv_g2[0], rwkv_k_k[0], rwkv_k_a[0], rwkv_r_k[0], rwkv_ln_w[0], rwkv_ln_b[0],
                 w_proj_a[0], w_proj_b[0], w_out[0], norm_moe[0], w_router_group[0], w_router_expert[0],
                 w_exp_gate[0], w_exp_up[0], w_exp_down[0], norm_ple[0], w_ple_gate[0], w_ple_proj[0],
                 norm_final)
    return out.reshape(bsz, seq, d)
```

```python
import functools

import numpy as np
import jax
import jax.numpy as jnp
from jax import lax
from jax.experimental import pallas as pl
from jax.experimental.pallas import tpu as pltpu

F32 = jnp.float32
BF16 = jnp.bfloat16
HIGHEST = lax.Precision.HIGHEST

D_MODEL = 1024
PLE_DIM = 256
HEAD_DIM = 64
RWKV_HEADS = 8
RWKV_DIM = RWKV_HEADS * HEAD_DIM
W_LORA, A_LORA, G_LORA = 64, 64, 128
LORA_COLS = W_LORA + A_LORA + G_LORA
ATTN_GROUPS = ((128, 1), (512, 4), (2048, 16))
HEADS_PER_GROUP = 8
ATTN_HEADS = HEADS_PER_GROUP * len(ATTN_GROUPS)
ATTN_DIM = ATTN_HEADS * HEAD_DIM
ATTN_OUT_DIM = HEADS_PER_GROUP * HEAD_DIM
Q_BLOCK = 128
ATTN_TQ = 512
RWKV_COLS = 3 * RWKV_DIM + LORA_COLS
ATTN_COLS = 3 * ATTN_DIM
N_EXPERT_GROUPS = 4
EXPERTS_PER_GROUP = 8
N_EXPERTS = N_EXPERT_GROUPS * EXPERTS_PER_GROUP
D_FF_EXPERT = 256
NORM_EPS = 1e-6
RWKV_GN_EPS = 64e-5
LANES = 128
MASKED = -1e30
WKV_CHUNK = 64
VMEM_LIMIT = 48 * 1024 * 1024
MOE_TILE = 512
ROW_TILE = 8


def _mm(a, b):
    return jnp.dot(a.astype(BF16), b.astype(BF16), preferred_element_type=F32)


def _mm_nt(a, b):
    return lax.dot_general(a.astype(BF16), b.astype(BF16), (((1,), (1,)), ((), ())),
                           preferred_element_type=F32)


def _mm_split(a, b):
    a_hi, b_hi = a.astype(BF16), b.astype(BF16)
    a_lo = (a - a_hi.astype(F32)).astype(BF16)
    b_lo = (b - b_hi.astype(F32)).astype(BF16)
    dot = functools.partial(jnp.dot, preferred_element_type=F32)
    return dot(a_hi, b_hi) + (dot(a_hi, b_lo) + dot(a_lo, b_hi))


def _sigmoid(x):
    return 1.0 / (1.0 + jnp.exp(-x))


def _rms(x, gain):
    return x * lax.rsqrt(jnp.mean(x * x, axis=-1, keepdims=True) + NORM_EPS) * gain


def _params(*sem):
    return pltpu.CompilerParams(dimension_semantics=sem, vmem_limit_bytes=VMEM_LIMIT)


def _qkv_rows_kernel(x_ref, g_ref, w_ref, o_ref, stage_ref, *, dil):
    h = _rms(x_ref[0], g_ref[...])
    if dil == 1:
        o_ref[0, 0] = _mm(h, w_ref[...]).astype(BF16)
        return
    chunks = h.shape[1] // LANES
    for c in range(chunks):
        stage_ref[c] = h[:, c * LANES:(c + 1) * LANES]
    n = h.shape[0] // dil
    for r in range(dil):
        rows = jnp.concatenate([stage_ref[c, pl.ds(r, n, stride=dil), :] for c in range(chunks)], axis=1)
        o_ref[0, r] = _mm(rows, w_ref[...]).astype(BF16)


def _qkv_strided_kernel(x_hbm, g_ref, w_ref, o_ref, x_buf, sem):
    tm = x_buf.shape[1]
    n_res, n_blk = pl.num_programs(1), pl.num_programs(2)
    step = (pl.program_id(0) * n_res + pl.program_id(1)) * n_blk + pl.program_id(2)
    steps = pl.num_programs(0) * n_res * n_blk

    def fetch(s, slot):
        b, r, i = s // (n_res * n_blk), (s // n_blk) % n_res, s % n_blk
        return pltpu.make_async_copy(x_hbm.at[b, pl.ds(i * tm, tm), r], x_buf.at[slot], sem.at[slot])

    slot = step % 2

    @pl.when(step == 0)
    def _():
        fetch(0, 0).start()

    @pl.when(step + 1 < steps)
    def _():
        fetch(step + 1, 1 - slot).start()

    fetch(step, slot).wait()
    o_ref[0, 0] = _mm(_rms(x_buf[slot], g_ref[...]), w_ref[...]).astype(BF16)


def _qkv_proj(x, gain, w, group, *, tm):
    bsz, seq, d = x.shape
    dil = ATTN_GROUPS[group][1]
    comp = seq // dil
    tn = w.shape[1]
    out_shape = jax.ShapeDtypeStruct((bsz, dil, comp, tn), BF16)
    if dil % ROW_TILE:
        return pl.pallas_call(
            functools.partial(_qkv_rows_kernel, dil=dil),
            grid=(bsz, seq // tm),
            in_specs=[pl.BlockSpec((1, tm, d), lambda b, i: (b, i, 0)),
                      pl.BlockSpec((1, d), lambda b, i: (0, 0)),
                      pl.BlockSpec((d, tn), lambda b, i: (0, 0))],
            out_specs=pl.BlockSpec((1, dil, tm // dil, tn), lambda b, i: (b, 0, i, 0)),
            out_shape=out_shape,
            scratch_shapes=[pltpu.VMEM((d // LANES, tm, LANES), F32)],
            compiler_params=_params("parallel", "parallel"),
            name=f"qkv_proj_{group}",
        )(x, gain, w)
    tm = min(tm, comp)
    return pl.pallas_call(
        _qkv_strided_kernel,
        grid=(bsz, dil, comp // tm),
        in_specs=[pl.BlockSpec(memory_space=pl.ANY),
                  pl.BlockSpec((1, d), lambda b, r, i: (0, 0)),
                  pl.BlockSpec((d, tn), lambda b, r, i: (0, 0))],
        out_specs=pl.BlockSpec((1, 1, tm, tn), lambda b, r, i: (b, r, i, 0)),
        out_shape=out_shape,
        scratch_shapes=[pltpu.VMEM((2, tm, d), F32), pltpu.SemaphoreType.DMA((2,))],
        compiler_params=_params("arbitrary", "arbitrary", "arbitrary"),
        name=f"qkv_proj_{group}",
    )(x.reshape(bsz, comp, dil, d), gain, w)


def _seg_sum(x, seg_ones):
    hi = x.astype(BF16)
    lo = (x - hi.astype(F32)).astype(BF16)
    return (jnp.dot(hi, seg_ones, preferred_element_type=F32)
            + jnp.dot(lo, seg_ones, preferred_element_type=F32))


def _rwkv_inputs(z, last, mu, w0, w2, a0, a2, g2, k_k, k_a, seg):
    row = lax.broadcasted_iota(jnp.int32, z.shape, 0)
    prev = jnp.where(row == 0, last, pltpu.roll(z, 1, axis=0))
    zs = z + (prev - z) * mu
    d = RWKV_DIM
    r, k, v = zs[:, :d], zs[:, d:2 * d], zs[:, 2 * d:3 * d]
    lora = zs[:, 3 * d:]
    lane = lax.broadcasted_iota(jnp.int32, lora.shape, 1)
    lora = jnp.where(lane < W_LORA, jnp.tanh(lora),
                     jnp.where(lane < W_LORA + A_LORA, lora, _sigmoid(lora)))
    w_lin = w0 + _mm_split(lora, w2)
    a_lin = a0 + _mm_split(lora, a2)
    g = _mm_split(lora, g2)
    u = -w_lin
    softplus = jnp.maximum(u, 0.0) + jnp.log(1.0 + jnp.exp(-jnp.abs(u)))
    lw = -jnp.exp(-softplus - 0.5)
    a = _sigmoid(a_lin)
    kk = k * k_k
    kk = kk * lax.rsqrt(_seg_sum(kk * kk, seg) + 1e-12)
    k = k * (1.0 + (a - 1.0) * k_a)
    return r, lw, k, v, kk, kk * a, g


def _bmm(a, b):
    return lax.dot_general(a.astype(BF16), b.astype(BF16), (((2,), (1,)), ((0,), (0,))),
                           preferred_element_type=F32)


def _bmm_nt(a, b):
    return lax.dot_general(a.astype(BF16), b.astype(BF16), (((2,), (2,)), ((0,), (0,))),
                           preferred_element_type=F32)


def _bmm_tn(a, b):
    return lax.dot_general(a.astype(BF16), b.astype(BF16), (((1,), (1,)), ((0,), (0,))),
                           preferred_element_type=F32)


def _rwkv_kernel(x_ref, gain_ref, wz_ref, mu_ref, w0_ref, w2_ref, a0_ref, a2_ref, g2_ref, kk_ref, ka_ref,
                 lnw_ref, lnb_ref, rk_ref, seg_ref, o_ref, blank_ref, s_ref, zlast_ref, *, chunk):
    @pl.when(pl.program_id(1) == 0)
    def _():
        s_ref[...] = jnp.zeros_like(s_ref)
        zlast_ref[...] = jnp.zeros_like(zlast_ref)

    tb = x_ref.shape[1]
    d = RWKV_DIM
    z = jnp.dot(_rms(x_ref[0], gain_ref[...]).astype(BF16), wz_ref[...], preferred_element_type=F32)
    last = zlast_ref[7:8, :]
    zlast_ref[...] = z[tb - 8:, :]
    r, lw, k, v, kk, b, g = _rwkv_inputs(z, last, mu_ref[...], w0_ref[...], w2_ref[...], a0_ref[...],
                                         a2_ref[...], g2_ref[...], kk_ref[...], ka_ref[...], seg_ref[...])
    nc, nh = tb // chunk, d // HEAD_DIM
    ii = lax.broadcasted_iota(jnp.int32, (chunk, chunk), 0)
    jj = lax.broadcasted_iota(jnp.int32, (chunk, chunk), 1)
    incl, strict = ii >= jj, ii > jj
    tri = incl.astype(F32)
    eye = (ii == jj).astype(F32)
    rows = lambda c: slice(c * chunk, (c + 1) * chunk)
    lanes = lambda h: slice(h * HEAD_DIM, (h + 1) * HEAD_DIM)

    def split(x):
        return jnp.stack([x[rows(c), lanes(h)] for c in range(nc) for h in range(nh)])

    cums = [jnp.dot(tri, lw[rows(c)], precision=HIGHEST, preferred_element_type=F32) for c in range(nc)]
    cum = jnp.concatenate(cums, axis=0)
    total = jnp.concatenate([jnp.broadcast_to(cs[chunk - 1:], (chunk, d)) for cs in cums], axis=0)
    g_inv = jnp.exp(-cum)
    g_tail = jnp.exp(total - cum)
    kq, rq = split(kk * jnp.exp(cum - lw)), split(r * jnp.exp(cum))
    bi, ki = split(b * g_inv), split(k * g_inv)
    bh, kh = split(b * g_tail), split(k * g_tail)
    vs = split(v)
    decay = jnp.stack([jnp.exp(cs[chunk - 1:, lanes(h)]) for cs in cums for h in range(nh)])

    qr = jnp.concatenate([kq, rq], axis=1)
    ab, ak = _bmm_nt(qr, bi), _bmm_nt(qr, ki)
    a_b, a_rb = jnp.where(strict, ab[:, :chunk], 0.0), jnp.where(incl, ab[:, chunk:], 0.0)
    a_k, a_rk = jnp.where(strict, ak[:, :chunk], 0.0), jnp.where(incl, ak[:, chunk:], 0.0)
    n = nc * nh
    is_b = lax.broadcasted_iota(jnp.int32, (chunk, 2 * chunk), 1) < chunk
    bt = jnp.concatenate([-a_b, jnp.broadcast_to(eye, (n, chunk, chunk))], axis=2)
    for _ in range(int(np.log2(chunk))):
        prod = _bmm(bt[:, :, :chunk], bt)
        bt = jnp.where(is_b, prod, bt + prod)
    inv = bt[:, :, chunk:]
    wu = _bmm(inv, jnp.concatenate([kq, -_bmm(a_k, vs)], axis=2))
    aw = _bmm(a_rb, wu)
    qh = rq - aw[:, :, :HEAD_DIM]
    y0 = aw[:, :, HEAD_DIM:] + _bmm(a_rk, vs)
    mu = _bmm_tn(wu, bh)
    mp = mu[:, :HEAD_DIM]
    s_add = mu[:, HEAD_DIM:] + _bmm_tn(vs, kh)

    state = s_ref[...]
    for c in range(nc):
        sl = slice(c * nh, (c + 1) * nh)
        y = _bmm_nt(qh[sl], state) + y0[sl]
        for h in range(nh):
            o_ref[0, rows(c), lanes(h)] = y[h]
        state = state * decay[sl] - _bmm(state, mp[sl]) + s_add[sl]
    s_ref[...] = state

    seg = seg_ref[...]
    y = o_ref[0]
    yc = y - _seg_sum(y, seg) * (1.0 / HEAD_DIM)
    var = _seg_sum(yc * yc, seg) * (1.0 / HEAD_DIM)
    y = yc * lax.rsqrt(var + RWKV_GN_EPS) * lnw_ref[...] + lnb_ref[...]
    y = y + _seg_sum(r * k * rk_ref[...], seg) * v
    o_ref[0] = y * g
    blank_ref[...] = jnp.zeros_like(blank_ref)


def _rwkv_branch(x, gain, wz, mu, w0, w2p, a0, a2p, g2p, k_k, k_a, ln_w, ln_b, r_k, seg_ones, blank_rows,
                 *, tb):
    bsz, seq, dm = x.shape
    d = RWKV_DIM
    steps = seq // tb
    full = lambda a: pl.BlockSpec(a.shape, lambda bi, t: (0,) * a.ndim)
    params = (gain, wz, mu, w0, w2p, a0, a2p, g2p, k_k, k_a, ln_w, ln_b, r_k, seg_ones)
    blank_blk = blank_rows // (bsz * steps)
    return pl.pallas_call(
        functools.partial(_rwkv_kernel, chunk=WKV_CHUNK),
        grid=(bsz, steps),
        in_specs=[pl.BlockSpec((1, tb, dm), lambda bi, t: (bi, t, 0))] + [full(a) for a in params],
        out_specs=[pl.BlockSpec((1, tb, d), lambda bi, t: (bi, t, 0)),
                   pl.BlockSpec((blank_blk, LANES), lambda bi, t: (bi * steps + t, 0))],
        out_shape=[jax.ShapeDtypeStruct((bsz, seq, d), F32),
                   jax.ShapeDtypeStruct((blank_rows, LANES), F32)],
        scratch_shapes=[pltpu.VMEM((d // HEAD_DIM, HEAD_DIM, HEAD_DIM), F32),
                        pltpu.VMEM((8, RWKV_COLS), F32)],
        compiler_params=_params("parallel", "arbitrary"),
        name="rwkv_branch",
    )(x, *params)


def _attn_kernel(q_ref, kp_ref, kc_ref, vp_ref, vc_ref, bias_ref, o_ref, l_ref):
    first = pl.program_id(2) == 0
    q = q_ref[0, 0] * (HEAD_DIM ** -0.5)
    kc, vc = kc_ref[0, 0], vc_ref[0, 0]
    subs = q.shape[0] // Q_BLOCK
    rows = lambda j: slice(j * Q_BLOCK, (j + 1) * Q_BLOCK)
    lanes = lambda h: slice(h * HEAD_DIM, (h + 1) * HEAD_DIM)
    k_prev = [kp_ref[0, 0]] + [kc[rows(j)] for j in range(subs - 1)]
    v_prev = [vp_ref[0, 0]] + [vc[rows(j)] for j in range(subs - 1)]
    work = [(j, h) for j in range(subs) for h in range(HEADS_PER_GROUP)]
    scores = []
    for j, h in work:
        bias = bias_ref[h]
        bias_prev = bias[:, :Q_BLOCK]
        if j == 0:
            bias_prev = jnp.where(first, MASKED, bias_prev)
        qh = q[rows(j), lanes(h)]
        scores.append((_mm_nt(qh, k_prev[j][:, lanes(h)]) + bias_prev,
                       _mm_nt(qh, kc[rows(j), lanes(h)]) + bias[:, Q_BLOCK:]))
    tops = [jnp.max(jnp.maximum(sp, sc), axis=-1, keepdims=True) for sp, sc in scores]
    probs = [(jnp.exp(sp - m), jnp.exp(sc - m)) for (sp, sc), m in zip(scores, tops)]
    dens = [jnp.sum(pp + pc, axis=-1, keepdims=True) for pp, pc in probs]
    accs = [_mm(pp, v_prev[j][:, lanes(h)]) + _mm(pc, vc[rows(j), lanes(h)])
            for (pp, pc), (j, h) in zip(probs, work)]
    lane = lax.broadcasted_iota(jnp.int32, (Q_BLOCK, LANES), 1)
    for j in range(subs):
        lse = jnp.zeros((Q_BLOCK, LANES), F32)
        for h in range(HEADS_PER_GROUP):
            i = j * HEADS_PER_GROUP + h
            o_ref[0, 0, rows(j), lanes(h)] = accs[i] * (1.0 / dens[i])
            lse = jnp.where(lane == h, tops[i] + jnp.log(dens[i]), lse)
        l_ref[0, 0, rows(j), :] = lse


def _attn_bias(group):
    window, dilation = ATTN_GROUPS[group]
    n_back = window // dilation
    heads = np.arange(group * HEADS_PER_GROUP, (group + 1) * HEADS_PER_GROUP)
    slopes = (2.0 ** (-8.0 * (heads + 1) / ATTN_HEADS)).astype(np.float32)
    delta = (np.arange(Q_BLOCK)[:, None] + Q_BLOCK) - np.arange(2 * Q_BLOCK)[None, :]
    valid = (delta >= 0) & (delta <= n_back)
    alibi = -slopes[:, None, None] * (dilation * delta).astype(np.float32)[None]
    return jnp.asarray(np.where(valid[None], alibi, np.float32(MASKED)).astype(np.float32))


def _attn_group(qkv, group):
    bsz, dil, comp, _ = qkv.shape
    tq = min(ATTN_TQ, comp)
    assert comp % tq == 0 and tq % Q_BLOCK == 0
    sub = tq // Q_BLOCK
    col = lambda which: (lambda bi, r, n: (bi, r, n, which))
    col_prev = lambda which: (lambda bi, r, n: (bi, r, jnp.maximum(n * sub - 1, 0), which))
    cur, prev = (1, 1, tq, ATTN_OUT_DIM), (1, 1, Q_BLOCK, ATTN_OUT_DIM)
    bias = _attn_bias(group)
    return pl.pallas_call(
        _attn_kernel,
        grid=(bsz, dil, comp // tq),
        in_specs=[pl.BlockSpec(cur, col(0)),
                  pl.BlockSpec(prev, col_prev(1)), pl.BlockSpec(cur, col(1)),
                  pl.BlockSpec(prev, col_prev(2)), pl.BlockSpec(cur, col(2)),
                  pl.BlockSpec(bias.shape, lambda bi, r, n: (0, 0, 0))],
        out_specs=[pl.BlockSpec(cur, col(0)),
                   pl.BlockSpec((1, 1, tq, LANES), col(0))],
        out_shape=[jax.ShapeDtypeStruct((bsz, dil, comp, ATTN_OUT_DIM), F32),
                   jax.ShapeDtypeStruct((bsz, dil, comp, LANES), F32)],
        compiler_params=_params("parallel", "parallel", "parallel"),
        name=f"dilated_attn_{group}",
    )(qkv, qkv, qkv, qkv, qkv, bias)


def _route(logits):
    lane = lax.broadcasted_iota(jnp.int32, logits.shape, 1)
    neg = -jnp.inf
    first_max = lambda vals, mx: jnp.min(jnp.where(vals == mx, lane, LANES), axis=-1, keepdims=True)
    is_grp = (lane >= N_EXPERTS) & (lane < N_EXPERTS + N_EXPERT_GROUPS)
    gl = jnp.where(is_grp, logits, neg)
    gmax = jnp.max(gl, axis=-1, keepdims=True)
    grp_w = 1.0 / jnp.sum(jnp.exp(gl - gmax), axis=-1, keepdims=True)
    grp = first_max(gl, gmax) - N_EXPERTS
    in_grp = (lane >= grp * EXPERTS_PER_GROUP) & (lane < (grp + 1) * EXPERTS_PER_GROUP)
    el = jnp.where(in_grp, logits, neg)
    v1 = jnp.max(el, axis=-1, keepdims=True)
    i1 = first_max(el, v1)
    el2 = jnp.where(lane == i1, neg, el)
    v2 = jnp.max(el2, axis=-1, keepdims=True)
    i2 = first_max(el2, v2)
    e2 = jnp.exp(v2 - v1)
    w1 = grp_w / (1.0 + e2)
    return i1, i2, w1, w1 * e2


def _merge_kernel(x_ref, ya_ref, o0_ref, o1_ref, o2_ref, l0_ref, l1_ref, l2_ref, gmix_ref, wgt_ref,
                  wa_ref, wb_ref, wo_ref, gain_ref, wr_ref, ex_ref, tri_ref, x1_ref, h_ref, route_ref,
                  count_ref, o_scr, l_scr):
    outs, lses = [], []
    for grp, (o_ref, l_ref) in enumerate(((o0_ref, l0_ref), (o1_ref, l1_ref), (o2_ref, l2_ref))):
        dil = o_ref.shape[1]
        if dil == 1:
            outs.append(o_ref[0, 0])
            lses.append(l_ref[0, 0])
            continue
        n = o_ref.shape[2]
        chunks = o_ref.shape[3] // LANES
        for r in range(dil):
            l_scr[grp - 1, pl.ds(r, n, stride=dil), :] = l_ref[0, r]
            for c in range(chunks):
                o_scr[grp - 1, c, pl.ds(r, n, stride=dil), :] = o_ref[0, r, :, c * LANES:(c + 1) * LANES]
        outs.append(jnp.concatenate([o_scr[grp - 1, c] for c in range(chunks)], axis=1))
        lses.append(l_scr[grp - 1])
    def mix(rows):
        l0, l1, l2 = (l[rows] for l in lses)
        m = jnp.maximum(jnp.maximum(l0, l1), l2)
        e0, e1, e2 = jnp.exp(l0 - m), jnp.exp(l1 - m), jnp.exp(l2 - m)
        inv = 1.0 / (e0 + e1 + e2)
        spread = ex_ref[...]
        yb = (_seg_sum(e0 * inv, spread) * outs[0][rows] + _seg_sum(e1 * inv, spread) * outs[1][rows]
              + _seg_sum(e2 * inv, spread) * outs[2][rows])
        x = x_ref[rows, :]
        gates = _sigmoid(_mm(_rms(x, gmix_ref[...]), wgt_ref[...]))
        merged = (gates[:, :D_MODEL] * _mm(ya_ref[rows, :], wa_ref[...])
                  + gates[:, D_MODEL:] * _mm(yb, wb_ref[...]))
        x1 = x + _mm(merged, wo_ref[...])
        h = _rms(x1, gain_ref[...])
        return (x1, h) + _route(_mm_split(h, wr_ref[...]))

    half = x_ref.shape[0] // 2
    parts = [mix(slice(0, half)), mix(slice(half, 2 * half))]
    x1, h, i1, i2, w1, w2 = (jnp.concatenate(both, axis=0) for both in zip(*parts))
    x1_ref[...] = x1
    h_ref[...] = h

    @pl.when(pl.program_id(0) == 0)
    def _():
        count_ref[...] = jnp.zeros_like(count_ref)

    lane = lax.broadcasted_iota(jnp.int32, (x1.shape[0], LANES), 1)
    onehot = jnp.where((lane == i1) | (lane == i2), 1.0, 0.0)
    before = count_ref[0:1, :] + jnp.dot(tri_ref[...], onehot.astype(BF16), preferred_element_type=F32)
    rank1 = jnp.sum(jnp.where(lane == i1, before, 0.0), axis=-1, keepdims=True)
    rank2 = jnp.sum(jnp.where(lane == i2, before, 0.0), axis=-1, keepdims=True)
    last = x1.shape[0] - 1
    count_ref[...] = jnp.broadcast_to(before[last:, :] + onehot[last:, :], count_ref.shape)
    cols = (i1.astype(F32), i2.astype(F32), w1, w2, rank1, rank2)
    info = jnp.zeros(lane.shape, F32)
    for c, col in enumerate(cols):
        info = jnp.where(lane == c, col, info)
    route_ref[...] = info


def _merge(x, ya, outs, lses, gain_mix, w_gates, wa, wb, wo, gain, wr, spread, seq, *, tm):
    t = x.shape[0]
    tps = seq // tm
    tok = lambda n: pl.BlockSpec((tm, n), lambda i: (i, 0))
    full = lambda a: pl.BlockSpec(a.shape, lambda i: (0, 0))
    res = lambda a: pl.BlockSpec((1, a.shape[1], tm // a.shape[1], a.shape[3]),
                                 lambda i: (i // tps, 0, i % tps, 0))
    dilated = len(ATTN_GROUPS) - 1
    tri = jnp.asarray(np.tri(tm, k=-1), dtype=BF16)
    return pl.pallas_call(
        _merge_kernel,
        grid=(t // tm,),
        in_specs=[tok(D_MODEL), tok(RWKV_DIM)] + [res(a) for a in outs] + [res(a) for a in lses] + [
                  full(gain_mix), full(w_gates), full(wa), full(wb), full(wo), full(gain), full(wr), full(spread),
                  full(tri)],
        out_specs=[tok(D_MODEL), tok(D_MODEL), tok(LANES), pl.BlockSpec((8, LANES), lambda i: (0, 0))],
        out_shape=[jax.ShapeDtypeStruct((t, D_MODEL), F32), jax.ShapeDtypeStruct((t, D_MODEL), F32),
                   jax.ShapeDtypeStruct((t, LANES), F32), jax.ShapeDtypeStruct((8, LANES), F32)],
        scratch_shapes=[pltpu.VMEM((dilated, ATTN_OUT_DIM // LANES, tm, LANES), F32),
                        pltpu.VMEM((dilated, tm, LANES), F32)],
        compiler_params=_params("arbitrary"),
        name="merge_route",
    )(x, ya, *outs, *lses, gain_mix, w_gates, wa, wb, wo, gain, wr, spread, tri)


ROUTE_E1, ROUTE_E2, ROUTE_W1, ROUTE_W2, ROUTE_R1, ROUTE_R2 = range(6)


def _moe_plan(counts, n_tiles):
    counts = counts[0, :N_EXPERTS].astype(jnp.int32)
    tile_end = jnp.cumsum((counts + MOE_TILE - 1) // MOE_TILE)
    n_valid = tile_end[-1]
    tile = jnp.minimum(jnp.arange(n_tiles, dtype=jnp.int32), n_valid - 1)
    tile_expert = jnp.sum(tile[:, None] >= tile_end[None, :], axis=1).astype(jnp.int32)
    return tile_expert, n_valid.reshape(1)


def _row_tiles(ref, rows, tile=ROW_TILE):
    return jnp.concatenate([ref[pl.ds(c, rows, stride=tile), :] for c in range(tile)], axis=1)


def _store_row_tiles(ref, x, tile=ROW_TILE):
    for c in range(tile):
        ref[pl.ds(c, x.shape[0], stride=tile), :] = x[:, c * LANES:(c + 1) * LANES]


def _row(ref, r, tile=ROW_TILE):
    return ref.at[pl.ds(pl.multiple_of(r * tile, tile), tile)]


def _dispatch_kernel(route_ref, count_ref, upper_ref, h_ref, xs_in_ref, xs_ref, pos_ref, rows_ref, pos_vmem,
                     pos_smem, sem, pos_sem):
    del xs_in_ref
    tm = h_ref.shape[0]
    tiles = jnp.floor((count_ref[...] + (MOE_TILE - 1)) * (1.0 / MOE_TILE))
    first_row = jnp.dot(tiles.astype(BF16), upper_ref[...], preferred_element_type=F32)[0:1, :] * MOE_TILE
    route = route_ref[...]
    lane = lax.broadcasted_iota(jnp.int32, route.shape, 1)
    col = lambda c: route[:, c:c + 1]
    place = lambda e, r: jnp.sum(jnp.where(lane == col(e).astype(jnp.int32), first_row, 0.0), axis=-1,
                                 keepdims=True) + col(r)
    both = jnp.where(lane == 0, place(ROUTE_E1, ROUTE_R1), jnp.where(lane == 1, place(ROUTE_E2, ROUTE_R2), 0.0))
    pos = both.T[0:ROW_TILE, :].astype(jnp.int32)
    pos_ref[0] = pos[0:2]
    pos_vmem[...] = pos
    to_smem = pltpu.make_async_copy(pos_vmem, pos_smem, pos_sem)
    to_smem.start()
    _store_row_tiles(rows_ref, h_ref[...])
    to_smem.wait()

    def copies(j):
        src = _row(rows_ref, j)
        return (pltpu.make_async_copy(src, _row(xs_ref, pos_smem[0, j]), sem.at[0]),
                pltpu.make_async_copy(src, _row(xs_ref, pos_smem[1, j]), sem.at[1]))

    def start(j, carry):
        for stream, cp in enumerate(copies(j)):
            cp.start(priority=stream)
        return carry

    lax.fori_loop(0, tm, start, 0, unroll=8)
    for stream in range(2):
        pltpu.make_async_copy(rows_ref, rows_ref, sem.at[stream]).wait()


def _dispatch(h, route, counts, blank, *, tm):
    t, d = h.shape
    upper = jnp.asarray(np.triu(np.ones((LANES, LANES)), k=1), dtype=BF16)
    tok = lambda n: pl.BlockSpec((tm, n), lambda i: (i, 0))
    full = lambda a: pl.BlockSpec(a.shape, lambda i: (0, 0))
    return pl.pallas_call(
        _dispatch_kernel,
        grid=(t // tm,),
        in_specs=[tok(LANES), full(counts), full(upper), tok(d), pl.BlockSpec(memory_space=pl.ANY)],
        out_specs=[pl.BlockSpec(memory_space=pl.ANY), pl.BlockSpec((1, 2, tm), lambda i: (i, 0, 0))],
        out_shape=[jax.ShapeDtypeStruct(blank.shape, blank.dtype),
                   jax.ShapeDtypeStruct((t // tm, 2, tm), jnp.int32)],
        scratch_shapes=[pltpu.VMEM((tm * ROW_TILE, LANES), F32), pltpu.VMEM((ROW_TILE, tm), jnp.int32),
                        pltpu.SMEM((ROW_TILE, tm), jnp.int32), pltpu.SemaphoreType.DMA((2,)),
                        pltpu.SemaphoreType.DMA],
        input_output_aliases={4: 0},
        compiler_params=_params("arbitrary"),
        name="moe_dispatch",
    )(route, counts, upper, h, blank)


def _expert_kernel(te_ref, nv_ref, x_ref, wg_ref, wu_ref, wd_ref, o_ref):
    live = pl.program_id(0) < nv_ref[0]

    @pl.when(live)
    def _():
        x = _row_tiles(x_ref, MOE_TILE)
        gate = _mm(x, wg_ref[0])
        up = _mm(x, wu_ref[0])
        _store_row_tiles(o_ref, _mm(gate * _sigmoid(gate) * up, wd_ref[0]))

    @pl.when(jnp.logical_not(live))
    def _():
        o_ref[...] = jnp.zeros_like(o_ref)


def _experts(xs, tile_expert, n_valid, wg, wu, wd):
    d = wg.shape[1]
    weight = lambda shape: pl.BlockSpec((1,) + shape, lambda i, te, nv: (te[i], 0, 0))
    blk = (MOE_TILE * ROW_TILE, LANES)
    return pl.pallas_call(
        _expert_kernel,
        grid_spec=pltpu.PrefetchScalarGridSpec(
            num_scalar_prefetch=2,
            grid=(xs.shape[0] // blk[0],),
            in_specs=[pl.BlockSpec(blk, lambda i, te, nv: (jnp.minimum(i, nv[0] - 1), 0)),
                      weight((d, D_FF_EXPERT)), weight((d, D_FF_EXPERT)), weight((D_FF_EXPERT, d))],
            out_specs=pl.BlockSpec(blk, lambda i, te, nv: (i, 0))),
        out_shape=jax.ShapeDtypeStruct(xs.shape, F32),
        compiler_params=_params("arbitrary"),
        name="moe_experts",
    )(tile_expert, n_valid, xs, wg, wu, wd)


def _ple_kernel(pos_ref, x_ref, route_ref, p_ref, gp_ref, wg_ref, wp_ref, gf_ref, ys_ref, o_ref,
                y_ref, sem, *, pos_tm):
    tm = x_ref.shape[0]
    step, steps = pl.program_id(0), pl.num_programs(0)
    per_pos_tile = pos_tm // tm

    def gather(tile, slot, inline):
        base = (tile // per_pos_tile) * (2 * pos_tm) + (tile % per_pos_tile) * tm

        def body(j, carry):
            for stream in range(2):
                pltpu.make_async_copy(_row(ys_ref, pos_ref[base + stream * pos_tm + j]),
                                      _row(y_ref.at[slot, stream], j), sem.at[slot, stream]
                                      ).start(priority=stream)
            return carry

        if inline:
            for j in range(tm):
                body(j, 0)
        else:
            lax.fori_loop(0, tm, body, 0, unroll=8)

    def drain(slot):
        for stream in range(2):
            buf = y_ref.at[slot, stream]
            pltpu.make_async_copy(buf, buf, sem.at[slot, stream]).wait()

    slot = step % 2

    @pl.when(step == 0)
    def _():
        gather(0, 0, False)

    drain(slot)
    gather(jnp.minimum(step + 1, steps - 1), 1 - slot, True)
    route = route_ref[...]
    w1, w2 = route[:, ROUTE_W1:ROUTE_W1 + 1], route[:, ROUTE_W2:ROUTE_W2 + 1]
    x = x_ref[...] + (w1 * _row_tiles(y_ref.at[slot, 0], tm) + w2 * _row_tiles(y_ref.at[slot, 1], tm))
    gate = _sigmoid(_mm(_rms(x, gp_ref[...]), wg_ref[...]))
    x = x + gate * _mm(p_ref[...], wp_ref[...])
    o_ref[...] = _rms(x, gf_ref[...])

    @pl.when(step == steps - 1)
    def _():
        drain(1 - slot)


def _ple(x, route, pos, ys, p, gain_ple, w_gate, w_proj, gain_final, *, tm):
    t = x.shape[0]
    pos_tm = pos.shape[2]
    assert pos_tm % tm == 0
    tok = lambda n: pl.BlockSpec((tm, n), lambda i, ps: (i, 0))
    full = lambda a: pl.BlockSpec(a.shape, lambda i, ps: (0, 0))
    return pl.pallas_call(
        functools.partial(_ple_kernel, pos_tm=pos_tm),
        grid_spec=pltpu.PrefetchScalarGridSpec(
            num_scalar_prefetch=1,
            grid=(t // tm,),
            in_specs=[tok(D_MODEL), tok(LANES), tok(PLE_DIM), full(gain_ple), full(w_gate), full(w_proj),
                      full(gain_final), pl.BlockSpec(memory_space=pl.ANY)],
            out_specs=tok(D_MODEL),
            scratch_shapes=[pltpu.VMEM((2, 2, tm * ROW_TILE, LANES), F32), pltpu.SemaphoreType.DMA((2, 2))]),
        out_shape=jax.ShapeDtypeStruct((t, D_MODEL), F32),
        compiler_params=_params("arbitrary"),
        name="ple_final",
    )(pos.reshape(-1), x, route, p, gain_ple, w_gate, w_proj, gain_final, ys)


def _pad_rows(w, start, total):
    return jnp.zeros((total, w.shape[1]), w.dtype).at[start:start + w.shape[0]].set(w)


def _layer(x, p, seq, norm_mix, w_in, rwkv_mu, rwkv_w0, rwkv_w2, rwkv_a0, rwkv_a2, rwkv_g2, rwkv_k_k,
           rwkv_k_a, rwkv_r_k, rwkv_ln_w, rwkv_ln_b, w_proj_a, w_proj_b, w_out, norm_moe, w_router_group,
           w_router_expert, w_exp_gate, w_exp_up, w_exp_down, norm_ple, w_ple_gate, w_ple_proj, norm_final):
    t = x.shape[0]
    bsz = t // seq
    row = lambda a: a.reshape(1, -1)
    c0, c1 = RWKV_COLS, RWKV_COLS + ATTN_COLS
    x3 = x.reshape(bsz, seq, D_MODEL)

    def w_group(grp):
        lo = lambda which: c0 + which * ATTN_DIM + grp * ATTN_OUT_DIM
        return jnp.concatenate([w_in[:, lo(which):lo(which) + ATTN_OUT_DIM] for which in range(3)],
                               axis=1).astype(BF16)

    qkv = [_qkv_proj(x3, row(norm_mix), w_group(grp), grp, tm=1024) for grp in range(len(ATTN_GROUPS))]

    seg = np.arange(RWKV_DIM) // HEAD_DIM
    seg_ones = jnp.asarray(seg[:, None] == seg[None, :], dtype=BF16)
    n_tiles = 2 * t // MOE_TILE + N_EXPERTS
    ya, blank = _rwkv_branch(
        x3, row(norm_mix), w_in[:, :c0].astype(BF16), row(rwkv_mu), row(rwkv_w0),
        _pad_rows(rwkv_w2, 0, LORA_COLS), row(rwkv_a0), _pad_rows(rwkv_a2, W_LORA, LORA_COLS),
        _pad_rows(rwkv_g2, W_LORA + A_LORA, LORA_COLS), row(rwkv_k_k), row(rwkv_k_a), row(rwkv_ln_w),
        row(rwkv_ln_b), row(rwkv_r_k), seg_ones, n_tiles * MOE_TILE * ROW_TILE, tb=256)
    ya = ya.reshape(t, RWKV_DIM)

    attn = [_attn_group(qkv[grp], grp) for grp in range(len(ATTN_GROUPS))]

    head_of_lane = np.arange(ATTN_OUT_DIM) // HEAD_DIM
    spread = jnp.asarray(np.arange(LANES)[:, None] == head_of_lane[None, :], dtype=BF16)
    w_route = jnp.concatenate(
        [jnp.moveaxis(w_router_expert, 0, 1).reshape(D_MODEL, N_EXPERTS), w_router_group,
         jnp.zeros((D_MODEL, LANES - N_EXPERTS - N_EXPERT_GROUPS), F32)], axis=1)
    x1, h_moe, route, counts = _merge(x, ya, [o for o, _ in attn], [l for _, l in attn],
                             row(norm_mix), w_in[:, c1:].astype(BF16),
                             w_proj_a.astype(BF16), w_proj_b.astype(BF16), w_out.astype(BF16),
                             row(norm_moe), w_route, spread, seq, tm=512)
    tile_expert, n_valid = _moe_plan(counts, n_tiles)
    dispatch_tm = 2048
    xs, pos = _dispatch(h_moe, route, counts, blank, tm=dispatch_tm)
    ys = _experts(xs, tile_expert, n_valid,
                  w_exp_gate.reshape(N_EXPERTS, D_MODEL, D_FF_EXPERT),
                  w_exp_up.reshape(N_EXPERTS, D_MODEL, D_FF_EXPERT),
                  w_exp_down.reshape(N_EXPERTS, D_FF_EXPERT, D_MODEL))
    return _ple(x1, route, pos, ys, p, row(norm_ple), w_ple_gate.astype(BF16),
                w_ple_proj.astype(BF16), row(norm_final), tm=512)


def kernel(x, p, norm_mix, w_in, rwkv_mu, rwkv_w0, rwkv_w2, rwkv_a0, rwkv_a2, rwkv_g2, rwkv_k_k, rwkv_k_a, rwkv_r_k, rwkv_ln_w, rwkv_ln_b, w_proj_a, w_proj_b, w_out, norm_moe, w_router_group, w_router_expert, w_exp_gate, w_exp_up, w_exp_down, norm_ple, w_ple_gate, w_ple_proj, norm_final):
    bsz, seq, d = x.shape
    depth = w_in.shape[0]
    assert depth == 1, "the final norm is fused into the (single) layer"
    out = _layer(x.reshape(bsz * seq, d).astype(F32), p[0].reshape(bsz * seq, PLE_DIM), seq,
                 norm_mix[0], w_in[0], rwkv_mu[0], rwkv_w0[0], rwkv_w2[0], rwkv_a0[0], rwkv_a2[0],
                 rwkv_g2[0], rwkv_k_k[0], rwkv_k_a[0], rwkv_r_k[0], rwkv_ln_w[0], rwkv_ln_b[0],
                 w_proj_a[0], w_proj_b[0], w_out[0], norm_moe[0], w_router_group[0], w_router_expert[0],
                 w_exp_gate[0], w_exp_up[0], w_exp_down[0], norm_ple[0], w_ple_gate[0], w_ple_proj[0],
                 norm_final)
    return out.reshape(bsz, seq, d)
```

```python
import functools

import numpy as np
import jax
import jax.numpy as jnp
from jax import lax
from jax.experimental import pallas as pl
from jax.experimental.pallas import tpu as pltpu

F32 = jnp.float32
BF16 = jnp.bfloat16

D_MODEL = 1024
PLE_DIM = 256
HEAD_DIM = 64
RWKV_HEADS = 8
RWKV_DIM = RWKV_HEADS * HEAD_DIM
W_LORA, A_LORA, G_LORA = 64, 64, 128
LORA_COLS = W_LORA + A_LORA + G_LORA
ATTN_GROUPS = ((128, 1), (512, 4), (2048, 16))
HEADS_PER_GROUP = 8
ATTN_HEADS = HEADS_PER_GROUP * len(ATTN_GROUPS)
ATTN_DIM = ATTN_HEADS * HEAD_DIM
ATTN_OUT_DIM = HEADS_PER_GROUP * HEAD_DIM
Q_BLOCK = 128
ATTN_TQ = 512
RWKV_COLS = 3 * RWKV_DIM + LORA_COLS
ATTN_COLS = 3 * ATTN_DIM
N_EXPERT_GROUPS = 4
EXPERTS_PER_GROUP = 8
N_EXPERTS = N_EXPERT_GROUPS * EXPERTS_PER_GROUP
D_FF_EXPERT = 256
NORM_EPS = 1e-6
RWKV_GN_EPS = 64e-5
LANES = 128
MASKED = -1e30
WKV_CHUNK = 64
VMEM_LIMIT = 48 * 1024 * 1024
MOE_TILE = 512
ROW_TILE = 8


def _mm(a, b):
    return jnp.dot(a.astype(BF16), b.astype(BF16), preferred_element_type=F32)


def _mm_nt(a, b):
    return lax.dot_general(a.astype(BF16), b.astype(BF16), (((1,), (1,)), ((), ())),
                           preferred_element_type=F32)


def _mm_split(a, b):
    a_hi, b_hi = a.astype(BF16), b.astype(BF16)
    a_lo = (a - a_hi.astype(F32)).astype(BF16)
    b_lo = (b - b_hi.astype(F32)).astype(BF16)
    dot = functools.partial(jnp.dot, preferred_element_type=F32)
    return dot(a_hi, b_hi) + (dot(a_hi, b_lo) + dot(a_lo, b_hi))


def _sigmoid(x):
    return 1.0 / (1.0 + jnp.exp(-x))


def _rms(x, gain):
    return x * lax.rsqrt(jnp.mean(x * x, axis=-1, keepdims=True) + NORM_EPS) * gain


def _params(*sem):
    return pltpu.CompilerParams(dimension_semantics=sem, vmem_limit_bytes=VMEM_LIMIT)


def _qkv_rows_kernel(x_ref, g_ref, w_ref, o_ref, stage_ref, *, dil):
    h = _rms(x_ref[0], g_ref[...])
    if dil == 1:
        o_ref[0, 0] = _mm(h, w_ref[...]).astype(BF16)
        return
    chunks = h.shape[1] // LANES
    for c in range(chunks):
        stage_ref[c] = h[:, c * LANES:(c + 1) * LANES]
    n = h.shape[0] // dil
    for r in range(dil):
        rows = jnp.concatenate([stage_ref[c, pl.ds(r, n, stride=dil), :] for c in range(chunks)], axis=1)
        o_ref[0, r] = _mm(rows, w_ref[...]).astype(BF16)


def _qkv_strided_kernel(x_hbm, g_ref, w_ref, o_ref, x_buf, sem):
    tm = x_buf.shape[1]
    n_res, n_blk = pl.num_programs(1), pl.num_programs(2)
    step = (pl.program_id(0) * n_res + pl.program_id(1)) * n_blk + pl.program_id(2)
    steps = pl.num_programs(0) * n_res * n_blk

    def fetch(s, slot):
        b, r, i = s // (n_res * n_blk), (s // n_blk) % n_res, s % n_blk
        return pltpu.make_async_copy(x_hbm.at[b, pl.ds(i * tm, tm), r], x_buf.at[slot], sem.at[slot])

    slot = step % 2

    @pl.when(step == 0)
    def _():
        fetch(0, 0).start()

    @pl.when(step + 1 < steps)
    def _():
        fetch(step + 1, 1 - slot).start()

    fetch(step, slot).wait()
    o_ref[0, 0] = _mm(_rms(x_buf[slot], g_ref[...]), w_ref[...]).astype(BF16)


def _qkv_proj(x, gain, w, group, *, tm):
    bsz, seq, d = x.shape
    dil = ATTN_GROUPS[group][1]
    comp = seq // dil
    tn = w.shape[1]
    out_shape = jax.ShapeDtypeStruct((bsz, dil, comp, tn), BF16)
    if dil % ROW_TILE:
        return pl.pallas_call(
            functools.partial(_qkv_rows_kernel, dil=dil),
            grid=(bsz, seq // tm),
            in_specs=[pl.BlockSpec((1, tm, d), lambda b, i: (b, i, 0)),
                      pl.BlockSpec((1, d), lambda b, i: (0, 0)),
                      pl.BlockSpec((d, tn), lambda b, i: (0, 0))],
            out_specs=pl.BlockSpec((1, dil, tm // dil, tn), lambda b, i: (b, 0, i, 0)),
            out_shape=out_shape,
            scratch_shapes=[pltpu.VMEM((d // LANES, tm, LANES), F32)],
            compiler_params=_params("parallel", "parallel"),
            name=f"qkv_proj_{group}",
        )(x, gain, w)
    tm = min(tm, comp)
    return pl.pallas_call(
        _qkv_strided_kernel,
        grid=(bsz, dil, comp // tm),
        in_specs=[pl.BlockSpec(memory_space=pl.ANY),
                  pl.BlockSpec((1, d), lambda b, r, i: (0, 0)),
                  pl.BlockSpec((d, tn), lambda b, r, i: (0, 0))],
        out_specs=pl.BlockSpec((1, 1, tm, tn), lambda b, r, i: (b, r, i, 0)),
        out_shape=out_shape,
        scratch_shapes=[pltpu.VMEM((2, tm, d), F32), pltpu.SemaphoreType.DMA((2,))],
        compiler_params=_params("arbitrary", "arbitrary", "arbitrary"),
        name=f"qkv_proj_{group}",
    )(x.reshape(bsz, comp, dil, d), gain, w)


def _seg_sum(x, seg_ones):
    hi = x.astype(BF16)
    lo = (x - hi.astype(F32)).astype(BF16)
    return (jnp.dot(hi, seg_ones, preferred_element_type=F32)
            + jnp.dot(lo, seg_ones, preferred_element_type=F32))


def _rwkv_inputs(z, last, mu, w0, w2, a0, a2, g2, k_k, k_a, seg):
    row = lax.broadcasted_iota(jnp.int32, z.shape, 0)
    prev = jnp.where(row == 0, last, pltpu.roll(z, 1, axis=0))
    zs = z + (prev - z) * mu
    d = RWKV_DIM
    r, k, v = zs[:, :d], zs[:, d:2 * d], zs[:, 2 * d:3 * d]
    lora = zs[:, 3 * d:]
    lane = lax.broadcasted_iota(jnp.int32, lora.shape, 1)
    lora = jnp.where(lane < W_LORA, jnp.tanh(lora),
                     jnp.where(lane < W_LORA + A_LORA, lora, _sigmoid(lora)))
    w_lin = w0 + _mm_split(lora, w2)
    a_lin = a0 + _mm_split(lora, a2)
    g = _mm_split(lora, g2)
    u = -w_lin
    softplus = jnp.maximum(u, 0.0) + jnp.log(1.0 + jnp.exp(-jnp.abs(u)))
    lw = -jnp.exp(-softplus - 0.5)
    a = _sigmoid(a_lin)
    kk = k * k_k
    kk = kk * lax.rsqrt(_seg_sum(kk * kk, seg) + 1e-12)
    k = k * (1.0 + (a - 1.0) * k_a)
    return r, lw, k, v, kk, kk * a, g


def _prefix_sum(tri, x):
    hi = x.astype(BF16)
    rest = x - hi.astype(F32)
    mid = rest.astype(BF16)
    lo = (rest - mid.astype(F32)).astype(BF16)
    dot = functools.partial(jnp.dot, preferred_element_type=F32)
    return dot(tri, hi) + (dot(tri, mid) + dot(tri, lo))


def _bmm(a, b):
    return lax.dot_general(a.astype(BF16), b.astype(BF16), (((2,), (1,)), ((0,), (0,))),
                           preferred_element_type=F32)


def _bmm_nt(a, b):
    return lax.dot_general(a.astype(BF16), b.astype(BF16), (((2,), (2,)), ((0,), (0,))),
                           preferred_element_type=F32)


def _bmm_tn(a, b):
    return lax.dot_general(a.astype(BF16), b.astype(BF16), (((1,), (1,)), ((0,), (0,))),
                           preferred_element_type=F32)


def _rwkv_kernel(x_ref, gain_ref, wz_ref, mu_ref, w0_ref, w2_ref, a0_ref, a2_ref, g2_ref, kk_ref, ka_ref,
                 lnw_ref, lnb_ref, rk_ref, seg_ref, o_ref, blank_ref, s_ref, zlast_ref, *, chunk):
    @pl.when(pl.program_id(1) == 0)
    def _():
        s_ref[...] = jnp.zeros_like(s_ref)
        zlast_ref[...] = jnp.zeros_like(zlast_ref)

    tb = x_ref.shape[1]
    d = RWKV_DIM
    z = jnp.dot(_rms(x_ref[0], gain_ref[...]).astype(BF16), wz_ref[...], preferred_element_type=F32)
    last = zlast_ref[7:8, :]
    zlast_ref[...] = z[tb - 8:, :]
    r, lw, k, v, kk, b, g = _rwkv_inputs(z, last, mu_ref[...], w0_ref[...], w2_ref[...], a0_ref[...],
                                         a2_ref[...], g2_ref[...], kk_ref[...], ka_ref[...], seg_ref[...])
    nc, nh = tb // chunk, d // HEAD_DIM
    ii = lax.broadcasted_iota(jnp.int32, (chunk, chunk), 0)
    jj = lax.broadcasted_iota(jnp.int32, (chunk, chunk), 1)
    incl, strict = ii >= jj, ii > jj
    tri = jnp.where(incl, 1.0, 0.0).astype(BF16)
    eye = (ii == jj).astype(F32)
    rows = lambda c: slice(c * chunk, (c + 1) * chunk)
    lanes = lambda h: slice(h * HEAD_DIM, (h + 1) * HEAD_DIM)

    def split(x):
        return jnp.stack([x[rows(c), lanes(h)] for c in range(nc) for h in range(nh)])

    cums = [_prefix_sum(tri, lw[rows(c)]) for c in range(nc)]
    cum = jnp.concatenate(cums, axis=0)
    total = jnp.concatenate([jnp.broadcast_to(cs[chunk - 1:], (chunk, d)) for cs in cums], axis=0)
    g_inv = jnp.exp(-cum)
    g_tail = jnp.exp(total - cum)
    kq, rq = split(kk * jnp.exp(cum - lw)), split(r * jnp.exp(cum))
    bi, ki = split(b * g_inv), split(k * g_inv)
    bh, kh = split(b * g_tail), split(k * g_tail)
    vs = split(v)
    decay = jnp.stack([jnp.exp(cs[chunk - 1:, lanes(h)]) for cs in cums for h in range(nh)])

    qr = jnp.concatenate([kq, rq], axis=1)
    ab, ak = _bmm_nt(qr, bi), _bmm_nt(qr, ki)
    a_b, a_rb = jnp.where(strict, ab[:, :chunk], 0.0), jnp.where(incl, ab[:, chunk:], 0.0)
    a_k, a_rk = jnp.where(strict, ak[:, :chunk], 0.0), jnp.where(incl, ak[:, chunk:], 0.0)
    n = nc * nh
    is_b = lax.broadcasted_iota(jnp.int32, (chunk, 2 * chunk), 1) < chunk
    bt = jnp.concatenate([-a_b, jnp.broadcast_to(eye, (n, chunk, chunk))], axis=2)
    for _ in range(int(np.log2(chunk))):
        prod = _bmm(bt[:, :, :chunk], bt)
        bt = jnp.where(is_b, prod, bt + prod)
    inv = bt[:, :, chunk:]
    wu = _bmm(inv, jnp.concatenate([kq, -_bmm(a_k, vs)], axis=2))
    aw = _bmm(a_rb, wu)
    qh = rq - aw[:, :, :HEAD_DIM]
    y0 = aw[:, :, HEAD_DIM:] + _bmm(a_rk, vs)
    mu = _bmm_tn(wu, bh)
    mp = mu[:, :HEAD_DIM]
    s_add = mu[:, HEAD_DIM:] + _bmm_tn(vs, kh)

    state = s_ref[...]
    for c in range(nc):
        sl = slice(c * nh, (c + 1) * nh)
        y = _bmm_nt(qh[sl], state) + y0[sl]
        for h in range(nh):
            o_ref[0, rows(c), lanes(h)] = y[h]
        state = state * decay[sl] - _bmm(state, mp[sl]) + s_add[sl]
    s_ref[...] = state

    seg = seg_ref[...]
    y = o_ref[0]
    yc = y - _seg_sum(y, seg) * (1.0 / HEAD_DIM)
    var = _seg_sum(yc * yc, seg) * (1.0 / HEAD_DIM)
    y = yc * lax.rsqrt(var + RWKV_GN_EPS) * lnw_ref[...] + lnb_ref[...]
    y = y + _seg_sum(r * k * rk_ref[...], seg) * v
    o_ref[0] = y * g
    blank_ref[...] = jnp.zeros_like(blank_ref)


def _rwkv_branch(x, gain, wz, mu, w0, w2p, a0, a2p, g2p, k_k, k_a, ln_w, ln_b, r_k, seg_ones, blank_rows,
                 *, tb):
    bsz, seq, dm = x.shape
    d = RWKV_DIM
    steps = seq // tb
    full = lambda a: pl.BlockSpec(a.shape, lambda bi, t: (0,) * a.ndim)
    params = (gain, wz, mu, w0, w2p, a0, a2p, g2p, k_k, k_a, ln_w, ln_b, r_k, seg_ones)
    blank_blk = blank_rows // (bsz * steps)
    return pl.pallas_call(
        functools.partial(_rwkv_kernel, chunk=WKV_CHUNK),
        grid=(bsz, steps),
        in_specs=[pl.BlockSpec((1, tb, dm), lambda bi, t: (bi, t, 0))] + [full(a) for a in params],
        out_specs=[pl.BlockSpec((1, tb, d), lambda bi, t: (bi, t, 0)),
                   pl.BlockSpec((blank_blk, LANES), lambda bi, t: (bi * steps + t, 0))],
        out_shape=[jax.ShapeDtypeStruct((bsz, seq, d), F32),
                   jax.ShapeDtypeStruct((blank_rows, LANES), F32)],
        scratch_shapes=[pltpu.VMEM((d // HEAD_DIM, HEAD_DIM, HEAD_DIM), F32),
                        pltpu.VMEM((8, RWKV_COLS), F32)],
        compiler_params=_params("parallel", "arbitrary"),
        name="rwkv_branch",
    )(x, *params)


def _attn_kernel(q_ref, kp_ref, kc_ref, vp_ref, vc_ref, bias_ref, o_ref, l_ref):
    first = pl.program_id(2) == 0
    q = q_ref[0, 0] * (HEAD_DIM ** -0.5)
    kc, vc = kc_ref[0, 0], vc_ref[0, 0]
    subs = q.shape[0] // Q_BLOCK
    rows = lambda j: slice(j * Q_BLOCK, (j + 1) * Q_BLOCK)
    lanes = lambda h: slice(h * HEAD_DIM, (h + 1) * HEAD_DIM)
    k_prev = [kp_ref[0, 0]] + [kc[rows(j)] for j in range(subs - 1)]
    v_prev = [vp_ref[0, 0]] + [vc[rows(j)] for j in range(subs - 1)]
    work = [(j, h) for j in range(subs) for h in range(HEADS_PER_GROUP)]
    scores = []
    for j, h in work:
        bias = bias_ref[h]
        bias_prev = bias[:, :Q_BLOCK]
        if j == 0:
            bias_prev = jnp.where(first, MASKED, bias_prev)
        qh = q[rows(j), lanes(h)]
        scores.append((_mm_nt(qh, k_prev[j][:, lanes(h)]) + bias_prev,
                       _mm_nt(qh, kc[rows(j), lanes(h)]) + bias[:, Q_BLOCK:]))
    tops = [jnp.max(jnp.maximum(sp, sc), axis=-1, keepdims=True) for sp, sc in scores]
    probs = [(jnp.exp(sp - m), jnp.exp(sc - m)) for (sp, sc), m in zip(scores, tops)]
    dens = [jnp.sum(pp + pc, axis=-1, keepdims=True) for pp, pc in probs]
    accs = [_mm(pp, v_prev[j][:, lanes(h)]) + _mm(pc, vc[rows(j), lanes(h)])
            for (pp, pc), (j, h) in zip(probs, work)]
    lane = lax.broadcasted_iota(jnp.int32, (Q_BLOCK, LANES), 1)
    for j in range(subs):
        lse = jnp.zeros((Q_BLOCK, LANES), F32)
        for h in range(HEADS_PER_GROUP):
            i = j * HEADS_PER_GROUP + h
            o_ref[0, 0, rows(j), lanes(h)] = accs[i] * (1.0 / dens[i])
            lse = jnp.where(lane == h, tops[i] + jnp.log(dens[i]), lse)
        l_ref[0, 0, rows(j), :] = lse


def _attn_bias(group):
    window, dilation = ATTN_GROUPS[group]
    n_back = window // dilation
    heads = np.arange(group * HEADS_PER_GROUP, (group + 1) * HEADS_PER_GROUP)
    slopes = (2.0 ** (-8.0 * (heads + 1) / ATTN_HEADS)).astype(np.float32)
    delta = (np.arange(Q_BLOCK)[:, None] + Q_BLOCK) - np.arange(2 * Q_BLOCK)[None, :]
    valid = (delta >= 0) & (delta <= n_back)
    alibi = -slopes[:, None, None] * (dilation * delta).astype(np.float32)[None]
    return jnp.asarray(np.where(valid[None], alibi, np.float32(MASKED)).astype(np.float32))


def _attn_group(qkv, group):
    bsz, dil, comp, _ = qkv.shape
    tq = min(ATTN_TQ, comp)
    assert comp % tq == 0 and tq % Q_BLOCK == 0
    sub = tq // Q_BLOCK
    col = lambda which: (lambda bi, r, n: (bi, r, n, which))
    col_prev = lambda which: (lambda bi, r, n: (bi, r, jnp.maximum(n * sub - 1, 0), which))
    cur, prev = (1, 1, tq, ATTN_OUT_DIM), (1, 1, Q_BLOCK, ATTN_OUT_DIM)
    bias = _attn_bias(group)
    return pl.pallas_call(
        _attn_kernel,
        grid=(bsz, dil, comp // tq),
        in_specs=[pl.BlockSpec(cur, col(0)),
                  pl.BlockSpec(prev, col_prev(1)), pl.BlockSpec(cur, col(1)),
                  pl.BlockSpec(prev, col_prev(2)), pl.BlockSpec(cur, col(2)),
                  pl.BlockSpec(bias.shape, lambda bi, r, n: (0, 0, 0))],
        out_specs=[pl.BlockSpec(cur, col(0)),
                   pl.BlockSpec((1, 1, tq, LANES), col(0))],
        out_shape=[jax.ShapeDtypeStruct((bsz, dil, comp, ATTN_OUT_DIM), F32),
                   jax.ShapeDtypeStruct((bsz, dil, comp, LANES), F32)],
        compiler_params=_params("parallel", "parallel", "parallel"),
        name=f"dilated_attn_{group}",
    )(qkv, qkv, qkv, qkv, qkv, bias)


def _route(logits):
    lane = lax.broadcasted_iota(jnp.int32, logits.shape, 1)
    neg = -jnp.inf
    first_max = lambda vals, mx: jnp.min(jnp.where(vals == mx, lane, LANES), axis=-1, keepdims=True)
    is_grp = (lane >= N_EXPERTS) & (lane < N_EXPERTS + N_EXPERT_GROUPS)
    gl = jnp.where(is_grp, logits, neg)
    gmax = jnp.max(gl, axis=-1, keepdims=True)
    grp_w = 1.0 / jnp.sum(jnp.exp(gl - gmax), axis=-1, keepdims=True)
    grp = first_max(gl, gmax) - N_EXPERTS
    in_grp = (lane >= grp * EXPERTS_PER_GROUP) & (lane < (grp + 1) * EXPERTS_PER_GROUP)
    el = jnp.where(in_grp, logits, neg)
    v1 = jnp.max(el, axis=-1, keepdims=True)
    i1 = first_max(el, v1)
    el2 = jnp.where(lane == i1, neg, el)
    v2 = jnp.max(el2, axis=-1, keepdims=True)
    i2 = first_max(el2, v2)
    e2 = jnp.exp(v2 - v1)
    w1 = grp_w / (1.0 + e2)
    return i1, i2, w1, w1 * e2


def _merge_kernel(x_ref, ya_ref, o0_ref, o1_ref, o2_ref, l0_ref, l1_ref, l2_ref, gmix_ref, wgt_ref,
                  wa_ref, wb_ref, wo_ref, gain_ref, wr_ref, ex_ref, tri_ref, x1_ref, h_ref, route_ref,
                  count_ref, o_scr, l_scr):
    outs, lses = [], []
    for grp, (o_ref, l_ref) in enumerate(((o0_ref, l0_ref), (o1_ref, l1_ref), (o2_ref, l2_ref))):
        dil = o_ref.shape[1]
        if dil == 1:
            outs.append(o_ref[0, 0])
            lses.append(l_ref[0, 0])
            continue
        n = o_ref.shape[2]
        chunks = o_ref.shape[3] // LANES
        for r in range(dil):
            l_scr[grp - 1, pl.ds(r, n, stride=dil), :] = l_ref[0, r]
            for c in range(chunks):
                o_scr[grp - 1, c, pl.ds(r, n, stride=dil), :] = o_ref[0, r, :, c * LANES:(c + 1) * LANES]
        outs.append(jnp.concatenate([o_scr[grp - 1, c] for c in range(chunks)], axis=1))
        lses.append(l_scr[grp - 1])
    def mix(rows):
        l0, l1, l2 = (l[rows] for l in lses)
        m = jnp.maximum(jnp.maximum(l0, l1), l2)
        e0, e1, e2 = jnp.exp(l0 - m), jnp.exp(l1 - m), jnp.exp(l2 - m)
        inv = 1.0 / (e0 + e1 + e2)
        spread = ex_ref[...]
        yb = (_seg_sum(e0 * inv, spread) * outs[0][rows] + _seg_sum(e1 * inv, spread) * outs[1][rows]
              + _seg_sum(e2 * inv, spread) * outs[2][rows])
        x = x_ref[rows, :]
        gates = _sigmoid(_mm(_rms(x, gmix_ref[...]), wgt_ref[...]))
        merged = (gates[:, :D_MODEL] * _mm(ya_ref[rows, :], wa_ref[...])
                  + gates[:, D_MODEL:] * _mm(yb, wb_ref[...]))
        x1 = x + _mm(merged, wo_ref[...])
        h = _rms(x1, gain_ref[...])
        return (x1, h) + _route(_mm_split(h, wr_ref[...]))

    half = x_ref.shape[0] // 2
    parts = [mix(slice(0, half)), mix(slice(half, 2 * half))]
    x1, h, i1, i2, w1, w2 = (jnp.concatenate(both, axis=0) for both in zip(*parts))
    x1_ref[...] = x1
    h_ref[...] = h

    @pl.when(pl.program_id(0) == 0)
    def _():
        count_ref[...] = jnp.zeros_like(count_ref)

    lane = lax.broadcasted_iota(jnp.int32, (x1.shape[0], LANES), 1)
    onehot = jnp.where((lane == i1) | (lane == i2), 1.0, 0.0)
    before = count_ref[0:1, :] + jnp.dot(tri_ref[...], onehot.astype(BF16), preferred_element_type=F32)
    rank1 = jnp.sum(jnp.where(lane == i1, before, 0.0), axis=-1, keepdims=True)
    rank2 = jnp.sum(jnp.where(lane == i2, before, 0.0), axis=-1, keepdims=True)
    last = x1.shape[0] - 1
    count_ref[...] = jnp.broadcast_to(before[last:, :] + onehot[last:, :], count_ref.shape)
    cols = (i1.astype(F32), i2.astype(F32), w1, w2, rank1, rank2)
    info = jnp.zeros(lane.shape, F32)
    for c, col in enumerate(cols):
        info = jnp.where(lane == c, col, info)
    route_ref[...] = info


def _merge(x, ya, outs, lses, gain_mix, w_gates, wa, wb, wo, gain, wr, spread, seq, *, tm):
    t = x.shape[0]
    tps = seq // tm
    tok = lambda n: pl.BlockSpec((tm, n), lambda i: (i, 0))
    full = lambda a: pl.BlockSpec(a.shape, lambda i: (0, 0))
    res = lambda a: pl.BlockSpec((1, a.shape[1], tm // a.shape[1], a.shape[3]),
                                 lambda i: (i // tps, 0, i % tps, 0))
    dilated = len(ATTN_GROUPS) - 1
    tri = jnp.asarray(np.tri(tm, k=-1), dtype=BF16)
    return pl.pallas_call(
        _merge_kernel,
        grid=(t // tm,),
        in_specs=[tok(D_MODEL), tok(RWKV_DIM)] + [res(a) for a in outs] + [res(a) for a in lses] + [
                  full(gain_mix), full(w_gates), full(wa), full(wb), full(wo), full(gain), full(wr), full(spread),
                  full(tri)],
        out_specs=[tok(D_MODEL), tok(D_MODEL), tok(LANES), pl.BlockSpec((8, LANES), lambda i: (0, 0))],
        out_shape=[jax.ShapeDtypeStruct((t, D_MODEL), F32), jax.ShapeDtypeStruct((t, D_MODEL), F32),
                   jax.ShapeDtypeStruct((t, LANES), F32), jax.ShapeDtypeStruct((8, LANES), F32)],
        scratch_shapes=[pltpu.VMEM((dilated, ATTN_OUT_DIM // LANES, tm, LANES), F32),
                        pltpu.VMEM((dilated, tm, LANES), F32)],
        compiler_params=_params("arbitrary"),
        name="merge_route",
    )(x, ya, *outs, *lses, gain_mix, w_gates, wa, wb, wo, gain, wr, spread, tri)


ROUTE_E1, ROUTE_E2, ROUTE_W1, ROUTE_W2, ROUTE_R1, ROUTE_R2 = range(6)


def _moe_plan(counts, n_tiles):
    counts = counts[0, :N_EXPERTS].astype(jnp.int32)
    tile_end = jnp.cumsum((counts + MOE_TILE - 1) // MOE_TILE)
    n_valid = tile_end[-1]
    tile = jnp.minimum(jnp.arange(n_tiles, dtype=jnp.int32), n_valid - 1)
    tile_expert = jnp.sum(tile[:, None] >= tile_end[None, :], axis=1).astype(jnp.int32)
    return tile_expert, n_valid.reshape(1)


def _row_tiles(ref, rows, tile=ROW_TILE):
    return jnp.concatenate([ref[pl.ds(c, rows, stride=tile), :] for c in range(tile)], axis=1)


def _store_row_tiles(ref, x, tile=ROW_TILE):
    for c in range(tile):
        ref[pl.ds(c, x.shape[0], stride=tile), :] = x[:, c * LANES:(c + 1) * LANES]


def _row(ref, r, tile=ROW_TILE):
    return ref.at[pl.ds(pl.multiple_of(r * tile, tile), tile)]


def _dispatch_kernel(route_ref, count_ref, upper_ref, h_ref, xs_in_ref, xs_ref, pos_ref, rows_ref, pos_vmem,
                     pos_smem, sem, pos_sem):
    del xs_in_ref
    tm = h_ref.shape[0]
    tiles = jnp.floor((count_ref[...] + (MOE_TILE - 1)) * (1.0 / MOE_TILE))
    first_row = jnp.dot(tiles.astype(BF16), upper_ref[...], preferred_element_type=F32)[0:1, :] * MOE_TILE
    route = route_ref[...]
    lane = lax.broadcasted_iota(jnp.int32, route.shape, 1)
    col = lambda c: route[:, c:c + 1]
    place = lambda e, r: jnp.sum(jnp.where(lane == col(e).astype(jnp.int32), first_row, 0.0), axis=-1,
                                 keepdims=True) + col(r)
    both = jnp.where(lane == 0, place(ROUTE_E1, ROUTE_R1), jnp.where(lane == 1, place(ROUTE_E2, ROUTE_R2), 0.0))
    pos = both.T[0:ROW_TILE, :].astype(jnp.int32)
    pos_ref[0] = pos[0:2]
    pos_vmem[...] = pos
    to_smem = pltpu.make_async_copy(pos_vmem, pos_smem, pos_sem)
    to_smem.start()
    _store_row_tiles(rows_ref, h_ref[...])
    to_smem.wait()

    def copies(j):
        src = _row(rows_ref, j)
        return (pltpu.make_async_copy(src, _row(xs_ref, pos_smem[0, j]), sem.at[0]),
                pltpu.make_async_copy(src, _row(xs_ref, pos_smem[1, j]), sem.at[1]))

    def start(j, carry):
        for stream, cp in enumerate(copies(j)):
            cp.start(priority=stream)
        return carry

    lax.fori_loop(0, tm, start, 0, unroll=8)
    for stream in range(2):
        pltpu.make_async_copy(rows_ref, rows_ref, sem.at[stream]).wait()


def _dispatch(h, route, counts, blank, *, tm):
    t, d = h.shape
    upper = jnp.asarray(np.triu(np.ones((LANES, LANES)), k=1), dtype=BF16)
    tok = lambda n: pl.BlockSpec((tm, n), lambda i: (i, 0))
    full = lambda a: pl.BlockSpec(a.shape, lambda i: (0, 0))
    return pl.pallas_call(
        _dispatch_kernel,
        grid=(t // tm,),
        in_specs=[tok(LANES), full(counts), full(upper), tok(d), pl.BlockSpec(memory_space=pl.ANY)],
        out_specs=[pl.BlockSpec(memory_space=pl.ANY), pl.BlockSpec((1, 2, tm), lambda i: (i, 0, 0))],
        out_shape=[jax.ShapeDtypeStruct(blank.shape, blank.dtype),
                   jax.ShapeDtypeStruct((t // tm, 2, tm), jnp.int32)],
        scratch_shapes=[pltpu.VMEM((tm * ROW_TILE, LANES), F32), pltpu.VMEM((ROW_TILE, tm), jnp.int32),
                        pltpu.SMEM((ROW_TILE, tm), jnp.int32), pltpu.SemaphoreType.DMA((2,)),
                        pltpu.SemaphoreType.DMA],
        input_output_aliases={4: 0},
        compiler_params=_params("arbitrary"),
        name="moe_dispatch",
    )(route, counts, upper, h, blank)


def _expert_kernel(te_ref, nv_ref, x_ref, wg_ref, wu_ref, wd_ref, o_ref):
    live = pl.program_id(0) < nv_ref[0]

    @pl.when(live)
    def _():
        x = _row_tiles(x_ref, MOE_TILE)
        gate = _mm(x, wg_ref[0])
        up = _mm(x, wu_ref[0])
        _store_row_tiles(o_ref, _mm(gate * _sigmoid(gate) * up, wd_ref[0]))

    @pl.when(jnp.logical_not(live))
    def _():
        o_ref[...] = jnp.zeros_like(o_ref)


def _experts(xs, tile_expert, n_valid, wg, wu, wd):
    d = wg.shape[1]
    weight = lambda shape: pl.BlockSpec((1,) + shape, lambda i, te, nv: (te[i], 0, 0))
    blk = (MOE_TILE * ROW_TILE, LANES)
    return pl.pallas_call(
        _expert_kernel,
        grid_spec=pltpu.PrefetchScalarGridSpec(
            num_scalar_prefetch=2,
            grid=(xs.shape[0] // blk[0],),
            in_specs=[pl.BlockSpec(blk, lambda i, te, nv: (jnp.minimum(i, nv[0] - 1), 0)),
                      weight((d, D_FF_EXPERT)), weight((d, D_FF_EXPERT)), weight((D_FF_EXPERT, d))],
            out_specs=pl.BlockSpec(blk, lambda i, te, nv: (i, 0))),
        out_shape=jax.ShapeDtypeStruct(xs.shape, F32),
        compiler_params=_params("arbitrary"),
        name="moe_experts",
    )(tile_expert, n_valid, xs, wg, wu, wd)


def _ple_kernel(pos_ref, x_ref, route_ref, p_ref, gp_ref, wg_ref, wp_ref, gf_ref, ys_ref, o_ref,
                y_ref, sem, *, pos_tm):
    tm = x_ref.shape[0]
    step, steps = pl.program_id(0), pl.num_programs(0)
    per_pos_tile = pos_tm // tm

    def gather(tile, slot, inline):
        base = (tile // per_pos_tile) * (2 * pos_tm) + (tile % per_pos_tile) * tm

        def body(j, carry):
            for stream in range(2):
                pltpu.make_async_copy(_row(ys_ref, pos_ref[base + stream * pos_tm + j]),
                                      _row(y_ref.at[slot, stream], j), sem.at[slot, stream]
                                      ).start(priority=stream)
            return carry

        if inline:
            for j in range(tm):
                body(j, 0)
        else:
            lax.fori_loop(0, tm, body, 0, unroll=8)

    def drain(slot):
        for stream in range(2):
            buf = y_ref.at[slot, stream]
            pltpu.make_async_copy(buf, buf, sem.at[slot, stream]).wait()

    slot = step % 2

    @pl.when(step == 0)
    def _():
        gather(0, 0, False)

    drain(slot)
    gather(jnp.minimum(step + 1, steps - 1), 1 - slot, True)
    route = route_ref[...]
    w1, w2 = route[:, ROUTE_W1:ROUTE_W1 + 1], route[:, ROUTE_W2:ROUTE_W2 + 1]
    x = x_ref[...] + (w1 * _row_tiles(y_ref.at[slot, 0], tm) + w2 * _row_tiles(y_ref.at[slot, 1], tm))
    gate = _sigmoid(_mm(_rms(x, gp_ref[...]), wg_ref[...]))
    x = x + gate * _mm(p_ref[...], wp_ref[...])
    o_ref[...] = _rms(x, gf_ref[...])

    @pl.when(step == steps - 1)
    def _():
        drain(1 - slot)


def _ple(x, route, pos, ys, p, gain_ple, w_gate, w_proj, gain_final, *, tm):
    t = x.shape[0]
    pos_tm = pos.shape[2]
    assert pos_tm % tm == 0
    tok = lambda n: pl.BlockSpec((tm, n), lambda i, ps: (i, 0))
    full = lambda a: pl.BlockSpec(a.shape, lambda i, ps: (0, 0))
    return pl.pallas_call(
        functools.partial(_ple_kernel, pos_tm=pos_tm),
        grid_spec=pltpu.PrefetchScalarGridSpec(
            num_scalar_prefetch=1,
            grid=(t // tm,),
            in_specs=[tok(D_MODEL), tok(LANES), tok(PLE_DIM), full(gain_ple), full(w_gate), full(w_proj),
                      full(gain_final), pl.BlockSpec(memory_space=pl.ANY)],
            out_specs=tok(D_MODEL),
            scratch_shapes=[pltpu.VMEM((2, 2, tm * ROW_TILE, LANES), F32), pltpu.SemaphoreType.DMA((2, 2))]),
        out_shape=jax.ShapeDtypeStruct((t, D_MODEL), F32),
        compiler_params=_params("arbitrary"),
        name="ple_final",
    )(pos.reshape(-1), x, route, p, gain_ple, w_gate, w_proj, gain_final, ys)


def _pad_rows(w, start, total):
    return jnp.zeros((total, w.shape[1]), w.dtype).at[start:start + w.shape[0]].set(w)


def _layer(x, p, seq, norm_mix, w_in, rwkv_mu, rwkv_w0, rwkv_w2, rwkv_a0, rwkv_a2, rwkv_g2, rwkv_k_k,
           rwkv_k_a, rwkv_r_k, rwkv_ln_w, rwkv_ln_b, w_proj_a, w_proj_b, w_out, norm_moe, w_router_group,
           w_router_expert, w_exp_gate, w_exp_up, w_exp_down, norm_ple, w_ple_gate, w_ple_proj, norm_final):
    t = x.shape[0]
    bsz = t // seq
    row = lambda a: a.reshape(1, -1)
    c0, c1 = RWKV_COLS, RWKV_COLS + ATTN_COLS
    x3 = x.reshape(bsz, seq, D_MODEL)

    def w_group(grp):
        lo = lambda which: c0 + which * ATTN_DIM + grp * ATTN_OUT_DIM
        return jnp.concatenate([w_in[:, lo(which):lo(which) + ATTN_OUT_DIM] for which in range(3)],
                               axis=1).astype(BF16)

    qkv = [_qkv_proj(x3, row(norm_mix), w_group(grp), grp, tm=1024) for grp in range(len(ATTN_GROUPS))]

    seg = np.arange(RWKV_DIM) // HEAD_DIM
    seg_ones = jnp.asarray(seg[:, None] == seg[None, :], dtype=BF16)
    n_tiles = 2 * t // MOE_TILE + N_EXPERTS
    ya, blank = _rwkv_branch(
        x3, row(norm_mix), w_in[:, :c0].astype(BF16), row(rwkv_mu), row(rwkv_w0),
        _pad_rows(rwkv_w2, 0, LORA_COLS), row(rwkv_a0), _pad_rows(rwkv_a2, W_LORA, LORA_COLS),
        _pad_rows(rwkv_g2, W_LORA + A_LORA, LORA_COLS), row(rwkv_k_k), row(rwkv_k_a), row(rwkv_ln_w),
        row(rwkv_ln_b), row(rwkv_r_k), seg_ones, n_tiles * MOE_TILE * ROW_TILE, tb=256)
    ya = ya.reshape(t, RWKV_DIM)

    attn = [_attn_group(qkv[grp], grp) for grp in range(len(ATTN_GROUPS))]

    head_of_lane = np.arange(ATTN_OUT_DIM) // HEAD_DIM
    spread = jnp.asarray(np.arange(LANES)[:, None] == head_of_lane[None, :], dtype=BF16)
    w_route = jnp.concatenate(
        [jnp.moveaxis(w_router_expert, 0, 1).reshape(D_MODEL, N_EXPERTS), w_router_group,
         jnp.zeros((D_MODEL, LANES - N_EXPERTS - N_EXPERT_GROUPS), F32)], axis=1)
    x1, h_moe, route, counts = _merge(x, ya, [o for o, _ in attn], [l for _, l in attn],
                             row(norm_mix), w_in[:, c1:].astype(BF16),
                             w_proj_a.astype(BF16), w_proj_b.astype(BF16), w_out.astype(BF16),
                             row(norm_moe), w_route, spread, seq, tm=512)
    tile_expert, n_valid = _moe_plan(counts, n_tiles)
    dispatch_tm = 2048
    xs, pos = _dispatch(h_moe, route, counts, blank, tm=dispatch_tm)
    ys = _experts(xs, tile_expert, n_valid,
                  w_exp_gate.reshape(N_EXPERTS, D_MODEL, D_FF_EXPERT),
                  w_exp_up.reshape(N_EXPERTS, D_MODEL, D_FF_EXPERT),
                  w_exp_down.reshape(N_EXPERTS, D_FF_EXPERT, D_MODEL))
    return _ple(x1, route, pos, ys, p, row(norm_ple), w_ple_gate.astype(BF16),
                w_ple_proj.astype(BF16), row(norm_final), tm=512)


def kernel(x, p, norm_mix, w_in, rwkv_mu, rwkv_w0, rwkv_w2, rwkv_a0, rwkv_a2, rwkv_g2, rwkv_k_k, rwkv_k_a, rwkv_r_k, rwkv_ln_w, rwkv_ln_b, w_proj_a, w_proj_b, w_out, norm_moe, w_router_group, w_router_expert, w_exp_gate, w_exp_up, w_exp_down, norm_ple, w_ple_gate, w_ple_proj, norm_final):
    bsz, seq, d = x.shape
    depth = w_in.shape[0]
    assert depth == 1, "the final norm is fused into the (single) layer"
    out = _layer(x.reshape(bsz * seq, d).astype(F32), p[0].reshape(bsz * seq, PLE_DIM), seq,
                 norm_mix[0], w_in[0], rwkv_mu[0], rwkv_w0[0], rwkv_w2[0], rwkv_a0[0], rwkv_a2[0],
                 rwkv_g2[0], rwkv_k_k[0], rwkv_k_a[0], rwkv_r_k[0], rwkv_ln_w[0], rwkv_ln_b[0],
                 w_proj_a[0], w_proj_b[0], w_out[0], norm_moe[0], w_router_group[0], w_router_expert[0],
                 w_exp_gate[0], w_exp_up[0], w_exp_down[0], norm_ple[0], w_ple_gate[0], w_ple_proj[0],
                 norm_final)
    return out.reshape(bsz, seq, d)
```

```python
import functools

import numpy as np
import jax
import jax.numpy as jnp
from jax import lax
from jax.experimental import pallas as pl
from jax.experimental.pallas import tpu as pltpu

F32 = jnp.float32
BF16 = jnp.bfloat16

D_MODEL = 1024
PLE_DIM = 256
HEAD_DIM = 64
RWKV_HEADS = 8
RWKV_DIM = RWKV_HEADS * HEAD_DIM
W_LORA, A_LORA, G_LORA = 64, 64, 128
LORA_COLS = W_LORA + A_LORA + G_LORA
ATTN_GROUPS = ((128, 1), (512, 4), (2048, 16))
HEADS_PER_GROUP = 8
ATTN_HEADS = HEADS_PER_GROUP * len(ATTN_GROUPS)
ATTN_DIM = ATTN_HEADS * HEAD_DIM
ATTN_OUT_DIM = HEADS_PER_GROUP * HEAD_DIM
Q_BLOCK = 128
ATTN_TQ = 512
RWKV_COLS = 3 * RWKV_DIM + LORA_COLS
ATTN_COLS = 3 * ATTN_DIM
N_EXPERT_GROUPS = 4
EXPERTS_PER_GROUP = 8
N_EXPERTS = N_EXPERT_GROUPS * EXPERTS_PER_GROUP
D_FF_EXPERT = 256
NORM_EPS = 1e-6
RWKV_GN_EPS = 64e-5
LANES = 128
MASKED = -1e30
WKV_CHUNK = 64
VMEM_LIMIT = 48 * 1024 * 1024
MOE_TILE = 512
ROW_TILE = 8


def _mm(a, b):
    return jnp.dot(a.astype(BF16), b.astype(BF16), preferred_element_type=F32)


def _mm_nt(a, b):
    return lax.dot_general(a.astype(BF16), b.astype(BF16), (((1,), (1,)), ((), ())),
                           preferred_element_type=F32)


def _hi_lo(w):
    hi = w.astype(BF16)
    return jnp.stack([hi, (w - hi.astype(F32)).astype(BF16)])


def _mm_split(a, b):
    a_hi = a.astype(BF16)
    a_lo = (a - a_hi.astype(F32)).astype(BF16)
    dot = functools.partial(jnp.dot, preferred_element_type=F32)
    return dot(a_hi, b[0]) + (dot(a_hi, b[1]) + dot(a_lo, b[0]))


def _sigmoid(x):
    return 1.0 / (1.0 + jnp.exp(-x))


def _rms(x, gain):
    return x * lax.rsqrt(jnp.mean(x * x, axis=-1, keepdims=True) + NORM_EPS) * gain


def _params(*sem):
    return pltpu.CompilerParams(dimension_semantics=sem, vmem_limit_bytes=VMEM_LIMIT)


def _qkv_rows_kernel(x_ref, g_ref, w_ref, o_ref, stage_ref, *, dil):
    h = _rms(x_ref[0], g_ref[...])
    if dil == 1:
        o_ref[0, 0] = _mm(h, w_ref[...]).astype(BF16)
        return
    chunks = h.shape[1] // LANES
    for c in range(chunks):
        stage_ref[c] = h[:, c * LANES:(c + 1) * LANES]
    n = h.shape[0] // dil
    for r in range(dil):
        rows = jnp.concatenate([stage_ref[c, pl.ds(r, n, stride=dil), :] for c in range(chunks)], axis=1)
        o_ref[0, r] = _mm(rows, w_ref[...]).astype(BF16)


def _qkv_strided_kernel(x_hbm, g_ref, w_ref, o_ref, x_buf, sem):
    tm = x_buf.shape[1]
    n_res, n_blk = pl.num_programs(1), pl.num_programs(2)
    step = (pl.program_id(0) * n_res + pl.program_id(1)) * n_blk + pl.program_id(2)
    steps = pl.num_programs(0) * n_res * n_blk

    def fetch(s, slot):
        b, r, i = s // (n_res * n_blk), (s // n_blk) % n_res, s % n_blk
        return pltpu.make_async_copy(x_hbm.at[b, pl.ds(i * tm, tm), r], x_buf.at[slot], sem.at[slot])

    slot = step % 2

    @pl.when(step == 0)
    def _():
        fetch(0, 0).start()

    @pl.when(step + 1 < steps)
    def _():
        fetch(step + 1, 1 - slot).start()

    fetch(step, slot).wait()
    o_ref[0, 0] = _mm(_rms(x_buf[slot], g_ref[...]), w_ref[...]).astype(BF16)


def _qkv_proj(x, gain, w, group, *, tm):
    bsz, seq, d = x.shape
    dil = ATTN_GROUPS[group][1]
    comp = seq // dil
    tn = w.shape[1]
    out_shape = jax.ShapeDtypeStruct((bsz, dil, comp, tn), BF16)
    if dil % ROW_TILE:
        return pl.pallas_call(
            functools.partial(_qkv_rows_kernel, dil=dil),
            grid=(bsz, seq // tm),
            in_specs=[pl.BlockSpec((1, tm, d), lambda b, i: (b, i, 0)),
                      pl.BlockSpec((1, d), lambda b, i: (0, 0)),
                      pl.BlockSpec((d, tn), lambda b, i: (0, 0))],
            out_specs=pl.BlockSpec((1, dil, tm // dil, tn), lambda b, i: (b, 0, i, 0)),
            out_shape=out_shape,
            scratch_shapes=[pltpu.VMEM((d // LANES, tm, LANES), F32)],
            compiler_params=_params("parallel", "parallel"),
            name=f"qkv_proj_{group}",
        )(x, gain, w)
    tm = min(tm, comp)
    return pl.pallas_call(
        _qkv_strided_kernel,
        grid=(bsz, dil, comp // tm),
        in_specs=[pl.BlockSpec(memory_space=pl.ANY),
                  pl.BlockSpec((1, d), lambda b, r, i: (0, 0)),
                  pl.BlockSpec((d, tn), lambda b, r, i: (0, 0))],
        out_specs=pl.BlockSpec((1, 1, tm, tn), lambda b, r, i: (b, r, i, 0)),
        out_shape=out_shape,
        scratch_shapes=[pltpu.VMEM((2, tm, d), F32), pltpu.SemaphoreType.DMA((2,))],
        compiler_params=_params("arbitrary", "arbitrary", "arbitrary"),
        name=f"qkv_proj_{group}",
    )(x.reshape(bsz, comp, dil, d), gain, w)


def _seg_sum(x, seg_ones):
    hi = x.astype(BF16)
    lo = (x - hi.astype(F32)).astype(BF16)
    return (jnp.dot(hi, seg_ones, preferred_element_type=F32)
            + jnp.dot(lo, seg_ones, preferred_element_type=F32))


def _rwkv_inputs(z, last, mu, w0, w2, a0, a2, g2, k_k, k_a, seg):
    row = lax.broadcasted_iota(jnp.int32, z.shape, 0)
    prev = jnp.where(row == 0, last, pltpu.roll(z, 1, axis=0))
    zs = z + (prev - z) * mu
    d = RWKV_DIM
    r, k, v = zs[:, :d], zs[:, d:2 * d], zs[:, 2 * d:3 * d]
    lora = zs[:, 3 * d:]
    lane = lax.broadcasted_iota(jnp.int32, lora.shape, 1)
    lora = jnp.where(lane < W_LORA, jnp.tanh(lora),
                     jnp.where(lane < W_LORA + A_LORA, lora, _sigmoid(lora)))
    w_lin = w0 + _mm_split(lora, w2)
    a_lin = a0 + _mm_split(lora, a2)
    g = _mm_split(lora, g2)
    u = -w_lin
    softplus = jnp.maximum(u, 0.0) + jnp.log(1.0 + jnp.exp(-jnp.abs(u)))
    lw = -jnp.exp(-softplus - 0.5)
    a = _sigmoid(a_lin)
    kk = k * k_k
    kk = kk * lax.rsqrt(_seg_sum(kk * kk, seg) + 1e-12)
    k = k * (1.0 + (a - 1.0) * k_a)
    return r, lw, k, v, kk, kk * a, g


def _prefix_sum(tri, x):
    hi = x.astype(BF16)
    rest = x - hi.astype(F32)
    mid = rest.astype(BF16)
    lo = (rest - mid.astype(F32)).astype(BF16)
    dot = functools.partial(jnp.dot, preferred_element_type=F32)
    return dot(tri, hi) + (dot(tri, mid) + dot(tri, lo))


def _bmm(a, b):
    return lax.dot_general(a.astype(BF16), b.astype(BF16), (((2,), (1,)), ((0,), (0,))),
                           preferred_element_type=F32)


def _bmm_nt(a, b):
    return lax.dot_general(a.astype(BF16), b.astype(BF16), (((2,), (2,)), ((0,), (0,))),
                           preferred_element_type=F32)


def _bmm_tn(a, b):
    return lax.dot_general(a.astype(BF16), b.astype(BF16), (((1,), (1,)), ((0,), (0,))),
                           preferred_element_type=F32)


def _rwkv_kernel(x_ref, gain_ref, wz_ref, mu_ref, w0_ref, w2_ref, a0_ref, a2_ref, g2_ref, kk_ref, ka_ref,
                 lnw_ref, lnb_ref, rk_ref, seg_ref, o_ref, blank_ref, s_ref, zlast_ref, *, chunk):
    @pl.when(pl.program_id(1) == 0)
    def _():
        s_ref[...] = jnp.zeros_like(s_ref)
        zlast_ref[...] = jnp.zeros_like(zlast_ref)

    tb = x_ref.shape[1]
    d = RWKV_DIM
    z = jnp.dot(_rms(x_ref[0], gain_ref[...]).astype(BF16), wz_ref[...], preferred_element_type=F32)
    last = zlast_ref[7:8, :]
    zlast_ref[...] = z[tb - 8:, :]
    r, lw, k, v, kk, b, g = _rwkv_inputs(z, last, mu_ref[...], w0_ref[...], w2_ref[...], a0_ref[...],
                                         a2_ref[...], g2_ref[...], kk_ref[...], ka_ref[...], seg_ref[...])
    nc, nh = tb // chunk, d // HEAD_DIM
    ii = lax.broadcasted_iota(jnp.int32, (chunk, chunk), 0)
    jj = lax.broadcasted_iota(jnp.int32, (chunk, chunk), 1)
    incl, strict = ii >= jj, ii > jj
    tri = jnp.where(incl, 1.0, 0.0).astype(BF16)
    eye = (ii == jj).astype(F32)
    rows = lambda c: slice(c * chunk, (c + 1) * chunk)
    lanes = lambda h: slice(h * HEAD_DIM, (h + 1) * HEAD_DIM)

    def split(x):
        return jnp.stack([x[rows(c), lanes(h)] for c in range(nc) for h in range(nh)])

    cums = [_prefix_sum(tri, lw[rows(c)]) for c in range(nc)]
    cum = jnp.concatenate(cums, axis=0)
    total = jnp.concatenate([jnp.broadcast_to(cs[chunk - 1:], (chunk, d)) for cs in cums], axis=0)
    g_inv = jnp.exp(-cum)
    g_tail = jnp.exp(total - cum)
    kq, rq = split(kk * jnp.exp(cum - lw)), split(r * jnp.exp(cum))
    bi, ki = split(b * g_inv), split(k * g_inv)
    bh, kh = split(b * g_tail), split(k * g_tail)
    vs = split(v)
    decay = jnp.stack([jnp.exp(cs[chunk - 1:, lanes(h)]) for cs in cums for h in range(nh)])

    qr = jnp.concatenate([kq, rq], axis=1)
    ab, ak = _bmm_nt(qr, bi), _bmm_nt(qr, ki)
    a_b, a_rb = jnp.where(strict, ab[:, :chunk], 0.0), jnp.where(incl, ab[:, chunk:], 0.0)
    a_k, a_rk = jnp.where(strict, ak[:, :chunk], 0.0), jnp.where(incl, ak[:, chunk:], 0.0)
    n = nc * nh
    is_b = lax.broadcasted_iota(jnp.int32, (chunk, 2 * chunk), 1) < chunk
    bt = jnp.concatenate([-a_b, jnp.broadcast_to(eye, (n, chunk, chunk))], axis=2)
    for _ in range(int(np.log2(chunk))):
        prod = _bmm(bt[:, :, :chunk], bt)
        bt = jnp.where(is_b, prod, bt + prod)
    inv = bt[:, :, chunk:]
    wu = _bmm(inv, jnp.concatenate([kq, -_bmm(a_k, vs)], axis=2))
    aw = _bmm(a_rb, wu)
    qh = rq - aw[:, :, :HEAD_DIM]
    y0 = aw[:, :, HEAD_DIM:] + _bmm(a_rk, vs)
    mu = _bmm_tn(wu, bh)
    mp = mu[:, :HEAD_DIM]
    s_add = mu[:, HEAD_DIM:] + _bmm_tn(vs, kh)

    state = s_ref[...]
    for c in range(nc):
        sl = slice(c * nh, (c + 1) * nh)
        y = _bmm_nt(qh[sl], state) + y0[sl]
        for h in range(nh):
            o_ref[0, rows(c), lanes(h)] = y[h]
        state = state * decay[sl] - _bmm(state, mp[sl]) + s_add[sl]
    s_ref[...] = state

    seg = seg_ref[...]
    y = o_ref[0]
    yc = y - _seg_sum(y, seg) * (1.0 / HEAD_DIM)
    var = _seg_sum(yc * yc, seg) * (1.0 / HEAD_DIM)
    y = yc * lax.rsqrt(var + RWKV_GN_EPS) * lnw_ref[...] + lnb_ref[...]
    y = y + _seg_sum(r * k * rk_ref[...], seg) * v
    o_ref[0] = y * g
    blank_ref[...] = jnp.zeros_like(blank_ref)


def _rwkv_branch(x, gain, wz, mu, w0, w2p, a0, a2p, g2p, k_k, k_a, ln_w, ln_b, r_k, seg_ones, blank_rows,
                 *, tb):
    bsz, seq, dm = x.shape
    d = RWKV_DIM
    steps = seq // tb
    full = lambda a: pl.BlockSpec(a.shape, lambda bi, t: (0,) * a.ndim)
    params = (gain, wz, mu, w0, w2p, a0, a2p, g2p, k_k, k_a, ln_w, ln_b, r_k, seg_ones)
    blank_blk = blank_rows // (bsz * steps)
    return pl.pallas_call(
        functools.partial(_rwkv_kernel, chunk=WKV_CHUNK),
        grid=(bsz, steps),
        in_specs=[pl.BlockSpec((1, tb, dm), lambda bi, t: (bi, t, 0))] + [full(a) for a in params],
        out_specs=[pl.BlockSpec((1, tb, d), lambda bi, t: (bi, t, 0)),
                   pl.BlockSpec((blank_blk, LANES), lambda bi, t: (bi * steps + t, 0))],
        out_shape=[jax.ShapeDtypeStruct((bsz, seq, d), F32),
                   jax.ShapeDtypeStruct((blank_rows, LANES), F32)],
        scratch_shapes=[pltpu.VMEM((d // HEAD_DIM, HEAD_DIM, HEAD_DIM), F32),
                        pltpu.VMEM((8, RWKV_COLS), F32)],
        compiler_params=_params("parallel", "arbitrary"),
        name="rwkv_branch",
    )(x, *params)


def _attn_kernel(q_ref, kp_ref, kc_ref, vp_ref, vc_ref, bias_ref, o_ref, l_ref):
    first = pl.program_id(2) == 0
    q = q_ref[0, 0] * (HEAD_DIM ** -0.5)
    kc, vc = kc_ref[0, 0], vc_ref[0, 0]
    subs = q.shape[0] // Q_BLOCK
    rows = lambda j: slice(j * Q_BLOCK, (j + 1) * Q_BLOCK)
    lanes = lambda h: slice(h * HEAD_DIM, (h + 1) * HEAD_DIM)
    k_prev = [kp_ref[0, 0]] + [kc[rows(j)] for j in range(subs - 1)]
    v_prev = [vp_ref[0, 0]] + [vc[rows(j)] for j in range(subs - 1)]
    work = [(j, h) for j in range(subs) for h in range(HEADS_PER_GROUP)]
    scores = []
    for j, h in work:
        bias = bias_ref[h]
        bias_prev = bias[:, :Q_BLOCK]
        if j == 0:
            bias_prev = jnp.where(first, MASKED, bias_prev)
        qh = q[rows(j), lanes(h)]
        scores.append((_mm_nt(qh, k_prev[j][:, lanes(h)]) + bias_prev,
                       _mm_nt(qh, kc[rows(j), lanes(h)]) + bias[:, Q_BLOCK:]))
    tops = [jnp.max(jnp.maximum(sp, sc), axis=-1, keepdims=True) for sp, sc in scores]
    probs = [(jnp.exp(sp - m), jnp.exp(sc - m)) for (sp, sc), m in zip(scores, tops)]
    dens = [jnp.sum(pp + pc, axis=-1, keepdims=True) for pp, pc in probs]
    accs = [_mm(pp, v_prev[j][:, lanes(h)]) + _mm(pc, vc[rows(j), lanes(h)])
            for (pp, pc), (j, h) in zip(probs, work)]
    lane = lax.broadcasted_iota(jnp.int32, (Q_BLOCK, LANES), 1)
    for j in range(subs):
        lse = jnp.zeros((Q_BLOCK, LANES), F32)
        for h in range(HEADS_PER_GROUP):
            i = j * HEADS_PER_GROUP + h
            o_ref[0, 0, rows(j), lanes(h)] = accs[i] * (1.0 / dens[i])
            lse = jnp.where(lane == h, tops[i] + jnp.log(dens[i]), lse)
        l_ref[0, 0, rows(j), :] = lse


def _attn_bias(group):
    window, dilation = ATTN_GROUPS[group]
    n_back = window // dilation
    heads = np.arange(group * HEADS_PER_GROUP, (group + 1) * HEADS_PER_GROUP)
    slopes = (2.0 ** (-8.0 * (heads + 1) / ATTN_HEADS)).astype(np.float32)
    delta = (np.arange(Q_BLOCK)[:, None] + Q_BLOCK) - np.arange(2 * Q_BLOCK)[None, :]
    valid = (delta >= 0) & (delta <= n_back)
    alibi = -slopes[:, None, None] * (dilation * delta).astype(np.float32)[None]
    return jnp.asarray(np.where(valid[None], alibi, np.float32(MASKED)).astype(np.float32))


def _attn_group(qkv, group):
    bsz, dil, comp, _ = qkv.shape
    tq = min(ATTN_TQ, comp)
    assert comp % tq == 0 and tq % Q_BLOCK == 0
    sub = tq // Q_BLOCK
    col = lambda which: (lambda bi, r, n: (bi, r, n, which))
    col_prev = lambda which: (lambda bi, r, n: (bi, r, jnp.maximum(n * sub - 1, 0), which))
    cur, prev = (1, 1, tq, ATTN_OUT_DIM), (1, 1, Q_BLOCK, ATTN_OUT_DIM)
    bias = _attn_bias(group)
    return pl.pallas_call(
        _attn_kernel,
        grid=(bsz, dil, comp // tq),
        in_specs=[pl.BlockSpec(cur, col(0)),
                  pl.BlockSpec(prev, col_prev(1)), pl.BlockSpec(cur, col(1)),
                  pl.BlockSpec(prev, col_prev(2)), pl.BlockSpec(cur, col(2)),
                  pl.BlockSpec(bias.shape, lambda bi, r, n: (0, 0, 0))],
        out_specs=[pl.BlockSpec(cur, col(0)),
                   pl.BlockSpec((1, 1, tq, LANES), col(0))],
        out_shape=[jax.ShapeDtypeStruct((bsz, dil, comp, ATTN_OUT_DIM), F32),
                   jax.ShapeDtypeStruct((bsz, dil, comp, LANES), F32)],
        compiler_params=_params("parallel", "parallel", "parallel"),
        name=f"dilated_attn_{group}",
    )(qkv, qkv, qkv, qkv, qkv, bias)


def _route(logits):
    lane = lax.broadcasted_iota(jnp.int32, logits.shape, 1)
    neg = -jnp.inf
    first_max = lambda vals, mx: jnp.min(jnp.where(vals == mx, lane, LANES), axis=-1, keepdims=True)
    is_grp = (lane >= N_EXPERTS) & (lane < N_EXPERTS + N_EXPERT_GROUPS)
    gl = jnp.where(is_grp, logits, neg)
    gmax = jnp.max(gl, axis=-1, keepdims=True)
    grp_w = 1.0 / jnp.sum(jnp.exp(gl - gmax), axis=-1, keepdims=True)
    grp = first_max(gl, gmax) - N_EXPERTS
    in_grp = (lane >= grp * EXPERTS_PER_GROUP) & (lane < (grp + 1) * EXPERTS_PER_GROUP)
    el = jnp.where(in_grp, logits, neg)
    v1 = jnp.max(el, axis=-1, keepdims=True)
    i1 = first_max(el, v1)
    el2 = jnp.where(lane == i1, neg, el)
    v2 = jnp.max(el2, axis=-1, keepdims=True)
    i2 = first_max(el2, v2)
    e2 = jnp.exp(v2 - v1)
    w1 = grp_w / (1.0 + e2)
    return i1, i2, w1, w1 * e2


def _merge_kernel(x_ref, ya_ref, o0_ref, o1_ref, o2_ref, l0_ref, l1_ref, l2_ref, gmix_ref, wgt_ref,
                  wa_ref, wb_ref, wo_ref, gain_ref, wr_ref, ex_ref, tri_ref, x1_ref, h_ref, route_ref,
                  count_ref, o_scr, l_scr):
    outs, lses = [], []
    for grp, (o_ref, l_ref) in enumerate(((o0_ref, l0_ref), (o1_ref, l1_ref), (o2_ref, l2_ref))):
        dil = o_ref.shape[1]
        if dil == 1:
            outs.append(o_ref[0, 0])
            lses.append(l_ref[0, 0])
            continue
        n = o_ref.shape[2]
        chunks = o_ref.shape[3] // LANES
        for r in range(dil):
            l_scr[grp - 1, pl.ds(r, n, stride=dil), :] = l_ref[0, r]
            for c in range(chunks):
                o_scr[grp - 1, c, pl.ds(r, n, stride=dil), :] = o_ref[0, r, :, c * LANES:(c + 1) * LANES]
        outs.append(jnp.concatenate([o_scr[grp - 1, c] for c in range(chunks)], axis=1))
        lses.append(l_scr[grp - 1])
    def mix(rows):
        l0, l1, l2 = (l[rows] for l in lses)
        m = jnp.maximum(jnp.maximum(l0, l1), l2)
        e0, e1, e2 = jnp.exp(l0 - m), jnp.exp(l1 - m), jnp.exp(l2 - m)
        inv = 1.0 / (e0 + e1 + e2)
        spread = ex_ref[...]
        yb = (_seg_sum(e0 * inv, spread) * outs[0][rows] + _seg_sum(e1 * inv, spread) * outs[1][rows]
              + _seg_sum(e2 * inv, spread) * outs[2][rows])
        x = x_ref[rows, :]
        gates = _sigmoid(_mm(_rms(x, gmix_ref[...]), wgt_ref[...]))
        merged = (gates[:, :D_MODEL] * _mm(ya_ref[rows, :], wa_ref[...])
                  + gates[:, D_MODEL:] * _mm(yb, wb_ref[...]))
        x1 = x + _mm(merged, wo_ref[...])
        h = _rms(x1, gain_ref[...])
        return (x1, h) + _route(_mm_split(h, wr_ref[...]))

    half = x_ref.shape[0] // 2
    parts = [mix(slice(0, half)), mix(slice(half, 2 * half))]
    x1, h, i1, i2, w1, w2 = (jnp.concatenate(both, axis=0) for both in zip(*parts))
    x1_ref[...] = x1
    h_ref[...] = h

    @pl.when(pl.program_id(0) == 0)
    def _():
        count_ref[...] = jnp.zeros_like(count_ref)

    lane = lax.broadcasted_iota(jnp.int32, (x1.shape[0], LANES), 1)
    onehot = jnp.where((lane == i1) | (lane == i2), 1.0, 0.0)
    before = count_ref[0:1, :] + jnp.dot(tri_ref[...], onehot.astype(BF16), preferred_element_type=F32)
    rank1 = jnp.sum(jnp.where(lane == i1, before, 0.0), axis=-1, keepdims=True)
    rank2 = jnp.sum(jnp.where(lane == i2, before, 0.0), axis=-1, keepdims=True)
    last = x1.shape[0] - 1
    count_ref[...] = jnp.broadcast_to(before[last:, :] + onehot[last:, :], count_ref.shape)
    cols = (i1.astype(F32), i2.astype(F32), w1, w2, rank1, rank2)
    info = jnp.zeros(lane.shape, F32)
    for c, col in enumerate(cols):
        info = jnp.where(lane == c, col, info)
    route_ref[...] = info


def _merge(x, ya, outs, lses, gain_mix, w_gates, wa, wb, wo, gain, wr, spread, seq, *, tm):
    t = x.shape[0]
    tps = seq // tm
    tok = lambda n: pl.BlockSpec((tm, n), lambda i: (i, 0))
    full = lambda a: pl.BlockSpec(a.shape, lambda i: (0,) * a.ndim)
    res = lambda a: pl.BlockSpec((1, a.shape[1], tm // a.shape[1], a.shape[3]),
                                 lambda i: (i // tps, 0, i % tps, 0))
    dilated = len(ATTN_GROUPS) - 1
    tri = jnp.asarray(np.tri(tm, k=-1), dtype=BF16)
    return pl.pallas_call(
        _merge_kernel,
        grid=(t // tm,),
        in_specs=[tok(D_MODEL), tok(RWKV_DIM)] + [res(a) for a in outs] + [res(a) for a in lses] + [
                  full(gain_mix), full(w_gates), full(wa), full(wb), full(wo), full(gain), full(wr), full(spread),
                  full(tri)],
        out_specs=[tok(D_MODEL), tok(D_MODEL), tok(LANES), pl.BlockSpec((8, LANES), lambda i: (0, 0))],
        out_shape=[jax.ShapeDtypeStruct((t, D_MODEL), F32), jax.ShapeDtypeStruct((t, D_MODEL), F32),
                   jax.ShapeDtypeStruct((t, LANES), F32), jax.ShapeDtypeStruct((8, LANES), F32)],
        scratch_shapes=[pltpu.VMEM((dilated, ATTN_OUT_DIM // LANES, tm, LANES), F32),
                        pltpu.VMEM((dilated, tm, LANES), F32)],
        compiler_params=_params("arbitrary"),
        name="merge_route",
    )(x, ya, *outs, *lses, gain_mix, w_gates, wa, wb, wo, gain, wr, spread, tri)


ROUTE_E1, ROUTE_E2, ROUTE_W1, ROUTE_W2, ROUTE_R1, ROUTE_R2 = range(6)


def _moe_plan(counts, n_tiles):
    counts = counts[0, :N_EXPERTS].astype(jnp.int32)
    tile_end = jnp.cumsum((counts + MOE_TILE - 1) // MOE_TILE)
    n_valid = tile_end[-1]
    tile = jnp.minimum(jnp.arange(n_tiles, dtype=jnp.int32), n_valid - 1)
    tile_expert = jnp.sum(tile[:, None] >= tile_end[None, :], axis=1).astype(jnp.int32)
    return tile_expert, n_valid.reshape(1)


def _row_tiles(ref, rows, tile=ROW_TILE):
    return jnp.concatenate([ref[pl.ds(c, rows, stride=tile), :] for c in range(tile)], axis=1)


def _store_row_tiles(ref, x, tile=ROW_TILE):
    for c in range(tile):
        ref[pl.ds(c, x.shape[0], stride=tile), :] = x[:, c * LANES:(c + 1) * LANES]


def _row(ref, r, tile=ROW_TILE):
    return ref.at[pl.ds(pl.multiple_of(r * tile, tile), tile)]


def _dispatch_kernel(route_ref, count_ref, upper_ref, h_ref, xs_in_ref, xs_ref, pos_ref, rows_ref, pos_vmem,
                     pos_smem, sem, pos_sem):
    del xs_in_ref
    tm = h_ref.shape[0]
    tiles = jnp.floor((count_ref[...] + (MOE_TILE - 1)) * (1.0 / MOE_TILE))
    first_row = jnp.dot(tiles.astype(BF16), upper_ref[...], preferred_element_type=F32)[0:1, :] * MOE_TILE
    route = route_ref[...]
    lane = lax.broadcasted_iota(jnp.int32, route.shape, 1)
    col = lambda c: route[:, c:c + 1]
    place = lambda e, r: jnp.sum(jnp.where(lane == col(e).astype(jnp.int32), first_row, 0.0), axis=-1,
                                 keepdims=True) + col(r)
    both = jnp.where(lane == 0, place(ROUTE_E1, ROUTE_R1), jnp.where(lane == 1, place(ROUTE_E2, ROUTE_R2), 0.0))
    pos = both.T[0:ROW_TILE, :].astype(jnp.int32)
    pos_ref[0] = pos[0:2]
    pos_vmem[...] = pos
    to_smem = pltpu.make_async_copy(pos_vmem, pos_smem, pos_sem)
    to_smem.start()
    _store_row_tiles(rows_ref, h_ref[...])
    to_smem.wait()

    def copies(j):
        src = _row(rows_ref, j)
        return (pltpu.make_async_copy(src, _row(xs_ref, pos_smem[0, j]), sem.at[0]),
                pltpu.make_async_copy(src, _row(xs_ref, pos_smem[1, j]), sem.at[1]))

    def start(j, carry):
        for stream, cp in enumerate(copies(j)):
            cp.start(priority=stream)
        return carry

    lax.fori_loop(0, tm, start, 0, unroll=8)
    for stream in range(2):
        pltpu.make_async_copy(rows_ref, rows_ref, sem.at[stream]).wait()


def _dispatch(h, route, counts, blank, *, tm):
    t, d = h.shape
    upper = jnp.asarray(np.triu(np.ones((LANES, LANES)), k=1), dtype=BF16)
    tok = lambda n: pl.BlockSpec((tm, n), lambda i: (i, 0))
    full = lambda a: pl.BlockSpec(a.shape, lambda i: (0, 0))
    return pl.pallas_call(
        _dispatch_kernel,
        grid=(t // tm,),
        in_specs=[tok(LANES), full(counts), full(upper), tok(d), pl.BlockSpec(memory_space=pl.ANY)],
        out_specs=[pl.BlockSpec(memory_space=pl.ANY), pl.BlockSpec((1, 2, tm), lambda i: (i, 0, 0))],
        out_shape=[jax.ShapeDtypeStruct(blank.shape, blank.dtype),
                   jax.ShapeDtypeStruct((t // tm, 2, tm), jnp.int32)],
        scratch_shapes=[pltpu.VMEM((tm * ROW_TILE, LANES), F32), pltpu.VMEM((ROW_TILE, tm), jnp.int32),
                        pltpu.SMEM((ROW_TILE, tm), jnp.int32), pltpu.SemaphoreType.DMA((2,)),
                        pltpu.SemaphoreType.DMA],
        input_output_aliases={4: 0},
        compiler_params=_params("arbitrary"),
        name="moe_dispatch",
    )(route, counts, upper, h, blank)


def _expert_kernel(te_ref, nv_ref, x_ref, wg_ref, wu_ref, wd_ref, o_ref):
    live = pl.program_id(0) < nv_ref[0]

    @pl.when(live)
    def _():
        x = _row_tiles(x_ref, MOE_TILE)
        gate = _mm(x, wg_ref[0])
        up = _mm(x, wu_ref[0])
        _store_row_tiles(o_ref, _mm(gate * _sigmoid(gate) * up, wd_ref[0]))

    @pl.when(jnp.logical_not(live))
    def _():
        o_ref[...] = jnp.zeros_like(o_ref)


def _experts(xs, tile_expert, n_valid, wg, wu, wd):
    d = wg.shape[1]
    weight = lambda shape: pl.BlockSpec((1,) + shape, lambda i, te, nv: (te[i], 0, 0))
    blk = (MOE_TILE * ROW_TILE, LANES)
    return pl.pallas_call(
        _expert_kernel,
        grid_spec=pltpu.PrefetchScalarGridSpec(
            num_scalar_prefetch=2,
            grid=(xs.shape[0] // blk[0],),
            in_specs=[pl.BlockSpec(blk, lambda i, te, nv: (jnp.minimum(i, nv[0] - 1), 0)),
                      weight((d, D_FF_EXPERT)), weight((d, D_FF_EXPERT)), weight((D_FF_EXPERT, d))],
            out_specs=pl.BlockSpec(blk, lambda i, te, nv: (i, 0))),
        out_shape=jax.ShapeDtypeStruct(xs.shape, F32),
        compiler_params=_params("arbitrary"),
        name="moe_experts",
    )(tile_expert, n_valid, xs, wg, wu, wd)


def _ple_kernel(pos_ref, x_ref, route_ref, p_ref, gp_ref, wg_ref, wp_ref, gf_ref, ys_ref, o_ref,
                y_ref, sem, *, pos_tm):
    tm = x_ref.shape[0]
    step, steps = pl.program_id(0), pl.num_programs(0)
    per_pos_tile = pos_tm // tm

    def gather(tile, slot, inline):
        base = (tile // per_pos_tile) * (2 * pos_tm) + (tile % per_pos_tile) * tm

        def body(j, carry):
            for stream in range(2):
                pltpu.make_async_copy(_row(ys_ref, pos_ref[base + stream * pos_tm + j]),
                                      _row(y_ref.at[slot, stream], j), sem.at[slot, stream]
                                      ).start(priority=stream)
            return carry

        if inline:
            for j in range(tm):
                body(j, 0)
        else:
            lax.fori_loop(0, tm, body, 0, unroll=8)

    def drain(slot):
        for stream in range(2):
            buf = y_ref.at[slot, stream]
            pltpu.make_async_copy(buf, buf, sem.at[slot, stream]).wait()

    slot = step % 2

    @pl.when(step == 0)
    def _():
        gather(0, 0, False)

    drain(slot)
    gather(jnp.minimum(step + 1, steps - 1), 1 - slot, True)
    route = route_ref[...]
    w1, w2 = route[:, ROUTE_W1:ROUTE_W1 + 1], route[:, ROUTE_W2:ROUTE_W2 + 1]
    x = x_ref[...] + (w1 * _row_tiles(y_ref.at[slot, 0], tm) + w2 * _row_tiles(y_ref.at[slot, 1], tm))
    gate = _sigmoid(_mm(_rms(x, gp_ref[...]), wg_ref[...]))
    x = x + gate * _mm(p_ref[...], wp_ref[...])
    o_ref[...] = _rms(x, gf_ref[...])

    @pl.when(step == steps - 1)
    def _():
        drain(1 - slot)


def _ple(x, route, pos, ys, p, gain_ple, w_gate, w_proj, gain_final, *, tm):
    t = x.shape[0]
    pos_tm = pos.shape[2]
    assert pos_tm % tm == 0
    tok = lambda n: pl.BlockSpec((tm, n), lambda i, ps: (i, 0))
    full = lambda a: pl.BlockSpec(a.shape, lambda i, ps: (0, 0))
    return pl.pallas_call(
        functools.partial(_ple_kernel, pos_tm=pos_tm),
        grid_spec=pltpu.PrefetchScalarGridSpec(
            num_scalar_prefetch=1,
            grid=(t // tm,),
            in_specs=[tok(D_MODEL), tok(LANES), tok(PLE_DIM), full(gain_ple), full(w_gate), full(w_proj),
                      full(gain_final), pl.BlockSpec(memory_space=pl.ANY)],
            out_specs=tok(D_MODEL),
            scratch_shapes=[pltpu.VMEM((2, 2, tm * ROW_TILE, LANES), F32), pltpu.SemaphoreType.DMA((2, 2))]),
        out_shape=jax.ShapeDtypeStruct((t, D_MODEL), F32),
        compiler_params=_params("arbitrary"),
        name="ple_final",
    )(pos.reshape(-1), x, route, p, gain_ple, w_gate, w_proj, gain_final, ys)


def _pad_rows(w, start, total):
    return jnp.zeros((total, w.shape[1]), w.dtype).at[start:start + w.shape[0]].set(w)


def _layer(x, p, seq, norm_mix, w_in, rwkv_mu, rwkv_w0, rwkv_w2, rwkv_a0, rwkv_a2, rwkv_g2, rwkv_k_k,
           rwkv_k_a, rwkv_r_k, rwkv_ln_w, rwkv_ln_b, w_proj_a, w_proj_b, w_out, norm_moe, w_router_group,
           w_router_expert, w_exp_gate, w_exp_up, w_exp_down, norm_ple, w_ple_gate, w_ple_proj, norm_final):
    t = x.shape[0]
    bsz = t // seq
    row = lambda a: a.reshape(1, -1)
    c0, c1 = RWKV_COLS, RWKV_COLS + ATTN_COLS
    x3 = x.reshape(bsz, seq, D_MODEL)

    def w_group(grp):
        lo = lambda which: c0 + which * ATTN_DIM + grp * ATTN_OUT_DIM
        return jnp.concatenate([w_in[:, lo(which):lo(which) + ATTN_OUT_DIM] for which in range(3)],
                               axis=1).astype(BF16)

    qkv = [_qkv_proj(x3, row(norm_mix), w_group(grp), grp, tm=1024) for grp in range(len(ATTN_GROUPS))]

    seg = np.arange(RWKV_DIM) // HEAD_DIM
    seg_ones = jnp.asarray(seg[:, None] == seg[None, :], dtype=BF16)
    n_tiles = 2 * t // MOE_TILE + N_EXPERTS
    ya, blank = _rwkv_branch(
        x3, row(norm_mix), w_in[:, :c0].astype(BF16), row(rwkv_mu), row(rwkv_w0),
        _hi_lo(_pad_rows(rwkv_w2, 0, LORA_COLS)), row(rwkv_a0), _hi_lo(_pad_rows(rwkv_a2, W_LORA, LORA_COLS)),
        _hi_lo(_pad_rows(rwkv_g2, W_LORA + A_LORA, LORA_COLS)), row(rwkv_k_k), row(rwkv_k_a), row(rwkv_ln_w),
        row(rwkv_ln_b), row(rwkv_r_k), seg_ones, n_tiles * MOE_TILE * ROW_TILE, tb=256)
    ya = ya.reshape(t, RWKV_DIM)

    attn = [_attn_group(qkv[grp], grp) for grp in range(len(ATTN_GROUPS))]

    head_of_lane = np.arange(ATTN_OUT_DIM) // HEAD_DIM
    spread = jnp.asarray(np.arange(LANES)[:, None] == head_of_lane[None, :], dtype=BF16)
    w_route = jnp.concatenate(
        [jnp.moveaxis(w_router_expert, 0, 1).reshape(D_MODEL, N_EXPERTS), w_router_group,
         jnp.zeros((D_MODEL, LANES - N_EXPERTS - N_EXPERT_GROUPS), F32)], axis=1)
    x1, h_moe, route, counts = _merge(x, ya, [o for o, _ in attn], [l for _, l in attn],
                             row(norm_mix), w_in[:, c1:].astype(BF16),
                             w_proj_a.astype(BF16), w_proj_b.astype(BF16), w_out.astype(BF16),
                             row(norm_moe), _hi_lo(w_route), spread, seq, tm=512)
    tile_expert, n_valid = _moe_plan(counts, n_tiles)
    dispatch_tm = 2048
    xs, pos = _dispatch(h_moe, route, counts, blank, tm=dispatch_tm)
    ys = _experts(xs, tile_expert, n_valid,
                  w_exp_gate.reshape(N_EXPERTS, D_MODEL, D_FF_EXPERT),
                  w_exp_up.reshape(N_EXPERTS, D_MODEL, D_FF_EXPERT),
                  w_exp_down.reshape(N_EXPERTS, D_FF_EXPERT, D_MODEL))
    return _ple(x1, route, pos, ys, p, row(norm_ple), w_ple_gate.astype(BF16),
                w_ple_proj.astype(BF16), row(norm_final), tm=512)


def kernel(x, p, norm_mix, w_in, rwkv_mu, rwkv_w0, rwkv_w2, rwkv_a0, rwkv_a2, rwkv_g2, rwkv_k_k, rwkv_k_a, rwkv_r_k, rwkv_ln_w, rwkv_ln_b, w_proj_a, w_proj_b, w_out, norm_moe, w_router_group, w_router_expert, w_exp_gate, w_exp_up, w_exp_down, norm_ple, w_ple_gate, w_ple_proj, norm_final):
    bsz, seq, d = x.shape
    depth = w_in.shape[0]
    assert depth == 1, "the final norm is fused into the (single) layer"
    out = _layer(x.reshape(bsz * seq, d).astype(F32), p[0].reshape(bsz * seq, PLE_DIM), seq,
                 norm_mix[0], w_in[0], rwkv_mu[0], rwkv_w0[0], rwkv_w2[0], rwkv_a0[0], rwkv_a2[0],
                 rwkv_g2[0], rwkv_k_k[0], rwkv_k_a[0], rwkv_r_k[0], rwkv_ln_w[0], rwkv_ln_b[0],
                 w_proj_a[0], w_proj_b[0], w_out[0], norm_moe[0], w_router_group[0], w_router_expert[0],
                 w_exp_gate[0], w_exp_up[0], w_exp_down[0], norm_ple[0], w_ple_gate[0], w_ple_proj[0],
                 norm_final)
    return out.reshape(bsz, seq, d)
```

```python
import functools

import numpy as np
import jax
import jax.numpy as jnp
from jax import lax
from jax.experimental import pallas as pl
from jax.experimental.pallas import tpu as pltpu

F32 = jnp.float32
BF16 = jnp.bfloat16

D_MODEL = 1024
PLE_DIM = 256
HEAD_DIM = 64
RWKV_HEADS = 8
RWKV_DIM = RWKV_HEADS * HEAD_DIM
W_LORA, A_LORA, G_LORA = 64, 64, 128
LORA_COLS = W_LORA + A_LORA + G_LORA
ATTN_GROUPS = ((128, 1), (512, 4), (2048, 16))
HEADS_PER_GROUP = 8
ATTN_HEADS = HEADS_PER_GROUP * len(ATTN_GROUPS)
ATTN_DIM = ATTN_HEADS * HEAD_DIM
ATTN_OUT_DIM = HEADS_PER_GROUP * HEAD_DIM
Q_BLOCK = 128
ATTN_TQ = 1024
RWKV_COLS = 3 * RWKV_DIM + LORA_COLS
ATTN_COLS = 3 * ATTN_DIM
N_EXPERT_GROUPS = 4
EXPERTS_PER_GROUP = 8
N_EXPERTS = N_EXPERT_GROUPS * EXPERTS_PER_GROUP
D_FF_EXPERT = 256
NORM_EPS = 1e-6
RWKV_GN_EPS = 64e-5
LANES = 128
MASKED = -1e30
WKV_CHUNK = 64
VMEM_LIMIT = 48 * 1024 * 1024
MOE_TILE = 512
ROW_TILE = 8


def _mm(a, b):
    return jnp.dot(a.astype(BF16), b.astype(BF16), preferred_element_type=F32)


def _mm_nt(a, b):
    return lax.dot_general(a.astype(BF16), b.astype(BF16), (((1,), (1,)), ((), ())),
                           preferred_element_type=F32)


def _mm_split(a, b):
    a_hi, b_hi = a.astype(BF16), b.astype(BF16)
    a_lo = (a - a_hi.astype(F32)).astype(BF16)
    b_lo = (b - b_hi.astype(F32)).astype(BF16)
    dot = functools.partial(jnp.dot, preferred_element_type=F32)
    return dot(a_hi, b_hi) + (dot(a_hi, b_lo) + dot(a_lo, b_hi))


def _sigmoid(x):
    return 1.0 / (1.0 + jnp.exp(-x))


def _rms(x, gain):
    return x * lax.rsqrt(jnp.mean(x * x, axis=-1, keepdims=True) + NORM_EPS) * gain


def _params(*sem):
    return pltpu.CompilerParams(dimension_semantics=sem, vmem_limit_bytes=VMEM_LIMIT)


def _qkv_rows_kernel(x_ref, g_ref, w_ref, o_ref, stage_ref, *, dil):
    h = _rms(x_ref[0], g_ref[...])
    if dil == 1:
        o_ref[0, 0] = _mm(h, w_ref[...]).astype(BF16)
        return
    chunks = h.shape[1] // LANES
    for c in range(chunks):
        stage_ref[c] = h[:, c * LANES:(c + 1) * LANES]
    n = h.shape[0] // dil
    for r in range(dil):
        rows = jnp.concatenate([stage_ref[c, pl.ds(r, n, stride=dil), :] for c in range(chunks)], axis=1)
        o_ref[0, r] = _mm(rows, w_ref[...]).astype(BF16)


def _qkv_strided_kernel(x_hbm, g_ref, w_ref, o_ref, x_buf, sem):
    tm = x_buf.shape[1]
    n_res, n_blk = pl.num_programs(1), pl.num_programs(2)
    step = (pl.program_id(0) * n_res + pl.program_id(1)) * n_blk + pl.program_id(2)
    steps = pl.num_programs(0) * n_res * n_blk

    def fetch(s, slot):
        b, r, i = s // (n_res * n_blk), (s // n_blk) % n_res, s % n_blk
        return pltpu.make_async_copy(x_hbm.at[b, pl.ds(i * tm, tm), r], x_buf.at[slot], sem.at[slot])

    slot = step % 2

    @pl.when(step == 0)
    def _():
        fetch(0, 0).start()

    @pl.when(step + 1 < steps)
    def _():
        fetch(step + 1, 1 - slot).start()

    fetch(step, slot).wait()
    o_ref[0, 0] = _mm(_rms(x_buf[slot], g_ref[...]), w_ref[...]).astype(BF16)


def _qkv_proj(x, gain, w, group, *, tm):
    bsz, seq, d = x.shape
    dil = ATTN_GROUPS[group][1]
    comp = seq // dil
    tn = w.shape[1]
    out_shape = jax.ShapeDtypeStruct((bsz, dil, comp, tn), BF16)
    if dil % ROW_TILE:
        return pl.pallas_call(
            functools.partial(_qkv_rows_kernel, dil=dil),
            grid=(bsz, seq // tm),
            in_specs=[pl.BlockSpec((1, tm, d), lambda b, i: (b, i, 0)),
                      pl.BlockSpec((1, d), lambda b, i: (0, 0)),
                      pl.BlockSpec((d, tn), lambda b, i: (0, 0))],
            out_specs=pl.BlockSpec((1, dil, tm // dil, tn), lambda b, i: (b, 0, i, 0)),
            out_shape=out_shape,
            scratch_shapes=[pltpu.VMEM((d // LANES, tm, LANES), F32)],
            compiler_params=_params("parallel", "parallel"),
            name=f"qkv_proj_{group}",
        )(x, gain, w)
    tm = min(tm, comp)
    return pl.pallas_call(
        _qkv_strided_kernel,
        grid=(bsz, dil, comp // tm),
        in_specs=[pl.BlockSpec(memory_space=pl.ANY),
                  pl.BlockSpec((1, d), lambda b, r, i: (0, 0)),
                  pl.BlockSpec((d, tn), lambda b, r, i: (0, 0))],
        out_specs=pl.BlockSpec((1, 1, tm, tn), lambda b, r, i: (b, r, i, 0)),
        out_shape=out_shape,
        scratch_shapes=[pltpu.VMEM((2, tm, d), F32), pltpu.SemaphoreType.DMA((2,))],
        compiler_params=_params("arbitrary", "arbitrary", "arbitrary"),
        name=f"qkv_proj_{group}",
    )(x.reshape(bsz, comp, dil, d), gain, w)


def _seg_sum(x, seg_ones):
    hi = x.astype(BF16)
    lo = (x - hi.astype(F32)).astype(BF16)
    return (jnp.dot(hi, seg_ones, preferred_element_type=F32)
            + jnp.dot(lo, seg_ones, preferred_element_type=F32))


def _rwkv_inputs(z, last, mu, w0, w2, a0, a2, g2, k_k, k_a, seg):
    row = lax.broadcasted_iota(jnp.int32, z.shape, 0)
    prev = jnp.where(row == 0, last, pltpu.roll(z, 1, axis=0))
    zs = z + (prev - z) * mu
    d = RWKV_DIM
    r, k, v = zs[:, :d], zs[:, d:2 * d], zs[:, 2 * d:3 * d]
    lora = zs[:, 3 * d:]
    lane = lax.broadcasted_iota(jnp.int32, lora.shape, 1)
    lora = jnp.where(lane < W_LORA, jnp.tanh(lora),
                     jnp.where(lane < W_LORA + A_LORA, lora, _sigmoid(lora)))
    w_lin = w0 + _mm_split(lora, w2)
    a_lin = a0 + _mm_split(lora, a2)
    g = _mm_split(lora, g2)
    u = -w_lin
    softplus = jnp.maximum(u, 0.0) + jnp.log(1.0 + jnp.exp(-jnp.abs(u)))
    lw = -jnp.exp(-softplus - 0.5)
    a = _sigmoid(a_lin)
    kk = k * k_k
    kk = kk * lax.rsqrt(_seg_sum(kk * kk, seg) + 1e-12)
    k = k * (1.0 + (a - 1.0) * k_a)
    return r, lw, k, v, kk, kk * a, g


def _prefix_sum(tri, x):
    hi = x.astype(BF16)
    rest = x - hi.astype(F32)
    mid = rest.astype(BF16)
    lo = (rest - mid.astype(F32)).astype(BF16)
    dot = functools.partial(jnp.dot, preferred_element_type=F32)
    return dot(tri, hi) + (dot(tri, mid) + dot(tri, lo))


def _bmm(a, b):
    return lax.dot_general(a.astype(BF16), b.astype(BF16), (((2,), (1,)), ((0,), (0,))),
                           preferred_element_type=F32)


def _bmm_nt(a, b):
    return lax.dot_general(a.astype(BF16), b.astype(BF16), (((2,), (2,)), ((0,), (0,))),
                           preferred_element_type=F32)


def _bmm_tn(a, b):
    return lax.dot_general(a.astype(BF16), b.astype(BF16), (((1,), (1,)), ((0,), (0,))),
                           preferred_element_type=F32)


def _rwkv_kernel(x_ref, gain_ref, wz_ref, mu_ref, w0_ref, w2_ref, a0_ref, a2_ref, g2_ref, kk_ref, ka_ref,
                 lnw_ref, lnb_ref, rk_ref, seg_ref, o_ref, blank_ref, s_ref, zlast_ref, *, chunk):
    @pl.when(pl.program_id(1) == 0)
    def _():
        s_ref[...] = jnp.zeros_like(s_ref)
        zlast_ref[...] = jnp.zeros_like(zlast_ref)

    tb = x_ref.shape[1]
    d = RWKV_DIM
    z = jnp.dot(_rms(x_ref[0], gain_ref[...]).astype(BF16), wz_ref[...], preferred_element_type=F32)
    last = zlast_ref[7:8, :]
    zlast_ref[...] = z[tb - 8:, :]
    r, lw, k, v, kk, b, g = _rwkv_inputs(z, last, mu_ref[...], w0_ref[...], w2_ref[...], a0_ref[...],
                                         a2_ref[...], g2_ref[...], kk_ref[...], ka_ref[...], seg_ref[...])
    nc, nh = tb // chunk, d // HEAD_DIM
    ii = lax.broadcasted_iota(jnp.int32, (chunk, chunk), 0)
    jj = lax.broadcasted_iota(jnp.int32, (chunk, chunk), 1)
    incl, strict = ii >= jj, ii > jj
    tri = jnp.where(incl, 1.0, 0.0).astype(BF16)
    eye = (ii == jj).astype(F32)
    rows = lambda c: slice(c * chunk, (c + 1) * chunk)
    lanes = lambda h: slice(h * HEAD_DIM, (h + 1) * HEAD_DIM)

    def split(x):
        return jnp.stack([x[rows(c), lanes(h)] for c in range(nc) for h in range(nh)])

    cums = [_prefix_sum(tri, lw[rows(c)]) for c in range(nc)]
    cum = jnp.concatenate(cums, axis=0)
    total = jnp.concatenate([jnp.broadcast_to(cs[chunk - 1:], (chunk, d)) for cs in cums], axis=0)
    g_inv = jnp.exp(-cum)
    g_tail = jnp.exp(total - cum)
    kq, rq = split(kk * jnp.exp(cum - lw)), split(r * jnp.exp(cum))
    bi, ki = split(b * g_inv), split(k * g_inv)
    bh, kh = split(b * g_tail), split(k * g_tail)
    vs = split(v)
    decay = jnp.stack([jnp.exp(cs[chunk - 1:, lanes(h)]) for cs in cums for h in range(nh)])

    qr = jnp.concatenate([kq, rq], axis=1)
    ab, ak = _bmm_nt(qr, bi), _bmm_nt(qr, ki)
    a_b, a_rb = jnp.where(strict, ab[:, :chunk], 0.0), jnp.where(incl, ab[:, chunk:], 0.0)
    a_k, a_rk = jnp.where(strict, ak[:, :chunk], 0.0), jnp.where(incl, ak[:, chunk:], 0.0)
    n = nc * nh
    is_b = lax.broadcasted_iota(jnp.int32, (chunk, 2 * chunk), 1) < chunk
    bt = jnp.concatenate([-a_b, jnp.broadcast_to(eye, (n, chunk, chunk))], axis=2)
    for _ in range(int(np.log2(chunk))):
        prod = _bmm(bt[:, :, :chunk], bt)
        bt = jnp.where(is_b, prod, bt + prod)
    inv = bt[:, :, chunk:]
    wu = _bmm(inv, jnp.concatenate([kq, -_bmm(a_k, vs)], axis=2))
    aw = _bmm(a_rb, wu)
    qh = rq - aw[:, :, :HEAD_DIM]
    y0 = aw[:, :, HEAD_DIM:] + _bmm(a_rk, vs)
    mu = _bmm_tn(wu, bh)
    mp = mu[:, :HEAD_DIM]
    s_add = mu[:, HEAD_DIM:] + _bmm_tn(vs, kh)

    state = s_ref[...]
    for c in range(nc):
        sl = slice(c * nh, (c + 1) * nh)
        y = _bmm_nt(qh[sl], state) + y0[sl]
        for h in range(nh):
            o_ref[0, rows(c), lanes(h)] = y[h]
        state = state * decay[sl] - _bmm(state, mp[sl]) + s_add[sl]
    s_ref[...] = state

    seg = seg_ref[...]
    y = o_ref[0]
    yc = y - _seg_sum(y, seg) * (1.0 / HEAD_DIM)
    var = _seg_sum(yc * yc, seg) * (1.0 / HEAD_DIM)
    y = yc * lax.rsqrt(var + RWKV_GN_EPS) * lnw_ref[...] + lnb_ref[...]
    y = y + _seg_sum(r * k * rk_ref[...], seg) * v
    o_ref[0] = y * g
    blank_ref[...] = jnp.zeros_like(blank_ref)


def _rwkv_branch(x, gain, wz, mu, w0, w2p, a0, a2p, g2p, k_k, k_a, ln_w, ln_b, r_k, seg_ones, blank_rows,
                 *, tb):
    bsz, seq, dm = x.shape
    d = RWKV_DIM
    steps = seq // tb
    full = lambda a: pl.BlockSpec(a.shape, lambda bi, t: (0,) * a.ndim)
    params = (gain, wz, mu, w0, w2p, a0, a2p, g2p, k_k, k_a, ln_w, ln_b, r_k, seg_ones)
    blank_blk = blank_rows // (bsz * steps)
    return pl.pallas_call(
        functools.partial(_rwkv_kernel, chunk=WKV_CHUNK),
        grid=(bsz, steps),
        in_specs=[pl.BlockSpec((1, tb, dm), lambda bi, t: (bi, t, 0))] + [full(a) for a in params],
        out_specs=[pl.BlockSpec((1, tb, d), lambda bi, t: (bi, t, 0)),
                   pl.BlockSpec((blank_blk, LANES), lambda bi, t: (bi * steps + t, 0))],
        out_shape=[jax.ShapeDtypeStruct((bsz, seq, d), F32),
                   jax.ShapeDtypeStruct((blank_rows, LANES), F32)],
        scratch_shapes=[pltpu.VMEM((d // HEAD_DIM, HEAD_DIM, HEAD_DIM), F32),
                        pltpu.VMEM((8, RWKV_COLS), F32)],
        compiler_params=_params("parallel", "arbitrary"),
        name="rwkv_branch",
    )(x, *params)


def _attn_kernel(q_ref, kp_ref, kc_ref, vp_ref, vc_ref, bias_ref, o_ref, l_ref):
    first = pl.program_id(2) == 0
    q = q_ref[0, 0] * (HEAD_DIM ** -0.5)
    kc, vc = kc_ref[0, 0], vc_ref[0, 0]
    subs = q.shape[0] // Q_BLOCK
    rows = lambda j: slice(j * Q_BLOCK, (j + 1) * Q_BLOCK)
    lanes = lambda h: slice(h * HEAD_DIM, (h + 1) * HEAD_DIM)
    k_prev = [kp_ref[0, 0]] + [kc[rows(j)] for j in range(subs - 1)]
    v_prev = [vp_ref[0, 0]] + [vc[rows(j)] for j in range(subs - 1)]
    work = [(j, h) for j in range(subs) for h in range(HEADS_PER_GROUP)]
    scores = []
    for j, h in work:
        bias = bias_ref[h]
        bias_prev = bias[:, :Q_BLOCK]
        if j == 0:
            bias_prev = jnp.where(first, MASKED, bias_prev)
        qh = q[rows(j), lanes(h)]
        scores.append((_mm_nt(qh, k_prev[j][:, lanes(h)]) + bias_prev,
                       _mm_nt(qh, kc[rows(j), lanes(h)]) + bias[:, Q_BLOCK:]))
    tops = [jnp.max(jnp.maximum(sp, sc), axis=-1, keepdims=True) for sp, sc in scores]
    probs = [(jnp.exp(sp - m), jnp.exp(sc - m)) for (sp, sc), m in zip(scores, tops)]
    dens = [jnp.sum(pp + pc, axis=-1, keepdims=True) for pp, pc in probs]
    accs = [_mm(pp, v_prev[j][:, lanes(h)]) + _mm(pc, vc[rows(j), lanes(h)])
            for (pp, pc), (j, h) in zip(probs, work)]
    lane = lax.broadcasted_iota(jnp.int32, (Q_BLOCK, LANES), 1)
    for j in range(subs):
        lse = jnp.zeros((Q_BLOCK, LANES), F32)
        for h in range(HEADS_PER_GROUP):
            i = j * HEADS_PER_GROUP + h
            o_ref[0, 0, rows(j), lanes(h)] = accs[i] * (1.0 / dens[i])
            lse = jnp.where(lane == h, tops[i] + jnp.log(dens[i]), lse)
        l_ref[0, 0, rows(j), :] = lse


def _attn_bias(group):
    window, dilation = ATTN_GROUPS[group]
    n_back = window // dilation
    heads = np.arange(group * HEADS_PER_GROUP, (group + 1) * HEADS_PER_GROUP)
    slopes = (2.0 ** (-8.0 * (heads + 1) / ATTN_HEADS)).astype(np.float32)
    delta = (np.arange(Q_BLOCK)[:, None] + Q_BLOCK) - np.arange(2 * Q_BLOCK)[None, :]
    valid = (delta >= 0) & (delta <= n_back)
    alibi = -slopes[:, None, None] * (dilation * delta).astype(np.float32)[None]
    return jnp.asarray(np.where(valid[None], alibi, np.float32(MASKED)).astype(np.float32))


def _attn_group(qkv, group):
    bsz, dil, comp, _ = qkv.shape
    tq = min(ATTN_TQ, comp)
    assert comp % tq == 0 and tq % Q_BLOCK == 0
    sub = tq // Q_BLOCK
    col = lambda which: (lambda bi, r, n: (bi, r, n, which))
    col_prev = lambda which: (lambda bi, r, n: (bi, r, jnp.maximum(n * sub - 1, 0), which))
    cur, prev = (1, 1, tq, ATTN_OUT_DIM), (1, 1, Q_BLOCK, ATTN_OUT_DIM)
    bias = _attn_bias(group)
    return pl.pallas_call(
        _attn_kernel,
        grid=(bsz, dil, comp // tq),
        in_specs=[pl.BlockSpec(cur, col(0)),
                  pl.BlockSpec(prev, col_prev(1)), pl.BlockSpec(cur, col(1)),
                  pl.BlockSpec(prev, col_prev(2)), pl.BlockSpec(cur, col(2)),
                  pl.BlockSpec(bias.shape, lambda bi, r, n: (0, 0, 0))],
        out_specs=[pl.BlockSpec(cur, col(0)),
                   pl.BlockSpec((1, 1, tq, LANES), col(0))],
        out_shape=[jax.ShapeDtypeStruct((bsz, dil, comp, ATTN_OUT_DIM), F32),
                   jax.ShapeDtypeStruct((bsz, dil, comp, LANES), F32)],
        compiler_params=_params("parallel", "parallel", "parallel"),
        name=f"dilated_attn_{group}",
    )(qkv, qkv, qkv, qkv, qkv, bias)


def _route(logits):
    lane = lax.broadcasted_iota(jnp.int32, logits.shape, 1)
    neg = -jnp.inf
    first_max = lambda vals, mx: jnp.min(jnp.where(vals == mx, lane, LANES), axis=-1, keepdims=True)
    is_grp = (lane >= N_EXPERTS) & (lane < N_EXPERTS + N_EXPERT_GROUPS)
    gl = jnp.where(is_grp, logits, neg)
    gmax = jnp.max(gl, axis=-1, keepdims=True)
    grp_w = 1.0 / jnp.sum(jnp.exp(gl - gmax), axis=-1, keepdims=True)
    grp = first_max(gl, gmax) - N_EXPERTS
    in_grp = (lane >= grp * EXPERTS_PER_GROUP) & (lane < (grp + 1) * EXPERTS_PER_GROUP)
    el = jnp.where(in_grp, logits, neg)
    v1 = jnp.max(el, axis=-1, keepdims=True)
    i1 = first_max(el, v1)
    el2 = jnp.where(lane == i1, neg, el)
    v2 = jnp.max(el2, axis=-1, keepdims=True)
    i2 = first_max(el2, v2)
    e2 = jnp.exp(v2 - v1)
    w1 = grp_w / (1.0 + e2)
    return i1, i2, w1, w1 * e2


def _merge_kernel(x_ref, ya_ref, o0_ref, o1_ref, o2_ref, l0_ref, l1_ref, l2_ref, gmix_ref, wgt_ref,
                  wa_ref, wb_ref, wo_ref, gain_ref, wr_ref, ex_ref, tri_ref, x1_ref, h_ref, route_ref,
                  count_ref, o_scr, l_scr):
    outs, lses = [], []
    for grp, (o_ref, l_ref) in enumerate(((o0_ref, l0_ref), (o1_ref, l1_ref), (o2_ref, l2_ref))):
        dil = o_ref.shape[1]
        if dil == 1:
            outs.append(o_ref[0, 0])
            lses.append(l_ref[0, 0])
            continue
        n = o_ref.shape[2]
        chunks = o_ref.shape[3] // LANES
        for r in range(dil):
            l_scr[grp - 1, pl.ds(r, n, stride=dil), :] = l_ref[0, r]
            for c in range(chunks):
                o_scr[grp - 1, c, pl.ds(r, n, stride=dil), :] = o_ref[0, r, :, c * LANES:(c + 1) * LANES]
        outs.append(jnp.concatenate([o_scr[grp - 1, c] for c in range(chunks)], axis=1))
        lses.append(l_scr[grp - 1])
    def mix(rows):
        l0, l1, l2 = (l[rows] for l in lses)
        m = jnp.maximum(jnp.maximum(l0, l1), l2)
        e0, e1, e2 = jnp.exp(l0 - m), jnp.exp(l1 - m), jnp.exp(l2 - m)
        inv = 1.0 / (e0 + e1 + e2)
        spread = ex_ref[...]
        yb = (_seg_sum(e0 * inv, spread) * outs[0][rows] + _seg_sum(e1 * inv, spread) * outs[1][rows]
              + _seg_sum(e2 * inv, spread) * outs[2][rows])
        x = x_ref[rows, :]
        gates = _sigmoid(_mm(_rms(x, gmix_ref[...]), wgt_ref[...]))
        merged = (gates[:, :D_MODEL] * _mm(ya_ref[rows, :], wa_ref[...])
                  + gates[:, D_MODEL:] * _mm(yb, wb_ref[...]))
        x1 = x + _mm(merged, wo_ref[...])
        h = _rms(x1, gain_ref[...])
        return (x1, h) + _route(_mm_split(h, wr_ref[...]))

    half = x_ref.shape[0] // 2
    parts = [mix(slice(0, half)), mix(slice(half, 2 * half))]
    x1, h, i1, i2, w1, w2 = (jnp.concatenate(both, axis=0) for both in zip(*parts))
    x1_ref[...] = x1
    h_ref[...] = h

    @pl.when(pl.program_id(0) == 0)
    def _():
        count_ref[...] = jnp.zeros_like(count_ref)

    lane = lax.broadcasted_iota(jnp.int32, (x1.shape[0], LANES), 1)
    onehot = jnp.where((lane == i1) | (lane == i2), 1.0, 0.0)
    before = count_ref[0:1, :] + jnp.dot(tri_ref[...], onehot.astype(BF16), preferred_element_type=F32)
    rank1 = jnp.sum(jnp.where(lane == i1, before, 0.0), axis=-1, keepdims=True)
    rank2 = jnp.sum(jnp.where(lane == i2, before, 0.0), axis=-1, keepdims=True)
    last = x1.shape[0] - 1
    count_ref[...] = jnp.broadcast_to(before[last:, :] + onehot[last:, :], count_ref.shape)
    cols = (i1.astype(F32), i2.astype(F32), w1, w2, rank1, rank2)
    info = jnp.zeros(lane.shape, F32)
    for c, col in enumerate(cols):
        info = jnp.where(lane == c, col, info)
    route_ref[...] = info


def _merge(x, ya, outs, lses, gain_mix, w_gates, wa, wb, wo, gain, wr, spread, seq, *, tm):
    t = x.shape[0]
    tps = seq // tm
    tok = lambda n: pl.BlockSpec((tm, n), lambda i: (i, 0))
    full = lambda a: pl.BlockSpec(a.shape, lambda i: (0, 0))
    res = lambda a: pl.BlockSpec((1, a.shape[1], tm // a.shape[1], a.shape[3]),
                                 lambda i: (i // tps, 0, i % tps, 0))
    dilated = len(ATTN_GROUPS) - 1
    tri = jnp.asarray(np.tri(tm, k=-1), dtype=BF16)
    return pl.pallas_call(
        _merge_kernel,
        grid=(t // tm,),
        in_specs=[tok(D_MODEL), tok(RWKV_DIM)] + [res(a) for a in outs] + [res(a) for a in lses] + [
                  full(gain_mix), full(w_gates), full(wa), full(wb), full(wo), full(gain), full(wr), full(spread),
                  full(tri)],
        out_specs=[tok(D_MODEL), tok(D_MODEL), tok(LANES), pl.BlockSpec((8, LANES), lambda i: (0, 0))],
        out_shape=[jax.ShapeDtypeStruct((t, D_MODEL), F32), jax.ShapeDtypeStruct((t, D_MODEL), F32),
                   jax.ShapeDtypeStruct((t, LANES), F32), jax.ShapeDtypeStruct((8, LANES), F32)],
        scratch_shapes=[pltpu.VMEM((dilated, ATTN_OUT_DIM // LANES, tm, LANES), F32),
                        pltpu.VMEM((dilated, tm, LANES), F32)],
        compiler_params=_params("arbitrary"),
        name="merge_route",
    )(x, ya, *outs, *lses, gain_mix, w_gates, wa, wb, wo, gain, wr, spread, tri)


ROUTE_E1, ROUTE_E2, ROUTE_W1, ROUTE_W2, ROUTE_R1, ROUTE_R2 = range(6)


def _moe_plan(counts, n_tiles):
    counts = counts[0, :N_EXPERTS].astype(jnp.int32)
    tile_end = jnp.cumsum((counts + MOE_TILE - 1) // MOE_TILE)
    n_valid = tile_end[-1]
    tile = jnp.minimum(jnp.arange(n_tiles, dtype=jnp.int32), n_valid - 1)
    tile_expert = jnp.sum(tile[:, None] >= tile_end[None, :], axis=1).astype(jnp.int32)
    return tile_expert, n_valid.reshape(1)


def _row_tiles(ref, rows, tile=ROW_TILE):
    return jnp.concatenate([ref[pl.ds(c, rows, stride=tile), :] for c in range(tile)], axis=1)


def _store_row_tiles(ref, x, tile=ROW_TILE):
    for c in range(tile):
        ref[pl.ds(c, x.shape[0], stride=tile), :] = x[:, c * LANES:(c + 1) * LANES]


def _row(ref, r, tile=ROW_TILE):
    return ref.at[pl.ds(pl.multiple_of(r * tile, tile), tile)]


def _dispatch_kernel(route_ref, count_ref, upper_ref, h_ref, xs_in_ref, xs_ref, pos_ref, rows_ref, pos_vmem,
                     pos_smem, sem, pos_sem):
    del xs_in_ref
    tm = h_ref.shape[0]
    tiles = jnp.floor((count_ref[...] + (MOE_TILE - 1)) * (1.0 / MOE_TILE))
    first_row = jnp.dot(tiles.astype(BF16), upper_ref[...], preferred_element_type=F32)[0:1, :] * MOE_TILE
    route = route_ref[...]
    lane = lax.broadcasted_iota(jnp.int32, route.shape, 1)
    col = lambda c: route[:, c:c + 1]
    place = lambda e, r: jnp.sum(jnp.where(lane == col(e).astype(jnp.int32), first_row, 0.0), axis=-1,
                                 keepdims=True) + col(r)
    both = jnp.where(lane == 0, place(ROUTE_E1, ROUTE_R1), jnp.where(lane == 1, place(ROUTE_E2, ROUTE_R2), 0.0))
    pos = both.T[0:ROW_TILE, :].astype(jnp.int32)
    pos_ref[0] = pos[0:2]
    pos_vmem[...] = pos
    to_smem = pltpu.make_async_copy(pos_vmem, pos_smem, pos_sem)
    to_smem.start()
    _store_row_tiles(rows_ref, h_ref[...])
    to_smem.wait()

    def copies(j):
        src = _row(rows_ref, j)
        return (pltpu.make_async_copy(src, _row(xs_ref, pos_smem[0, j]), sem.at[0]),
                pltpu.make_async_copy(src, _row(xs_ref, pos_smem[1, j]), sem.at[1]))

    def start(j, carry):
        for stream, cp in enumerate(copies(j)):
            cp.start(priority=stream)
        return carry

    lax.fori_loop(0, tm, start, 0, unroll=8)
    for stream in range(2):
        pltpu.make_async_copy(rows_ref, rows_ref, sem.at[stream]).wait()


def _dispatch(h, route, counts, blank, *, tm):
    t, d = h.shape
    upper = jnp.asarray(np.triu(np.ones((LANES, LANES)), k=1), dtype=BF16)
    tok = lambda n: pl.BlockSpec((tm, n), lambda i: (i, 0))
    full = lambda a: pl.BlockSpec(a.shape, lambda i: (0, 0))
    return pl.pallas_call(
        _dispatch_kernel,
        grid=(t // tm,),
        in_specs=[tok(LANES), full(counts), full(upper), tok(d), pl.BlockSpec(memory_space=pl.ANY)],
        out_specs=[pl.BlockSpec(memory_space=pl.ANY), pl.BlockSpec((1, 2, tm), lambda i: (i, 0, 0))],
        out_shape=[jax.ShapeDtypeStruct(blank.shape, blank.dtype),
                   jax.ShapeDtypeStruct((t // tm, 2, tm), jnp.int32)],
        scratch_shapes=[pltpu.VMEM((tm * ROW_TILE, LANES), F32), pltpu.VMEM((ROW_TILE, tm), jnp.int32),
                        pltpu.SMEM((ROW_TILE, tm), jnp.int32), pltpu.SemaphoreType.DMA((2,)),
                        pltpu.SemaphoreType.DMA],
        input_output_aliases={4: 0},
        compiler_params=_params("arbitrary"),
        name="moe_dispatch",
    )(route, counts, upper, h, blank)


def _expert_kernel(te_ref, nv_ref, x_ref, wg_ref, wu_ref, wd_ref, o_ref):
    live = pl.program_id(0) < nv_ref[0]

    @pl.when(live)
    def _():
        x = _row_tiles(x_ref, MOE_TILE)
        gate = _mm(x, wg_ref[0])
        up = _mm(x, wu_ref[0])
        _store_row_tiles(o_ref, _mm(gate * _sigmoid(gate) * up, wd_ref[0]))

    @pl.when(jnp.logical_not(live))
    def _():
        o_ref[...] = jnp.zeros_like(o_ref)


def _experts(xs, tile_expert, n_valid, wg, wu, wd):
    d = wg.shape[1]
    weight = lambda shape: pl.BlockSpec((1,) + shape, lambda i, te, nv: (te[i], 0, 0))
    blk = (MOE_TILE * ROW_TILE, LANES)
    return pl.pallas_call(
        _expert_kernel,
        grid_spec=pltpu.PrefetchScalarGridSpec(
            num_scalar_prefetch=2,
            grid=(xs.shape[0] // blk[0],),
            in_specs=[pl.BlockSpec(blk, lambda i, te, nv: (jnp.minimum(i, nv[0] - 1), 0)),
                      weight((d, D_FF_EXPERT)), weight((d, D_FF_EXPERT)), weight((D_FF_EXPERT, d))],
            out_specs=pl.BlockSpec(blk, lambda i, te, nv: (i, 0))),
        out_shape=jax.ShapeDtypeStruct(xs.shape, F32),
        compiler_params=_params("arbitrary"),
        name="moe_experts",
    )(tile_expert, n_valid, xs, wg, wu, wd)


def _ple_kernel(pos_ref, x_ref, route_ref, p_ref, gp_ref, wg_ref, wp_ref, gf_ref, ys_ref, o_ref,
                y_ref, sem, *, pos_tm):
    tm = x_ref.shape[0]
    step, steps = pl.program_id(0), pl.num_programs(0)
    per_pos_tile = pos_tm // tm

    def gather(tile, slot, inline):
        base = (tile // per_pos_tile) * (2 * pos_tm) + (tile % per_pos_tile) * tm

        def body(j, carry):
            for stream in range(2):
                pltpu.make_async_copy(_row(ys_ref, pos_ref[base + stream * pos_tm + j]),
                                      _row(y_ref.at[slot, stream], j), sem.at[slot, stream]
                                      ).start(priority=stream)
            return carry

        if inline:
            for j in range(tm):
                body(j, 0)
        else:
            lax.fori_loop(0, tm, body, 0, unroll=8)

    def drain(slot):
        for stream in range(2):
            buf = y_ref.at[slot, stream]
            pltpu.make_async_copy(buf, buf, sem.at[slot, stream]).wait()

    slot = step % 2

    @pl.when(step == 0)
    def _():
        gather(0, 0, False)

    drain(slot)
    gather(jnp.minimum(step + 1, steps - 1), 1 - slot, True)
    route = route_ref[...]
    w1, w2 = route[:, ROUTE_W1:ROUTE_W1 + 1], route[:, ROUTE_W2:ROUTE_W2 + 1]
    x = x_ref[...] + (w1 * _row_tiles(y_ref.at[slot, 0], tm) + w2 * _row_tiles(y_ref.at[slot, 1], tm))
    gate = _sigmoid(_mm(_rms(x, gp_ref[...]), wg_ref[...]))
    x = x + gate * _mm(p_ref[...], wp_ref[...])
    o_ref[...] = _rms(x, gf_ref[...])

    @pl.when(step == steps - 1)
    def _():
        drain(1 - slot)


def _ple(x, route, pos, ys, p, gain_ple, w_gate, w_proj, gain_final, *, tm):
    t = x.shape[0]
    pos_tm = pos.shape[2]
    assert pos_tm % tm == 0
    tok = lambda n: pl.BlockSpec((tm, n), lambda i, ps: (i, 0))
    full = lambda a: pl.BlockSpec(a.shape, lambda i, ps: (0, 0))
    return pl.pallas_call(
        functools.partial(_ple_kernel, pos_tm=pos_tm),
        grid_spec=pltpu.PrefetchScalarGridSpec(
            num_scalar_prefetch=1,
            grid=(t // tm,),
            in_specs=[tok(D_MODEL), tok(LANES), tok(PLE_DIM), full(gain_ple), full(w_gate), full(w_proj),
                      full(gain_final), pl.BlockSpec(memory_space=pl.ANY)],
            out_specs=tok(D_MODEL),
            scratch_shapes=[pltpu.VMEM((2, 2, tm * ROW_TILE, LANES), F32), pltpu.SemaphoreType.DMA((2, 2))]),
        out_shape=jax.ShapeDtypeStruct((t, D_MODEL), F32),
        compiler_params=_params("arbitrary"),
        name="ple_final",
    )(pos.reshape(-1), x, route, p, gain_ple, w_gate, w_proj, gain_final, ys)


def _pad_rows(w, start, total):
    return jnp.zeros((total, w.shape[1]), w.dtype).at[start:start + w.shape[0]].set(w)


def _layer(x, p, seq, norm_mix, w_in, rwkv_mu, rwkv_w0, rwkv_w2, rwkv_a0, rwkv_a2, rwkv_g2, rwkv_k_k,
           rwkv_k_a, rwkv_r_k, rwkv_ln_w, rwkv_ln_b, w_proj_a, w_proj_b, w_out, norm_moe, w_router_group,
           w_router_expert, w_exp_gate, w_exp_up, w_exp_down, norm_ple, w_ple_gate, w_ple_proj, norm_final):
    t = x.shape[0]
    bsz = t // seq
    row = lambda a: a.reshape(1, -1)
    c0, c1 = RWKV_COLS, RWKV_COLS + ATTN_COLS
    x3 = x.reshape(bsz, seq, D_MODEL)

    def w_group(grp):
        lo = lambda which: c0 + which * ATTN_DIM + grp * ATTN_OUT_DIM
        return jnp.concatenate([w_in[:, lo(which):lo(which) + ATTN_OUT_DIM] for which in range(3)],
                               axis=1).astype(BF16)

    qkv = [_qkv_proj(x3, row(norm_mix), w_group(grp), grp, tm=1024) for grp in range(len(ATTN_GROUPS))]

    seg = np.arange(RWKV_DIM) // HEAD_DIM
    seg_ones = jnp.asarray(seg[:, None] == seg[None, :], dtype=BF16)
    n_tiles = 2 * t // MOE_TILE + N_EXPERTS
    ya, blank = _rwkv_branch(
        x3, row(norm_mix), w_in[:, :c0].astype(BF16), row(rwkv_mu), row(rwkv_w0),
        _pad_rows(rwkv_w2, 0, LORA_COLS), row(rwkv_a0), _pad_rows(rwkv_a2, W_LORA, LORA_COLS),
        _pad_rows(rwkv_g2, W_LORA + A_LORA, LORA_COLS), row(rwkv_k_k), row(rwkv_k_a), row(rwkv_ln_w),
        row(rwkv_ln_b), row(rwkv_r_k), seg_ones, n_tiles * MOE_TILE * ROW_TILE, tb=256)
    ya = ya.reshape(t, RWKV_DIM)

    attn = [_attn_group(qkv[grp], grp) for grp in range(len(ATTN_GROUPS))]

    head_of_lane = np.arange(ATTN_OUT_DIM) // HEAD_DIM
    spread = jnp.asarray(np.arange(LANES)[:, None] == head_of_lane[None, :], dtype=BF16)
    w_route = jnp.concatenate(
        [jnp.moveaxis(w_router_expert, 0, 1).reshape(D_MODEL, N_EXPERTS), w_router_group,
         jnp.zeros((D_MODEL, LANES - N_EXPERTS - N_EXPERT_GROUPS), F32)], axis=1)
    x1, h_moe, route, counts = _merge(x, ya, [o for o, _ in attn], [l for _, l in attn],
                             row(norm_mix), w_in[:, c1:].astype(BF16),
                             w_proj_a.astype(BF16), w_proj_b.astype(BF16), w_out.astype(BF16),
                             row(norm_moe), w_route, spread, seq, tm=512)
    tile_expert, n_valid = _moe_plan(counts, n_tiles)
    dispatch_tm = 2048
    xs, pos = _dispatch(h_moe, route, counts, blank, tm=dispatch_tm)
    ys = _experts(xs, tile_expert, n_valid,
                  w_exp_gate.reshape(N_EXPERTS, D_MODEL, D_FF_EXPERT),
                  w_exp_up.reshape(N_EXPERTS, D_MODEL, D_FF_EXPERT),
                  w_exp_down.reshape(N_EXPERTS, D_FF_EXPERT, D_MODEL))
    return _ple(x1, route, pos, ys, p, row(norm_ple), w_ple_gate.astype(BF16),
                w_ple_proj.astype(BF16), row(norm_final), tm=512)


def kernel(x, p, norm_mix, w_in, rwkv_mu, rwkv_w0, rwkv_w2, rwkv_a0, rwkv_a2, rwkv_g2, rwkv_k_k, rwkv_k_a, rwkv_r_k, rwkv_ln_w, rwkv_ln_b, w_proj_a, w_proj_b, w_out, norm_moe, w_router_group, w_router_expert, w_exp_gate, w_exp_up, w_exp_down, norm_ple, w_ple_gate, w_ple_proj, norm_final):
    bsz, seq, d = x.shape
    depth = w_in.shape[0]
    assert depth == 1, "the final norm is fused into the (single) layer"
    out = _layer(x.reshape(bsz * seq, d).astype(F32), p[0].reshape(bsz * seq, PLE_DIM), seq,
                 norm_mix[0], w_in[0], rwkv_mu[0], rwkv_w0[0], rwkv_w2[0], rwkv_a0[0], rwkv_a2[0],
                 rwkv_g2[0], rwkv_k_k[0], rwkv_k_a[0], rwkv_r_k[0], rwkv_ln_w[0], rwkv_ln_b[0],
                 w_proj_a[0], w_proj_b[0], w_out[0], norm_moe[0], w_router_group[0], w_router_expert[0],
                 w_exp_gate[0], w_exp_up[0], w_exp_down[0], norm_ple[0], w_ple_gate[0], w_ple_proj[0],
                 norm_final)
    return out.reshape(bsz, seq, d)
```

```python
import functools

import numpy as np
import jax
import jax.numpy as jnp
from jax import lax
from jax.experimental import pallas as pl
from jax.experimental.pallas import tpu as pltpu

F32 = jnp.float32
BF16 = jnp.bfloat16

D_MODEL = 1024
PLE_DIM = 256
HEAD_DIM = 64
RWKV_HEADS = 8
RWKV_DIM = RWKV_HEADS * HEAD_DIM
W_LORA, A_LORA, G_LORA = 64, 64, 128
LORA_COLS = W_LORA + A_LORA + G_LORA
ATTN_GROUPS = ((128, 1), (512, 4), (2048, 16))
HEADS_PER_GROUP = 8
ATTN_HEADS = HEADS_PER_GROUP * len(ATTN_GROUPS)
ATTN_DIM = ATTN_HEADS * HEAD_DIM
ATTN_OUT_DIM = HEADS_PER_GROUP * HEAD_DIM
Q_BLOCK = 128
ATTN_TQ = 1024
RWKV_COLS = 3 * RWKV_DIM + LORA_COLS
ATTN_COLS = 3 * ATTN_DIM
N_EXPERT_GROUPS = 4
EXPERTS_PER_GROUP = 8
N_EXPERTS = N_EXPERT_GROUPS * EXPERTS_PER_GROUP
D_FF_EXPERT = 256
NORM_EPS = 1e-6
RWKV_GN_EPS = 64e-5
LANES = 128
MASKED = -1e30
WKV_CHUNK = 64
VMEM_LIMIT = 48 * 1024 * 1024
MOE_TILE = 512
ROW_TILE = 8
QKV_ROWS = 1024
RWKV_ROWS = 256
MERGE_ROWS = 512
DISPATCH_ROWS = 2048
COMBINE_ROWS = 512


def _mm(a, b):
    return jnp.dot(a.astype(BF16), b.astype(BF16), preferred_element_type=F32)


def _mm_nt(a, b):
    return lax.dot_general(a.astype(BF16), b.astype(BF16), (((1,), (1,)), ((), ())),
                           preferred_element_type=F32)


def _mm_split(a, b):
    a_hi, b_hi = a.astype(BF16), b.astype(BF16)
    a_lo = (a - a_hi.astype(F32)).astype(BF16)
    b_lo = (b - b_hi.astype(F32)).astype(BF16)
    dot = functools.partial(jnp.dot, preferred_element_type=F32)
    return dot(a_hi, b_hi) + (dot(a_hi, b_lo) + dot(a_lo, b_hi))


def _sigmoid(x):
    return 1.0 / (1.0 + jnp.exp(-x))


def _rms(x, gain):
    return x * lax.rsqrt(jnp.mean(x * x, axis=-1, keepdims=True) + NORM_EPS) * gain


def _params(*sem):
    return pltpu.CompilerParams(dimension_semantics=sem, vmem_limit_bytes=VMEM_LIMIT)


def _qkv_rows_kernel(x_ref, g_ref, w_ref, o_ref, stage_ref, *, dil):
    h = _rms(x_ref[0], g_ref[...])
    if dil == 1:
        o_ref[0, 0] = _mm(h, w_ref[...]).astype(BF16)
        return
    chunks = h.shape[1] // LANES
    for c in range(chunks):
        stage_ref[c] = h[:, c * LANES:(c + 1) * LANES]
    n = h.shape[0] // dil
    for r in range(dil):
        rows = jnp.concatenate([stage_ref[c, pl.ds(r, n, stride=dil), :] for c in range(chunks)], axis=1)
        o_ref[0, r] = _mm(rows, w_ref[...]).astype(BF16)


def _qkv_strided_kernel(x_hbm, g_ref, w_ref, o_ref, x_buf, sem):
    tm = x_buf.shape[1]
    n_res, n_blk = pl.num_programs(1), pl.num_programs(2)
    step = (pl.program_id(0) * n_res + pl.program_id(1)) * n_blk + pl.program_id(2)
    steps = pl.num_programs(0) * n_res * n_blk

    def fetch(s, slot):
        b, r, i = s // (n_res * n_blk), (s // n_blk) % n_res, s % n_blk
        return pltpu.make_async_copy(x_hbm.at[b, pl.ds(i * tm, tm), r], x_buf.at[slot], sem.at[slot])

    slot = step % 2

    @pl.when(step == 0)
    def _():
        fetch(0, 0).start()

    @pl.when(step + 1 < steps)
    def _():
        fetch(step + 1, 1 - slot).start()

    fetch(step, slot).wait()
    o_ref[0, 0] = _mm(_rms(x_buf[slot], g_ref[...]), w_ref[...]).astype(BF16)


def _qkv_proj(x, gain, w, group, *, tm):
    bsz, seq, d = x.shape
    dil = ATTN_GROUPS[group][1]
    comp = seq // dil
    tn = w.shape[1]
    out_shape = jax.ShapeDtypeStruct((bsz, dil, comp, tn), BF16)
    if dil % ROW_TILE:
        return pl.pallas_call(
            functools.partial(_qkv_rows_kernel, dil=dil),
            grid=(bsz, seq // tm),
            in_specs=[pl.BlockSpec((1, tm, d), lambda b, i: (b, i, 0)),
                      pl.BlockSpec((1, d), lambda b, i: (0, 0)),
                      pl.BlockSpec((d, tn), lambda b, i: (0, 0))],
            out_specs=pl.BlockSpec((1, dil, tm // dil, tn), lambda b, i: (b, 0, i, 0)),
            out_shape=out_shape,
            scratch_shapes=[pltpu.VMEM((d // LANES, tm, LANES), F32)],
            compiler_params=_params("parallel", "parallel"),
            name=f"qkv_proj_{group}",
        )(x, gain, w)
    tm = min(tm, comp)
    return pl.pallas_call(
        _qkv_strided_kernel,
        grid=(bsz, dil, comp // tm),
        in_specs=[pl.BlockSpec(memory_space=pl.ANY),
                  pl.BlockSpec((1, d), lambda b, r, i: (0, 0)),
                  pl.BlockSpec((d, tn), lambda b, r, i: (0, 0))],
        out_specs=pl.BlockSpec((1, 1, tm, tn), lambda b, r, i: (b, r, i, 0)),
        out_shape=out_shape,
        scratch_shapes=[pltpu.VMEM((2, tm, d), F32), pltpu.SemaphoreType.DMA((2,))],
        compiler_params=_params("arbitrary", "arbitrary", "arbitrary"),
        name=f"qkv_proj_{group}",
    )(x.reshape(bsz, comp, dil, d), gain, w)


def _seg_sum(x, seg_ones):
    hi = x.astype(BF16)
    lo = (x - hi.astype(F32)).astype(BF16)
    return (jnp.dot(hi, seg_ones, preferred_element_type=F32)
            + jnp.dot(lo, seg_ones, preferred_element_type=F32))


def _rwkv_inputs(z, last, mu, w0, w2, a0, a2, g2, k_k, k_a, seg):
    row = lax.broadcasted_iota(jnp.int32, z.shape, 0)
    prev = jnp.where(row == 0, last, pltpu.roll(z, 1, axis=0))
    zs = z + (prev - z) * mu
    d = RWKV_DIM
    r, k, v = zs[:, :d], zs[:, d:2 * d], zs[:, 2 * d:3 * d]
    lora = zs[:, 3 * d:]
    lane = lax.broadcasted_iota(jnp.int32, lora.shape, 1)
    lora = jnp.where(lane < W_LORA, jnp.tanh(lora),
                     jnp.where(lane < W_LORA + A_LORA, lora, _sigmoid(lora)))
    w_lin = w0 + _mm_split(lora, w2)
    a_lin = a0 + _mm_split(lora, a2)
    g = _mm_split(lora, g2)
    u = -w_lin
    softplus = jnp.maximum(u, 0.0) + jnp.log(1.0 + jnp.exp(-jnp.abs(u)))
    lw = -jnp.exp(-softplus - 0.5)
    a = _sigmoid(a_lin)
    kk = k * k_k
    kk = kk * lax.rsqrt(_seg_sum(kk * kk, seg) + 1e-12)
    k = k * (1.0 + (a - 1.0) * k_a)
    return r, lw, k, v, kk, kk * a, g


def _prefix_sum(tri, x):
    hi = x.astype(BF16)
    rest = x - hi.astype(F32)
    mid = rest.astype(BF16)
    lo = (rest - mid.astype(F32)).astype(BF16)
    dot = functools.partial(jnp.dot, preferred_element_type=F32)
    return dot(tri, hi) + (dot(tri, mid) + dot(tri, lo))


def _bmm(a, b):
    return lax.dot_general(a.astype(BF16), b.astype(BF16), (((2,), (1,)), ((0,), (0,))),
                           preferred_element_type=F32)


def _bmm_nt(a, b):
    return lax.dot_general(a.astype(BF16), b.astype(BF16), (((2,), (2,)), ((0,), (0,))),
                           preferred_element_type=F32)


def _bmm_tn(a, b):
    return lax.dot_general(a.astype(BF16), b.astype(BF16), (((1,), (1,)), ((0,), (0,))),
                           preferred_element_type=F32)


def _rwkv_kernel(x_ref, gain_ref, wz_ref, mu_ref, w0_ref, w2_ref, a0_ref, a2_ref, g2_ref, kk_ref, ka_ref,
                 lnw_ref, lnb_ref, rk_ref, seg_ref, o_ref, blank_ref, s_ref, zlast_ref, *, chunk):
    @pl.when(pl.program_id(1) == 0)
    def _():
        s_ref[...] = jnp.zeros_like(s_ref)
        zlast_ref[...] = jnp.zeros_like(zlast_ref)

    tb = x_ref.shape[1]
    d = RWKV_DIM
    z = jnp.dot(_rms(x_ref[0], gain_ref[...]).astype(BF16), wz_ref[...], preferred_element_type=F32)
    last = zlast_ref[7:8, :]
    zlast_ref[...] = z[tb - 8:, :]
    r, lw, k, v, kk, b, g = _rwkv_inputs(z, last, mu_ref[...], w0_ref[...], w2_ref[...], a0_ref[...],
                                         a2_ref[...], g2_ref[...], kk_ref[...], ka_ref[...], seg_ref[...])
    nc, nh = tb // chunk, d // HEAD_DIM
    ii = lax.broadcasted_iota(jnp.int32, (chunk, chunk), 0)
    jj = lax.broadcasted_iota(jnp.int32, (chunk, chunk), 1)
    incl, strict = ii >= jj, ii > jj
    tri = jnp.where(incl, 1.0, 0.0).astype(BF16)
    eye = (ii == jj).astype(F32)
    rows = lambda c: slice(c * chunk, (c + 1) * chunk)
    lanes = lambda h: slice(h * HEAD_DIM, (h + 1) * HEAD_DIM)

    def split(x):
        return jnp.stack([x[rows(c), lanes(h)] for c in range(nc) for h in range(nh)])

    cums = [_prefix_sum(tri, lw[rows(c)]) for c in range(nc)]
    cum = jnp.concatenate(cums, axis=0)
    total = jnp.concatenate([jnp.broadcast_to(cs[chunk - 1:], (chunk, d)) for cs in cums], axis=0)
    g_inv = jnp.exp(-cum)
    g_tail = jnp.exp(total - cum)
    kq, rq = split(kk * jnp.exp(cum - lw)), split(r * jnp.exp(cum))
    bi, ki = split(b * g_inv), split(k * g_inv)
    bh, kh = split(b * g_tail), split(k * g_tail)
    vs = split(v)
    decay = jnp.stack([jnp.exp(cs[chunk - 1:, lanes(h)]) for cs in cums for h in range(nh)])

    qr = jnp.concatenate([kq, rq], axis=1)
    ab, ak = _bmm_nt(qr, bi), _bmm_nt(qr, ki)
    a_b, a_rb = jnp.where(strict, ab[:, :chunk], 0.0), jnp.where(incl, ab[:, chunk:], 0.0)
    a_k, a_rk = jnp.where(strict, ak[:, :chunk], 0.0), jnp.where(incl, ak[:, chunk:], 0.0)
    n = nc * nh
    is_b = lax.broadcasted_iota(jnp.int32, (chunk, 2 * chunk), 1) < chunk
    bt = jnp.concatenate([-a_b, jnp.broadcast_to(eye, (n, chunk, chunk))], axis=2)
    for _ in range(int(np.log2(chunk))):
        prod = _bmm(bt[:, :, :chunk], bt)
        bt = jnp.where(is_b, prod, bt + prod)
    inv = bt[:, :, chunk:]
    wu = _bmm(inv, jnp.concatenate([kq, -_bmm(a_k, vs)], axis=2))
    aw = _bmm(a_rb, wu)
    qh = rq - aw[:, :, :HEAD_DIM]
    y0 = aw[:, :, HEAD_DIM:] + _bmm(a_rk, vs)
    mu = _bmm_tn(wu, bh)
    mp = mu[:, :HEAD_DIM]
    s_add = mu[:, HEAD_DIM:] + _bmm_tn(vs, kh)

    state = s_ref[...]
    for c in range(nc):
        sl = slice(c * nh, (c + 1) * nh)
        y = _bmm_nt(qh[sl], state) + y0[sl]
        for h in range(nh):
            o_ref[0, rows(c), lanes(h)] = y[h]
        state = state * decay[sl] - _bmm(state, mp[sl]) + s_add[sl]
    s_ref[...] = state

    seg = seg_ref[...]
    y = o_ref[0]
    yc = y - _seg_sum(y, seg) * (1.0 / HEAD_DIM)
    var = _seg_sum(yc * yc, seg) * (1.0 / HEAD_DIM)
    y = yc * lax.rsqrt(var + RWKV_GN_EPS) * lnw_ref[...] + lnb_ref[...]
    y = y + _seg_sum(r * k * rk_ref[...], seg) * v
    o_ref[0] = y * g
    blank_ref[...] = jnp.zeros_like(blank_ref)


def _rwkv_branch(x, gain, wz, mu, w0, w2p, a0, a2p, g2p, k_k, k_a, ln_w, ln_b, r_k, seg_ones, blank_rows,
                 *, tb):
    bsz, seq, dm = x.shape
    d = RWKV_DIM
    steps = seq // tb
    full = lambda a: pl.BlockSpec(a.shape, lambda bi, t: (0,) * a.ndim)
    params = (gain, wz, mu, w0, w2p, a0, a2p, g2p, k_k, k_a, ln_w, ln_b, r_k, seg_ones)
    blank_blk = blank_rows // (bsz * steps)
    return pl.pallas_call(
        functools.partial(_rwkv_kernel, chunk=WKV_CHUNK),
        grid=(bsz, steps),
        in_specs=[pl.BlockSpec((1, tb, dm), lambda bi, t: (bi, t, 0))] + [full(a) for a in params],
        out_specs=[pl.BlockSpec((1, tb, d), lambda bi, t: (bi, t, 0)),
                   pl.BlockSpec((blank_blk, LANES), lambda bi, t: (bi * steps + t, 0))],
        out_shape=[jax.ShapeDtypeStruct((bsz, seq, d), F32),
                   jax.ShapeDtypeStruct((blank_rows, LANES), F32)],
        scratch_shapes=[pltpu.VMEM((d // HEAD_DIM, HEAD_DIM, HEAD_DIM), F32),
                        pltpu.VMEM((8, RWKV_COLS), F32)],
        compiler_params=_params("parallel", "arbitrary"),
        name="rwkv_branch",
    )(x, *params)


def _attn_kernel(q_ref, kp_ref, kc_ref, vp_ref, vc_ref, bias_ref, o_ref, l_ref):
    first = pl.program_id(2) == 0
    q = q_ref[0, 0] * (HEAD_DIM ** -0.5)
    kc, vc = kc_ref[0, 0], vc_ref[0, 0]
    subs = q.shape[0] // Q_BLOCK
    rows = lambda j: slice(j * Q_BLOCK, (j + 1) * Q_BLOCK)
    lanes = lambda h: slice(h * HEAD_DIM, (h + 1) * HEAD_DIM)
    k_prev = [kp_ref[0, 0]] + [kc[rows(j)] for j in range(subs - 1)]
    v_prev = [vp_ref[0, 0]] + [vc[rows(j)] for j in range(subs - 1)]
    work = [(j, h) for j in range(subs) for h in range(HEADS_PER_GROUP)]
    scores = []
    for j, h in work:
        bias = bias_ref[h]
        bias_prev = bias[:, :Q_BLOCK]
        if j == 0:
            bias_prev = jnp.where(first, MASKED, bias_prev)
        qh = q[rows(j), lanes(h)]
        scores.append((_mm_nt(qh, k_prev[j][:, lanes(h)]) + bias_prev,
                       _mm_nt(qh, kc[rows(j), lanes(h)]) + bias[:, Q_BLOCK:]))
    tops = [jnp.max(jnp.maximum(sp, sc), axis=-1, keepdims=True) for sp, sc in scores]
    probs = [(jnp.exp(sp - m), jnp.exp(sc - m)) for (sp, sc), m in zip(scores, tops)]
    dens = [jnp.sum(pp + pc, axis=-1, keepdims=True) for pp, pc in probs]
    accs = [_mm(pp, v_prev[j][:, lanes(h)]) + _mm(pc, vc[rows(j), lanes(h)])
            for (pp, pc), (j, h) in zip(probs, work)]
    lane = lax.broadcasted_iota(jnp.int32, (Q_BLOCK, LANES), 1)
    for j in range(subs):
        lse = jnp.zeros((Q_BLOCK, LANES), F32)
        for h in range(HEADS_PER_GROUP):
            i = j * HEADS_PER_GROUP + h
            o_ref[0, 0, rows(j), lanes(h)] = accs[i] * (1.0 / dens[i])
            lse = jnp.where(lane == h, tops[i] + jnp.log(dens[i]), lse)
        l_ref[0, 0, rows(j), :] = lse


def _attn_bias(group):
    window, dilation = ATTN_GROUPS[group]
    n_back = window // dilation
    heads = np.arange(group * HEADS_PER_GROUP, (group + 1) * HEADS_PER_GROUP)
    slopes = (2.0 ** (-8.0 * (heads + 1) / ATTN_HEADS)).astype(np.float32)
    delta = (np.arange(Q_BLOCK)[:, None] + Q_BLOCK) - np.arange(2 * Q_BLOCK)[None, :]
    valid = (delta >= 0) & (delta <= n_back)
    alibi = -slopes[:, None, None] * (dilation * delta).astype(np.float32)[None]
    return jnp.asarray(np.where(valid[None], alibi, np.float32(MASKED)).astype(np.float32))


def _attn_group(qkv, group):
    bsz, dil, comp, _ = qkv.shape
    tq = min(ATTN_TQ, comp)
    assert comp % tq == 0 and tq % Q_BLOCK == 0
    sub = tq // Q_BLOCK
    col = lambda which: (lambda bi, r, n: (bi, r, n, which))
    col_prev = lambda which: (lambda bi, r, n: (bi, r, jnp.maximum(n * sub - 1, 0), which))
    cur, prev = (1, 1, tq, ATTN_OUT_DIM), (1, 1, Q_BLOCK, ATTN_OUT_DIM)
    bias = _attn_bias(group)
    return pl.pallas_call(
        _attn_kernel,
        grid=(bsz, dil, comp // tq),
        in_specs=[pl.BlockSpec(cur, col(0)),
                  pl.BlockSpec(prev, col_prev(1)), pl.BlockSpec(cur, col(1)),
                  pl.BlockSpec(prev, col_prev(2)), pl.BlockSpec(cur, col(2)),
                  pl.BlockSpec(bias.shape, lambda bi, r, n: (0, 0, 0))],
        out_specs=[pl.BlockSpec(cur, col(0)),
                   pl.BlockSpec((1, 1, tq, LANES), col(0))],
        out_shape=[jax.ShapeDtypeStruct((bsz, dil, comp, ATTN_OUT_DIM), F32),
                   jax.ShapeDtypeStruct((bsz, dil, comp, LANES), F32)],
        compiler_params=_params("parallel", "parallel", "parallel"),
        name=f"dilated_attn_{group}",
    )(qkv, qkv, qkv, qkv, qkv, bias)


def _route(logits):
    lane = lax.broadcasted_iota(jnp.int32, logits.shape, 1)
    neg = -jnp.inf
    first_max = lambda vals, mx: jnp.min(jnp.where(vals == mx, lane, LANES), axis=-1, keepdims=True)
    is_grp = (lane >= N_EXPERTS) & (lane < N_EXPERTS + N_EXPERT_GROUPS)
    gl = jnp.where(is_grp, logits, neg)
    gmax = jnp.max(gl, axis=-1, keepdims=True)
    grp_w = 1.0 / jnp.sum(jnp.exp(gl - gmax), axis=-1, keepdims=True)
    grp = first_max(gl, gmax) - N_EXPERTS
    in_grp = (lane >= grp * EXPERTS_PER_GROUP) & (lane < (grp + 1) * EXPERTS_PER_GROUP)
    el = jnp.where(in_grp, logits, neg)
    v1 = jnp.max(el, axis=-1, keepdims=True)
    i1 = first_max(el, v1)
    el2 = jnp.where(lane == i1, neg, el)
    v2 = jnp.max(el2, axis=-1, keepdims=True)
    i2 = first_max(el2, v2)
    e2 = jnp.exp(v2 - v1)
    w1 = grp_w / (1.0 + e2)
    return i1, i2, w1, w1 * e2


def _merge_kernel(x_ref, ya_ref, o0_ref, o1_ref, o2_ref, l0_ref, l1_ref, l2_ref, gmix_ref, wgt_ref,
                  wa_ref, wb_ref, wo_ref, gain_ref, wr_ref, ex_ref, tri_ref, x1_ref, h_ref, route_ref,
                  count_ref, o_scr, l_scr):
    outs, lses = [], []
    for grp, (o_ref, l_ref) in enumerate(((o0_ref, l0_ref), (o1_ref, l1_ref), (o2_ref, l2_ref))):
        dil = o_ref.shape[1]
        if dil == 1:
            outs.append(o_ref[0, 0])
            lses.append(l_ref[0, 0])
            continue
        n = o_ref.shape[2]
        chunks = o_ref.shape[3] // LANES
        for r in range(dil):
            l_scr[grp - 1, pl.ds(r, n, stride=dil), :] = l_ref[0, r]
            for c in range(chunks):
                o_scr[grp - 1, c, pl.ds(r, n, stride=dil), :] = o_ref[0, r, :, c * LANES:(c + 1) * LANES]
        outs.append(jnp.concatenate([o_scr[grp - 1, c] for c in range(chunks)], axis=1))
        lses.append(l_scr[grp - 1])
    def mix(rows):
        l0, l1, l2 = (l[rows] for l in lses)
        m = jnp.maximum(jnp.maximum(l0, l1), l2)
        e0, e1, e2 = jnp.exp(l0 - m), jnp.exp(l1 - m), jnp.exp(l2 - m)
        inv = 1.0 / (e0 + e1 + e2)
        spread = ex_ref[...]
        yb = (_seg_sum(e0 * inv, spread) * outs[0][rows] + _seg_sum(e1 * inv, spread) * outs[1][rows]
              + _seg_sum(e2 * inv, spread) * outs[2][rows])
        x = x_ref[rows, :]
        gates = _sigmoid(_mm(_rms(x, gmix_ref[...]), wgt_ref[...]))
        merged = (gates[:, :D_MODEL] * _mm(ya_ref[rows, :], wa_ref[...])
                  + gates[:, D_MODEL:] * _mm(yb, wb_ref[...]))
        x1 = x + _mm(merged, wo_ref[...])
        h = _rms(x1, gain_ref[...])
        return (x1, h) + _route(_mm_split(h, wr_ref[...]))

    half = x_ref.shape[0] // 2
    parts = [mix(slice(0, half)), mix(slice(half, 2 * half))]
    x1, h, i1, i2, w1, w2 = (jnp.concatenate(both, axis=0) for both in zip(*parts))
    x1_ref[...] = x1
    h_ref[...] = h

    @pl.when(pl.program_id(0) == 0)
    def _():
        count_ref[...] = jnp.zeros_like(count_ref)

    lane = lax.broadcasted_iota(jnp.int32, (x1.shape[0], LANES), 1)
    onehot = jnp.where((lane == i1) | (lane == i2), 1.0, 0.0)
    before = count_ref[0:1, :] + jnp.dot(tri_ref[...], onehot.astype(BF16), preferred_element_type=F32)
    rank1 = jnp.sum(jnp.where(lane == i1, before, 0.0), axis=-1, keepdims=True)
    rank2 = jnp.sum(jnp.where(lane == i2, before, 0.0), axis=-1, keepdims=True)
    last = x1.shape[0] - 1
    count_ref[...] = jnp.broadcast_to(before[last:, :] + onehot[last:, :], count_ref.shape)
    cols = (i1.astype(F32), i2.astype(F32), w1, w2, rank1, rank2)
    info = jnp.zeros(lane.shape, F32)
    for c, col in enumerate(cols):
        info = jnp.where(lane == c, col, info)
    route_ref[...] = info


def _merge(x, ya, outs, lses, gain_mix, w_gates, wa, wb, wo, gain, wr, spread, seq, *, tm):
    t = x.shape[0]
    tps = seq // tm
    tok = lambda n: pl.BlockSpec((tm, n), lambda i: (i, 0))
    full = lambda a: pl.BlockSpec(a.shape, lambda i: (0, 0))
    res = lambda a: pl.BlockSpec((1, a.shape[1], tm // a.shape[1], a.shape[3]),
                                 lambda i: (i // tps, 0, i % tps, 0))
    dilated = len(ATTN_GROUPS) - 1
    tri = jnp.asarray(np.tri(tm, k=-1), dtype=BF16)
    return pl.pallas_call(
        _merge_kernel,
        grid=(t // tm,),
        in_specs=[tok(D_MODEL), tok(RWKV_DIM)] + [res(a) for a in outs] + [res(a) for a in lses] + [
                  full(gain_mix), full(w_gates), full(wa), full(wb), full(wo), full(gain), full(wr), full(spread),
                  full(tri)],
        out_specs=[tok(D_MODEL), tok(D_MODEL), tok(LANES), pl.BlockSpec((8, LANES), lambda i: (0, 0))],
        out_shape=[jax.ShapeDtypeStruct((t, D_MODEL), F32), jax.ShapeDtypeStruct((t, D_MODEL), F32),
                   jax.ShapeDtypeStruct((t, LANES), F32), jax.ShapeDtypeStruct((8, LANES), F32)],
        scratch_shapes=[pltpu.VMEM((dilated, ATTN_OUT_DIM // LANES, tm, LANES), F32),
                        pltpu.VMEM((dilated, tm, LANES), F32)],
        compiler_params=_params("arbitrary"),
        name="merge_route",
    )(x, ya, *outs, *lses, gain_mix, w_gates, wa, wb, wo, gain, wr, spread, tri)


ROUTE_E1, ROUTE_E2, ROUTE_W1, ROUTE_W2, ROUTE_R1, ROUTE_R2 = range(6)


def _moe_plan(counts, n_tiles):
    counts = counts[0, :N_EXPERTS].astype(jnp.int32)
    tile_end = jnp.cumsum((counts + MOE_TILE - 1) // MOE_TILE)
    n_valid = tile_end[-1]
    tile = jnp.minimum(jnp.arange(n_tiles, dtype=jnp.int32), n_valid - 1)
    tile_expert = jnp.sum(tile[:, None] >= tile_end[None, :], axis=1).astype(jnp.int32)
    return tile_expert, n_valid.reshape(1)


def _row_tiles(ref, rows, tile=ROW_TILE):
    return jnp.concatenate([ref[pl.ds(c, rows, stride=tile), :] for c in range(tile)], axis=1)


def _store_row_tiles(ref, x, tile=ROW_TILE):
    for c in range(tile):
        ref[pl.ds(c, x.shape[0], stride=tile), :] = x[:, c * LANES:(c + 1) * LANES]


def _row(ref, r, tile=ROW_TILE):
    return ref.at[pl.ds(pl.multiple_of(r * tile, tile), tile)]


def _dispatch_kernel(route_ref, count_ref, upper_ref, h_ref, xs_in_ref, xs_ref, pos_ref, rows_ref, pos_vmem,
                     pos_smem, sem, pos_sem):
    del xs_in_ref
    tm = h_ref.shape[0]
    tiles = jnp.floor((count_ref[...] + (MOE_TILE - 1)) * (1.0 / MOE_TILE))
    first_row = jnp.dot(tiles.astype(BF16), upper_ref[...], preferred_element_type=F32)[0:1, :] * MOE_TILE
    route = route_ref[...]
    lane = lax.broadcasted_iota(jnp.int32, route.shape, 1)
    col = lambda c: route[:, c:c + 1]
    place = lambda e, r: jnp.sum(jnp.where(lane == col(e).astype(jnp.int32), first_row, 0.0), axis=-1,
                                 keepdims=True) + col(r)
    both = jnp.where(lane == 0, place(ROUTE_E1, ROUTE_R1), jnp.where(lane == 1, place(ROUTE_E2, ROUTE_R2), 0.0))
    pos = both.T[0:ROW_TILE, :].astype(jnp.int32)
    pos_ref[0] = pos[0:2]
    pos_vmem[...] = pos
    to_smem = pltpu.make_async_copy(pos_vmem, pos_smem, pos_sem)
    to_smem.start()
    _store_row_tiles(rows_ref, h_ref[...])
    to_smem.wait()

    def copies(j):
        src = _row(rows_ref, j)
        return (pltpu.make_async_copy(src, _row(xs_ref, pos_smem[0, j]), sem.at[0]),
                pltpu.make_async_copy(src, _row(xs_ref, pos_smem[1, j]), sem.at[1]))

    def start(j, carry):
        for stream, cp in enumerate(copies(j)):
            cp.start(priority=stream)
        return carry

    lax.fori_loop(0, tm, start, 0, unroll=8)
    for stream in range(2):
        pltpu.make_async_copy(rows_ref, rows_ref, sem.at[stream]).wait()


def _dispatch(h, route, counts, blank, *, tm):
    t, d = h.shape
    upper = jnp.asarray(np.triu(np.ones((LANES, LANES)), k=1), dtype=BF16)
    tok = lambda n: pl.BlockSpec((tm, n), lambda i: (i, 0))
    full = lambda a: pl.BlockSpec(a.shape, lambda i: (0, 0))
    return pl.pallas_call(
        _dispatch_kernel,
        grid=(t // tm,),
        in_specs=[tok(LANES), full(counts), full(upper), tok(d), pl.BlockSpec(memory_space=pl.ANY)],
        out_specs=[pl.BlockSpec(memory_space=pl.ANY), pl.BlockSpec((1, 2, tm), lambda i: (i, 0, 0))],
        out_shape=[jax.ShapeDtypeStruct(blank.shape, blank.dtype),
                   jax.ShapeDtypeStruct((t // tm, 2, tm), jnp.int32)],
        scratch_shapes=[pltpu.VMEM((tm * ROW_TILE, LANES), F32), pltpu.VMEM((ROW_TILE, tm), jnp.int32),
                        pltpu.SMEM((ROW_TILE, tm), jnp.int32), pltpu.SemaphoreType.DMA((2,)),
                        pltpu.SemaphoreType.DMA],
        input_output_aliases={4: 0},
        compiler_params=_params("arbitrary"),
        name="moe_dispatch",
    )(route, counts, upper, h, blank)


def _expert_kernel(te_ref, nv_ref, x_ref, wg_ref, wu_ref, wd_ref, o_ref):
    live = pl.program_id(0) < nv_ref[0]

    @pl.when(live)
    def _():
        x = _row_tiles(x_ref, MOE_TILE)
        gate = _mm(x, wg_ref[0])
        up = _mm(x, wu_ref[0])
        _store_row_tiles(o_ref, _mm(gate * _sigmoid(gate) * up, wd_ref[0]))

    @pl.when(jnp.logical_not(live))
    def _():
        o_ref[...] = jnp.zeros_like(o_ref)


def _experts(xs, tile_expert, n_valid, wg, wu, wd):
    d = wg.shape[1]
    weight = lambda shape: pl.BlockSpec((1,) + shape, lambda i, te, nv: (te[i], 0, 0))
    blk = (MOE_TILE * ROW_TILE, LANES)
    return pl.pallas_call(
        _expert_kernel,
        grid_spec=pltpu.PrefetchScalarGridSpec(
            num_scalar_prefetch=2,
            grid=(xs.shape[0] // blk[0],),
            in_specs=[pl.BlockSpec(blk, lambda i, te, nv: (jnp.minimum(i, nv[0] - 1), 0)),
                      weight((d, D_FF_EXPERT)), weight((d, D_FF_EXPERT)), weight((D_FF_EXPERT, d))],
            out_specs=pl.BlockSpec(blk, lambda i, te, nv: (i, 0))),
        out_shape=jax.ShapeDtypeStruct(xs.shape, F32),
        compiler_params=_params("arbitrary"),
        name="moe_experts",
    )(tile_expert, n_valid, xs, wg, wu, wd)


def _ple_kernel(pos_ref, x_ref, route_ref, p_ref, gp_ref, wg_ref, wp_ref, gf_ref, ys_ref, o_ref,
                y_ref, sem, *, pos_tm):
    tm = x_ref.shape[0]
    step, steps = pl.program_id(0), pl.num_programs(0)
    per_pos_tile = pos_tm // tm

    def gather(tile, slot, inline):
        base = (tile // per_pos_tile) * (2 * pos_tm) + (tile % per_pos_tile) * tm

        def body(j, carry):
            for stream in range(2):
                pltpu.make_async_copy(_row(ys_ref, pos_ref[base + stream * pos_tm + j]),
                                      _row(y_ref.at[slot, stream], j), sem.at[slot, stream]
                                      ).start(priority=stream)
            return carry

        if inline:
            for j in range(tm):
                body(j, 0)
        else:
            lax.fori_loop(0, tm, body, 0, unroll=8)

    def drain(slot):
        for stream in range(2):
            buf = y_ref.at[slot, stream]
            pltpu.make_async_copy(buf, buf, sem.at[slot, stream]).wait()

    slot = step % 2

    @pl.when(step == 0)
    def _():
        gather(0, 0, False)

    drain(slot)
    gather(jnp.minimum(step + 1, steps - 1), 1 - slot, True)
    route = route_ref[...]
    w1, w2 = route[:, ROUTE_W1:ROUTE_W1 + 1], route[:, ROUTE_W2:ROUTE_W2 + 1]
    x = x_ref[...] + (w1 * _row_tiles(y_ref.at[slot, 0], tm) + w2 * _row_tiles(y_ref.at[slot, 1], tm))
    gate = _sigmoid(_mm(_rms(x, gp_ref[...]), wg_ref[...]))
    x = x + gate * _mm(p_ref[...], wp_ref[...])
    o_ref[...] = _rms(x, gf_ref[...])

    @pl.when(step == steps - 1)
    def _():
        drain(1 - slot)


def _ple(x, route, pos, ys, p, gain_ple, w_gate, w_proj, gain_final, *, tm):
    t = x.shape[0]
    pos_tm = pos.shape[2]
    assert pos_tm % tm == 0
    tok = lambda n: pl.BlockSpec((tm, n), lambda i, ps: (i, 0))
    full = lambda a: pl.BlockSpec(a.shape, lambda i, ps: (0, 0))
    return pl.pallas_call(
        functools.partial(_ple_kernel, pos_tm=pos_tm),
        grid_spec=pltpu.PrefetchScalarGridSpec(
            num_scalar_prefetch=1,
            grid=(t // tm,),
            in_specs=[tok(D_MODEL), tok(LANES), tok(PLE_DIM), full(gain_ple), full(w_gate), full(w_proj),
                      full(gain_final), pl.BlockSpec(memory_space=pl.ANY)],
            out_specs=tok(D_MODEL),
            scratch_shapes=[pltpu.VMEM((2, 2, tm * ROW_TILE, LANES), F32), pltpu.SemaphoreType.DMA((2, 2))]),
        out_shape=jax.ShapeDtypeStruct((t, D_MODEL), F32),
        compiler_params=_params("arbitrary"),
        name="ple_final",
    )(pos.reshape(-1), x, route, p, gain_ple, w_gate, w_proj, gain_final, ys)


def _pad_rows(w, start, total):
    return jnp.zeros((total, w.shape[1]), w.dtype).at[start:start + w.shape[0]].set(w)


def _layer(x, p, seq, norm_mix, w_in, rwkv_mu, rwkv_w0, rwkv_w2, rwkv_a0, rwkv_a2, rwkv_g2, rwkv_k_k,
           rwkv_k_a, rwkv_r_k, rwkv_ln_w, rwkv_ln_b, w_proj_a, w_proj_b, w_out, norm_moe, w_router_group,
           w_router_expert, w_exp_gate, w_exp_up, w_exp_down, norm_ple, w_ple_gate, w_ple_proj, norm_final):
    t = x.shape[0]
    bsz = t // seq
    row = lambda a: a.reshape(1, -1)
    c0, c1 = RWKV_COLS, RWKV_COLS + ATTN_COLS
    x3 = x.reshape(bsz, seq, D_MODEL)

    def w_group(grp):
        lo = lambda which: c0 + which * ATTN_DIM + grp * ATTN_OUT_DIM
        return jnp.concatenate([w_in[:, lo(which):lo(which) + ATTN_OUT_DIM] for which in range(3)],
                               axis=1).astype(BF16)

    qkv = [_qkv_proj(x3, row(norm_mix), w_group(grp), grp, tm=QKV_ROWS) for grp in range(len(ATTN_GROUPS))]

    seg = np.arange(RWKV_DIM) // HEAD_DIM
    seg_ones = jnp.asarray(seg[:, None] == seg[None, :], dtype=BF16)
    n_tiles = 2 * t // MOE_TILE + N_EXPERTS
    ya, blank = _rwkv_branch(
        x3, row(norm_mix), w_in[:, :c0].astype(BF16), row(rwkv_mu), row(rwkv_w0),
        _pad_rows(rwkv_w2, 0, LORA_COLS), row(rwkv_a0), _pad_rows(rwkv_a2, W_LORA, LORA_COLS),
        _pad_rows(rwkv_g2, W_LORA + A_LORA, LORA_COLS), row(rwkv_k_k), row(rwkv_k_a), row(rwkv_ln_w),
        row(rwkv_ln_b), row(rwkv_r_k), seg_ones, n_tiles * MOE_TILE * ROW_TILE, tb=RWKV_ROWS)
    ya = ya.reshape(t, RWKV_DIM)

    attn = [_attn_group(qkv[grp], grp) for grp in range(len(ATTN_GROUPS))]

    head_of_lane = np.arange(ATTN_OUT_DIM) // HEAD_DIM
    spread = jnp.asarray(np.arange(LANES)[:, None] == head_of_lane[None, :], dtype=BF16)
    w_route = jnp.concatenate(
        [jnp.moveaxis(w_router_expert, 0, 1).reshape(D_MODEL, N_EXPERTS), w_router_group,
         jnp.zeros((D_MODEL, LANES - N_EXPERTS - N_EXPERT_GROUPS), F32)], axis=1)
    x1, h_moe, route, counts = _merge(x, ya, [o for o, _ in attn], [l for _, l in attn],
                             row(norm_mix), w_in[:, c1:].astype(BF16),
                             w_proj_a.astype(BF16), w_proj_b.astype(BF16), w_out.astype(BF16),
                             row(norm_moe), w_route, spread, seq, tm=MERGE_ROWS)
    tile_expert, n_valid = _moe_plan(counts, n_tiles)
    xs, pos = _dispatch(h_moe, route, counts, blank, tm=DISPATCH_ROWS)
    ys = _experts(xs, tile_expert, n_valid,
                  w_exp_gate.reshape(N_EXPERTS, D_MODEL, D_FF_EXPERT),
                  w_exp_up.reshape(N_EXPERTS, D_MODEL, D_FF_EXPERT),
                  w_exp_down.reshape(N_EXPERTS, D_FF_EXPERT, D_MODEL))
    return _ple(x1, route, pos, ys, p, row(norm_ple), w_ple_gate.astype(BF16),
                w_ple_proj.astype(BF16), row(norm_final), tm=COMBINE_ROWS)


def kernel(x, p, norm_mix, w_in, rwkv_mu, rwkv_w0, rwkv_w2, rwkv_a0, rwkv_a2, rwkv_g2, rwkv_k_k, rwkv_k_a, rwkv_r_k, rwkv_ln_w, rwkv_ln_b, w_proj_a, w_proj_b, w_out, norm_moe, w_router_group, w_router_expert, w_exp_gate, w_exp_up, w_exp_down, norm_ple, w_ple_gate, w_ple_proj, norm_final):
    bsz, seq, d = x.shape
    depth = w_in.shape[0]
    assert depth == 1, "the final norm is fused into the (single) layer"
    out = _layer(x.reshape(bsz * seq, d).astype(F32), p[0].reshape(bsz * seq, PLE_DIM), seq,
                 norm_mix[0], w_in[0], rwkv_mu[0], rwkv_w0[0], rwkv_w2[0], rwkv_a0[0], rwkv_a2[0],
                 rwkv_g2[0], rwkv_k_k[0], rwkv_k_a[0], rwkv_r_k[0], rwkv_ln_w[0], rwkv_ln_b[0],
                 w_proj_a[0], w_proj_b[0], w_out[0], norm_moe[0], w_router_group[0], w_router_expert[0],
                 w_exp_gate[0], w_exp_up[0], w_exp_down[0], norm_ple[0], w_ple_gate[0], w_ple_proj[0],
                 norm_final)
    return out.reshape(bsz, seq, d)
```

```python
import functools

import numpy as np
import jax
import jax.numpy as jnp
from jax import lax
from jax.experimental import pallas as pl
from jax.experimental.pallas import tpu as pltpu

F32 = jnp.float32
BF16 = jnp.bfloat16

D_MODEL = 1024
PLE_DIM = 256
HEAD_DIM = 64
RWKV_HEADS = 8
RWKV_DIM = RWKV_HEADS * HEAD_DIM
W_LORA, A_LORA, G_LORA = 64, 64, 128
LORA_COLS = W_LORA + A_LORA + G_LORA
ATTN_GROUPS = ((128, 1), (512, 4), (2048, 16))
HEADS_PER_GROUP = 8
ATTN_HEADS = HEADS_PER_GROUP * len(ATTN_GROUPS)
ATTN_DIM = ATTN_HEADS * HEAD_DIM
ATTN_OUT_DIM = HEADS_PER_GROUP * HEAD_DIM
Q_BLOCK = 128
ATTN_TQ = 1024
RWKV_COLS = 3 * RWKV_DIM + LORA_COLS
ATTN_COLS = 3 * ATTN_DIM
N_EXPERT_GROUPS = 4
EXPERTS_PER_GROUP = 8
N_EXPERTS = N_EXPERT_GROUPS * EXPERTS_PER_GROUP
D_FF_EXPERT = 256
NORM_EPS = 1e-6
RWKV_GN_EPS = 64e-5
LANES = 128
MASKED = -1e30
WKV_CHUNK = 64
VMEM_LIMIT = 48 * 1024 * 1024
MOE_TILE = 512
ROW_TILE = 8
QKV_ROWS = 1024
RWKV_ROWS = 256
MERGE_ROWS = 512
DISPATCH_ROWS = 2048
COMBINE_ROWS = 512


def _mm(a, b):
    return jnp.dot(a.astype(BF16), b.astype(BF16), preferred_element_type=F32)


def _mm_nt(a, b):
    return lax.dot_general(a.astype(BF16), b.astype(BF16), (((1,), (1,)), ((), ())),
                           preferred_element_type=F32)


def _mm_split(a, b):
    a_hi, b_hi = a.astype(BF16), b.astype(BF16)
    a_lo = (a - a_hi.astype(F32)).astype(BF16)
    b_lo = (b - b_hi.astype(F32)).astype(BF16)
    dot = functools.partial(jnp.dot, preferred_element_type=F32)
    return dot(a_hi, b_hi) + (dot(a_hi, b_lo) + dot(a_lo, b_hi))


def _sigmoid(x):
    return 1.0 / (1.0 + jnp.exp(-x))


def _rms(x, gain):
    return x * lax.rsqrt(jnp.mean(x * x, axis=-1, keepdims=True) + NORM_EPS) * gain


def _params(*sem):
    return pltpu.CompilerParams(dimension_semantics=sem, vmem_limit_bytes=VMEM_LIMIT)


def _qkv_rows_kernel(x_ref, g_ref, w_ref, o_ref, stage_ref, *, dil):
    h = _rms(x_ref[0], g_ref[...])
    if dil == 1:
        o_ref[0, 0] = _mm(h, w_ref[...]).astype(BF16)
        return
    chunks = h.shape[1] // LANES
    for c in range(chunks):
        stage_ref[c] = h[:, c * LANES:(c + 1) * LANES]
    n = h.shape[0] // dil
    for r in range(dil):
        rows = jnp.concatenate([stage_ref[c, pl.ds(r, n, stride=dil), :] for c in range(chunks)], axis=1)
        o_ref[0, r] = _mm(rows, w_ref[...]).astype(BF16)


def _qkv_strided_kernel(x_hbm, g_ref, w_ref, o_ref, x_buf, sem):
    tm = x_buf.shape[1]
    n_res, n_blk = pl.num_programs(1), pl.num_programs(2)
    step = (pl.program_id(0) * n_res + pl.program_id(1)) * n_blk + pl.program_id(2)
    steps = pl.num_programs(0) * n_res * n_blk

    def fetch(s, slot):
        b, r, i = s // (n_res * n_blk), (s // n_blk) % n_res, s % n_blk
        return pltpu.make_async_copy(x_hbm.at[b, pl.ds(i * tm, tm), r], x_buf.at[slot], sem.at[slot])

    slot = step % 2

    @pl.when(step == 0)
    def _():
        fetch(0, 0).start()

    @pl.when(step + 1 < steps)
    def _():
        fetch(step + 1, 1 - slot).start()

    fetch(step, slot).wait()
    o_ref[0, 0] = _mm(_rms(x_buf[slot], g_ref[...]), w_ref[...]).astype(BF16)


def _qkv_proj(x, gain, w, group, *, tm):
    bsz, seq, d = x.shape
    dil = ATTN_GROUPS[group][1]
    comp = seq // dil
    tn = w.shape[1]
    out_shape = jax.ShapeDtypeStruct((bsz, dil, comp, tn), BF16)
    if dil % ROW_TILE:
        return pl.pallas_call(
            functools.partial(_qkv_rows_kernel, dil=dil),
            grid=(bsz, seq // tm),
            in_specs=[pl.BlockSpec((1, tm, d), lambda b, i: (b, i, 0)),
                      pl.BlockSpec((1, d), lambda b, i: (0, 0)),
                      pl.BlockSpec((d, tn), lambda b, i: (0, 0))],
            out_specs=pl.BlockSpec((1, dil, tm // dil, tn), lambda b, i: (b, 0, i, 0)),
            out_shape=out_shape,
            scratch_shapes=[pltpu.VMEM((d // LANES, tm, LANES), F32)],
            compiler_params=_params("parallel", "parallel"),
            name=f"qkv_proj_{group}",
        )(x, gain, w)
    tm = min(tm, comp)
    return pl.pallas_call(
        _qkv_strided_kernel,
        grid=(bsz, dil, comp // tm),
        in_specs=[pl.BlockSpec(memory_space=pl.ANY),
                  pl.BlockSpec((1, d), lambda b, r, i: (0, 0)),
                  pl.BlockSpec((d, tn), lambda b, r, i: (0, 0))],
        out_specs=pl.BlockSpec((1, 1, tm, tn), lambda b, r, i: (b, r, i, 0)),
        out_shape=out_shape,
        scratch_shapes=[pltpu.VMEM((2, tm, d), F32), pltpu.SemaphoreType.DMA((2,))],
        compiler_params=_params("arbitrary", "arbitrary", "arbitrary"),
        name=f"qkv_proj_{group}",
    )(x.reshape(bsz, comp, dil, d), gain, w)


def _seg_sum(x, seg_ones):
    hi = x.astype(BF16)
    lo = (x - hi.astype(F32)).astype(BF16)
    return (jnp.dot(hi, seg_ones, preferred_element_type=F32)
            + jnp.dot(lo, seg_ones, preferred_element_type=F32))


def _rwkv_inputs(z, last, mu, w0, w2, a0, a2, g2, k_k, k_a, seg):
    row = lax.broadcasted_iota(jnp.int32, z.shape, 0)
    prev = jnp.where(row == 0, last, pltpu.roll(z, 1, axis=0))
    zs = z + (prev - z) * mu
    d = RWKV_DIM
    r, k, v = zs[:, :d], zs[:, d:2 * d], zs[:, 2 * d:3 * d]
    lora = zs[:, 3 * d:]
    lane = lax.broadcasted_iota(jnp.int32, lora.shape, 1)
    lora = jnp.where(lane < W_LORA, jnp.tanh(lora),
                     jnp.where(lane < W_LORA + A_LORA, lora, _sigmoid(lora)))
    w_lin = w0 + _mm_split(lora, w2)
    a_lin = a0 + _mm_split(lora, a2)
    g = _mm_split(lora, g2)
    u = -w_lin
    softplus = jnp.maximum(u, 0.0) + jnp.log(1.0 + jnp.exp(-jnp.abs(u)))
    lw = -jnp.exp(-softplus - 0.5)
    a = _sigmoid(a_lin)
    kk = k * k_k
    kk = kk * lax.rsqrt(_seg_sum(kk * kk, seg) + 1e-12)
    k = k * (1.0 + (a - 1.0) * k_a)
    return r, lw, k, v, kk, kk * a, g


def _prefix_sum(tri, x):
    hi = x.astype(BF16)
    rest = x - hi.astype(F32)
    mid = rest.astype(BF16)
    lo = (rest - mid.astype(F32)).astype(BF16)
    dot = functools.partial(jnp.dot, preferred_element_type=F32)
    return dot(tri, hi) + (dot(tri, mid) + dot(tri, lo))


def _bmm(a, b):
    return lax.dot_general(a.astype(BF16), b.astype(BF16), (((2,), (1,)), ((0,), (0,))),
                           preferred_element_type=F32)


def _bmm_nt(a, b):
    return lax.dot_general(a.astype(BF16), b.astype(BF16), (((2,), (2,)), ((0,), (0,))),
                           preferred_element_type=F32)


def _bmm_tn(a, b):
    return lax.dot_general(a.astype(BF16), b.astype(BF16), (((1,), (1,)), ((0,), (0,))),
                           preferred_element_type=F32)


def _rwkv_kernel(x_ref, gain_ref, wz_ref, mu_ref, w0_ref, w2_ref, a0_ref, a2_ref, g2_ref, kk_ref, ka_ref,
                 lnw_ref, lnb_ref, rk_ref, seg_ref, o_ref, blank_ref, s_ref, zlast_ref, *, chunk):
    @pl.when(pl.program_id(1) == 0)
    def _():
        s_ref[...] = jnp.zeros_like(s_ref)
        zlast_ref[...] = jnp.zeros_like(zlast_ref)

    tb = x_ref.shape[1]
    d = RWKV_DIM
    z = jnp.dot(_rms(x_ref[0], gain_ref[...]).astype(BF16), wz_ref[...], preferred_element_type=F32)
    last = zlast_ref[7:8, :]
    zlast_ref[...] = z[tb - 8:, :]
    r, lw, k, v, kk, b, g = _rwkv_inputs(z, last, mu_ref[...], w0_ref[...], w2_ref[...], a0_ref[...],
                                         a2_ref[...], g2_ref[...], kk_ref[...], ka_ref[...], seg_ref[...])
    nc, nh = tb // chunk, d // HEAD_DIM
    ii = lax.broadcasted_iota(jnp.int32, (chunk, chunk), 0)
    jj = lax.broadcasted_iota(jnp.int32, (chunk, chunk), 1)
    incl, strict = ii >= jj, ii > jj
    tri = jnp.where(incl, 1.0, 0.0).astype(BF16)
    eye = (ii == jj).astype(F32)
    rows = lambda c: slice(c * chunk, (c + 1) * chunk)
    lanes = lambda h: slice(h * HEAD_DIM, (h + 1) * HEAD_DIM)

    def split(x):
        return jnp.stack([x[rows(c), lanes(h)] for c in range(nc) for h in range(nh)])

    cums = [_prefix_sum(tri, lw[rows(c)]) for c in range(nc)]
    cum = jnp.concatenate(cums, axis=0)
    total = jnp.concatenate([jnp.broadcast_to(cs[chunk - 1:], (chunk, d)) for cs in cums], axis=0)
    g_inv = jnp.exp(-cum)
    g_tail = jnp.exp(total - cum)
    kq, rq = split(kk * jnp.exp(cum - lw)), split(r * jnp.exp(cum))
    bi, ki = split(b * g_inv), split(k * g_inv)
    bh, kh = split(b * g_tail), split(k * g_tail)
    vs = split(v)
    decay = jnp.stack([jnp.exp(cs[chunk - 1:, lanes(h)]) for cs in cums for h in range(nh)])

    qr = jnp.concatenate([kq, rq], axis=1)
    ab, ak = _bmm_nt(qr, bi), _bmm_nt(qr, ki)
    a_b, a_rb = jnp.where(strict, ab[:, :chunk], 0.0), jnp.where(incl, ab[:, chunk:], 0.0)
    a_k, a_rk = jnp.where(strict, ak[:, :chunk], 0.0), jnp.where(incl, ak[:, chunk:], 0.0)
    n = nc * nh
    is_b = lax.broadcasted_iota(jnp.int32, (chunk, 2 * chunk), 1) < chunk
    bt = jnp.concatenate([-a_b, jnp.broadcast_to(eye, (n, chunk, chunk))], axis=2)
    for _ in range(int(np.log2(chunk))):
        prod = _bmm(bt[:, :, :chunk], bt)
        bt = jnp.where(is_b, prod, bt + prod)
    inv = bt[:, :, chunk:]
    wu = _bmm(inv, jnp.concatenate([kq, -_bmm(a_k, vs)], axis=2))
    aw = _bmm(a_rb, wu)
    qh = rq - aw[:, :, :HEAD_DIM]
    y0 = aw[:, :, HEAD_DIM:] + _bmm(a_rk, vs)
    mu = _bmm_tn(wu, bh)
    mp = mu[:, :HEAD_DIM]
    s_add = mu[:, HEAD_DIM:] + _bmm_tn(vs, kh)

    state = s_ref[...]
    for c in range(nc):
        sl = slice(c * nh, (c + 1) * nh)
        y = _bmm_nt(qh[sl], state) + y0[sl]
        for h in range(nh):
            o_ref[0, rows(c), lanes(h)] = y[h]
        state = state * decay[sl] - _bmm(state, mp[sl]) + s_add[sl]
    s_ref[...] = state

    seg = seg_ref[...]
    y = o_ref[0]
    yc = y - _seg_sum(y, seg) * (1.0 / HEAD_DIM)
    var = _seg_sum(yc * yc, seg) * (1.0 / HEAD_DIM)
    y = yc * lax.rsqrt(var + RWKV_GN_EPS) * lnw_ref[...] + lnb_ref[...]
    y = y + _seg_sum(r * k * rk_ref[...], seg) * v
    o_ref[0] = y * g
    blank_ref[...] = jnp.zeros_like(blank_ref)


def _rwkv_branch(x, gain, wz, mu, w0, w2p, a0, a2p, g2p, k_k, k_a, ln_w, ln_b, r_k, seg_ones, blank_rows,
                 *, tb):
    bsz, seq, dm = x.shape
    d = RWKV_DIM
    steps = seq // tb
    full = lambda a: pl.BlockSpec(a.shape, lambda bi, t: (0,) * a.ndim)
    params = (gain, wz, mu, w0, w2p, a0, a2p, g2p, k_k, k_a, ln_w, ln_b, r_k, seg_ones)
    blank_blk = blank_rows // (bsz * steps)
    return pl.pallas_call(
        functools.partial(_rwkv_kernel, chunk=WKV_CHUNK),
        grid=(bsz, steps),
        in_specs=[pl.BlockSpec((1, tb, dm), lambda bi, t: (bi, t, 0))] + [full(a) for a in params],
        out_specs=[pl.BlockSpec((1, tb, d), lambda bi, t: (bi, t, 0)),
                   pl.BlockSpec((blank_blk, LANES), lambda bi, t: (bi * steps + t, 0))],
        out_shape=[jax.ShapeDtypeStruct((bsz, seq, d), F32),
                   jax.ShapeDtypeStruct((blank_rows, LANES), F32)],
        scratch_shapes=[pltpu.VMEM((d // HEAD_DIM, HEAD_DIM, HEAD_DIM), F32),
                        pltpu.VMEM((8, RWKV_COLS), F32)],
        compiler_params=_params("parallel", "arbitrary"),
        name="rwkv_branch",
    )(x, *params)


def _attn_kernel(q_ref, kp_ref, kc_ref, vp_ref, vc_ref, bias_ref, o_ref, l_ref):
    first = pl.program_id(2) == 0
    q = q_ref[0, 0] * (HEAD_DIM ** -0.5)
    kc, vc = kc_ref[0, 0], vc_ref[0, 0]
    subs = q.shape[0] // Q_BLOCK
    rows = lambda j: slice(j * Q_BLOCK, (j + 1) * Q_BLOCK)
    lanes = lambda h: slice(h * HEAD_DIM, (h + 1) * HEAD_DIM)
    k_prev = [kp_ref[0, 0]] + [kc[rows(j)] for j in range(subs - 1)]
    v_prev = [vp_ref[0, 0]] + [vc[rows(j)] for j in range(subs - 1)]
    work = [(j, h) for j in range(subs) for h in range(HEADS_PER_GROUP)]
    scores = []
    for j, h in work:
        bias = bias_ref[h]
        bias_prev = bias[:, :Q_BLOCK]
        if j == 0:
            bias_prev = jnp.where(first, MASKED, bias_prev)
        qh = q[rows(j), lanes(h)]
        scores.append((_mm_nt(qh, k_prev[j][:, lanes(h)]) + bias_prev,
                       _mm_nt(qh, kc[rows(j), lanes(h)]) + bias[:, Q_BLOCK:]))
    tops = [jnp.max(jnp.maximum(sp, sc), axis=-1, keepdims=True) for sp, sc in scores]
    probs = [(jnp.exp(sp - m), jnp.exp(sc - m)) for (sp, sc), m in zip(scores, tops)]
    dens = [jnp.sum(pp + pc, axis=-1, keepdims=True) for pp, pc in probs]
    accs = [_mm(pp, v_prev[j][:, lanes(h)]) + _mm(pc, vc[rows(j), lanes(h)])
            for (pp, pc), (j, h) in zip(probs, work)]
    lane = lax.broadcasted_iota(jnp.int32, (Q_BLOCK, LANES), 1)
    for j in range(subs):
        lse = jnp.zeros((Q_BLOCK, LANES), F32)
        for h in range(HEADS_PER_GROUP):
            i = j * HEADS_PER_GROUP + h
            o_ref[0, 0, rows(j), lanes(h)] = accs[i] * (1.0 / dens[i])
            lse = jnp.where(lane == h, tops[i] + jnp.log(dens[i]), lse)
        l_ref[0, 0, rows(j), :] = lse


def _attn_bias(group):
    window, dilation = ATTN_GROUPS[group]
    n_back = window // dilation
    heads = np.arange(group * HEADS_PER_GROUP, (group + 1) * HEADS_PER_GROUP)
    slopes = (2.0 ** (-8.0 * (heads + 1) / ATTN_HEADS)).astype(np.float32)
    delta = (np.arange(Q_BLOCK)[:, None] + Q_BLOCK) - np.arange(2 * Q_BLOCK)[None, :]
    valid = (delta >= 0) & (delta <= n_back)
    alibi = -slopes[:, None, None] * (dilation * delta).astype(np.float32)[None]
    return jnp.asarray(np.where(valid[None], alibi, np.float32(MASKED)).astype(np.float32))


def _attn_group(qkv, group):
    bsz, dil, comp, _ = qkv.shape
    tq = min(ATTN_TQ, comp)
    assert comp % tq == 0 and tq % Q_BLOCK == 0
    sub = tq // Q_BLOCK
    col = lambda which: (lambda bi, r, n: (bi, r, n, which))
    col_prev = lambda which: (lambda bi, r, n: (bi, r, jnp.maximum(n * sub - 1, 0), which))
    cur, prev = (1, 1, tq, ATTN_OUT_DIM), (1, 1, Q_BLOCK, ATTN_OUT_DIM)
    bias = _attn_bias(group)
    return pl.pallas_call(
        _attn_kernel,
        grid=(bsz, dil, comp // tq),
        in_specs=[pl.BlockSpec(cur, col(0)),
                  pl.BlockSpec(prev, col_prev(1)), pl.BlockSpec(cur, col(1)),
                  pl.BlockSpec(prev, col_prev(2)), pl.BlockSpec(cur, col(2)),
                  pl.BlockSpec(bias.shape, lambda bi, r, n: (0, 0, 0))],
        out_specs=[pl.BlockSpec(cur, col(0)),
                   pl.BlockSpec((1, 1, tq, LANES), col(0))],
        out_shape=[jax.ShapeDtypeStruct((bsz, dil, comp, ATTN_OUT_DIM), F32),
                   jax.ShapeDtypeStruct((bsz, dil, comp, LANES), F32)],
        compiler_params=_params("parallel", "parallel", "parallel"),
        name=f"dilated_attn_{group}",
    )(qkv, qkv, qkv, qkv, qkv, bias)


def _route(logits):
    lane = lax.broadcasted_iota(jnp.int32, logits.shape, 1)
    neg = -jnp.inf
    first_max = lambda vals, mx: jnp.min(jnp.where(vals == mx, lane, LANES), axis=-1, keepdims=True)
    is_grp = (lane >= N_EXPERTS) & (lane < N_EXPERTS + N_EXPERT_GROUPS)
    gl = jnp.where(is_grp, logits, neg)
    gmax = jnp.max(gl, axis=-1, keepdims=True)
    grp_w = 1.0 / jnp.sum(jnp.exp(gl - gmax), axis=-1, keepdims=True)
    grp = first_max(gl, gmax) - N_EXPERTS
    in_grp = (lane >= grp * EXPERTS_PER_GROUP) & (lane < (grp + 1) * EXPERTS_PER_GROUP)
    el = jnp.where(in_grp, logits, neg)
    v1 = jnp.max(el, axis=-1, keepdims=True)
    i1 = first_max(el, v1)
    el2 = jnp.where(lane == i1, neg, el)
    v2 = jnp.max(el2, axis=-1, keepdims=True)
    i2 = first_max(el2, v2)
    e2 = jnp.exp(v2 - v1)
    w1 = grp_w / (1.0 + e2)
    return i1, i2, w1, w1 * e2


def _merge_kernel(x_ref, ya_ref, o0_ref, o1_ref, o2_ref, l0_ref, l1_ref, l2_ref, gmix_ref, wgt_ref,
                  wa_ref, wb_ref, wo_ref, gain_ref, wr_ref, ex_ref, tri_ref, x1_ref, h_ref, route_ref,
                  count_ref, o_scr, l_scr):
    outs, lses = [], []
    for grp, (o_ref, l_ref) in enumerate(((o0_ref, l0_ref), (o1_ref, l1_ref), (o2_ref, l2_ref))):
        dil = o_ref.shape[1]
        if dil == 1:
            outs.append(o_ref[0, 0])
            lses.append(l_ref[0, 0])
            continue
        n = o_ref.shape[2]
        chunks = o_ref.shape[3] // LANES
        for r in range(dil):
            l_scr[grp - 1, pl.ds(r, n, stride=dil), :] = l_ref[0, r]
            for c in range(chunks):
                o_scr[grp - 1, c, pl.ds(r, n, stride=dil), :] = o_ref[0, r, :, c * LANES:(c + 1) * LANES]
        outs.append(jnp.concatenate([o_scr[grp - 1, c] for c in range(chunks)], axis=1))
        lses.append(l_scr[grp - 1])
    def mix(rows):
        l0, l1, l2 = (l[rows] for l in lses)
        m = jnp.maximum(jnp.maximum(l0, l1), l2)
        e0, e1, e2 = jnp.exp(l0 - m), jnp.exp(l1 - m), jnp.exp(l2 - m)
        inv = 1.0 / (e0 + e1 + e2)
        spread = ex_ref[...]
        yb = (_seg_sum(e0 * inv, spread) * outs[0][rows] + _seg_sum(e1 * inv, spread) * outs[1][rows]
              + _seg_sum(e2 * inv, spread) * outs[2][rows])
        x = x_ref[rows, :]
        gates = _sigmoid(_mm(_rms(x, gmix_ref[...]), wgt_ref[...]))
        merged = (gates[:, :D_MODEL] * _mm(ya_ref[rows, :], wa_ref[...])
                  + gates[:, D_MODEL:] * _mm(yb, wb_ref[...]))
        x1 = x + _mm(merged, wo_ref[...])
        h = _rms(x1, gain_ref[...])
        return (x1, h) + _route(_mm_split(h, wr_ref[...]))

    half = x_ref.shape[0] // 2
    parts = [mix(slice(0, half)), mix(slice(half, 2 * half))]
    x1, h, i1, i2, w1, w2 = (jnp.concatenate(both, axis=0) for both in zip(*parts))
    x1_ref[...] = x1
    h_ref[...] = h

    @pl.when(pl.program_id(0) == 0)
    def _():
        count_ref[...] = jnp.zeros_like(count_ref)

    lane = lax.broadcasted_iota(jnp.int32, (x1.shape[0], LANES), 1)
    onehot = jnp.where((lane == i1) | (lane == i2), 1.0, 0.0)
    before = count_ref[0:1, :] + jnp.dot(tri_ref[...], onehot.astype(BF16), preferred_element_type=F32)
    rank1 = jnp.sum(jnp.where(lane == i1, before, 0.0), axis=-1, keepdims=True)
    rank2 = jnp.sum(jnp.where(lane == i2, before, 0.0), axis=-1, keepdims=True)
    last = x1.shape[0] - 1
    count_ref[...] = jnp.broadcast_to(before[last:, :] + onehot[last:, :], count_ref.shape)
    cols = (i1.astype(F32), i2.astype(F32), w1, w2, rank1, rank2)
    info = jnp.zeros(lane.shape, F32)
    for c, col in enumerate(cols):
        info = jnp.where(lane == c, col, info)
    route_ref[...] = info


def _merge(x, ya, outs, lses, gain_mix, w_gates, wa, wb, wo, gain, wr, spread, seq, *, tm):
    t = x.shape[0]
    tps = seq // tm
    tok = lambda n: pl.BlockSpec((tm, n), lambda i: (i, 0))
    full = lambda a: pl.BlockSpec(a.shape, lambda i: (0, 0))
    res = lambda a: pl.BlockSpec((1, a.shape[1], tm // a.shape[1], a.shape[3]),
                                 lambda i: (i // tps, 0, i % tps, 0))
    dilated = len(ATTN_GROUPS) - 1
    tri = jnp.asarray(np.tri(tm, k=-1), dtype=BF16)
    return pl.pallas_call(
        _merge_kernel,
        grid=(t // tm,),
        in_specs=[tok(D_MODEL), tok(RWKV_DIM)] + [res(a) for a in outs] + [res(a) for a in lses] + [
                  full(gain_mix), full(w_gates), full(wa), full(wb), full(wo), full(gain), full(wr), full(spread),
                  full(tri)],
        out_specs=[tok(D_MODEL), tok(D_MODEL), tok(LANES), pl.BlockSpec((8, LANES), lambda i: (0, 0))],
        out_shape=[jax.ShapeDtypeStruct((t, D_MODEL), F32), jax.ShapeDtypeStruct((t, D_MODEL), F32),
                   jax.ShapeDtypeStruct((t, LANES), F32), jax.ShapeDtypeStruct((8, LANES), F32)],
        scratch_shapes=[pltpu.VMEM((dilated, ATTN_OUT_DIM // LANES, tm, LANES), F32),
                        pltpu.VMEM((dilated, tm, LANES), F32)],
        compiler_params=_params("arbitrary"),
        name="merge_route",
    )(x, ya, *outs, *lses, gain_mix, w_gates, wa, wb, wo, gain, wr, spread, tri)


ROUTE_E1, ROUTE_E2, ROUTE_W1, ROUTE_W2, ROUTE_R1, ROUTE_R2 = range(6)


def _moe_plan(counts, n_tiles):
    counts = counts[0, :N_EXPERTS].astype(jnp.int32)
    tile_end = jnp.cumsum((counts + MOE_TILE - 1) // MOE_TILE)
    n_valid = tile_end[-1]
    tile = jnp.minimum(jnp.arange(n_tiles, dtype=jnp.int32), n_valid - 1)
    tile_expert = jnp.sum(tile[:, None] >= tile_end[None, :], axis=1).astype(jnp.int32)
    return tile_expert, n_valid.reshape(1)


def _row_tiles(ref, rows, tile=ROW_TILE):
    return jnp.concatenate([ref[pl.ds(c, rows, stride=tile), :] for c in range(tile)], axis=1)


def _store_row_tiles(ref, x, tile=ROW_TILE):
    for c in range(tile):
        ref[pl.ds(c, x.shape[0], stride=tile), :] = x[:, c * LANES:(c + 1) * LANES]


def _row(ref, r, tile=ROW_TILE):
    return ref.at[pl.ds(pl.multiple_of(r * tile, tile), tile)]


def _dispatch_kernel(route_ref, count_ref, upper_ref, h_ref, xs_in_ref, xs_ref, pos_ref, rows_ref, pos_vmem,
                     pos_smem, sem, pos_sem):
    del xs_in_ref
    tm = h_ref.shape[0]
    tiles = jnp.floor((count_ref[...] + (MOE_TILE - 1)) * (1.0 / MOE_TILE))
    first_row = jnp.dot(tiles.astype(BF16), upper_ref[...], preferred_element_type=F32)[0:1, :] * MOE_TILE
    route = route_ref[...]
    lane = lax.broadcasted_iota(jnp.int32, route.shape, 1)
    col = lambda c: route[:, c:c + 1]
    place = lambda e, r: jnp.sum(jnp.where(lane == col(e).astype(jnp.int32), first_row, 0.0), axis=-1,
                                 keepdims=True) + col(r)
    both = jnp.where(lane == 0, place(ROUTE_E1, ROUTE_R1), jnp.where(lane == 1, place(ROUTE_E2, ROUTE_R2), 0.0))
    pos = both.T[0:ROW_TILE, :].astype(jnp.int32)
    pos_ref[0] = pos[0:2]
    pos_vmem[...] = pos
    to_smem = pltpu.make_async_copy(pos_vmem, pos_smem, pos_sem)
    to_smem.start()
    _store_row_tiles(rows_ref, h_ref[...])
    to_smem.wait()

    def copies(j):
        src = _row(rows_ref, j)
        return (pltpu.make_async_copy(src, _row(xs_ref, pos_smem[0, j]), sem.at[0]),
                pltpu.make_async_copy(src, _row(xs_ref, pos_smem[1, j]), sem.at[1]))

    def start(j, carry):
        for stream, cp in enumerate(copies(j)):
            cp.start(priority=stream)
        return carry

    lax.fori_loop(0, tm, start, 0, unroll=8)
    for stream in range(2):
        pltpu.make_async_copy(rows_ref, rows_ref, sem.at[stream]).wait()


def _dispatch(h, route, counts, blank, *, tm):
    t, d = h.shape
    upper = jnp.asarray(np.triu(np.ones((LANES, LANES)), k=1), dtype=BF16)
    tok = lambda n: pl.BlockSpec((tm, n), lambda i: (i, 0))
    full = lambda a: pl.BlockSpec(a.shape, lambda i: (0, 0))
    return pl.pallas_call(
        _dispatch_kernel,
        grid=(t // tm,),
        in_specs=[tok(LANES), full(counts), full(upper), tok(d), pl.BlockSpec(memory_space=pl.ANY)],
        out_specs=[pl.BlockSpec(memory_space=pl.ANY), pl.BlockSpec((1, 2, tm), lambda i: (i, 0, 0))],
        out_shape=[jax.ShapeDtypeStruct(blank.shape, blank.dtype),
                   jax.ShapeDtypeStruct((t // tm, 2, tm), jnp.int32)],
        scratch_shapes=[pltpu.VMEM((tm * ROW_TILE, LANES), F32), pltpu.VMEM((ROW_TILE, tm), jnp.int32),
                        pltpu.SMEM((ROW_TILE, tm), jnp.int32), pltpu.SemaphoreType.DMA((2,)),
                        pltpu.SemaphoreType.DMA],
        input_output_aliases={4: 0},
        compiler_params=_params("arbitrary"),
        name="moe_dispatch",
    )(route, counts, upper, h, blank)


def _expert_kernel(te_ref, nv_ref, x_ref, wg_ref, wu_ref, wd_ref, o_ref):
    live = pl.program_id(0) < nv_ref[0]

    @pl.when(live)
    def _():
        x = _row_tiles(x_ref, MOE_TILE)
        gate = _mm(x, wg_ref[0])
        up = _mm(x, wu_ref[0])
        _store_row_tiles(o_ref, _mm(gate * _sigmoid(gate) * up, wd_ref[0]))

    @pl.when(jnp.logical_not(live))
    def _():
        o_ref[...] = jnp.zeros_like(o_ref)


def _experts(xs, tile_expert, n_valid, wg, wu, wd):
    d = wg.shape[1]
    weight = lambda shape: pl.BlockSpec((1,) + shape, lambda i, te, nv: (te[i], 0, 0))
    blk = (MOE_TILE * ROW_TILE, LANES)
    return pl.pallas_call(
        _expert_kernel,
        grid_spec=pltpu.PrefetchScalarGridSpec(
            num_scalar_prefetch=2,
            grid=(xs.shape[0] // blk[0],),
            in_specs=[pl.BlockSpec(blk, lambda i, te, nv: (jnp.minimum(i, nv[0] - 1), 0)),
                      weight((d, D_FF_EXPERT)), weight((d, D_FF_EXPERT)), weight((D_FF_EXPERT, d))],
            out_specs=pl.BlockSpec(blk, lambda i, te, nv: (i, 0))),
        out_shape=jax.ShapeDtypeStruct(xs.shape, F32),
        compiler_params=_params("arbitrary"),
        name="moe_experts",
    )(tile_expert, n_valid, xs, wg, wu, wd)


def _ple_kernel(pos_ref, x_ref, route_ref, p_ref, gp_ref, wg_ref, wp_ref, gf_ref, ys_ref, o_ref,
                y_ref, sem, *, pos_tm):
    tm = x_ref.shape[0]
    step, steps = pl.program_id(0), pl.num_programs(0)
    per_pos_tile = pos_tm // tm

    def gather(tile, slot, inline):
        base = (tile // per_pos_tile) * (2 * pos_tm) + (tile % per_pos_tile) * tm

        def body(j, carry):
            for stream in range(2):
                pltpu.make_async_copy(_row(ys_ref, pos_ref[base + stream * pos_tm + j]),
                                      _row(y_ref.at[slot, stream], j), sem.at[slot, stream]
                                      ).start(priority=stream)
            return carry

        if inline:
            for j in range(tm):
                body(j, 0)
        else:
            lax.fori_loop(0, tm, body, 0, unroll=8)

    def drain(slot):
        for stream in range(2):
            buf = y_ref.at[slot, stream]
            pltpu.make_async_copy(buf, buf, sem.at[slot, stream]).wait()

    slot = step % 2

    @pl.when(step == 0)
    def _():
        gather(0, 0, False)

    drain(slot)
    route = route_ref[...]
    w1, w2 = route[:, ROUTE_W1:ROUTE_W1 + 1], route[:, ROUTE_W2:ROUTE_W2 + 1]
    x = x_ref[...] + (w1 * _row_tiles(y_ref.at[slot, 0], tm) + w2 * _row_tiles(y_ref.at[slot, 1], tm))
    gather(jnp.minimum(step + 1, steps - 1), 1 - slot, True)
    gate = _sigmoid(_mm(_rms(x, gp_ref[...]), wg_ref[...]))
    x = x + gate * _mm(p_ref[...], wp_ref[...])
    o_ref[...] = _rms(x, gf_ref[...])

    @pl.when(step == steps - 1)
    def _():
        drain(1 - slot)


def _ple(x, route, pos, ys, p, gain_ple, w_gate, w_proj, gain_final, *, tm):
    t = x.shape[0]
    pos_tm = pos.shape[2]
    assert pos_tm % tm == 0
    tok = lambda n: pl.BlockSpec((tm, n), lambda i, ps: (i, 0))
    full = lambda a: pl.BlockSpec(a.shape, lambda i, ps: (0, 0))
    return pl.pallas_call(
        functools.partial(_ple_kernel, pos_tm=pos_tm),
        grid_spec=pltpu.PrefetchScalarGridSpec(
            num_scalar_prefetch=1,
            grid=(t // tm,),
            in_specs=[tok(D_MODEL), tok(LANES), tok(PLE_DIM), full(gain_ple), full(w_gate), full(w_proj),
                      full(gain_final), pl.BlockSpec(memory_space=pl.ANY)],
            out_specs=tok(D_MODEL),
            scratch_shapes=[pltpu.VMEM((2, 2, tm * ROW_TILE, LANES), F32), pltpu.SemaphoreType.DMA((2, 2))]),
        out_shape=jax.ShapeDtypeStruct((t, D_MODEL), F32),
        compiler_params=_params("arbitrary"),
        name="ple_final",
    )(pos.reshape(-1), x, route, p, gain_ple, w_gate, w_proj, gain_final, ys)


def _pad_rows(w, start, total):
    return jnp.zeros((total, w.shape[1]), w.dtype).at[start:start + w.shape[0]].set(w)


def _layer(x, p, seq, norm_mix, w_in, rwkv_mu, rwkv_w0, rwkv_w2, rwkv_a0, rwkv_a2, rwkv_g2, rwkv_k_k,
           rwkv_k_a, rwkv_r_k, rwkv_ln_w, rwkv_ln_b, w_proj_a, w_proj_b, w_out, norm_moe, w_router_group,
           w_router_expert, w_exp_gate, w_exp_up, w_exp_down, norm_ple, w_ple_gate, w_ple_proj, norm_final):
    t = x.shape[0]
    bsz = t // seq
    row = lambda a: a.reshape(1, -1)
    c0, c1 = RWKV_COLS, RWKV_COLS + ATTN_COLS
    x3 = x.reshape(bsz, seq, D_MODEL)

    def w_group(grp):
        lo = lambda which: c0 + which * ATTN_DIM + grp * ATTN_OUT_DIM
        return jnp.concatenate([w_in[:, lo(which):lo(which) + ATTN_OUT_DIM] for which in range(3)],
                               axis=1).astype(BF16)

    qkv = [_qkv_proj(x3, row(norm_mix), w_group(grp), grp, tm=QKV_ROWS) for grp in range(len(ATTN_GROUPS))]

    seg = np.arange(RWKV_DIM) // HEAD_DIM
    seg_ones = jnp.asarray(seg[:, None] == seg[None, :], dtype=BF16)
    n_tiles = 2 * t // MOE_TILE + N_EXPERTS
    ya, blank = _rwkv_branch(
        x3, row(norm_mix), w_in[:, :c0].astype(BF16), row(rwkv_mu), row(rwkv_w0),
        _pad_rows(rwkv_w2, 0, LORA_COLS), row(rwkv_a0), _pad_rows(rwkv_a2, W_LORA, LORA_COLS),
        _pad_rows(rwkv_g2, W_LORA + A_LORA, LORA_COLS), row(rwkv_k_k), row(rwkv_k_a), row(rwkv_ln_w),
        row(rwkv_ln_b), row(rwkv_r_k), seg_ones, n_tiles * MOE_TILE * ROW_TILE, tb=RWKV_ROWS)
    ya = ya.reshape(t, RWKV_DIM)

    attn = [_attn_group(qkv[grp], grp) for grp in range(len(ATTN_GROUPS))]

    head_of_lane = np.arange(ATTN_OUT_DIM) // HEAD_DIM
    spread = jnp.asarray(np.arange(LANES)[:, None] == head_of_lane[None, :], dtype=BF16)
    w_route = jnp.concatenate(
        [jnp.moveaxis(w_router_expert, 0, 1).reshape(D_MODEL, N_EXPERTS), w_router_group,
         jnp.zeros((D_MODEL, LANES - N_EXPERTS - N_EXPERT_GROUPS), F32)], axis=1)
    x1, h_moe, route, counts = _merge(x, ya, [o for o, _ in attn], [l for _, l in attn],
                             row(norm_mix), w_in[:, c1:].astype(BF16),
                             w_proj_a.astype(BF16), w_proj_b.astype(BF16), w_out.astype(BF16),
                             row(norm_moe), w_route, spread, seq, tm=MERGE_ROWS)
    tile_expert, n_valid = _moe_plan(counts, n_tiles)
    xs, pos = _dispatch(h_moe, route, counts, blank, tm=DISPATCH_ROWS)
    ys = _experts(xs, tile_expert, n_valid,
                  w_exp_gate.reshape(N_EXPERTS, D_MODEL, D_FF_EXPERT),
                  w_exp_up.reshape(N_EXPERTS, D_MODEL, D_FF_EXPERT),
                  w_exp_down.reshape(N_EXPERTS, D_FF_EXPERT, D_MODEL))
    return _ple(x1, route, pos, ys, p, row(norm_ple), w_ple_gate.astype(BF16),
                w_ple_proj.astype(BF16), row(norm_final), tm=COMBINE_ROWS)


def kernel(x, p, norm_mix, w_in, rwkv_mu, rwkv_w0, rwkv_w2, rwkv_a0, rwkv_a2, rwkv_g2, rwkv_k_k, rwkv_k_a, rwkv_r_k, rwkv_ln_w, rwkv_ln_b, w_proj_a, w_proj_b, w_out, norm_moe, w_router_group, w_router_expert, w_exp_gate, w_exp_up, w_exp_down, norm_ple, w_ple_gate, w_ple_proj, norm_final):
    bsz, seq, d = x.shape
    depth = w_in.shape[0]
    assert depth == 1, "the final norm is fused into the (single) layer"
    out = _layer(x.reshape(bsz * seq, d).astype(F32), p[0].reshape(bsz * seq, PLE_DIM), seq,
                 norm_mix[0], w_in[0], rwkv_mu[0], rwkv_w0[0], rwkv_w2[0], rwkv_a0[0], rwkv_a2[0],
                 rwkv_g2[0], rwkv_k_k[0], rwkv_k_a[0], rwkv_r_k[0], rwkv_ln_w[0], rwkv_ln_b[0],
                 w_proj_a[0], w_proj_b[0], w_out[0], norm_moe[0], w_router_group[0], w_router_expert[0],
                 w_exp_gate[0], w_exp_up[0], w_exp_down[0], norm_ple[0], w_ple_gate[0], w_ple_proj[0],
                 norm_final)
    return out.reshape(bsz, seq, d)
```
